```python
import math
import jax, jax.numpy as jnp
from jax import lax
import numpy as np

D_MODEL = 1024
BATCH = 8
SEQ = 4096
DEPTH = 4

SSM_GROUP = 16
SSM_GROUPS = 32
SSM_WIDTH = SSM_GROUP * SSM_GROUPS
SSM_STATE = 64
HEAD_DIM = 64
HEADS_PER_GROUP = 8
ATTN_PATTERNS = ((128, 1), (512, 4), (2048, 16))
N_ATTN_GROUPS = 3
ATTN_WIDTH = HEADS_PER_GROUP * HEAD_DIM
QKV_WIDTH = N_ATTN_GROUPS * ATTN_WIDTH
QBLK = 128
IN_WIDTH = SSM_WIDTH + 3 * QKV_WIDTH + 2 * D_MODEL
SPLIT_POINTS = (SSM_WIDTH, SSM_WIDTH + QKV_WIDTH, SSM_WIDTH + 2 * QKV_WIDTH,
                SSM_WIDTH + 3 * QKV_WIDTH, SSM_WIDTH + 3 * QKV_WIDTH + D_MODEL)
N_EXPERTS = 64
TOP_K = 8
N_EXPERT_GROUPS = 8
TOPK_GROUPS = 4
EXPERT_HIDDEN = 256
SHARED_HIDDEN = 256
ROUTED_SCALE = 2.5
DISPATCH_BLK = 128
LN_EPS = 1e-5
DEEPNORM_ALPHA = (2 * DEPTH) ** 0.25
DEEPNORM_BETA = (8 * DEPTH) ** -0.25

kernel_name = "s5_dilated_attn_gated_moe_deepnorm"


def layer_norm(x, g, b):
    xf = x.astype(jnp.float32)
    mu = jnp.mean(xf, axis=-1, keepdims=True)
    var = jnp.mean(jnp.square(xf - mu), axis=-1, keepdims=True)
    y = (xf - mu) * lax.rsqrt(var + LN_EPS) * g.astype(jnp.float32) + b.astype(jnp.float32)
    return y.astype(x.dtype)


def swiglu(x, w_gate, w_up, w_down):
    return (jax.nn.silu(x @ w_gate) * (x @ w_up)) @ w_down


def s5_branch(u, lam_re, lam_im, log_dt, b_re, b_im, c_re, c_im, d_skip, w_glu):
    B, L, _ = u.shape
    f32 = jnp.float32
    lam = lax.complex(lam_re.astype(f32), lam_im.astype(f32))
    dt = jnp.exp(log_dt.astype(f32))[:, None]
    lam_bar = jnp.exp(lam * dt)
    b_bar = ((lam_bar - 1.0) / lam)[:, :, None] * lax.complex(b_re.astype(f32), b_im.astype(f32))
    c = lax.complex(c_re.astype(f32), c_im.astype(f32))
    ug = u.astype(f32).reshape(B, L, SSM_GROUPS, SSM_GROUP)
    bu = jnp.einsum('blgc,gpc->blgp', ug.astype(jnp.complex64), b_bar)
    a = jnp.broadcast_to(lam_bar, (1, L) + lam_bar.shape)

    def combine(e1, e2):
        a1, s1 = e1
        a2, s2 = e2
        return a1 * a2, a2 * s1 + s2

    _, states = lax.associative_scan(combine, (a, bu), axis=1)
    y = jnp.real(jnp.einsum('blgp,gcp->blgc', states, c)) + d_skip.astype(f32) * ug
    y = jax.nn.gelu(y.reshape(B, L, SSM_WIDTH))
    y = y * jax.nn.sigmoid(y @ w_glu.astype(f32))
    return y.astype(u.dtype)


def dilated_window_attention(q, k, v, window, dilation):
    B, L, H, Dh = q.shape
    steps = window // dilation
    n = L // dilation
    nb = -(-n // QBLK)
    pad = nb * QBLK - n

    def to_blocks(t):
        t = t.reshape(B, n, dilation, H, Dh).transpose(0, 2, 3, 1, 4)
        t = jnp.pad(t, ((0, 0), (0, 0), (0, 0), (0, pad), (0, 0)))
        return t.reshape(B, dilation, H, nb, QBLK, Dh)

    def with_prev(t):
        prev = jnp.pad(t, ((0, 0), (0, 0), (0, 0), (1, 0), (0, 0), (0, 0)))[:, :, :, :-1]
        return jnp.concatenate([prev, t], axis=4)

    qb = to_blocks(q)
    kc = with_prev(to_blocks(k))
    vc = with_prev(to_blocks(v))
    s = jnp.einsum('bdhnqc,bdhnkc->bdhnqk', qb, kc).astype(jnp.float32) * (HEAD_DIM ** -0.5)
    qi = jnp.arange(QBLK)[:, None]
    kj = jnp.arange(2 * QBLK)[None, :]
    dist = qi + QBLK - kj
    blk = jnp.arange(nb)[:, None, None]
    allowed = (dist >= 0) & (dist <= steps) & (blk * QBLK + kj - QBLK >= 0)
    s = jnp.where(allowed, s, -jnp.inf)
    m = jnp.max(s, axis=-1, keepdims=True)
    p = jnp.exp(s - m)
    denom = jnp.sum(p, axis=-1)
    o = jnp.einsum('bdhnqk,bdhnkc->bdhnqc', p, vc.astype(jnp.float32)) / denom[..., None]
    lse = m[..., 0] + jnp.log(denom)

    def from_blocks(t):
        t = t.reshape((B, dilation, H, nb * QBLK) + t.shape[5:])[:, :, :, :n]
        t = jnp.moveaxis(t, 3, 1)
        return t.reshape((B, L, H) + t.shape[4:])

    return from_blocks(o), from_blocks(lse)


def token_mixer(x, w_in, b_in, lam_re, lam_im, log_dt, b_re, b_im, c_re, c_im, d_skip,
                w_glu, w_br_ssm, w_br_attn, w_out):
    B, L, _ = x.shape
    proj = x @ w_in + b_in
    u, q, k, v, g_ssm, g_attn = jnp.split(proj, SPLIT_POINTS, axis=-1)
    y_ssm = s5_branch(u, lam_re, lam_im, log_dt, b_re, b_im, c_re, c_im, d_skip, w_glu) @ w_br_ssm
    q = q.reshape(B, L, N_ATTN_GROUPS, HEADS_PER_GROUP, HEAD_DIM)
    k = k.reshape(B, L, N_ATTN_GROUPS, HEADS_PER_GROUP, HEAD_DIM)
    v = v.reshape(B, L, N_ATTN_GROUPS, HEADS_PER_GROUP, HEAD_DIM)
    outs, lses = [], []
    for g, (window, dilation) in enumerate(ATTN_PATTERNS):
        o_g, lse_g = dilated_window_attention(q[:, :, g], k[:, :, g], v[:, :, g], window, dilation)
        outs.append(o_g)
        lses.append(lse_g)
    wts = jax.nn.softmax(jnp.stack(lses, axis=0), axis=0)
    y_attn = jnp.einsum('gblh,gblhc->blhc', wts, jnp.stack(outs, axis=0))
    y_attn = y_attn.reshape(B, L, ATTN_WIDTH).astype(x.dtype) @ w_br_attn
    merged = jax.nn.sigmoid(g_ssm) * y_ssm + jax.nn.sigmoid(g_attn) * y_attn
    return merged @ w_out


def moe_ffn(x, router_w, router_bias, w_gate, w_up, w_down, sh_gate, sh_up, sh_down):
    B, L, D = x.shape
    N = B * L
    x2 = x.reshape(N, D)
    scores = jax.nn.sigmoid((x2 @ router_w).astype(jnp.float32))
    sel = scores + router_bias.astype(jnp.float32)
    grp_score = lax.top_k(sel.reshape(N, N_EXPERT_GROUPS, N_EXPERTS // N_EXPERT_GROUPS), 2)[0].sum(-1)
    _, top_grp = lax.top_k(grp_score, TOPK_GROUPS)
    grp_mask = jax.nn.one_hot(top_grp, N_EXPERT_GROUPS).sum(axis=1) > 0
    expert_mask = jnp.repeat(grp_mask, N_EXPERTS // N_EXPERT_GROUPS, axis=1)
    _, idx = lax.top_k(jnp.where(expert_mask, sel, -jnp.inf), TOP_K)
    gates = jnp.take_along_axis(scores, idx, axis=1)
    gates = gates / jnp.sum(gates, axis=-1, keepdims=True) * ROUTED_SCALE
    nk = N * TOP_K
    flat_e = idx.reshape(nk)
    order = jnp.argsort(flat_e)
    sorted_e = flat_e[order]
    tok = (order // TOP_K).astype(jnp.int32)
    gate_sorted = gates.reshape(nk)[order]
    counts = jnp.bincount(flat_e, length=N_EXPERTS)
    start = jnp.cumsum(counts) - counts
    pad_counts = (counts + DISPATCH_BLK - 1) // DISPATCH_BLK * DISPATCH_BLK
    pad_end = jnp.cumsum(pad_counts)
    pad_start = pad_end - pad_counts
    dest = pad_start[sorted_e] + jnp.arange(nk, dtype=jnp.int32) - start[sorted_e]
    n_rows = nk + N_EXPERTS * DISPATCH_BLK
    n_blocks = n_rows // DISPATCH_BLK
    row_tok = jnp.zeros((n_rows,), jnp.int32).at[dest].set(tok)
    row_gate = jnp.zeros((n_rows,), jnp.float32).at[dest].set(gate_sorted)
    blk_e = jnp.minimum(jnp.searchsorted(pad_end, jnp.arange(n_blocks, dtype=jnp.int32) * DISPATCH_BLK,
                                         side='right'), N_EXPERTS - 1)

    def expert_block(acc, blk):
        e, rows, g = blk
        yb = swiglu(x2[rows], w_gate[e], w_up[e], w_down[e])
        return acc.at[rows].add(yb * g[:, None].astype(x2.dtype)), None

    routed, _ = lax.scan(expert_block, jnp.zeros_like(x2),
                         (blk_e, row_tok.reshape(n_blocks, DISPATCH_BLK),
                          row_gate.reshape(n_blocks, DISPATCH_BLK)))
    shared = swiglu(x2, sh_gate, sh_up, sh_down)
    return (routed + shared).reshape(B, L, D)


def setup_inputs(seed: int = 0) -> dict:
    key = jax.random.key(seed)
    ks = jax.random.split(key, 32)
    f32 = jnp.float32
    nrm = lambda k, shape, scale: jax.random.normal(k, shape, f32) * scale
    Dl = DEPTH
    lam_re = -0.5 + nrm(ks[3], (Dl, SSM_GROUPS, SSM_STATE), 0.01)
    lam_im = math.pi * jnp.arange(SSM_STATE, dtype=f32) + nrm(ks[4], (Dl, SSM_GROUPS, SSM_STATE), 0.01)
    log_dt = jax.random.uniform(ks[5], (Dl, SSM_GROUPS), f32, math.log(1e-3), math.log(1e-1))
    return {
        "x": nrm(ks[0], (BATCH, SEQ, D_MODEL), 1.0),
        "w_in": nrm(ks[1], (Dl, D_MODEL, IN_WIDTH), D_MODEL ** -0.5),
        "b_in": nrm(ks[2], (Dl, IN_WIDTH), 0.02),
        "ssm_lam_re": lam_re,
        "ssm_lam_im": lam_im,
        "ssm_log_dt": log_dt,
        "ssm_b_re": nrm(ks[6], (Dl, SSM_GROUPS, SSM_STATE, SSM_GROUP), (2 * SSM_GROUP) ** -0.5),
        "ssm_b_im": nrm(ks[7], (Dl, SSM_GROUPS, SSM_STATE, SSM_GROUP), (2 * SSM_GROUP) ** -0.5),
        "ssm_c_re": nrm(ks[8], (Dl, SSM_GROUPS, SSM_GROUP, SSM_STATE), (2 * SSM_STATE) ** -0.5),
        "ssm_c_im": nrm(ks[9], (Dl, SSM_GROUPS, SSM_GROUP, SSM_STATE), (2 * SSM_STATE) ** -0.5),
        "ssm_d": nrm(ks[10], (Dl, SSM_GROUPS, SSM_GROUP), 1.0),
        "w_glu": nrm(ks[11], (Dl, SSM_WIDTH, SSM_WIDTH), SSM_WIDTH ** -0.5),
        "w_br_ssm": nrm(ks[12], (Dl, SSM_WIDTH, D_MODEL), SSM_WIDTH ** -0.5 * DEEPNORM_BETA),
        "w_br_attn": nrm(ks[13], (Dl, ATTN_WIDTH, D_MODEL), ATTN_WIDTH ** -0.5 * DEEPNORM_BETA),
        "w_out": nrm(ks[14], (Dl, D_MODEL, D_MODEL), D_MODEL ** -0.5 * DEEPNORM_BETA),
        "ln1_g": 1.0 + nrm(ks[15], (Dl, D_MODEL), 0.02),
        "ln1_b": nrm(ks[16], (Dl, D_MODEL), 0.02),
        "router_w": nrm(ks[17], (Dl, D_MODEL, N_EXPERTS), D_MODEL ** -0.5),
        "router_bias": nrm(ks[18], (Dl, N_EXPERTS), 0.01),
        "exp_w_gate": nrm(ks[19], (Dl, N_EXPERTS, D_MODEL, EXPERT_HIDDEN), D_MODEL ** -0.5),
        "exp_w_up": nrm(ks[20], (Dl, N_EXPERTS, D_MODEL, EXPERT_HIDDEN), D_MODEL ** -0.5),
        "exp_w_down": nrm(ks[21], (Dl, N_EXPERTS, EXPERT_HIDDEN, D_MODEL), EXPERT_HIDDEN ** -0.5 * DEEPNORM_BETA),
        "sh_w_gate": nrm(ks[22], (Dl, D_MODEL, SHARED_HIDDEN), D_MODEL ** -0.5),
        "sh_w_up": nrm(ks[23], (Dl, D_MODEL, SHARED_HIDDEN), D_MODEL ** -0.5),
        "sh_w_down": nrm(ks[24], (Dl, SHARED_HIDDEN, D_MODEL), SHARED_HIDDEN ** -0.5 * DEEPNORM_BETA),
        "ln2_g": 1.0 + nrm(ks[25], (Dl, D_MODEL), 0.02),
        "ln2_b": nrm(ks[26], (Dl, D_MODEL), 0.02),
    }


def reference(x, w_in, b_in, ssm_lam_re, ssm_lam_im, ssm_log_dt, ssm_b_re, ssm_b_im, ssm_c_re, ssm_c_im,
              ssm_d, w_glu, w_br_ssm, w_br_attn, w_out, ln1_g, ln1_b, router_w, router_bias,
              exp_w_gate, exp_w_up, exp_w_down, sh_w_gate, sh_w_up, sh_w_down, ln2_g, ln2_b):
    for l in range(DEPTH):
        mix = token_mixer(x, w_in[l], b_in[l], ssm_lam_re[l], ssm_lam_im[l], ssm_log_dt[l],
                          ssm_b_re[l], ssm_b_im[l], ssm_c_re[l], ssm_c_im[l], ssm_d[l],
                          w_glu[l], w_br_ssm[l], w_br_attn[l], w_out[l])
        x = layer_norm(DEEPNORM_ALPHA * x + mix, ln1_g[l], ln1_b[l])
        ffn = moe_ffn(x, router_w[l], router_bias[l], exp_w_gate[l], exp_w_up[l], exp_w_down[l],
                      sh_w_gate[l], sh_w_up[l], sh_w_down[l])
        x = layer_norm(DEEPNORM_ALPHA * x + ffn, ln2_g[l], ln2_b[l])
    return x
```

```python
import functools
import math

import jax
import jax.numpy as jnp
from jax import lax
from jax.experimental import pallas as pl
from jax.experimental.pallas import tpu as pltpu

F32 = jnp.float32
BF16 = jnp.bfloat16

SSM_GROUP = 16
SSM_GROUPS = 32
SSM_WIDTH = SSM_GROUP * SSM_GROUPS
SSM_STATE = 64
HEAD_DIM = 64
HEADS = 8
ATTN_PATTERNS = ((128, 1), (512, 4), (2048, 16))
ATTN_WIDTH = HEADS * HEAD_DIM
QBLK = 128
N_EXPERTS = 64
TOP_K = 8
N_EXPERT_GROUPS = 8
GROUP_SIZE = N_EXPERTS // N_EXPERT_GROUPS
TOPK_GROUPS = 4
ROUTED_SCALE = 2.5
LN_EPS = 1e-5
NEG = -1e30

LANES = 128
SUBLANES = 8
VMEM_LIMIT = 56 * 1024 * 1024

INPROJ_TM = 1024
INPROJ_TN = 1024
S5_T = 128
S5_KB = 4
MERGE_TM = 512
ROUTE_TM = 512
MOE_BLK = 512
GATHER_G = 2048
COMBINE_TM = 256


def _cparams(sem):
    return pltpu.CompilerParams(dimension_semantics=sem, vmem_limit_bytes=VMEM_LIMIT)


def _inproj_kernel(x_ref, w_ref, b_ref, o_ref, xb_ref):
    @pl.when(pl.program_id(1) == 0)
    def _():
        xb_ref[...] = x_ref[...].astype(BF16)

    acc = jnp.dot(xb_ref[...], w_ref[...], preferred_element_type=F32)
    o_ref[...] = (acc + b_ref[...]).astype(BF16)


def _inproj(x, w, b, tm=INPROJ_TM, tn=INPROJ_TN):
    n, d = x.shape
    width = w.shape[1]
    tm = min(tm, n)
    return pl.pallas_call(
        _inproj_kernel,
        grid=(n // tm, width // tn),
        in_specs=[pl.BlockSpec((tm, d), lambda i, j: (i, 0)),
                  pl.BlockSpec((d, tn), lambda i, j: (0, j)),
                  pl.BlockSpec((1, tn), lambda i, j: (0, j))],
        out_specs=pl.BlockSpec((tm, tn), lambda i, j: (i, j)),
        out_shape=jax.ShapeDtypeStruct((n, width), BF16),
        scratch_shapes=[pltpu.VMEM((tm, d), BF16)],
        compiler_params=_cparams(("arbitrary", "arbitrary")),
        name="inproj",
    )(x, w, b)


def _gelu_tanh(x):
    c = math.sqrt(2.0 / math.pi)
    return 0.5 * x * (1.0 + jnp.tanh(c * (x + 0.044715 * (x * x * x))))


def _s5_kernel(u_ref, bm_ref, cre_ref, cim_ref, are_ref, aim_ref, d_ref, wglu_ref,
               o_ref, sre_ref, sim_ref, st_re_ref, st_im_ref, y_ref, *, steps, batch):
    kw = SSM_WIDTH // S5_KB
    sw = SSM_GROUPS * SSM_STATE // S5_KB

    @pl.when(pl.program_id(0) == 0)
    def _():
        st_re_ref[...] = jnp.zeros_like(st_re_ref)
        st_im_ref[...] = jnp.zeros_like(st_im_ref)

    for k in range(S5_KB):
        ls = slice(k * sw, (k + 1) * sw)
        bu = jnp.dot(u_ref[:, k * kw:(k + 1) * kw], bm_ref[k], preferred_element_type=F32)
        sre_ref[:, ls] = bu[:, :sw]
        sim_ref[:, ls] = bu[:, sw:]

        ar = jnp.broadcast_to(are_ref[:, ls], (batch, sw))
        ai = jnp.broadcast_to(aim_ref[:, ls], (batch, sw))

        def step(t, carry):
            sr, si = carry
            rows = pl.ds(pl.multiple_of(t * batch, batch), batch)
            nr = ar * sr - ai * si + sre_ref[rows, ls]
            ni = ar * si + ai * sr + sim_ref[rows, ls]
            sre_ref[rows, ls] = nr
            sim_ref[rows, ls] = ni
            return nr, ni

        sr, si = lax.fori_loop(0, steps, step, (st_re_ref[:, ls], st_im_ref[:, ls]))
        st_re_ref[:, ls] = sr
        st_im_ref[:, ls] = si

        y_ref[:, k * kw:(k + 1) * kw] = (
            jnp.dot(sre_ref[:, ls].astype(BF16), cre_ref[k], preferred_element_type=F32)
            + jnp.dot(sim_ref[:, ls].astype(BF16), cim_ref[k], preferred_element_type=F32))

    y = y_ref[...] + d_ref[...] * u_ref[...].astype(F32)
    y = _gelu_tanh(y)
    z = jnp.dot(y.astype(BF16), wglu_ref[...], preferred_element_type=F32)
    o_ref[...] = (y * jax.nn.sigmoid(z)).astype(BF16)


def _s5(proj, bm, cre, cim, a_re, a_im, d_skip, w_glu, batch, steps=S5_T):
    n = proj.shape[0]
    seq = n // batch
    steps = min(steps, seq)
    rows = steps * batch
    nstate = SSM_GROUPS * SSM_STATE
    const = lambda shape: pl.BlockSpec(shape, lambda i: (0,) * len(shape))
    return pl.pallas_call(
        functools.partial(_s5_kernel, steps=steps, batch=batch),
        grid=(seq // steps,),
        in_specs=[pl.BlockSpec((rows, SSM_WIDTH), lambda i: (i, 0)),
                  const(bm.shape), const(cre.shape), const(cim.shape),
                  const(a_re.shape), const(a_im.shape), const(d_skip.shape), const(w_glu.shape)],
        out_specs=pl.BlockSpec((rows, SSM_WIDTH), lambda i: (i, 0)),
        out_shape=jax.ShapeDtypeStruct((n, SSM_WIDTH), BF16),
        scratch_shapes=[pltpu.VMEM((rows, nstate), F32), pltpu.VMEM((rows, nstate), F32),
                        pltpu.VMEM((batch, nstate), F32), pltpu.VMEM((batch, nstate), F32),
                        pltpu.VMEM((rows, SSM_WIDTH), F32)],
        compiler_params=_cparams(("arbitrary",)),
        name="s5",
    )(proj, bm, cre, cim, a_re, a_im, d_skip, w_glu)


def _s5_params(lam_re, lam_im, log_dt, b_re, b_im, c_re, c_im, d_skip):
    lam = lax.complex(lam_re.astype(F32), lam_im.astype(F32))
    dt = jnp.exp(log_dt.astype(F32))[:, None]
    lam_bar = jnp.exp(lam * dt)
    b_bar = ((lam_bar - 1.0) / lam)[:, :, None] * lax.complex(b_re.astype(F32), b_im.astype(F32))
    gl = SSM_GROUPS // S5_KB
    eye = jnp.eye(gl, dtype=F32)

    def in_blocks(b):
        b = b.reshape(S5_KB, gl, SSM_STATE, SSM_GROUP)
        m = jnp.einsum('kgpc,gh->kgchp', b, eye)
        return m.reshape(S5_KB, gl * SSM_GROUP, gl * SSM_STATE)

    def out_blocks(c):
        c = c.reshape(S5_KB, gl, SSM_GROUP, SSM_STATE)
        m = jnp.einsum('kgcp,gh->kgphc', c, eye)
        return m.reshape(S5_KB, gl * SSM_STATE, gl * SSM_GROUP)

    bm = jnp.concatenate([in_blocks(jnp.real(b_bar)), in_blocks(jnp.imag(b_bar))], axis=-1).astype(BF16)
    cre = out_blocks(c_re.astype(F32)).astype(BF16)
    cim = out_blocks(-c_im.astype(F32)).astype(BF16)
    a_re = jnp.real(lam_bar).reshape(1, -1)
    a_im = jnp.imag(lam_bar).reshape(1, -1)
    return bm, cre, cim, a_re, a_im, d_skip.astype(F32).reshape(1, -1)


def _attn_kernel(q_ref, kp_ref, kc_ref, vp_ref, vc_ref, o_ref, lse_ref, *, steps):
    jb = pl.program_id(1)
    q = q_ref[...]
    k2 = jnp.concatenate([kp_ref[...], kc_ref[...]], axis=0)
    v2 = jnp.concatenate([vp_ref[...], vc_ref[...]], axis=0)
    qi = lax.broadcasted_iota(jnp.int32, (QBLK, 2 * QBLK), 0)
    kj = lax.broadcasted_iota(jnp.int32, (QBLK, 2 * QBLK), 1)
    dist = qi + QBLK - kj
    allowed = (dist >= 0) & (dist <= steps) & ((kj >= QBLK) | (jb > 0))
    outs, lses = [], []
    for h in range(HEADS):
        hs = slice(h * HEAD_DIM, (h + 1) * HEAD_DIM)
        s = lax.dot_general(q[:, hs], k2[:, hs], (((1,), (1,)), ((), ())),
                            preferred_element_type=F32) * (HEAD_DIM ** -0.5)
        s = jnp.where(allowed, s, NEG)
        m = jnp.max(s, axis=-1, keepdims=True)
        p = jnp.exp(s - m)
        denom = jnp.sum(p, axis=-1, keepdims=True)
        o = jnp.dot(p.astype(BF16), v2[:, hs], preferred_element_type=F32) / denom
        outs.append(o)
        lses.append(m + jnp.log(denom))
    o_ref[...] = jnp.concatenate(outs, axis=-1).astype(BF16)
    lse_ref[...] = jnp.concatenate(lses, axis=-1)


def _attn(q, k, v, steps):
    classes, n, width = q.shape
    nb = n // QBLK
    cur = pl.BlockSpec((None, QBLK, width), lambda c, j: (c, j, 0))
    prev = pl.BlockSpec((None, QBLK, width), lambda c, j: (c, jnp.maximum(j - 1, 0), 0))
    return pl.pallas_call(
        functools.partial(_attn_kernel, steps=steps),
        grid=(classes, nb),
        in_specs=[cur, prev, cur, prev, cur],
        out_specs=[cur, pl.BlockSpec((None, QBLK, HEADS), lambda c, j: (c, j, 0))],
        out_shape=[jax.ShapeDtypeStruct((classes, n, width), BF16),
                   jax.ShapeDtypeStruct((classes, n, HEADS), F32)],
        compiler_params=_cparams(("arbitrary", "arbitrary")),
        name="attn",
    )(q, k, k, v, v)


def _layer_norm(z, g, b):
    mu = jnp.mean(z, axis=-1, keepdims=True)
    zc = z - mu
    var = jnp.mean(zc * zc, axis=-1, keepdims=True)
    return zc * lax.rsqrt(var + LN_EPS) * g + b


def _expand_heads(w, e_ref):
    hi = w.astype(BF16)
    lo = (w - hi.astype(F32)).astype(BF16)
    return (jnp.dot(hi, e_ref[...], preferred_element_type=F32)
            + jnp.dot(lo, e_ref[...], preferred_element_type=F32))


def _merge_kernel(x_ref, yssm_ref, o0_ref, o1_ref, o2_ref, lse_ref, gs_ref, ga_ref,
                  wbs_ref, wba_ref, wout_ref, e_ref, g_ref, b_ref, o_ref, *, alpha):
    lse = lse_ref[...]
    l0, l1, l2 = lse[:, 0:HEADS], lse[:, HEADS:2 * HEADS], lse[:, 2 * HEADS:3 * HEADS]
    m = jnp.maximum(jnp.maximum(l0, l1), l2)
    e0, e1, e2 = jnp.exp(l0 - m), jnp.exp(l1 - m), jnp.exp(l2 - m)
    den = e0 + e1 + e2
    y_attn = (_expand_heads(e0 / den, e_ref) * o0_ref[...].astype(F32)
              + _expand_heads(e1 / den, e_ref) * o1_ref[...].astype(F32)
              + _expand_heads(e2 / den, e_ref) * o2_ref[...].astype(F32))
    ya = jnp.dot(y_attn.astype(BF16), wba_ref[...], preferred_element_type=F32)
    ys = jnp.dot(yssm_ref[...], wbs_ref[...], preferred_element_type=F32)
    merged = (jax.nn.sigmoid(gs_ref[...].astype(F32)) * ys
              + jax.nn.sigmoid(ga_ref[...].astype(F32)) * ya)
    mix = jnp.dot(merged.astype(BF16), wout_ref[...], preferred_element_type=F32)
    o_ref[...] = _layer_norm(alpha * x_ref[...] + mix, g_ref[...], b_ref[...])


def _merge(x, y_ssm, o0, o1, o2, lse, proj, w_br_ssm, w_br_attn, w_out, expand, ln_g, ln_b,
           alpha, tm=MERGE_TM):
    n, d = x.shape
    tm = min(tm, n)
    gate_blk = proj.shape[1] // d
    row = lambda w: pl.BlockSpec((tm, w), lambda i: (i, 0))
    const = lambda a: pl.BlockSpec(a.shape, lambda i: (0,) * a.ndim)
    return pl.pallas_call(
        functools.partial(_merge_kernel, alpha=alpha),
        grid=(n // tm,),
        in_specs=[row(d), row(SSM_WIDTH), row(ATTN_WIDTH), row(ATTN_WIDTH), row(ATTN_WIDTH),
                  row(lse.shape[1]),
                  pl.BlockSpec((tm, d), lambda i: (i, gate_blk - 2)),
                  pl.BlockSpec((tm, d), lambda i: (i, gate_blk - 1)),
                  const(w_br_ssm), const(w_br_attn), const(w_out), const(expand),
                  const(ln_g), const(ln_b)],
        out_specs=row(d),
        out_shape=jax.ShapeDtypeStruct((n, d), F32),
        compiler_params=_cparams(("arbitrary",)),
        name="merge",
    )(x, y_ssm, o0, o1, o2, lse, proj, proj, w_br_ssm, w_br_attn, w_out, expand, ln_g, ln_b)


def _first_argmax(v, iota, size, axis):
    m = jnp.max(v, axis=axis, keepdims=True)
    idx = jnp.min(jnp.where(v == m, iota, size), axis=axis, keepdims=True)
    return m, idx


def _route_kernel(x_ref, rwh_ref, rwl_ref, bias_ref, eg_ref, tri_ref,
                  eidx_ref, gate_ref, rank_ref, cnt_ref, carry_ref):
    @pl.when(pl.program_id(0) == 0)
    def _():
        carry_ref[...] = jnp.zeros_like(carry_ref)

    x = x_ref[...]
    tm = x.shape[0]
    xh = x.astype(BF16)
    xl = (x - xh.astype(F32)).astype(BF16)
    nt = (((1,), (1,)), ((), ()))
    logits = (lax.dot_general(rwh_ref[...], xh, nt, preferred_element_type=F32)
              + lax.dot_general(rwh_ref[...], xl, nt, preferred_element_type=F32)
              + lax.dot_general(rwl_ref[...], xh, nt, preferred_element_type=F32))
    scores = jax.nn.sigmoid(logits)
    sel = scores + bias_ref[...]

    sel3 = sel.reshape(N_EXPERT_GROUPS, GROUP_SIZE, tm)
    iw = lax.broadcasted_iota(jnp.int32, sel3.shape, 1)
    m1, i1 = _first_argmax(sel3, iw, GROUP_SIZE, 1)
    m2 = jnp.max(jnp.where(iw == i1, NEG, sel3), axis=1, keepdims=True)
    gs = (m1 + m2).reshape(N_EXPERT_GROUPS, tm)

    ig = lax.broadcasted_iota(jnp.int32, gs.shape, 0)
    gmask = jnp.zeros(gs.shape, F32)
    for _ in range(TOPK_GROUPS):
        _, gi = _first_argmax(gs, ig, N_EXPERT_GROUPS, 0)
        hit = ig == gi
        gmask = jnp.where(hit, 1.0, gmask)
        gs = jnp.where(hit, NEG, gs)
    emask = jnp.dot(eg_ref[...], gmask.astype(BF16), preferred_element_type=F32) > 0.5

    masked = jnp.where(emask, sel, NEG)
    ie = lax.broadcasted_iota(jnp.int32, masked.shape, 0)
    chosen = jnp.zeros(masked.shape, F32)
    idxs, vals = [], []
    for _ in range(TOP_K):
        _, ei = _first_argmax(masked, ie, N_EXPERTS, 0)
        hit = ie == ei
        idxs.append(ei)
        vals.append(jnp.sum(jnp.where(hit, scores, 0.0), axis=0, keepdims=True))
        chosen = jnp.where(hit, 1.0, chosen)
        masked = jnp.where(hit, NEG, masked)
    total = vals[0]
    for v in vals[1:]:
        total = total + v

    prefix = jnp.dot(chosen.astype(BF16), tri_ref[...], preferred_element_type=F32)
    pos = carry_ref[:, 0:1] + prefix - 1.0
    ranks = [jnp.sum(jnp.where(ie == ei, pos, 0.0), axis=0, keepdims=True) for ei in idxs]
    carry = carry_ref[...] + jnp.sum(chosen, axis=1, keepdims=True)
    carry_ref[...] = carry
    cnt_ref[...] = carry

    eidx_ref[...] = jnp.concatenate(idxs, axis=0)
    gate_ref[...] = jnp.concatenate([v / total * ROUTED_SCALE for v in vals], axis=0)
    rank_ref[...] = jnp.concatenate(ranks, axis=0).astype(jnp.int32)


def _route(x, rw_hi, rw_lo, bias, eg, tri, tm=ROUTE_TM):
    n, d = x.shape
    tm = min(tm, n)
    const = lambda a: pl.BlockSpec(a.shape, lambda i: (0,) * a.ndim)
    col = pl.BlockSpec((TOP_K, tm), lambda i: (0, i))
    return pl.pallas_call(
        _route_kernel,
        grid=(n // tm,),
        in_specs=[pl.BlockSpec((tm, d), lambda i: (i, 0)),
                  const(rw_hi), const(rw_lo), const(bias), const(eg), const(tri)],
        out_specs=[col, col, col, pl.BlockSpec((N_EXPERTS, LANES), lambda i: (0, 0))],
        out_shape=[jax.ShapeDtypeStruct((TOP_K, n), jnp.int32),
                   jax.ShapeDtypeStruct((TOP_K, n), F32),
                   jax.ShapeDtypeStruct((TOP_K, n), jnp.int32),
                   jax.ShapeDtypeStruct((N_EXPERTS, LANES), F32)],
        scratch_shapes=[pltpu.VMEM((N_EXPERTS, LANES), F32)],
        compiler_params=_cparams(("arbitrary",)),
        name="route",
    )(x, rw_hi, rw_lo, bias, eg, tri)


def _gather_kernel(idx_ref, src_ref, o_ref, sem, *, rows):
    def issue(j, _):
        pltpu.make_async_copy(src_ref.at[idx_ref[0, 0, j]], o_ref.at[j], sem).start()
        return 0

    lax.fori_loop(0, rows, issue, 0)

    def drain(j, _):
        pltpu.make_async_copy(src_ref.at[0], o_ref.at[j], sem).wait()
        return 0

    lax.fori_loop(0, rows, drain, 0)


def _gather_rows(src, idx, rows=GATHER_G):
    s, d = src.shape
    m = idx.shape[0]
    rows = min(rows, m)
    sub = d // LANES
    out = pl.pallas_call(
        functools.partial(_gather_kernel, rows=rows),
        grid=(m // rows,),
        in_specs=[pl.BlockSpec((1, 1, rows), lambda i: (i, 0, 0), memory_space=pltpu.SMEM),
                  pl.BlockSpec(memory_space=pl.ANY)],
        out_specs=pl.BlockSpec((rows, sub, LANES), lambda i: (i, 0, 0)),
        out_shape=jax.ShapeDtypeStruct((m, sub, LANES), src.dtype),
        scratch_shapes=[pltpu.SemaphoreType.DMA],
        compiler_params=_cparams(("arbitrary",)),
        name="gather_rows",
    )(idx.reshape(m // rows, 1, rows), src.reshape(s, sub, LANES))
    return out.reshape(m, d)


def _gmm_kernel(be_ref, nu_ref, x_ref, wgu_ref, wd_ref, o_ref, *, hidden):
    @pl.when(pl.program_id(0) < nu_ref[0])
    def _():
        x = x_ref[...].astype(BF16)
        gu = jnp.dot(x, wgu_ref[...], preferred_element_type=F32)
        g, u = gu[:, :hidden], gu[:, hidden:]
        h = (g * jax.nn.sigmoid(g) * u).astype(BF16)
        o_ref[...] = jnp.dot(h, wd_ref[...], preferred_element_type=F32)


def _gmm(xs, w_gu, w_down, blk_e, n_used, blk=MOE_BLK):
    n_rows, d = xs.shape
    hidden = w_down.shape[1]
    row = lambda b, be, nu: (jnp.minimum(b, nu[0] - 1), 0)
    return pl.pallas_call(
        functools.partial(_gmm_kernel, hidden=hidden),
        grid_spec=pltpu.PrefetchScalarGridSpec(
            num_scalar_prefetch=2,
            grid=(n_rows // blk,),
            in_specs=[pl.BlockSpec((blk, d), row),
                      pl.BlockSpec((None, d, 2 * hidden), lambda b, be, nu: (be[b], 0, 0)),
                      pl.BlockSpec((None, hidden, d), lambda b, be, nu: (be[b], 0, 0))],
            out_specs=pl.BlockSpec((blk, d), row)),
        out_shape=jax.ShapeDtypeStruct((n_rows, d), F32),
        compiler_params=_cparams(("arbitrary",)),
        name="expert_gmm",
    )(blk_e, n_used, xs, w_gu, w_down)


def _combine_kernel(x_ref, yg_ref, gate_ref, wgu_ref, wd_ref, g_ref, b_ref, o_ref, *, alpha, hidden):
    x = x_ref[...]
    gu = jnp.dot(x.astype(BF16), wgu_ref[...], preferred_element_type=F32)
    g, u = gu[:, :hidden], gu[:, hidden:]
    h = (g * jax.nn.sigmoid(g) * u).astype(BF16)
    acc = jnp.dot(h, wd_ref[...], preferred_element_type=F32)
    gates = gate_ref[...]
    for k in range(TOP_K):
        acc = acc + gates[:, k:k + 1] * yg_ref[k]
    o_ref[...] = _layer_norm(alpha * x + acc, g_ref[...], b_ref[...])


def _combine(x, yg, gates_t, w_gu, w_down, ln_g, ln_b, alpha, tm=COMBINE_TM):
    n, d = x.shape
    tm = min(tm, n)
    hidden = w_down.shape[0]
    const = lambda a: pl.BlockSpec(a.shape, lambda i: (0,) * a.ndim)
    return pl.pallas_call(
        functools.partial(_combine_kernel, alpha=alpha, hidden=hidden),
        grid=(n // tm,),
        in_specs=[pl.BlockSpec((tm, d), lambda i: (i, 0)),
                  pl.BlockSpec((TOP_K, tm, d), lambda i: (0, i, 0)),
                  pl.BlockSpec((tm, TOP_K), lambda i: (i, 0)),
                  const(w_gu), const(w_down), const(ln_g), const(ln_b)],
        out_specs=pl.BlockSpec((tm, d), lambda i: (i, 0)),
        out_shape=jax.ShapeDtypeStruct((n, d), F32),
        compiler_params=_cparams(("arbitrary",)),
        name="combine",
    )(x, yg, gates_t, w_gu, w_down, ln_g, ln_b)


def _token_mixer(x, batch, w_in, b_in, s5p, w_glu, w_br_ssm, w_br_attn, w_out, expand, ln_g, ln_b, alpha):
    n, d = x.shape
    seq = n // batch
    proj = _inproj(x, w_in, b_in)
    y_ssm = _s5(proj, *s5p, w_glu, batch)

    qkv0 = SSM_WIDTH
    n_grp = len(ATTN_PATTERNS)
    qkv_w = n_grp * ATTN_WIDTH
    outs, lses = [], []
    for g, (window, dil) in enumerate(ATTN_PATTERNS):
        assert window // dil == QBLK and (seq // dil) % QBLK == 0

        def classes(col0):
            t = proj[:, col0 + g * ATTN_WIDTH: col0 + (g + 1) * ATTN_WIDTH]
            t = t.reshape(seq // dil, dil * batch, ATTN_WIDTH)
            return jnp.swapaxes(t, 0, 1)

        o_g, lse_g = _attn(classes(qkv0), classes(qkv0 + qkv_w), classes(qkv0 + 2 * qkv_w),
                           window // dil)
        outs.append(jnp.swapaxes(o_g, 0, 1).reshape(n, ATTN_WIDTH))
        lses.append(jnp.swapaxes(lse_g, 0, 1).reshape(n, HEADS))
    lse = jnp.concatenate(lses, axis=-1)
    return _merge(x, y_ssm, outs[0], outs[1], outs[2], lse, proj, w_br_ssm, w_br_attn, w_out,
                  expand, ln_g, ln_b, alpha)


def _moe(x, rw_hi, rw_lo, bias, eg, tri, w_gu, w_down, sh_gu, sh_down, ln_g, ln_b, alpha):
    n, d = x.shape
    eidx, gates, rank, cnt = _route(x, rw_hi, rw_lo, bias, eg, tri)
    counts = cnt[:, 0].astype(jnp.int32)
    nk = n * TOP_K
    n_rows = nk + N_EXPERTS * MOE_BLK
    n_blocks = n_rows // MOE_BLK
    pad_counts = (counts + MOE_BLK - 1) // MOE_BLK * MOE_BLK
    pad_end = jnp.cumsum(pad_counts)
    pad_start = pad_end - pad_counts
    dest = (pad_start[eidx] + rank).reshape(nk)
    tok = jnp.broadcast_to(jnp.arange(n, dtype=jnp.int32)[None, :], (TOP_K, n)).reshape(nk)
    row_tok = jnp.zeros((n_rows,), jnp.int32).at[dest].set(tok)
    blk_e = jnp.minimum(jnp.searchsorted(pad_end, jnp.arange(n_blocks, dtype=jnp.int32) * MOE_BLK,
                                         side='right'), N_EXPERTS - 1).astype(jnp.int32)
    n_used = (pad_end[-1:] // MOE_BLK).astype(jnp.int32)

    xs = _gather_rows(x, row_tok)
    ys = _gmm(xs, w_gu, w_down, blk_e, n_used)
    yg = _gather_rows(ys, dest).reshape(TOP_K, n, d)
    return _combine(x, yg, gates.T, sh_gu, sh_down, ln_g, ln_b, alpha)


def kernel(x, w_in, b_in, ssm_lam_re, ssm_lam_im, ssm_log_dt, ssm_b_re, ssm_b_im, ssm_c_re, ssm_c_im, ssm_d, w_glu, w_br_ssm, w_br_attn, w_out, ln1_g, ln1_b, router_w, router_bias, exp_w_gate, exp_w_up, exp_w_down, sh_w_gate, sh_w_up, sh_w_down, ln2_g, ln2_b):
    batch, seq, d = x.shape
    depth = w_in.shape[0]
    assert batch == SUBLANES
    alpha = (2 * depth) ** 0.25
    n = batch * seq
    xt = jnp.swapaxes(x, 0, 1).reshape(n, d)

    expand = jnp.repeat(jnp.eye(HEADS, dtype=BF16), HEAD_DIM, axis=1)
    eg = jnp.repeat(jnp.eye(N_EXPERT_GROUPS, dtype=BF16), GROUP_SIZE, axis=0)
    tm = min(ROUTE_TM, n)
    tri = (jnp.arange(tm)[:, None] <= jnp.arange(tm)[None, :]).astype(BF16)
    row = lambda a: a.astype(F32).reshape(1, -1)

    for l in range(depth):
        s5p = _s5_params(ssm_lam_re[l], ssm_lam_im[l], ssm_log_dt[l], ssm_b_re[l], ssm_b_im[l],
                         ssm_c_re[l], ssm_c_im[l], ssm_d[l])
        xt = _token_mixer(xt, batch, w_in[l].astype(BF16), row(b_in[l]), s5p, w_glu[l].astype(BF16),
                          w_br_ssm[l].astype(BF16), w_br_attn[l].astype(BF16), w_out[l].astype(BF16),
                          expand, row(ln1_g[l]), row(ln1_b[l]), alpha)
        rwt = router_w[l].astype(F32).T
        rw_hi = rwt.astype(BF16)
        rw_lo = (rwt - rw_hi.astype(F32)).astype(BF16)
        w_gu = jnp.concatenate([exp_w_gate[l], exp_w_up[l]], axis=-1).astype(BF16)
        sh_gu = jnp.concatenate([sh_w_gate[l], sh_w_up[l]], axis=-1).astype(BF16)
        xt = _moe(xt, rw_hi, rw_lo, router_bias[l].astype(F32).reshape(-1, 1), eg, tri,
                  w_gu, exp_w_down[l].astype(BF16), sh_gu, sh_w_down[l].astype(BF16),
                  row(ln2_g[l]), row(ln2_b[l]), alpha)
    return jnp.swapaxes(xt.reshape(seq, batch, d), 0, 1)
```

```python
import functools
import math

import jax
import jax.numpy as jnp
from jax import lax
from jax.experimental import pallas as pl
from jax.experimental.pallas import tpu as pltpu

F32 = jnp.float32
BF16 = jnp.bfloat16

SSM_GROUP = 16
SSM_GROUPS = 32
SSM_WIDTH = SSM_GROUP * SSM_GROUPS
SSM_STATE = 64
HEAD_DIM = 64
HEADS = 8
ATTN_PATTERNS = ((128, 1), (512, 4), (2048, 16))
ATTN_WIDTH = HEADS * HEAD_DIM
QBLK = 128
N_EXPERTS = 64
TOP_K = 8
N_EXPERT_GROUPS = 8
GROUP_SIZE = N_EXPERTS // N_EXPERT_GROUPS
TOPK_GROUPS = 4
ROUTED_SCALE = 2.5
LN_EPS = 1e-5
NEG = -1e30

LANES = 128
SUBLANES = 8
VMEM_LIMIT = 56 * 1024 * 1024

INPROJ_TM = 1024
INPROJ_TN = 1024
S5_T = 128
S5_KB = 4
MERGE_TM = 512
ROUTE_TM = 512
MOE_BLK = 512
GATHER_G = 2048
COMBINE_TM = 256


def _cparams(sem):
    return pltpu.CompilerParams(dimension_semantics=sem, vmem_limit_bytes=VMEM_LIMIT)


def _inproj_kernel(x_ref, w_ref, b_ref, o_ref, xb_ref):
    @pl.when(pl.program_id(1) == 0)
    def _():
        xb_ref[...] = x_ref[...].astype(BF16)

    acc = jnp.dot(xb_ref[...], w_ref[...], preferred_element_type=F32)
    o_ref[...] = (acc + b_ref[...]).astype(BF16)


def _inproj(x, w, b, tm=INPROJ_TM, tn=INPROJ_TN):
    n, d = x.shape
    width = w.shape[1]
    tm = min(tm, n)
    return pl.pallas_call(
        _inproj_kernel,
        grid=(n // tm, width // tn),
        in_specs=[pl.BlockSpec((tm, d), lambda i, j: (i, 0)),
                  pl.BlockSpec((d, tn), lambda i, j: (0, j)),
                  pl.BlockSpec((1, tn), lambda i, j: (0, j))],
        out_specs=pl.BlockSpec((tm, tn), lambda i, j: (i, j)),
        out_shape=jax.ShapeDtypeStruct((n, width), BF16),
        scratch_shapes=[pltpu.VMEM((tm, d), BF16)],
        compiler_params=_cparams(("arbitrary", "arbitrary")),
        name="inproj",
    )(x, w, b)


def _gelu_tanh(x):
    c = math.sqrt(2.0 / math.pi)
    return 0.5 * x * (1.0 + jnp.tanh(c * (x + 0.044715 * (x * x * x))))


def _s5_kernel(u_ref, bm_ref, cre_ref, cim_ref, are_ref, aim_ref, d_ref, wglu_ref,
               o_ref, sre_ref, sim_ref, st_re_ref, st_im_ref, y_ref, *, steps, batch):
    kw = SSM_WIDTH // S5_KB
    sw = SSM_GROUPS * SSM_STATE // S5_KB

    @pl.when(pl.program_id(0) == 0)
    def _():
        st_re_ref[...] = jnp.zeros_like(st_re_ref)
        st_im_ref[...] = jnp.zeros_like(st_im_ref)

    for k in range(S5_KB):
        ls = slice(k * sw, (k + 1) * sw)
        bu = jnp.dot(u_ref[:, k * kw:(k + 1) * kw], bm_ref[k], preferred_element_type=F32)
        sre_ref[:, ls] = bu[:, :sw]
        sim_ref[:, ls] = bu[:, sw:]

        ar = jnp.broadcast_to(are_ref[:, ls], (batch, sw))
        ai = jnp.broadcast_to(aim_ref[:, ls], (batch, sw))

        def step(t, carry):
            sr, si = carry
            rows = pl.ds(pl.multiple_of(t * batch, batch), batch)
            nr = ar * sr - ai * si + sre_ref[rows, ls]
            ni = ar * si + ai * sr + sim_ref[rows, ls]
            sre_ref[rows, ls] = nr
            sim_ref[rows, ls] = ni
            return nr, ni

        sr, si = lax.fori_loop(0, steps, step, (st_re_ref[:, ls], st_im_ref[:, ls]))
        st_re_ref[:, ls] = sr
        st_im_ref[:, ls] = si

        y_ref[:, k * kw:(k + 1) * kw] = (
            jnp.dot(sre_ref[:, ls].astype(BF16), cre_ref[k], preferred_element_type=F32)
            + jnp.dot(sim_ref[:, ls].astype(BF16), cim_ref[k], preferred_element_type=F32))

    y = y_ref[...] + d_ref[...] * u_ref[...].astype(F32)
    y = _gelu_tanh(y)
    z = jnp.dot(y.astype(BF16), wglu_ref[...], preferred_element_type=F32)
    o_ref[...] = (y * jax.nn.sigmoid(z)).astype(BF16)


def _s5(proj, bm, cre, cim, a_re, a_im, d_skip, w_glu, batch, steps=S5_T):
    n = proj.shape[0]
    seq = n // batch
    steps = min(steps, seq)
    rows = steps * batch
    nstate = SSM_GROUPS * SSM_STATE
    const = lambda shape: pl.BlockSpec(shape, lambda i: (0,) * len(shape))
    return pl.pallas_call(
        functools.partial(_s5_kernel, steps=steps, batch=batch),
        grid=(seq // steps,),
        in_specs=[pl.BlockSpec((rows, SSM_WIDTH), lambda i: (i, 0)),
                  const(bm.shape), const(cre.shape), const(cim.shape),
                  const(a_re.shape), const(a_im.shape), const(d_skip.shape), const(w_glu.shape)],
        out_specs=pl.BlockSpec((rows, SSM_WIDTH), lambda i: (i, 0)),
        out_shape=jax.ShapeDtypeStruct((n, SSM_WIDTH), BF16),
        scratch_shapes=[pltpu.VMEM((rows, nstate), F32), pltpu.VMEM((rows, nstate), F32),
                        pltpu.VMEM((batch, nstate), F32), pltpu.VMEM((batch, nstate), F32),
                        pltpu.VMEM((rows, SSM_WIDTH), F32)],
        compiler_params=_cparams(("arbitrary",)),
        name="s5",
    )(proj, bm, cre, cim, a_re, a_im, d_skip, w_glu)


def _s5_params(lam_re, lam_im, log_dt, b_re, b_im, c_re, c_im, d_skip):
    lam = lax.complex(lam_re.astype(F32), lam_im.astype(F32))
    dt = jnp.exp(log_dt.astype(F32))[:, None]
    lam_bar = jnp.exp(lam * dt)
    b_bar = ((lam_bar - 1.0) / lam)[:, :, None] * lax.complex(b_re.astype(F32), b_im.astype(F32))
    gl = SSM_GROUPS // S5_KB
    eye = jnp.eye(gl, dtype=F32)

    def in_blocks(b):
        b = b.reshape(S5_KB, gl, SSM_STATE, SSM_GROUP)
        m = jnp.einsum('kgpc,gh->kgchp', b, eye)
        return m.reshape(S5_KB, gl * SSM_GROUP, gl * SSM_STATE)

    def out_blocks(c):
        c = c.reshape(S5_KB, gl, SSM_GROUP, SSM_STATE)
        m = jnp.einsum('kgcp,gh->kgphc', c, eye)
        return m.reshape(S5_KB, gl * SSM_STATE, gl * SSM_GROUP)

    bm = jnp.concatenate([in_blocks(jnp.real(b_bar)), in_blocks(jnp.imag(b_bar))], axis=-1).astype(BF16)
    cre = out_blocks(c_re.astype(F32)).astype(BF16)
    cim = out_blocks(-c_im.astype(F32)).astype(BF16)
    a_re = jnp.real(lam_bar).reshape(1, -1)
    a_im = jnp.imag(lam_bar).reshape(1, -1)
    return bm, cre, cim, a_re, a_im, d_skip.astype(F32).reshape(1, -1)


def _attn_kernel(q_ref, kp_ref, kc_ref, vp_ref, vc_ref, o_ref, lse_ref, *, steps):
    jb = pl.program_id(1)
    q = q_ref[...]
    k2 = jnp.concatenate([kp_ref[...], kc_ref[...]], axis=0)
    v2 = jnp.concatenate([vp_ref[...], vc_ref[...]], axis=0)
    qi = lax.broadcasted_iota(jnp.int32, (QBLK, 2 * QBLK), 0)
    kj = lax.broadcasted_iota(jnp.int32, (QBLK, 2 * QBLK), 1)
    dist = qi + QBLK - kj
    allowed = (dist >= 0) & (dist <= steps) & ((kj >= QBLK) | (jb > 0))
    outs, lses = [], []
    for h in range(HEADS):
        hs = slice(h * HEAD_DIM, (h + 1) * HEAD_DIM)
        s = lax.dot_general(q[:, hs], k2[:, hs], (((1,), (1,)), ((), ())),
                            preferred_element_type=F32) * (HEAD_DIM ** -0.5)
        s = jnp.where(allowed, s, NEG)
        m = jnp.max(s, axis=-1, keepdims=True)
        p = jnp.exp(s - m)
        denom = jnp.sum(p, axis=-1, keepdims=True)
        o = jnp.dot(p.astype(BF16), v2[:, hs], preferred_element_type=F32) / denom
        outs.append(o)
        lses.append(m + jnp.log(denom))
    o_ref[...] = jnp.concatenate(outs, axis=-1).astype(BF16)
    lse_ref[...] = jnp.concatenate(lses, axis=-1)


def _attn(q, k, v, steps):
    classes, n, width = q.shape
    nb = n // QBLK
    cur = pl.BlockSpec((None, QBLK, width), lambda c, j: (c, j, 0))
    prev = pl.BlockSpec((None, QBLK, width), lambda c, j: (c, jnp.maximum(j - 1, 0), 0))
    return pl.pallas_call(
        functools.partial(_attn_kernel, steps=steps),
        grid=(classes, nb),
        in_specs=[cur, prev, cur, prev, cur],
        out_specs=[cur, pl.BlockSpec((None, QBLK, HEADS), lambda c, j: (c, j, 0))],
        out_shape=[jax.ShapeDtypeStruct((classes, n, width), BF16),
                   jax.ShapeDtypeStruct((classes, n, HEADS), F32)],
        compiler_params=_cparams(("arbitrary", "arbitrary")),
        name="attn",
    )(q, k, k, v, v)


def _layer_norm(z, g, b):
    mu = jnp.mean(z, axis=-1, keepdims=True)
    zc = z - mu
    var = jnp.mean(zc * zc, axis=-1, keepdims=True)
    return zc * lax.rsqrt(var + LN_EPS) * g + b


def _store_rows(ref, y):
    rows, d = y.shape
    sub = d // LANES
    for s in range(sub):
        ref[pl.ds(s, rows, stride=sub), :] = y[:, s * LANES:(s + 1) * LANES]


def _load_rows(ref, rows, sub):
    return jnp.concatenate([ref[pl.ds(s, rows, stride=sub), :] for s in range(sub)], axis=-1)


def _expand_heads(w, e_ref):
    hi = w.astype(BF16)
    lo = (w - hi.astype(F32)).astype(BF16)
    return (jnp.dot(hi, e_ref[...], preferred_element_type=F32)
            + jnp.dot(lo, e_ref[...], preferred_element_type=F32))


def _merge_kernel(x_ref, yssm_ref, o0_ref, o1_ref, o2_ref, lse_ref, gs_ref, ga_ref,
                  wbs_ref, wba_ref, wout_ref, e_ref, g_ref, b_ref, o_ref, orow_ref, *, alpha):
    lse = lse_ref[...]
    l0, l1, l2 = lse[:, 0:HEADS], lse[:, HEADS:2 * HEADS], lse[:, 2 * HEADS:3 * HEADS]
    m = jnp.maximum(jnp.maximum(l0, l1), l2)
    e0, e1, e2 = jnp.exp(l0 - m), jnp.exp(l1 - m), jnp.exp(l2 - m)
    den = e0 + e1 + e2
    y_attn = (_expand_heads(e0 / den, e_ref) * o0_ref[...].astype(F32)
              + _expand_heads(e1 / den, e_ref) * o1_ref[...].astype(F32)
              + _expand_heads(e2 / den, e_ref) * o2_ref[...].astype(F32))
    ya = jnp.dot(y_attn.astype(BF16), wba_ref[...], preferred_element_type=F32)
    ys = jnp.dot(yssm_ref[...], wbs_ref[...], preferred_element_type=F32)
    merged = (jax.nn.sigmoid(gs_ref[...].astype(F32)) * ys
              + jax.nn.sigmoid(ga_ref[...].astype(F32)) * ya)
    mix = jnp.dot(merged.astype(BF16), wout_ref[...], preferred_element_type=F32)
    y = _layer_norm(alpha * x_ref[...] + mix, g_ref[...], b_ref[...])
    o_ref[...] = y
    _store_rows(orow_ref, y)


def _merge(x, y_ssm, o0, o1, o2, lse, proj, w_br_ssm, w_br_attn, w_out, expand, ln_g, ln_b,
           alpha, tm=MERGE_TM):
    n, d = x.shape
    tm = min(tm, n)
    gate_blk = proj.shape[1] // d
    row = lambda w: pl.BlockSpec((tm, w), lambda i: (i, 0))
    const = lambda a: pl.BlockSpec(a.shape, lambda i: (0,) * a.ndim)
    return pl.pallas_call(
        functools.partial(_merge_kernel, alpha=alpha),
        grid=(n // tm,),
        in_specs=[row(d), row(SSM_WIDTH), row(ATTN_WIDTH), row(ATTN_WIDTH), row(ATTN_WIDTH),
                  row(lse.shape[1]),
                  pl.BlockSpec((tm, d), lambda i: (i, gate_blk - 2)),
                  pl.BlockSpec((tm, d), lambda i: (i, gate_blk - 1)),
                  const(w_br_ssm), const(w_br_attn), const(w_out), const(expand),
                  const(ln_g), const(ln_b)],
        out_specs=[row(d), pl.BlockSpec((tm * (d // LANES), LANES), lambda i: (i, 0))],
        out_shape=[jax.ShapeDtypeStruct((n, d), F32),
                   jax.ShapeDtypeStruct((n * (d // LANES), LANES), F32)],
        compiler_params=_cparams(("arbitrary",)),
        name="merge",
    )(x, y_ssm, o0, o1, o2, lse, proj, proj, w_br_ssm, w_br_attn, w_out, expand, ln_g, ln_b)


def _first_argmax(v, iota, size, axis):
    m = jnp.max(v, axis=axis, keepdims=True)
    idx = jnp.min(jnp.where(v == m, iota, size), axis=axis, keepdims=True)
    return m, idx


def _route_kernel(x_ref, rwh_ref, rwl_ref, bias_ref, eg_ref, tri_ref,
                  eidx_ref, gate_ref, rank_ref, cnt_ref, carry_ref):
    @pl.when(pl.program_id(0) == 0)
    def _():
        carry_ref[...] = jnp.zeros_like(carry_ref)

    x = x_ref[...]
    tm = x.shape[0]
    xh = x.astype(BF16)
    xl = (x - xh.astype(F32)).astype(BF16)
    nt = (((1,), (1,)), ((), ()))
    logits = (lax.dot_general(rwh_ref[...], xh, nt, preferred_element_type=F32)
              + lax.dot_general(rwh_ref[...], xl, nt, preferred_element_type=F32)
              + lax.dot_general(rwl_ref[...], xh, nt, preferred_element_type=F32))
    scores = jax.nn.sigmoid(logits)
    sel = scores + bias_ref[...]

    sel3 = sel.reshape(N_EXPERT_GROUPS, GROUP_SIZE, tm)
    iw = lax.broadcasted_iota(jnp.int32, sel3.shape, 1)
    m1, i1 = _first_argmax(sel3, iw, GROUP_SIZE, 1)
    m2 = jnp.max(jnp.where(iw == i1, NEG, sel3), axis=1, keepdims=True)
    gs = (m1 + m2).reshape(N_EXPERT_GROUPS, tm)

    ig = lax.broadcasted_iota(jnp.int32, gs.shape, 0)
    gmask = jnp.zeros(gs.shape, F32)
    for _ in range(TOPK_GROUPS):
        _, gi = _first_argmax(gs, ig, N_EXPERT_GROUPS, 0)
        hit = ig == gi
        gmask = jnp.where(hit, 1.0, gmask)
        gs = jnp.where(hit, NEG, gs)
    emask = jnp.dot(eg_ref[...], gmask.astype(BF16), preferred_element_type=F32) > 0.5

    masked = jnp.where(emask, sel, NEG)
    ie = lax.broadcasted_iota(jnp.int32, masked.shape, 0)
    chosen = jnp.zeros(masked.shape, F32)
    idxs, vals = [], []
    for _ in range(TOP_K):
        _, ei = _first_argmax(masked, ie, N_EXPERTS, 0)
        hit = ie == ei
        idxs.append(ei)
        vals.append(jnp.sum(jnp.where(hit, scores, 0.0), axis=0, keepdims=True))
        chosen = jnp.where(hit, 1.0, chosen)
        masked = jnp.where(hit, NEG, masked)
    total = vals[0]
    for v in vals[1:]:
        total = total + v

    prefix = jnp.dot(chosen.astype(BF16), tri_ref[...], preferred_element_type=F32)
    pos = carry_ref[:, 0:1] + prefix - 1.0
    ranks = [jnp.sum(jnp.where(ie == ei, pos, 0.0), axis=0, keepdims=True) for ei in idxs]
    carry = carry_ref[...] + jnp.sum(chosen, axis=1, keepdims=True)
    carry_ref[...] = carry
    cnt_ref[...] = carry

    eidx_ref[...] = jnp.concatenate(idxs, axis=0)
    gate_ref[...] = jnp.concatenate([v / total * ROUTED_SCALE for v in vals], axis=0)
    rank_ref[...] = jnp.concatenate(ranks, axis=0).astype(jnp.int32)


def _route(x, rw_hi, rw_lo, bias, eg, tri, tm=ROUTE_TM):
    n, d = x.shape
    tm = min(tm, n)
    const = lambda a: pl.BlockSpec(a.shape, lambda i: (0,) * a.ndim)
    col = pl.BlockSpec((TOP_K, tm), lambda i: (0, i))
    return pl.pallas_call(
        _route_kernel,
        grid=(n // tm,),
        in_specs=[pl.BlockSpec((tm, d), lambda i: (i, 0)),
                  const(rw_hi), const(rw_lo), const(bias), const(eg), const(tri)],
        out_specs=[col, col, col, pl.BlockSpec((N_EXPERTS, LANES), lambda i: (0, 0))],
        out_shape=[jax.ShapeDtypeStruct((TOP_K, n), jnp.int32),
                   jax.ShapeDtypeStruct((TOP_K, n), F32),
                   jax.ShapeDtypeStruct((TOP_K, n), jnp.int32),
                   jax.ShapeDtypeStruct((N_EXPERTS, LANES), F32)],
        scratch_shapes=[pltpu.VMEM((N_EXPERTS, LANES), F32)],
        compiler_params=_cparams(("arbitrary",)),
        name="route",
    )(x, rw_hi, rw_lo, bias, eg, tri)


def _gather_kernel(idx_ref, src_ref, o_ref, sem, *, rows):
    def issue(j, _):
        pltpu.make_async_copy(src_ref.at[idx_ref[0, 0, j]], o_ref.at[j], sem).start()
        return 0

    lax.fori_loop(0, rows, issue, 0)

    def drain(j, _):
        pltpu.make_async_copy(src_ref.at[0], o_ref.at[j], sem).wait()
        return 0

    lax.fori_loop(0, rows, drain, 0)


def _gather_rows(src, idx, sub, rows=GATHER_G):
    s = src.shape[0] // sub
    m = idx.shape[0]
    rows = min(rows, m)
    out = pl.pallas_call(
        functools.partial(_gather_kernel, rows=rows),
        grid=(m // rows,),
        in_specs=[pl.BlockSpec((1, 1, rows), lambda i: (i, 0, 0), memory_space=pltpu.SMEM),
                  pl.BlockSpec(memory_space=pl.ANY)],
        out_specs=pl.BlockSpec((rows, sub, LANES), lambda i: (i, 0, 0)),
        out_shape=jax.ShapeDtypeStruct((m, sub, LANES), src.dtype),
        scratch_shapes=[pltpu.SemaphoreType.DMA],
        compiler_params=_cparams(("arbitrary",)),
        name="gather_rows",
    )(idx.reshape(m // rows, 1, rows), src.reshape(s, sub, LANES))
    return out.reshape(m * sub, LANES)


def _gmm_kernel(be_ref, nu_ref, x_ref, wgu_ref, wd_ref, o_ref, *, hidden, blk, sub):
    @pl.when(pl.program_id(0) < nu_ref[0])
    def _():
        x = _load_rows(x_ref, blk, sub).astype(BF16)
        gu = jnp.dot(x, wgu_ref[...], preferred_element_type=F32)
        g, u = gu[:, :hidden], gu[:, hidden:]
        h = (g * jax.nn.sigmoid(g) * u).astype(BF16)
        _store_rows(o_ref, jnp.dot(h, wd_ref[...], preferred_element_type=F32))


def _gmm(xs, w_gu, w_down, blk_e, n_used, blk=MOE_BLK):
    hidden, d = w_down.shape[1:]
    sub = d // LANES
    n_rows = xs.shape[0] // sub
    row = lambda b, be, nu: (jnp.minimum(b, nu[0] - 1), 0)
    return pl.pallas_call(
        functools.partial(_gmm_kernel, hidden=hidden, blk=blk, sub=sub),
        grid_spec=pltpu.PrefetchScalarGridSpec(
            num_scalar_prefetch=2,
            grid=(n_rows // blk,),
            in_specs=[pl.BlockSpec((blk * sub, LANES), row),
                      pl.BlockSpec((None, d, 2 * hidden), lambda b, be, nu: (be[b], 0, 0)),
                      pl.BlockSpec((None, hidden, d), lambda b, be, nu: (be[b], 0, 0))],
            out_specs=pl.BlockSpec((blk * sub, LANES), row)),
        out_shape=jax.ShapeDtypeStruct((n_rows * sub, LANES), F32),
        compiler_params=_cparams(("arbitrary",)),
        name="expert_gmm",
    )(blk_e, n_used, xs, w_gu, w_down)


def _combine_kernel(x_ref, yg_ref, gate_ref, wgu_ref, wd_ref, g_ref, b_ref, o_ref, *, alpha, hidden):
    x = x_ref[...]
    gu = jnp.dot(x.astype(BF16), wgu_ref[...], preferred_element_type=F32)
    g, u = gu[:, :hidden], gu[:, hidden:]
    h = (g * jax.nn.sigmoid(g) * u).astype(BF16)
    acc = jnp.dot(h, wd_ref[...], preferred_element_type=F32)
    gates = gate_ref[...]
    tm, d = x.shape
    for k in range(TOP_K):
        acc = acc + gates[:, k:k + 1] * _load_rows(yg_ref.at[k], tm, d // LANES)
    o_ref[...] = _layer_norm(alpha * x + acc, g_ref[...], b_ref[...])


def _combine(x, yg, gates_t, w_gu, w_down, ln_g, ln_b, alpha, tm=COMBINE_TM):
    n, d = x.shape
    tm = min(tm, n)
    sub = d // LANES
    hidden = w_down.shape[0]
    const = lambda a: pl.BlockSpec(a.shape, lambda i: (0,) * a.ndim)
    return pl.pallas_call(
        functools.partial(_combine_kernel, alpha=alpha, hidden=hidden),
        grid=(n // tm,),
        in_specs=[pl.BlockSpec((tm, d), lambda i: (i, 0)),
                  pl.BlockSpec((TOP_K, tm * sub, LANES), lambda i: (0, i, 0)),
                  pl.BlockSpec((tm, TOP_K), lambda i: (i, 0)),
                  const(w_gu), const(w_down), const(ln_g), const(ln_b)],
        out_specs=pl.BlockSpec((tm, d), lambda i: (i, 0)),
        out_shape=jax.ShapeDtypeStruct((n, d), F32),
        compiler_params=_cparams(("arbitrary",)),
        name="combine",
    )(x, yg, gates_t, w_gu, w_down, ln_g, ln_b)


def _token_mixer(x, batch, w_in, b_in, s5p, w_glu, w_br_ssm, w_br_attn, w_out, expand, ln_g, ln_b, alpha):
    n, d = x.shape
    seq = n // batch
    proj = _inproj(x, w_in, b_in)
    y_ssm = _s5(proj, *s5p, w_glu, batch)

    qkv0 = SSM_WIDTH
    n_grp = len(ATTN_PATTERNS)
    qkv_w = n_grp * ATTN_WIDTH
    outs, lses = [], []
    for g, (window, dil) in enumerate(ATTN_PATTERNS):
        assert window // dil == QBLK and (seq // dil) % QBLK == 0

        def classes(col0):
            t = proj[:, col0 + g * ATTN_WIDTH: col0 + (g + 1) * ATTN_WIDTH]
            t = t.reshape(seq // dil, dil * batch, ATTN_WIDTH)
            return jnp.swapaxes(t, 0, 1)

        o_g, lse_g = _attn(classes(qkv0), classes(qkv0 + qkv_w), classes(qkv0 + 2 * qkv_w),
                           window // dil)
        outs.append(jnp.swapaxes(o_g, 0, 1).reshape(n, ATTN_WIDTH))
        lses.append(jnp.swapaxes(lse_g, 0, 1).reshape(n, HEADS))
    lse = jnp.concatenate(lses, axis=-1)
    return _merge(x, y_ssm, outs[0], outs[1], outs[2], lse, proj, w_br_ssm, w_br_attn, w_out,
                  expand, ln_g, ln_b, alpha)


def _moe(x, x_rows, rw_hi, rw_lo, bias, eg, tri, w_gu, w_down, sh_gu, sh_down, ln_g, ln_b, alpha):
    n, d = x.shape
    sub = d // LANES
    eidx, gates, rank, cnt = _route(x, rw_hi, rw_lo, bias, eg, tri)
    counts = cnt[:, 0].astype(jnp.int32)
    nk = n * TOP_K
    n_rows = nk + N_EXPERTS * MOE_BLK
    n_blocks = n_rows // MOE_BLK
    pad_counts = (counts + MOE_BLK - 1) // MOE_BLK * MOE_BLK
    pad_end = jnp.cumsum(pad_counts)
    pad_start = pad_end - pad_counts
    onehot = eidx[:, :, None] == jnp.arange(N_EXPERTS, dtype=jnp.int32)
    dest = (jnp.sum(jnp.where(onehot, pad_start, 0), axis=-1) + rank).reshape(nk)
    tok = jnp.broadcast_to(jnp.arange(n, dtype=jnp.int32)[None, :], (TOP_K, n)).reshape(nk)
    row_tok = jnp.zeros((n_rows,), jnp.int32).at[dest].set(tok)
    blk_first = jnp.arange(n_blocks, dtype=jnp.int32)[:, None] * MOE_BLK
    blk_e = jnp.minimum(jnp.sum((blk_first >= pad_end[None, :]).astype(jnp.int32), axis=1), N_EXPERTS - 1)
    n_used = (pad_end[-1:] // MOE_BLK).astype(jnp.int32)

    xs = _gather_rows(x_rows, row_tok, sub)
    ys = _gmm(xs, w_gu, w_down, blk_e, n_used)
    yg = _gather_rows(ys, dest, sub).reshape(TOP_K, n * sub, LANES)
    return _combine(x, yg, gates.T, sh_gu, sh_down, ln_g, ln_b, alpha)


def kernel(x, w_in, b_in, ssm_lam_re, ssm_lam_im, ssm_log_dt, ssm_b_re, ssm_b_im, ssm_c_re, ssm_c_im, ssm_d, w_glu, w_br_ssm, w_br_attn, w_out, ln1_g, ln1_b, router_w, router_bias, exp_w_gate, exp_w_up, exp_w_down, sh_w_gate, sh_w_up, sh_w_down, ln2_g, ln2_b):
    batch, seq, d = x.shape
    depth = w_in.shape[0]
    assert batch == SUBLANES
    alpha = (2 * depth) ** 0.25
    n = batch * seq
    xt = jnp.swapaxes(x, 0, 1).reshape(n, d)

    expand = jnp.repeat(jnp.eye(HEADS, dtype=BF16), HEAD_DIM, axis=1)
    eg = jnp.repeat(jnp.eye(N_EXPERT_GROUPS, dtype=BF16), GROUP_SIZE, axis=0)
    tm = min(ROUTE_TM, n)
    tri = (jnp.arange(tm)[:, None] <= jnp.arange(tm)[None, :]).astype(BF16)
    row = lambda a: a.astype(F32).reshape(1, -1)

    for l in range(depth):
        s5p = _s5_params(ssm_lam_re[l], ssm_lam_im[l], ssm_log_dt[l], ssm_b_re[l], ssm_b_im[l],
                         ssm_c_re[l], ssm_c_im[l], ssm_d[l])
        xt, xt_rows = _token_mixer(xt, batch, w_in[l].astype(BF16), row(b_in[l]), s5p,
                                   w_glu[l].astype(BF16), w_br_ssm[l].astype(BF16),
                                   w_br_attn[l].astype(BF16), w_out[l].astype(BF16),
                                   expand, row(ln1_g[l]), row(ln1_b[l]), alpha)
        rwt = router_w[l].astype(F32).T
        rw_hi = rwt.astype(BF16)
        rw_lo = (rwt - rw_hi.astype(F32)).astype(BF16)
        w_gu = jnp.concatenate([exp_w_gate[l], exp_w_up[l]], axis=-1).astype(BF16)
        sh_gu = jnp.concatenate([sh_w_gate[l], sh_w_up[l]], axis=-1).astype(BF16)
        xt = _moe(xt, xt_rows, rw_hi, rw_lo, router_bias[l].astype(F32).reshape(-1, 1), eg, tri,
                  w_gu, exp_w_down[l].astype(BF16), sh_gu, sh_w_down[l].astype(BF16),
                  row(ln2_g[l]), row(ln2_b[l]), alpha)
    return jnp.swapaxes(xt.reshape(seq, batch, d), 0, 1)
```

```python
import functools
import math

import jax
import jax.numpy as jnp
from jax import lax
from jax.experimental import pallas as pl
from jax.experimental.pallas import tpu as pltpu
from jax.experimental.pallas import tpu_sc as plsc

F32 = jnp.float32
BF16 = jnp.bfloat16

SSM_GROUP = 16
SSM_GROUPS = 32
SSM_WIDTH = SSM_GROUP * SSM_GROUPS
SSM_STATE = 64
HEAD_DIM = 64
HEADS = 8
ATTN_PATTERNS = ((128, 1), (512, 4), (2048, 16))
ATTN_WIDTH = HEADS * HEAD_DIM
QBLK = 128
N_EXPERTS = 64
TOP_K = 8
N_EXPERT_GROUPS = 8
GROUP_SIZE = N_EXPERTS // N_EXPERT_GROUPS
TOPK_GROUPS = 4
ROUTED_SCALE = 2.5
LN_EPS = 1e-5
NEG = -1e30

LANES = 128
SUBLANES = 8
VMEM_LIMIT = 56 * 1024 * 1024

INPROJ_TM = 1024
INPROJ_TN = 1024
S5_T = 128
S5_KB = 4
MERGE_TM = 512
ROUTE_TM = 512
MOE_BLK = 512
GATHER_G = 2048
SC_CHUNK = 64
COMBINE_TM = 256


def _cparams(sem):
    return pltpu.CompilerParams(dimension_semantics=sem, vmem_limit_bytes=VMEM_LIMIT)


def _inproj_kernel(x_ref, w_ref, b_ref, o_ref, xb_ref):
    @pl.when(pl.program_id(1) == 0)
    def _():
        xb_ref[...] = x_ref[...].astype(BF16)

    acc = jnp.dot(xb_ref[...], w_ref[...], preferred_element_type=F32)
    o_ref[...] = (acc + b_ref[...]).astype(BF16)


def _inproj(x, w, b, tm=INPROJ_TM, tn=INPROJ_TN):
    n, d = x.shape
    width = w.shape[1]
    tm = min(tm, n)
    return pl.pallas_call(
        _inproj_kernel,
        grid=(n // tm, width // tn),
        in_specs=[pl.BlockSpec((tm, d), lambda i, j: (i, 0)),
                  pl.BlockSpec((d, tn), lambda i, j: (0, j)),
                  pl.BlockSpec((1, tn), lambda i, j: (0, j))],
        out_specs=pl.BlockSpec((tm, tn), lambda i, j: (i, j)),
        out_shape=jax.ShapeDtypeStruct((n, width), BF16),
        scratch_shapes=[pltpu.VMEM((tm, d), BF16)],
        compiler_params=_cparams(("arbitrary", "arbitrary")),
        name="inproj",
    )(x, w, b)


def _gelu_tanh(x):
    c = math.sqrt(2.0 / math.pi)
    return 0.5 * x * (1.0 + jnp.tanh(c * (x + 0.044715 * (x * x * x))))


def _s5_kernel(u_ref, bm_ref, cre_ref, cim_ref, are_ref, aim_ref, d_ref, wglu_ref,
               o_ref, sre_ref, sim_ref, st_re_ref, st_im_ref, y_ref, *, steps, batch):
    kw = SSM_WIDTH // S5_KB
    sw = SSM_GROUPS * SSM_STATE // S5_KB

    @pl.when(pl.program_id(0) == 0)
    def _():
        st_re_ref[...] = jnp.zeros_like(st_re_ref)
        st_im_ref[...] = jnp.zeros_like(st_im_ref)

    for k in range(S5_KB):
        ls = slice(k * sw, (k + 1) * sw)
        bu = jnp.dot(u_ref[:, k * kw:(k + 1) * kw], bm_ref[k], preferred_element_type=F32)
        sre_ref[:, ls] = bu[:, :sw]
        sim_ref[:, ls] = bu[:, sw:]

        ar = jnp.broadcast_to(are_ref[:, ls], (batch, sw))
        ai = jnp.broadcast_to(aim_ref[:, ls], (batch, sw))

        def step(t, carry):
            sr, si = carry
            rows = pl.ds(pl.multiple_of(t * batch, batch), batch)
            nr = ar * sr - ai * si + sre_ref[rows, ls]
            ni = ar * si + ai * sr + sim_ref[rows, ls]
            sre_ref[rows, ls] = nr
            sim_ref[rows, ls] = ni
            return nr, ni

        sr, si = lax.fori_loop(0, steps, step, (st_re_ref[:, ls], st_im_ref[:, ls]))
        st_re_ref[:, ls] = sr
        st_im_ref[:, ls] = si

        y_ref[:, k * kw:(k + 1) * kw] = (
            jnp.dot(sre_ref[:, ls].astype(BF16), cre_ref[k], preferred_element_type=F32)
            + jnp.dot(sim_ref[:, ls].astype(BF16), cim_ref[k], preferred_element_type=F32))

    y = y_ref[...] + d_ref[...] * u_ref[...].astype(F32)
    y = _gelu_tanh(y)
    z = jnp.dot(y.astype(BF16), wglu_ref[...], preferred_element_type=F32)
    o_ref[...] = (y * jax.nn.sigmoid(z)).astype(BF16)


def _s5(proj, bm, cre, cim, a_re, a_im, d_skip, w_glu, batch, steps=S5_T):
    n = proj.shape[0]
    seq = n // batch
    steps = min(steps, seq)
    rows = steps * batch
    nstate = SSM_GROUPS * SSM_STATE
    const = lambda shape: pl.BlockSpec(shape, lambda i: (0,) * len(shape))
    return pl.pallas_call(
        functools.partial(_s5_kernel, steps=steps, batch=batch),
        grid=(seq // steps,),
        in_specs=[pl.BlockSpec((rows, SSM_WIDTH), lambda i: (i, 0)),
                  const(bm.shape), const(cre.shape), const(cim.shape),
                  const(a_re.shape), const(a_im.shape), const(d_skip.shape), const(w_glu.shape)],
        out_specs=pl.BlockSpec((rows, SSM_WIDTH), lambda i: (i, 0)),
        out_shape=jax.ShapeDtypeStruct((n, SSM_WIDTH), BF16),
        scratch_shapes=[pltpu.VMEM((rows, nstate), F32), pltpu.VMEM((rows, nstate), F32),
                        pltpu.VMEM((batch, nstate), F32), pltpu.VMEM((batch, nstate), F32),
                        pltpu.VMEM((rows, SSM_WIDTH), F32)],
        compiler_params=_cparams(("arbitrary",)),
        name="s5",
    )(proj, bm, cre, cim, a_re, a_im, d_skip, w_glu)


def _s5_params(lam_re, lam_im, log_dt, b_re, b_im, c_re, c_im, d_skip):
    lam = lax.complex(lam_re.astype(F32), lam_im.astype(F32))
    dt = jnp.exp(log_dt.astype(F32))[:, None]
    lam_bar = jnp.exp(lam * dt)
    b_bar = ((lam_bar - 1.0) / lam)[:, :, None] * lax.complex(b_re.astype(F32), b_im.astype(F32))
    gl = SSM_GROUPS // S5_KB
    eye = jnp.eye(gl, dtype=F32)

    def in_blocks(b):
        b = b.reshape(S5_KB, gl, SSM_STATE, SSM_GROUP)
        m = jnp.einsum('kgpc,gh->kgchp', b, eye)
        return m.reshape(S5_KB, gl * SSM_GROUP, gl * SSM_STATE)

    def out_blocks(c):
        c = c.reshape(S5_KB, gl, SSM_GROUP, SSM_STATE)
        m = jnp.einsum('kgcp,gh->kgphc', c, eye)
        return m.reshape(S5_KB, gl * SSM_STATE, gl * SSM_GROUP)

    bm = jnp.concatenate([in_blocks(jnp.real(b_bar)), in_blocks(jnp.imag(b_bar))], axis=-1).astype(BF16)
    cre = out_blocks(c_re.astype(F32)).astype(BF16)
    cim = out_blocks(-c_im.astype(F32)).astype(BF16)
    a_re = jnp.real(lam_bar).reshape(1, -1)
    a_im = jnp.imag(lam_bar).reshape(1, -1)
    return bm, cre, cim, a_re, a_im, d_skip.astype(F32).reshape(1, -1)


def _attn_kernel(q_ref, kp_ref, kc_ref, vp_ref, vc_ref, o_ref, lse_ref, *, steps):
    jb = pl.program_id(1)
    q = q_ref[...]
    k2 = jnp.concatenate([kp_ref[...], kc_ref[...]], axis=0)
    v2 = jnp.concatenate([vp_ref[...], vc_ref[...]], axis=0)
    qi = lax.broadcasted_iota(jnp.int32, (QBLK, 2 * QBLK), 0)
    kj = lax.broadcasted_iota(jnp.int32, (QBLK, 2 * QBLK), 1)
    dist = qi + QBLK - kj
    allowed = (dist >= 0) & (dist <= steps) & ((kj >= QBLK) | (jb > 0))
    outs, lses = [], []
    for h in range(HEADS):
        hs = slice(h * HEAD_DIM, (h + 1) * HEAD_DIM)
        s = lax.dot_general(q[:, hs], k2[:, hs], (((1,), (1,)), ((), ())),
                            preferred_element_type=F32) * (HEAD_DIM ** -0.5)
        s = jnp.where(allowed, s, NEG)
        m = jnp.max(s, axis=-1, keepdims=True)
        p = jnp.exp(s - m)
        denom = jnp.sum(p, axis=-1, keepdims=True)
        o = jnp.dot(p.astype(BF16), v2[:, hs], preferred_element_type=F32) / denom
        outs.append(o)
        lses.append(m + jnp.log(denom))
    o_ref[...] = jnp.concatenate(outs, axis=-1).astype(BF16)
    lse_ref[...] = jnp.concatenate(lses, axis=-1)


def _attn(q, k, v, steps):
    classes, n, width = q.shape
    nb = n // QBLK
    cur = pl.BlockSpec((None, QBLK, width), lambda c, j: (c, j, 0))
    prev = pl.BlockSpec((None, QBLK, width), lambda c, j: (c, jnp.maximum(j - 1, 0), 0))
    return pl.pallas_call(
        functools.partial(_attn_kernel, steps=steps),
        grid=(classes, nb),
        in_specs=[cur, prev, cur, prev, cur],
        out_specs=[cur, pl.BlockSpec((None, QBLK, HEADS), lambda c, j: (c, j, 0))],
        out_shape=[jax.ShapeDtypeStruct((classes, n, width), BF16),
                   jax.ShapeDtypeStruct((classes, n, HEADS), F32)],
        compiler_params=_cparams(("arbitrary", "arbitrary")),
        name="attn",
    )(q, k, k, v, v)


def _layer_norm(z, g, b):
    mu = jnp.mean(z, axis=-1, keepdims=True)
    zc = z - mu
    var = jnp.mean(zc * zc, axis=-1, keepdims=True)
    return zc * lax.rsqrt(var + LN_EPS) * g + b


def _store_rows(ref, y):
    rows, d = y.shape
    sub = d // LANES
    for s in range(sub):
        ref[pl.ds(s, rows, stride=sub), :] = y[:, s * LANES:(s + 1) * LANES]


def _load_rows(ref, rows, sub):
    return jnp.concatenate([ref[pl.ds(s, rows, stride=sub), :] for s in range(sub)], axis=-1)


def _expand_heads(w, e_ref):
    hi = w.astype(BF16)
    lo = (w - hi.astype(F32)).astype(BF16)
    return (jnp.dot(hi, e_ref[...], preferred_element_type=F32)
            + jnp.dot(lo, e_ref[...], preferred_element_type=F32))


def _merge_kernel(x_ref, yssm_ref, o0_ref, o1_ref, o2_ref, lse_ref, gs_ref, ga_ref,
                  wbs_ref, wba_ref, wout_ref, e_ref, g_ref, b_ref, o_ref, orow_ref, *, alpha):
    lse = lse_ref[...]
    l0, l1, l2 = lse[:, 0:HEADS], lse[:, HEADS:2 * HEADS], lse[:, 2 * HEADS:3 * HEADS]
    m = jnp.maximum(jnp.maximum(l0, l1), l2)
    e0, e1, e2 = jnp.exp(l0 - m), jnp.exp(l1 - m), jnp.exp(l2 - m)
    den = e0 + e1 + e2
    y_attn = (_expand_heads(e0 / den, e_ref) * o0_ref[...].astype(F32)
              + _expand_heads(e1 / den, e_ref) * o1_ref[...].astype(F32)
              + _expand_heads(e2 / den, e_ref) * o2_ref[...].astype(F32))
    ya = jnp.dot(y_attn.astype(BF16), wba_ref[...], preferred_element_type=F32)
    ys = jnp.dot(yssm_ref[...], wbs_ref[...], preferred_element_type=F32)
    merged = (jax.nn.sigmoid(gs_ref[...].astype(F32)) * ys
              + jax.nn.sigmoid(ga_ref[...].astype(F32)) * ya)
    mix = jnp.dot(merged.astype(BF16), wout_ref[...], preferred_element_type=F32)
    y = _layer_norm(alpha * x_ref[...] + mix, g_ref[...], b_ref[...])
    o_ref[...] = y
    _store_rows(orow_ref, y)


def _merge(x, y_ssm, o0, o1, o2, lse, proj, w_br_ssm, w_br_attn, w_out, expand, ln_g, ln_b,
           alpha, tm=MERGE_TM):
    n, d = x.shape
    tm = min(tm, n)
    gate_blk = proj.shape[1] // d
    row = lambda w: pl.BlockSpec((tm, w), lambda i: (i, 0))
    const = lambda a: pl.BlockSpec(a.shape, lambda i: (0,) * a.ndim)
    return pl.pallas_call(
        functools.partial(_merge_kernel, alpha=alpha),
        grid=(n // tm,),
        in_specs=[row(d), row(SSM_WIDTH), row(ATTN_WIDTH), row(ATTN_WIDTH), row(ATTN_WIDTH),
                  row(lse.shape[1]),
                  pl.BlockSpec((tm, d), lambda i: (i, gate_blk - 2)),
                  pl.BlockSpec((tm, d), lambda i: (i, gate_blk - 1)),
                  const(w_br_ssm), const(w_br_attn), const(w_out), const(expand),
                  const(ln_g), const(ln_b)],
        out_specs=[row(d), pl.BlockSpec((tm * (d // LANES), LANES), lambda i: (i, 0))],
        out_shape=[jax.ShapeDtypeStruct((n, d), F32),
                   jax.ShapeDtypeStruct((n * (d // LANES), LANES), F32)],
        compiler_params=_cparams(("arbitrary",)),
        name="merge",
    )(x, y_ssm, o0, o1, o2, lse, proj, proj, w_br_ssm, w_br_attn, w_out, expand, ln_g, ln_b)


def _first_argmax(v, iota, size, axis):
    m = jnp.max(v, axis=axis, keepdims=True)
    idx = jnp.min(jnp.where(v == m, iota, size), axis=axis, keepdims=True)
    return m, idx


def _route_kernel(x_ref, rwh_ref, rwl_ref, bias_ref, eg_ref, tri_ref,
                  eidx_ref, gate_ref, rank_ref, cnt_ref, carry_ref):
    @pl.when(pl.program_id(0) == 0)
    def _():
        carry_ref[...] = jnp.zeros_like(carry_ref)

    x = x_ref[...]
    tm = x.shape[0]
    xh = x.astype(BF16)
    xl = (x - xh.astype(F32)).astype(BF16)
    nt = (((1,), (1,)), ((), ()))
    logits = (lax.dot_general(rwh_ref[...], xh, nt, preferred_element_type=F32)
              + lax.dot_general(rwh_ref[...], xl, nt, preferred_element_type=F32)
              + lax.dot_general(rwl_ref[...], xh, nt, preferred_element_type=F32))
    scores = jax.nn.sigmoid(logits)
    sel = scores + bias_ref[...]

    sel3 = sel.reshape(N_EXPERT_GROUPS, GROUP_SIZE, tm)
    iw = lax.broadcasted_iota(jnp.int32, sel3.shape, 1)
    m1, i1 = _first_argmax(sel3, iw, GROUP_SIZE, 1)
    m2 = jnp.max(jnp.where(iw == i1, NEG, sel3), axis=1, keepdims=True)
    gs = (m1 + m2).reshape(N_EXPERT_GROUPS, tm)

    ig = lax.broadcasted_iota(jnp.int32, gs.shape, 0)
    gmask = jnp.zeros(gs.shape, F32)
    for _ in range(TOPK_GROUPS):
        _, gi = _first_argmax(gs, ig, N_EXPERT_GROUPS, 0)
        hit = ig == gi
        gmask = jnp.where(hit, 1.0, gmask)
        gs = jnp.where(hit, NEG, gs)
    emask = jnp.dot(eg_ref[...], gmask.astype(BF16), preferred_element_type=F32) > 0.5

    masked = jnp.where(emask, sel, NEG)
    ie = lax.broadcasted_iota(jnp.int32, masked.shape, 0)
    chosen = jnp.zeros(masked.shape, F32)
    idxs, vals = [], []
    for _ in range(TOP_K):
        _, ei = _first_argmax(masked, ie, N_EXPERTS, 0)
        hit = ie == ei
        idxs.append(ei)
        vals.append(jnp.sum(jnp.where(hit, scores, 0.0), axis=0, keepdims=True))
        chosen = jnp.where(hit, 1.0, chosen)
        masked = jnp.where(hit, NEG, masked)
    total = vals[0]
    for v in vals[1:]:
        total = total + v

    prefix = jnp.dot(chosen.astype(BF16), tri_ref[...], preferred_element_type=F32)
    pos = carry_ref[:, 0:1] + prefix - 1.0
    ranks = [jnp.sum(jnp.where(ie == ei, pos, 0.0), axis=0, keepdims=True) for ei in idxs]
    carry = carry_ref[...] + jnp.sum(chosen, axis=1, keepdims=True)
    carry_ref[...] = carry
    cnt_ref[...] = carry

    eidx_ref[...] = jnp.concatenate(idxs, axis=0)
    gate_ref[...] = jnp.concatenate([v / total * ROUTED_SCALE for v in vals], axis=0)
    rank_ref[...] = jnp.concatenate(ranks, axis=0).astype(jnp.int32)


def _route(x, rw_hi, rw_lo, bias, eg, tri, tm=ROUTE_TM):
    n, d = x.shape
    tm = min(tm, n)
    const = lambda a: pl.BlockSpec(a.shape, lambda i: (0,) * a.ndim)
    col = pl.BlockSpec((TOP_K, tm), lambda i: (0, i))
    return pl.pallas_call(
        _route_kernel,
        grid=(n // tm,),
        in_specs=[pl.BlockSpec((tm, d), lambda i: (i, 0)),
                  const(rw_hi), const(rw_lo), const(bias), const(eg), const(tri)],
        out_specs=[col, col, col, pl.BlockSpec((N_EXPERTS, LANES), lambda i: (0, 0))],
        out_shape=[jax.ShapeDtypeStruct((TOP_K, n), jnp.int32),
                   jax.ShapeDtypeStruct((TOP_K, n), F32),
                   jax.ShapeDtypeStruct((TOP_K, n), jnp.int32),
                   jax.ShapeDtypeStruct((N_EXPERTS, LANES), F32)],
        scratch_shapes=[pltpu.VMEM((N_EXPERTS, LANES), F32)],
        compiler_params=_cparams(("arbitrary",)),
        name="route",
    )(x, rw_hi, rw_lo, bias, eg, tri)


def _gather_kernel(idx_ref, src_ref, o_ref, sem, *, rows):
    def issue(j, _):
        pltpu.make_async_copy(src_ref.at[idx_ref[0, 0, j]], o_ref.at[j], sem).start()
        return 0

    lax.fori_loop(0, rows, issue, 0)

    def drain(j, _):
        pltpu.make_async_copy(src_ref.at[0], o_ref.at[j], sem).wait()
        return 0

    lax.fori_loop(0, rows, drain, 0)


def _gather_rows(src, idx, sub, rows=GATHER_G):
    s = src.shape[0] // sub
    m = idx.shape[0]
    rows = min(rows, m)
    out = pl.pallas_call(
        functools.partial(_gather_kernel, rows=rows),
        grid=(m // rows,),
        in_specs=[pl.BlockSpec((1, 1, rows), lambda i: (i, 0, 0), memory_space=pltpu.SMEM),
                  pl.BlockSpec(memory_space=pl.ANY)],
        out_specs=pl.BlockSpec((rows, sub, LANES), lambda i: (i, 0, 0)),
        out_shape=jax.ShapeDtypeStruct((m, sub, LANES), src.dtype),
        scratch_shapes=[pltpu.SemaphoreType.DMA],
        compiler_params=_cparams(("arbitrary",)),
        name="gather_rows",
    )(idx.reshape(m // rows, 1, rows), src.reshape(s, sub, LANES))
    return out.reshape(m * sub, LANES)


def _sc_gather_rows(src, idx, sub, chunk=SC_CHUNK):
    s = src.shape[0] // sub
    m = idx.shape[0]
    info = plsc.get_sparse_core_info()
    n_workers = info.num_cores * info.num_subcores
    per_worker = m // n_workers
    n_chunks = per_worker // chunk
    assert n_chunks * chunk * n_workers == m
    mesh = plsc.VectorSubcoreMesh(core_axis_name="c", subcore_axis_name="s")

    @functools.partial(
        pl.kernel, mesh=mesh,
        out_type=jax.ShapeDtypeStruct((m, sub, LANES), src.dtype),
        scratch_types=[pltpu.VMEM((n_chunks, chunk), jnp.int32),
                       pltpu.VMEM((chunk, sub, LANES), src.dtype),
                       pltpu.SemaphoreType.DMA])
    def gather(src_hbm, idx_hbm, out_hbm, idx_v, rows_v, sem):
        wid = lax.axis_index("s") * info.num_cores + lax.axis_index("c")
        pltpu.sync_copy(idx_hbm.at[wid], idx_v)

        @pl.loop(0, n_chunks)
        def _(j):
            pltpu.async_copy(src_hbm.at[idx_v.at[j]], rows_v, sem).wait()
            pltpu.sync_copy(rows_v, out_hbm.at[pl.ds(wid * per_worker + j * chunk, chunk)])

    out = gather(src.reshape(s, sub, LANES), idx.reshape(n_workers, n_chunks, chunk))
    return out.reshape(m * sub, LANES)


def _gmm_kernel(be_ref, nu_ref, x_ref, wgu_ref, wd_ref, o_ref, *, hidden, blk, sub):
    @pl.when(pl.program_id(0) < nu_ref[0])
    def _():
        x = _load_rows(x_ref, blk, sub).astype(BF16)
        gu = jnp.dot(x, wgu_ref[...], preferred_element_type=F32)
        g, u = gu[:, :hidden], gu[:, hidden:]
        h = (g * jax.nn.sigmoid(g) * u).astype(BF16)
        _store_rows(o_ref, jnp.dot(h, wd_ref[...], preferred_element_type=F32))


def _gmm(xs, w_gu, w_down, blk_e, n_used, blk=MOE_BLK):
    hidden, d = w_down.shape[1:]
    sub = d // LANES
    n_rows = xs.shape[0] // sub
    row = lambda b, be, nu: (jnp.minimum(b, nu[0] - 1), 0)
    return pl.pallas_call(
        functools.partial(_gmm_kernel, hidden=hidden, blk=blk, sub=sub),
        grid_spec=pltpu.PrefetchScalarGridSpec(
            num_scalar_prefetch=2,
            grid=(n_rows // blk,),
            in_specs=[pl.BlockSpec((blk * sub, LANES), row),
                      pl.BlockSpec((None, d, 2 * hidden), lambda b, be, nu: (be[b], 0, 0)),
                      pl.BlockSpec((None, hidden, d), lambda b, be, nu: (be[b], 0, 0))],
            out_specs=pl.BlockSpec((blk * sub, LANES), row)),
        out_shape=jax.ShapeDtypeStruct((n_rows * sub, LANES), F32),
        compiler_params=_cparams(("arbitrary",)),
        name="expert_gmm",
    )(blk_e, n_used, xs, w_gu, w_down)


def _combine_kernel(x_ref, yg_ref, gate_ref, wgu_ref, wd_ref, g_ref, b_ref, o_ref, *, alpha, hidden):
    x = x_ref[...]
    gu = jnp.dot(x.astype(BF16), wgu_ref[...], preferred_element_type=F32)
    g, u = gu[:, :hidden], gu[:, hidden:]
    h = (g * jax.nn.sigmoid(g) * u).astype(BF16)
    acc = jnp.dot(h, wd_ref[...], preferred_element_type=F32)
    gates = gate_ref[...]
    tm, d = x.shape
    for k in range(TOP_K):
        acc = acc + gates[:, k:k + 1] * _load_rows(yg_ref.at[k], tm, d // LANES)
    o_ref[...] = _layer_norm(alpha * x + acc, g_ref[...], b_ref[...])


def _combine(x, yg, gates_t, w_gu, w_down, ln_g, ln_b, alpha, tm=COMBINE_TM):
    n, d = x.shape
    tm = min(tm, n)
    sub = d // LANES
    hidden = w_down.shape[0]
    const = lambda a: pl.BlockSpec(a.shape, lambda i: (0,) * a.ndim)
    return pl.pallas_call(
        functools.partial(_combine_kernel, alpha=alpha, hidden=hidden),
        grid=(n // tm,),
        in_specs=[pl.BlockSpec((tm, d), lambda i: (i, 0)),
                  pl.BlockSpec((TOP_K, tm * sub, LANES), lambda i: (0, i, 0)),
                  pl.BlockSpec((tm, TOP_K), lambda i: (i, 0)),
                  const(w_gu), const(w_down), const(ln_g), const(ln_b)],
        out_specs=pl.BlockSpec((tm, d), lambda i: (i, 0)),
        out_shape=jax.ShapeDtypeStruct((n, d), F32),
        compiler_params=_cparams(("arbitrary",)),
        name="combine",
    )(x, yg, gates_t, w_gu, w_down, ln_g, ln_b)


def _token_mixer(x, batch, w_in, b_in, s5p, w_glu, w_br_ssm, w_br_attn, w_out, expand, ln_g, ln_b, alpha):
    n, d = x.shape
    seq = n // batch
    proj = _inproj(x, w_in, b_in)
    y_ssm = _s5(proj, *s5p, w_glu, batch)

    qkv0 = SSM_WIDTH
    n_grp = len(ATTN_PATTERNS)
    qkv_w = n_grp * ATTN_WIDTH
    outs, lses = [], []
    for g, (window, dil) in enumerate(ATTN_PATTERNS):
        assert window // dil == QBLK and (seq // dil) % QBLK == 0

        def classes(col0):
            t = proj[:, col0 + g * ATTN_WIDTH: col0 + (g + 1) * ATTN_WIDTH]
            t = t.reshape(seq // dil, dil * batch, ATTN_WIDTH)
            return jnp.swapaxes(t, 0, 1)

        o_g, lse_g = _attn(classes(qkv0), classes(qkv0 + qkv_w), classes(qkv0 + 2 * qkv_w),
                           window // dil)
        outs.append(jnp.swapaxes(o_g, 0, 1).reshape(n, ATTN_WIDTH))
        lses.append(jnp.swapaxes(lse_g, 0, 1).reshape(n, HEADS))
    lse = jnp.concatenate(lses, axis=-1)
    return _merge(x, y_ssm, outs[0], outs[1], outs[2], lse, proj, w_br_ssm, w_br_attn, w_out,
                  expand, ln_g, ln_b, alpha)


def _moe(x, x_rows, rw_hi, rw_lo, bias, eg, tri, w_gu, w_down, sh_gu, sh_down, ln_g, ln_b, alpha):
    n, d = x.shape
    sub = d // LANES
    eidx, gates, rank, cnt = _route(x, rw_hi, rw_lo, bias, eg, tri)
    counts = cnt[:, 0].astype(jnp.int32)
    nk = n * TOP_K
    n_rows = nk + N_EXPERTS * MOE_BLK
    n_blocks = n_rows // MOE_BLK
    pad_counts = (counts + MOE_BLK - 1) // MOE_BLK * MOE_BLK
    pad_end = jnp.cumsum(pad_counts)
    pad_start = pad_end - pad_counts
    onehot = eidx[:, :, None] == jnp.arange(N_EXPERTS, dtype=jnp.int32)
    dest = (jnp.sum(jnp.where(onehot, pad_start, 0), axis=-1) + rank).reshape(nk)
    tok = jnp.broadcast_to(jnp.arange(n, dtype=jnp.int32)[None, :], (TOP_K, n)).reshape(nk)
    row_tok = jnp.zeros((n_rows,), jnp.int32).at[dest].set(tok)
    blk_first = jnp.arange(n_blocks, dtype=jnp.int32)[:, None] * MOE_BLK
    blk_e = jnp.minimum(jnp.sum((blk_first >= pad_end[None, :]).astype(jnp.int32), axis=1), N_EXPERTS - 1)
    n_used = (pad_end[-1:] // MOE_BLK).astype(jnp.int32)

    xs = _sc_gather_rows(x_rows, row_tok, sub)
    ys = _gmm(xs, w_gu, w_down, blk_e, n_used)
    yg = _sc_gather_rows(ys, dest, sub).reshape(TOP_K, n * sub, LANES)
    return _combine(x, yg, gates.T, sh_gu, sh_down, ln_g, ln_b, alpha)


def kernel(x, w_in, b_in, ssm_lam_re, ssm_lam_im, ssm_log_dt, ssm_b_re, ssm_b_im, ssm_c_re, ssm_c_im, ssm_d, w_glu, w_br_ssm, w_br_attn, w_out, ln1_g, ln1_b, router_w, router_bias, exp_w_gate, exp_w_up, exp_w_down, sh_w_gate, sh_w_up, sh_w_down, ln2_g, ln2_b):
    batch, seq, d = x.shape
    depth = w_in.shape[0]
    assert batch == SUBLANES
    alpha = (2 * depth) ** 0.25
    n = batch * seq
    xt = jnp.swapaxes(x, 0, 1).reshape(n, d)

    expand = jnp.repeat(jnp.eye(HEADS, dtype=BF16), HEAD_DIM, axis=1)
    eg = jnp.repeat(jnp.eye(N_EXPERT_GROUPS, dtype=BF16), GROUP_SIZE, axis=0)
    tm = min(ROUTE_TM, n)
    tri = (jnp.arange(tm)[:, None] <= jnp.arange(tm)[None, :]).astype(BF16)
    row = lambda a: a.astype(F32).reshape(1, -1)

    for l in range(depth):
        s5p = _s5_params(ssm_lam_re[l], ssm_lam_im[l], ssm_log_dt[l], ssm_b_re[l], ssm_b_im[l],
                         ssm_c_re[l], ssm_c_im[l], ssm_d[l])
        xt, xt_rows = _token_mixer(xt, batch, w_in[l].astype(BF16), row(b_in[l]), s5p,
                                   w_glu[l].astype(BF16), w_br_ssm[l].astype(BF16),
                                   w_br_attn[l].astype(BF16), w_out[l].astype(BF16),
                                   expand, row(ln1_g[l]), row(ln1_b[l]), alpha)
        rwt = router_w[l].astype(F32).T
        rw_hi = rwt.astype(BF16)
        rw_lo = (rwt - rw_hi.astype(F32)).astype(BF16)
        w_gu = jnp.concatenate([exp_w_gate[l], exp_w_up[l]], axis=-1).astype(BF16)
        sh_gu = jnp.concatenate([sh_w_gate[l], sh_w_up[l]], axis=-1).astype(BF16)
        xt = _moe(xt, xt_rows, rw_hi, rw_lo, router_bias[l].astype(F32).reshape(-1, 1), eg, tri,
                  w_gu, exp_w_down[l].astype(BF16), sh_gu, sh_w_down[l].astype(BF16),
                  row(ln2_g[l]), row(ln2_b[l]), alpha)
    return jnp.swapaxes(xt.reshape(seq, batch, d), 0, 1)
```

```python
import functools
import math

import jax
import jax.numpy as jnp
from jax import lax
from jax.experimental import pallas as pl
from jax.experimental.pallas import tpu as pltpu
from jax.experimental.pallas import tpu_sc as plsc

F32 = jnp.float32
BF16 = jnp.bfloat16

SSM_GROUP = 16
SSM_GROUPS = 32
SSM_WIDTH = SSM_GROUP * SSM_GROUPS
SSM_STATE = 64
HEAD_DIM = 64
HEADS = 8
ATTN_PATTERNS = ((128, 1), (512, 4), (2048, 16))
ATTN_WIDTH = HEADS * HEAD_DIM
QBLK = 128
N_EXPERTS = 64
TOP_K = 8
N_EXPERT_GROUPS = 8
GROUP_SIZE = N_EXPERTS // N_EXPERT_GROUPS
TOPK_GROUPS = 4
ROUTED_SCALE = 2.5
LN_EPS = 1e-5
NEG = -1e30

LANES = 128
SUBLANES = 8
VMEM_LIMIT = 56 * 1024 * 1024

INPROJ_TM = 1024
INPROJ_TN = 1024
S5_T = 128
S5_KB = 4
MERGE_TM = 512
ROUTE_TM = 512
MOE_BLK = 512
GATHER_G = 2048
SC_CHUNK = 64
COMBINE_TM = 256


def _cparams(sem):
    return pltpu.CompilerParams(dimension_semantics=sem, vmem_limit_bytes=VMEM_LIMIT)


def _inproj_kernel(x_ref, w_ref, b_ref, o_ref, xb_ref):
    @pl.when(pl.program_id(1) == 0)
    def _():
        xb_ref[...] = x_ref[...].astype(BF16)

    acc = jnp.dot(xb_ref[...], w_ref[...], preferred_element_type=F32)
    o_ref[...] = (acc + b_ref[...]).astype(BF16)


def _inproj(x, w, b, tm=INPROJ_TM, tn=INPROJ_TN):
    n, d = x.shape
    width = w.shape[1]
    tm = min(tm, n)
    return pl.pallas_call(
        _inproj_kernel,
        grid=(n // tm, width // tn),
        in_specs=[pl.BlockSpec((tm, d), lambda i, j: (i, 0)),
                  pl.BlockSpec((d, tn), lambda i, j: (0, j)),
                  pl.BlockSpec((1, tn), lambda i, j: (0, j))],
        out_specs=pl.BlockSpec((tm, tn), lambda i, j: (i, j)),
        out_shape=jax.ShapeDtypeStruct((n, width), BF16),
        scratch_shapes=[pltpu.VMEM((tm, d), BF16)],
        compiler_params=_cparams(("arbitrary", "arbitrary")),
        name="inproj",
    )(x, w, b)


def _gelu_tanh(x):
    c = math.sqrt(2.0 / math.pi)
    return 0.5 * x * (1.0 + jnp.tanh(c * (x + 0.044715 * (x * x * x))))


def _s5_kernel(u_ref, bm_ref, cre_ref, cim_ref, are_ref, aim_ref, d_ref, wglu_ref,
               o_ref, sre_ref, sim_ref, st_re_ref, st_im_ref, y_ref, *, steps, batch):
    kw = SSM_WIDTH // S5_KB
    sw = SSM_GROUPS * SSM_STATE // S5_KB

    @pl.when(pl.program_id(0) == 0)
    def _():
        st_re_ref[...] = jnp.zeros_like(st_re_ref)
        st_im_ref[...] = jnp.zeros_like(st_im_ref)

    for k in range(S5_KB):
        ls = slice(k * sw, (k + 1) * sw)
        bu = jnp.dot(u_ref[:, k * kw:(k + 1) * kw], bm_ref[k], preferred_element_type=F32)
        sre_ref[:, ls] = bu[:, :sw]
        sim_ref[:, ls] = bu[:, sw:]

        ar = jnp.broadcast_to(are_ref[:, ls], (batch, sw))
        ai = jnp.broadcast_to(aim_ref[:, ls], (batch, sw))

        def step(t, carry):
            sr, si = carry
            rows = pl.ds(pl.multiple_of(t * batch, batch), batch)
            nr = ar * sr - ai * si + sre_ref[rows, ls]
            ni = ar * si + ai * sr + sim_ref[rows, ls]
            sre_ref[rows, ls] = nr
            sim_ref[rows, ls] = ni
            return nr, ni

        sr, si = lax.fori_loop(0, steps, step, (st_re_ref[:, ls], st_im_ref[:, ls]))
        st_re_ref[:, ls] = sr
        st_im_ref[:, ls] = si

        y_ref[:, k * kw:(k + 1) * kw] = (
            jnp.dot(sre_ref[:, ls].astype(BF16), cre_ref[k], preferred_element_type=F32)
            + jnp.dot(sim_ref[:, ls].astype(BF16), cim_ref[k], preferred_element_type=F32))

    y = y_ref[...] + d_ref[...] * u_ref[...].astype(F32)
    y = _gelu_tanh(y)
    z = jnp.dot(y.astype(BF16), wglu_ref[...], preferred_element_type=F32)
    o_ref[...] = (y * jax.nn.sigmoid(z)).astype(BF16)


def _s5(proj, bm, cre, cim, a_re, a_im, d_skip, w_glu, batch, steps=S5_T):
    n = proj.shape[0]
    seq = n // batch
    steps = min(steps, seq)
    rows = steps * batch
    nstate = SSM_GROUPS * SSM_STATE
    const = lambda shape: pl.BlockSpec(shape, lambda i: (0,) * len(shape))
    return pl.pallas_call(
        functools.partial(_s5_kernel, steps=steps, batch=batch),
        grid=(seq // steps,),
        in_specs=[pl.BlockSpec((rows, SSM_WIDTH), lambda i: (i, 0)),
                  const(bm.shape), const(cre.shape), const(cim.shape),
                  const(a_re.shape), const(a_im.shape), const(d_skip.shape), const(w_glu.shape)],
        out_specs=pl.BlockSpec((rows, SSM_WIDTH), lambda i: (i, 0)),
        out_shape=jax.ShapeDtypeStruct((n, SSM_WIDTH), BF16),
        scratch_shapes=[pltpu.VMEM((rows, nstate), F32), pltpu.VMEM((rows, nstate), F32),
                        pltpu.VMEM((batch, nstate), F32), pltpu.VMEM((batch, nstate), F32),
                        pltpu.VMEM((rows, SSM_WIDTH), F32)],
        compiler_params=_cparams(("arbitrary",)),
        name="s5",
    )(proj, bm, cre, cim, a_re, a_im, d_skip, w_glu)


def _s5_params(lam_re, lam_im, log_dt, b_re, b_im, c_re, c_im, d_skip):
    lam = lax.complex(lam_re.astype(F32), lam_im.astype(F32))
    dt = jnp.exp(log_dt.astype(F32))[:, None]
    lam_bar = jnp.exp(lam * dt)
    b_bar = ((lam_bar - 1.0) / lam)[:, :, None] * lax.complex(b_re.astype(F32), b_im.astype(F32))
    gl = SSM_GROUPS // S5_KB
    eye = jnp.eye(gl, dtype=F32)

    def in_blocks(b):
        b = b.reshape(S5_KB, gl, SSM_STATE, SSM_GROUP)
        m = jnp.einsum('kgpc,gh->kgchp', b, eye)
        return m.reshape(S5_KB, gl * SSM_GROUP, gl * SSM_STATE)

    def out_blocks(c):
        c = c.reshape(S5_KB, gl, SSM_GROUP, SSM_STATE)
        m = jnp.einsum('kgcp,gh->kgphc', c, eye)
        return m.reshape(S5_KB, gl * SSM_STATE, gl * SSM_GROUP)

    bm = jnp.concatenate([in_blocks(jnp.real(b_bar)), in_blocks(jnp.imag(b_bar))], axis=-1).astype(BF16)
    cre = out_blocks(c_re.astype(F32)).astype(BF16)
    cim = out_blocks(-c_im.astype(F32)).astype(BF16)
    a_re = jnp.real(lam_bar).reshape(1, -1)
    a_im = jnp.imag(lam_bar).reshape(1, -1)
    return bm, cre, cim, a_re, a_im, d_skip.astype(F32).reshape(1, -1)


def _attn_kernel(q_ref, kp_ref, kc_ref, vp_ref, vc_ref, o_ref, lse_ref, *, steps):
    jb = pl.program_id(1)
    q = q_ref[...]
    k2 = jnp.concatenate([kp_ref[...], kc_ref[...]], axis=0)
    v2 = jnp.concatenate([vp_ref[...], vc_ref[...]], axis=0)
    qi = lax.broadcasted_iota(jnp.int32, (QBLK, 2 * QBLK), 0)
    kj = lax.broadcasted_iota(jnp.int32, (QBLK, 2 * QBLK), 1)
    dist = qi + QBLK - kj
    allowed = (dist >= 0) & (dist <= steps) & ((kj >= QBLK) | (jb > 0))
    outs, lses = [], []
    for h in range(HEADS):
        hs = slice(h * HEAD_DIM, (h + 1) * HEAD_DIM)
        s = lax.dot_general(q[:, hs], k2[:, hs], (((1,), (1,)), ((), ())),
                            preferred_element_type=F32) * (HEAD_DIM ** -0.5)
        s = jnp.where(allowed, s, NEG)
        m = jnp.max(s, axis=-1, keepdims=True)
        p = jnp.exp(s - m)
        denom = jnp.sum(p, axis=-1, keepdims=True)
        o = jnp.dot(p.astype(BF16), v2[:, hs], preferred_element_type=F32) / denom
        outs.append(o)
        lses.append(m + jnp.log(denom))
    o_ref[...] = jnp.concatenate(outs, axis=-1).astype(BF16)
    lse_ref[...] = jnp.concatenate(lses, axis=-1)


def _attn(q, k, v, steps):
    classes, n, width = q.shape
    nb = n // QBLK
    cur = pl.BlockSpec((None, QBLK, width), lambda c, j: (c, j, 0))
    prev = pl.BlockSpec((None, QBLK, width), lambda c, j: (c, jnp.maximum(j - 1, 0), 0))
    return pl.pallas_call(
        functools.partial(_attn_kernel, steps=steps),
        grid=(classes, nb),
        in_specs=[cur, prev, cur, prev, cur],
        out_specs=[cur, pl.BlockSpec((None, QBLK, HEADS), lambda c, j: (c, j, 0))],
        out_shape=[jax.ShapeDtypeStruct((classes, n, width), BF16),
                   jax.ShapeDtypeStruct((classes, n, HEADS), F32)],
        compiler_params=_cparams(("arbitrary", "arbitrary")),
        name="attn",
    )(q, k, k, v, v)


def _layer_norm(z, g, b):
    mu = jnp.mean(z, axis=-1, keepdims=True)
    zc = z - mu
    var = jnp.mean(zc * zc, axis=-1, keepdims=True)
    return zc * lax.rsqrt(var + LN_EPS) * g + b


def _store_rows(ref, y):
    rows, d = y.shape
    sub = d // LANES
    for s in range(sub):
        ref[pl.ds(s, rows, stride=sub), :] = y[:, s * LANES:(s + 1) * LANES]


def _load_rows(ref, rows, sub):
    return jnp.concatenate([ref[pl.ds(s, rows, stride=sub), :] for s in range(sub)], axis=-1)


def _expand_heads(w, e_ref):
    hi = w.astype(BF16)
    lo = (w - hi.astype(F32)).astype(BF16)
    return (jnp.dot(hi, e_ref[...], preferred_element_type=F32)
            + jnp.dot(lo, e_ref[...], preferred_element_type=F32))


def _merge_kernel(x_ref, yssm_ref, o0_ref, o1_ref, o2_ref, lse_ref, gs_ref, ga_ref,
                  wbs_ref, wba_ref, wout_ref, e_ref, g_ref, b_ref, o_ref, orow_ref, *, alpha):
    lse = lse_ref[...]
    l0, l1, l2 = lse[:, 0:HEADS], lse[:, HEADS:2 * HEADS], lse[:, 2 * HEADS:3 * HEADS]
    m = jnp.maximum(jnp.maximum(l0, l1), l2)
    e0, e1, e2 = jnp.exp(l0 - m), jnp.exp(l1 - m), jnp.exp(l2 - m)
    den = e0 + e1 + e2
    y_attn = (_expand_heads(e0 / den, e_ref) * o0_ref[...].astype(F32)
              + _expand_heads(e1 / den, e_ref) * o1_ref[...].astype(F32)
              + _expand_heads(e2 / den, e_ref) * o2_ref[...].astype(F32))
    ya = jnp.dot(y_attn.astype(BF16), wba_ref[...], preferred_element_type=F32)
    ys = jnp.dot(yssm_ref[...], wbs_ref[...], preferred_element_type=F32)
    merged = (jax.nn.sigmoid(gs_ref[...].astype(F32)) * ys
              + jax.nn.sigmoid(ga_ref[...].astype(F32)) * ya)
    mix = jnp.dot(merged.astype(BF16), wout_ref[...], preferred_element_type=F32)
    y = _layer_norm(alpha * x_ref[...] + mix, g_ref[...], b_ref[...])
    o_ref[...] = y
    _store_rows(orow_ref, y)


def _merge(x, y_ssm, o0, o1, o2, lse, proj, w_br_ssm, w_br_attn, w_out, expand, ln_g, ln_b,
           alpha, tm=MERGE_TM):
    n, d = x.shape
    tm = min(tm, n)
    gate_blk = proj.shape[1] // d
    row = lambda w: pl.BlockSpec((tm, w), lambda i: (i, 0))
    const = lambda a: pl.BlockSpec(a.shape, lambda i: (0,) * a.ndim)
    return pl.pallas_call(
        functools.partial(_merge_kernel, alpha=alpha),
        grid=(n // tm,),
        in_specs=[row(d), row(SSM_WIDTH), row(ATTN_WIDTH), row(ATTN_WIDTH), row(ATTN_WIDTH),
                  row(lse.shape[1]),
                  pl.BlockSpec((tm, d), lambda i: (i, gate_blk - 2)),
                  pl.BlockSpec((tm, d), lambda i: (i, gate_blk - 1)),
                  const(w_br_ssm), const(w_br_attn), const(w_out), const(expand),
                  const(ln_g), const(ln_b)],
        out_specs=[row(d), pl.BlockSpec((tm * (d // LANES), LANES), lambda i: (i, 0))],
        out_shape=[jax.ShapeDtypeStruct((n, d), F32),
                   jax.ShapeDtypeStruct((n * (d // LANES), LANES), F32)],
        compiler_params=_cparams(("arbitrary",)),
        name="merge",
    )(x, y_ssm, o0, o1, o2, lse, proj, proj, w_br_ssm, w_br_attn, w_out, expand, ln_g, ln_b)


def _first_argmax(v, iota, size, axis):
    m = jnp.max(v, axis=axis, keepdims=True)
    idx = jnp.min(jnp.where(v == m, iota, size), axis=axis, keepdims=True)
    return m, idx


def _route_kernel(x_ref, rwh_ref, rwl_ref, bias_ref, eg_ref, tri_ref,
                  eidx_ref, gate_ref, rank_ref, cnt_ref, carry_ref):
    @pl.when(pl.program_id(0) == 0)
    def _():
        carry_ref[...] = jnp.zeros_like(carry_ref)

    x = x_ref[...]
    tm = x.shape[0]
    xh = x.astype(BF16)
    xl = (x - xh.astype(F32)).astype(BF16)
    nt = (((1,), (1,)), ((), ()))
    logits = (lax.dot_general(rwh_ref[...], xh, nt, preferred_element_type=F32)
              + lax.dot_general(rwh_ref[...], xl, nt, preferred_element_type=F32)
              + lax.dot_general(rwl_ref[...], xh, nt, preferred_element_type=F32))
    scores = jax.nn.sigmoid(logits)
    sel = scores + bias_ref[...]

    sel3 = sel.reshape(N_EXPERT_GROUPS, GROUP_SIZE, tm)
    iw = lax.broadcasted_iota(jnp.int32, sel3.shape, 1)
    m1, i1 = _first_argmax(sel3, iw, GROUP_SIZE, 1)
    m2 = jnp.max(jnp.where(iw == i1, NEG, sel3), axis=1, keepdims=True)
    gs = (m1 + m2).reshape(N_EXPERT_GROUPS, tm)

    ig = lax.broadcasted_iota(jnp.int32, gs.shape, 0)
    gmask = jnp.zeros(gs.shape, F32)
    for _ in range(TOPK_GROUPS):
        _, gi = _first_argmax(gs, ig, N_EXPERT_GROUPS, 0)
        hit = ig == gi
        gmask = jnp.where(hit, 1.0, gmask)
        gs = jnp.where(hit, NEG, gs)
    emask = jnp.dot(eg_ref[...], gmask.astype(BF16), preferred_element_type=F32) > 0.5

    masked = jnp.where(emask, sel, NEG)
    ie = lax.broadcasted_iota(jnp.int32, masked.shape, 0)
    chosen = jnp.zeros(masked.shape, F32)
    idxs, vals = [], []
    for _ in range(TOP_K):
        _, ei = _first_argmax(masked, ie, N_EXPERTS, 0)
        hit = ie == ei
        idxs.append(ei)
        vals.append(jnp.sum(jnp.where(hit, scores, 0.0), axis=0, keepdims=True))
        chosen = jnp.where(hit, 1.0, chosen)
        masked = jnp.where(hit, NEG, masked)
    total = vals[0]
    for v in vals[1:]:
        total = total + v

    prefix = jnp.dot(chosen.astype(BF16), tri_ref[...], preferred_element_type=F32)
    pos = carry_ref[:, 0:1] + prefix - 1.0
    ranks = [jnp.sum(jnp.where(ie == ei, pos, 0.0), axis=0, keepdims=True) for ei in idxs]
    carry = carry_ref[...] + jnp.sum(chosen, axis=1, keepdims=True)
    carry_ref[...] = carry
    cnt_ref[...] = carry

    eidx_ref[...] = jnp.concatenate(idxs, axis=0)
    gate_ref[...] = jnp.concatenate([v / total * ROUTED_SCALE for v in vals], axis=0)
    rank_ref[...] = jnp.concatenate(ranks, axis=0).astype(jnp.int32)


def _route(x, rw_hi, rw_lo, bias, eg, tri, tm=ROUTE_TM):
    n, d = x.shape
    tm = min(tm, n)
    const = lambda a: pl.BlockSpec(a.shape, lambda i: (0,) * a.ndim)
    col = pl.BlockSpec((TOP_K, tm), lambda i: (0, i))
    return pl.pallas_call(
        _route_kernel,
        grid=(n // tm,),
        in_specs=[pl.BlockSpec((tm, d), lambda i: (i, 0)),
                  const(rw_hi), const(rw_lo), const(bias), const(eg), const(tri)],
        out_specs=[col, col, col, pl.BlockSpec((N_EXPERTS, LANES), lambda i: (0, 0))],
        out_shape=[jax.ShapeDtypeStruct((TOP_K, n), jnp.int32),
                   jax.ShapeDtypeStruct((TOP_K, n), F32),
                   jax.ShapeDtypeStruct((TOP_K, n), jnp.int32),
                   jax.ShapeDtypeStruct((N_EXPERTS, LANES), F32)],
        scratch_shapes=[pltpu.VMEM((N_EXPERTS, LANES), F32)],
        compiler_params=_cparams(("arbitrary",)),
        name="route",
    )(x, rw_hi, rw_lo, bias, eg, tri)


def _gather_kernel(idx_ref, src_ref, o_ref, sem, *, rows):
    def issue(j, _):
        pltpu.make_async_copy(src_ref.at[idx_ref[0, 0, j]], o_ref.at[j], sem).start()
        return 0

    lax.fori_loop(0, rows, issue, 0)

    def drain(j, _):
        pltpu.make_async_copy(src_ref.at[0], o_ref.at[j], sem).wait()
        return 0

    lax.fori_loop(0, rows, drain, 0)


def _gather_rows(src, idx, sub, rows=GATHER_G):
    s = src.shape[0] // sub
    m = idx.shape[0]
    rows = min(rows, m)
    out = pl.pallas_call(
        functools.partial(_gather_kernel, rows=rows),
        grid=(m // rows,),
        in_specs=[pl.BlockSpec((1, 1, rows), lambda i: (i, 0, 0), memory_space=pltpu.SMEM),
                  pl.BlockSpec(memory_space=pl.ANY)],
        out_specs=pl.BlockSpec((rows, sub, LANES), lambda i: (i, 0, 0)),
        out_shape=jax.ShapeDtypeStruct((m, sub, LANES), src.dtype),
        scratch_shapes=[pltpu.SemaphoreType.DMA],
        compiler_params=_cparams(("arbitrary",)),
        name="gather_rows",
    )(idx.reshape(m // rows, 1, rows), src.reshape(s, sub, LANES))
    return out.reshape(m * sub, LANES)


def _sc_gather_rows(src, idx, sub, chunk=SC_CHUNK):
    s = src.shape[0] // sub
    m = idx.shape[0]
    info = plsc.get_sparse_core_info()
    n_workers = info.num_cores * info.num_subcores
    per_worker = m // n_workers
    n_chunks = per_worker // chunk
    assert n_chunks * chunk * n_workers == m
    mesh = plsc.VectorSubcoreMesh(core_axis_name="c", subcore_axis_name="s")

    @functools.partial(
        pl.kernel, mesh=mesh,
        out_type=jax.ShapeDtypeStruct((m, sub, LANES), src.dtype),
        scratch_types=[pltpu.VMEM((n_chunks, chunk), jnp.int32),
                       pltpu.VMEM((chunk, sub, LANES), src.dtype),
                       pltpu.SemaphoreType.DMA])
    def gather(src_hbm, idx_hbm, out_hbm, idx_v, rows_v, sem):
        wid = lax.axis_index("s") * info.num_cores + lax.axis_index("c")
        pltpu.sync_copy(idx_hbm.at[wid], idx_v)

        @pl.loop(0, n_chunks)
        def _(j):
            pltpu.async_copy(src_hbm.at[idx_v.at[j]], rows_v, sem).wait()
            pltpu.sync_copy(rows_v, out_hbm.at[pl.ds(wid * per_worker + j * chunk, chunk)])

    out = gather(src.reshape(s, sub, LANES), idx.reshape(n_workers, n_chunks, chunk))
    return out.reshape(m * sub, LANES)


def _sc_scatter_rows(src, dest, n_out, sub, chunk=SC_CHUNK):
    copies, n = dest.shape
    info = plsc.get_sparse_core_info()
    n_workers = info.num_cores * info.num_subcores
    per_worker = n // n_workers
    n_chunks = per_worker // chunk
    assert n_chunks * chunk * n_workers == n and copies * n == n_out
    mesh = plsc.VectorSubcoreMesh(core_axis_name="c", subcore_axis_name="s")
    idx = dest.reshape(copies, n_workers, n_chunks, chunk).transpose(1, 2, 0, 3)
    idx = idx.reshape(n_workers, n_chunks * copies, chunk)

    @functools.partial(
        pl.kernel, mesh=mesh,
        out_type=jax.ShapeDtypeStruct((n_out, sub, LANES), src.dtype),
        scratch_types=[pltpu.VMEM((n_chunks * copies, chunk), jnp.int32),
                       pltpu.VMEM((chunk, sub, LANES), src.dtype),
                       pltpu.SemaphoreType.DMA])
    def scatter(src_hbm, idx_hbm, out_hbm, idx_v, rows_v, sem):
        wid = lax.axis_index("s") * info.num_cores + lax.axis_index("c")
        pltpu.sync_copy(idx_hbm.at[wid], idx_v)

        @pl.loop(0, n_chunks)
        def _(j):
            pltpu.sync_copy(src_hbm.at[pl.ds(wid * per_worker + j * chunk, chunk)], rows_v)
            for c in range(copies):
                pltpu.async_copy(rows_v, out_hbm.at[idx_v.at[j * copies + c]], sem).wait()

    out = scatter(src.reshape(n, sub, LANES), idx)
    return out.reshape(n_out * sub, LANES)


def _gmm_kernel(be_ref, nu_ref, x_ref, wgu_ref, wd_ref, o_ref, *, hidden, blk, sub):
    @pl.when(pl.program_id(0) < nu_ref[0])
    def _():
        x = _load_rows(x_ref, blk, sub).astype(BF16)
        gu = jnp.dot(x, wgu_ref[...], preferred_element_type=F32)
        g, u = gu[:, :hidden], gu[:, hidden:]
        h = (g * jax.nn.sigmoid(g) * u).astype(BF16)
        _store_rows(o_ref, jnp.dot(h, wd_ref[...], preferred_element_type=F32))


def _gmm(xs, w_gu, w_down, blk_e, n_used, blk=MOE_BLK):
    hidden, d = w_down.shape[1:]
    sub = d // LANES
    n_rows = xs.shape[0] // sub
    row = lambda b, be, nu: (jnp.minimum(b, nu[0] - 1), 0)
    return pl.pallas_call(
        functools.partial(_gmm_kernel, hidden=hidden, blk=blk, sub=sub),
        grid_spec=pltpu.PrefetchScalarGridSpec(
            num_scalar_prefetch=2,
            grid=(n_rows // blk,),
            in_specs=[pl.BlockSpec((blk * sub, LANES), row),
                      pl.BlockSpec((None, d, 2 * hidden), lambda b, be, nu: (be[b], 0, 0)),
                      pl.BlockSpec((None, hidden, d), lambda b, be, nu: (be[b], 0, 0))],
            out_specs=pl.BlockSpec((blk * sub, LANES), row)),
        out_shape=jax.ShapeDtypeStruct((n_rows * sub, LANES), F32),
        compiler_params=_cparams(("arbitrary",)),
        name="expert_gmm",
    )(blk_e, n_used, xs, w_gu, w_down)


def _combine_kernel(x_ref, yg_ref, gate_ref, wgu_ref, wd_ref, g_ref, b_ref, o_ref, *, alpha, hidden):
    x = x_ref[...]
    gu = jnp.dot(x.astype(BF16), wgu_ref[...], preferred_element_type=F32)
    g, u = gu[:, :hidden], gu[:, hidden:]
    h = (g * jax.nn.sigmoid(g) * u).astype(BF16)
    acc = jnp.dot(h, wd_ref[...], preferred_element_type=F32)
    gates = gate_ref[...]
    tm, d = x.shape
    for k in range(TOP_K):
        acc = acc + gates[:, k:k + 1] * _load_rows(yg_ref.at[k], tm, d // LANES)
    o_ref[...] = _layer_norm(alpha * x + acc, g_ref[...], b_ref[...])


def _combine(x, yg, gates_t, w_gu, w_down, ln_g, ln_b, alpha, tm=COMBINE_TM):
    n, d = x.shape
    tm = min(tm, n)
    sub = d // LANES
    hidden = w_down.shape[0]
    const = lambda a: pl.BlockSpec(a.shape, lambda i: (0,) * a.ndim)
    return pl.pallas_call(
        functools.partial(_combine_kernel, alpha=alpha, hidden=hidden),
        grid=(n // tm,),
        in_specs=[pl.BlockSpec((tm, d), lambda i: (i, 0)),
                  pl.BlockSpec((TOP_K, tm * sub, LANES), lambda i: (0, i, 0)),
                  pl.BlockSpec((tm, TOP_K), lambda i: (i, 0)),
                  const(w_gu), const(w_down), const(ln_g), const(ln_b)],
        out_specs=pl.BlockSpec((tm, d), lambda i: (i, 0)),
        out_shape=jax.ShapeDtypeStruct((n, d), F32),
        compiler_params=_cparams(("arbitrary",)),
        name="combine",
    )(x, yg, gates_t, w_gu, w_down, ln_g, ln_b)


def _token_mixer(x, batch, w_in, b_in, s5p, w_glu, w_br_ssm, w_br_attn, w_out, expand, ln_g, ln_b, alpha):
    n, d = x.shape
    seq = n // batch
    proj = _inproj(x, w_in, b_in)
    y_ssm = _s5(proj, *s5p, w_glu, batch)

    qkv0 = SSM_WIDTH
    n_grp = len(ATTN_PATTERNS)
    qkv_w = n_grp * ATTN_WIDTH
    outs, lses = [], []
    for g, (window, dil) in enumerate(ATTN_PATTERNS):
        assert window // dil == QBLK and (seq // dil) % QBLK == 0

        def classes(col0):
            t = proj[:, col0 + g * ATTN_WIDTH: col0 + (g + 1) * ATTN_WIDTH]
            t = t.reshape(seq // dil, dil * batch, ATTN_WIDTH)
            return jnp.swapaxes(t, 0, 1)

        o_g, lse_g = _attn(classes(qkv0), classes(qkv0 + qkv_w), classes(qkv0 + 2 * qkv_w),
                           window // dil)
        outs.append(jnp.swapaxes(o_g, 0, 1).reshape(n, ATTN_WIDTH))
        lses.append(jnp.swapaxes(lse_g, 0, 1).reshape(n, HEADS))
    lse = jnp.concatenate(lses, axis=-1)
    return _merge(x, y_ssm, outs[0], outs[1], outs[2], lse, proj, w_br_ssm, w_br_attn, w_out,
                  expand, ln_g, ln_b, alpha)


def _moe(x, x_rows, rw_hi, rw_lo, bias, eg, tri, w_gu, w_down, sh_gu, sh_down, ln_g, ln_b, alpha):
    n, d = x.shape
    sub = d // LANES
    eidx, gates, rank, cnt = _route(x, rw_hi, rw_lo, bias, eg, tri)
    counts = cnt[:, 0].astype(jnp.int32)
    nk = n * TOP_K
    n_rows = nk + N_EXPERTS * MOE_BLK
    n_blocks = n_rows // MOE_BLK
    pad_counts = (counts + MOE_BLK - 1) // MOE_BLK * MOE_BLK
    pad_end = jnp.cumsum(pad_counts)
    pad_start = pad_end - pad_counts
    onehot = eidx[:, :, None] == jnp.arange(N_EXPERTS, dtype=jnp.int32)
    dest = jnp.sum(jnp.where(onehot, pad_start, 0), axis=-1) + rank
    seg_start = jnp.concatenate([pad_start + counts, pad_end[-1:]])
    seg_len = jnp.concatenate([pad_counts - counts, n_rows - pad_end[-1:]])
    seg_end = jnp.cumsum(seg_len)
    q = jnp.arange(n, dtype=jnp.int32)[:, None]
    seg = jnp.sum((q >= seg_end[None, :]).astype(jnp.int32), axis=1, keepdims=True)
    seg_hot = seg == jnp.arange(N_EXPERTS + 1, dtype=jnp.int32)[None, :]
    pad_dest = jnp.sum(jnp.where(seg_hot, seg_start - (seg_end - seg_len), 0), axis=1) + q[:, 0]
    dest_all = jnp.concatenate([dest, pad_dest[None, :]], axis=0).astype(jnp.int32)
    blk_first = jnp.arange(n_blocks, dtype=jnp.int32)[:, None] * MOE_BLK
    blk_e = jnp.minimum(jnp.sum((blk_first >= pad_end[None, :]).astype(jnp.int32), axis=1), N_EXPERTS - 1)
    n_used = (pad_end[-1:] // MOE_BLK).astype(jnp.int32)

    xs = _sc_scatter_rows(x_rows, dest_all, n_rows, sub)
    ys = _gmm(xs, w_gu, w_down, blk_e, n_used)
    yg = _sc_gather_rows(ys, dest.reshape(nk), sub).reshape(TOP_K, n * sub, LANES)
    return _combine(x, yg, gates.T, sh_gu, sh_down, ln_g, ln_b, alpha)


def kernel(x, w_in, b_in, ssm_lam_re, ssm_lam_im, ssm_log_dt, ssm_b_re, ssm_b_im, ssm_c_re, ssm_c_im, ssm_d, w_glu, w_br_ssm, w_br_attn, w_out, ln1_g, ln1_b, router_w, router_bias, exp_w_gate, exp_w_up, exp_w_down, sh_w_gate, sh_w_up, sh_w_down, ln2_g, ln2_b):
    batch, seq, d = x.shape
    depth = w_in.shape[0]
    assert batch == SUBLANES
    alpha = (2 * depth) ** 0.25
    n = batch * seq
    xt = jnp.swapaxes(x, 0, 1).reshape(n, d)

    expand = jnp.repeat(jnp.eye(HEADS, dtype=BF16), HEAD_DIM, axis=1)
    eg = jnp.repeat(jnp.eye(N_EXPERT_GROUPS, dtype=BF16), GROUP_SIZE, axis=0)
    tm = min(ROUTE_TM, n)
    tri = (jnp.arange(tm)[:, None] <= jnp.arange(tm)[None, :]).astype(BF16)
    row = lambda a: a.astype(F32).reshape(1, -1)

    for l in range(depth):
        s5p = _s5_params(ssm_lam_re[l], ssm_lam_im[l], ssm_log_dt[l], ssm_b_re[l], ssm_b_im[l],
                         ssm_c_re[l], ssm_c_im[l], ssm_d[l])
        xt, xt_rows = _token_mixer(xt, batch, w_in[l].astype(BF16), row(b_in[l]), s5p,
                                   w_glu[l].astype(BF16), w_br_ssm[l].astype(BF16),
                                   w_br_attn[l].astype(BF16), w_out[l].astype(BF16),
                                   expand, row(ln1_g[l]), row(ln1_b[l]), alpha)
        rwt = router_w[l].astype(F32).T
        rw_hi = rwt.astype(BF16)
        rw_lo = (rwt - rw_hi.astype(F32)).astype(BF16)
        w_gu = jnp.concatenate([exp_w_gate[l], exp_w_up[l]], axis=-1).astype(BF16)
        sh_gu = jnp.concatenate([sh_w_gate[l], sh_w_up[l]], axis=-1).astype(BF16)
        xt = _moe(xt, xt_rows, rw_hi, rw_lo, router_bias[l].astype(F32).reshape(-1, 1), eg, tri,
                  w_gu, exp_w_down[l].astype(BF16), sh_gu, sh_w_down[l].astype(BF16),
                  row(ln2_g[l]), row(ln2_b[l]), alpha)
    return jnp.swapaxes(xt.reshape(seq, batch, d), 0, 1)
```

```python
import functools
import math

import jax
import jax.numpy as jnp
import numpy as np
from jax import lax
from jax.experimental import pallas as pl
from jax.experimental.pallas import tpu as pltpu
from jax.experimental.pallas import tpu_sc as plsc

F32 = jnp.float32
BF16 = jnp.bfloat16

SSM_GROUP = 16
SSM_GROUPS = 32
SSM_WIDTH = SSM_GROUP * SSM_GROUPS
SSM_STATE = 64
HEAD_DIM = 64
HEADS = 8
ATTN_PATTERNS = ((128, 1), (512, 4), (2048, 16))
ATTN_WIDTH = HEADS * HEAD_DIM
QBLK = 128
N_EXPERTS = 64
TOP_K = 8
N_EXPERT_GROUPS = 8
GROUP_SIZE = N_EXPERTS // N_EXPERT_GROUPS
TOPK_GROUPS = 4
ROUTED_SCALE = 2.5
LN_EPS = 1e-5
NEG = -1e30

LANES = 128
SUBLANES = 8
VMEM_LIMIT = 56 * 1024 * 1024

INPROJ_TM = 1024
INPROJ_TN = 1024
S5_T = 128
S5_KB = 4
MERGE_TM = 512
ROUTE_TM = 512
MOE_BLK = 512
SC_CHUNK = 128
HI_HALF = np.uint32(0xFFFF0000)
COMBINE_TM = 256


def _cparams(sem):
    return pltpu.CompilerParams(dimension_semantics=sem, vmem_limit_bytes=VMEM_LIMIT)


def _inproj_kernel(x_ref, w_ref, b_ref, o_ref, xb_ref):
    @pl.when(pl.program_id(1) == 0)
    def _():
        xb_ref[...] = x_ref[...].astype(BF16)

    acc = jnp.dot(xb_ref[...], w_ref[...], preferred_element_type=F32)
    o_ref[...] = (acc + b_ref[...]).astype(BF16)


def _inproj(x, w, b, tm=INPROJ_TM, tn=INPROJ_TN):
    n, d = x.shape
    width = w.shape[1]
    tm = min(tm, n)
    return pl.pallas_call(
        _inproj_kernel,
        grid=(n // tm, width // tn),
        in_specs=[pl.BlockSpec((tm, d), lambda i, j: (i, 0)),
                  pl.BlockSpec((d, tn), lambda i, j: (0, j)),
                  pl.BlockSpec((1, tn), lambda i, j: (0, j))],
        out_specs=pl.BlockSpec((tm, tn), lambda i, j: (i, j)),
        out_shape=jax.ShapeDtypeStruct((n, width), BF16),
        scratch_shapes=[pltpu.VMEM((tm, d), BF16)],
        compiler_params=_cparams(("arbitrary", "arbitrary")),
        name="inproj",
    )(x, w, b)


def _gelu_tanh(x):
    c = math.sqrt(2.0 / math.pi)
    return 0.5 * x * (1.0 + jnp.tanh(c * (x + 0.044715 * (x * x * x))))


def _s5_kernel(u_ref, bm_ref, cre_ref, cim_ref, are_ref, aim_ref, d_ref, wglu_ref,
               o_ref, sre_ref, sim_ref, st_re_ref, st_im_ref, y_ref, *, steps, batch):
    kw = SSM_WIDTH // S5_KB
    sw = SSM_GROUPS * SSM_STATE // S5_KB

    @pl.when(pl.program_id(0) == 0)
    def _():
        st_re_ref[...] = jnp.zeros_like(st_re_ref)
        st_im_ref[...] = jnp.zeros_like(st_im_ref)

    for k in range(S5_KB):
        ls = slice(k * sw, (k + 1) * sw)
        bu = jnp.dot(u_ref[:, k * kw:(k + 1) * kw], bm_ref[k], preferred_element_type=F32)
        sre_ref[:, ls] = bu[:, :sw]
        sim_ref[:, ls] = bu[:, sw:]

        ar = jnp.broadcast_to(are_ref[:, ls], (batch, sw))
        ai = jnp.broadcast_to(aim_ref[:, ls], (batch, sw))

        def step(t, carry):
            sr, si = carry
            rows = pl.ds(pl.multiple_of(t * batch, batch), batch)
            nr = ar * sr - ai * si + sre_ref[rows, ls]
            ni = ar * si + ai * sr + sim_ref[rows, ls]
            sre_ref[rows, ls] = nr
            sim_ref[rows, ls] = ni
            return nr, ni

        sr, si = lax.fori_loop(0, steps, step, (st_re_ref[:, ls], st_im_ref[:, ls]))
        st_re_ref[:, ls] = sr
        st_im_ref[:, ls] = si

        y_ref[:, k * kw:(k + 1) * kw] = (
            jnp.dot(sre_ref[:, ls].astype(BF16), cre_ref[k], preferred_element_type=F32)
            + jnp.dot(sim_ref[:, ls].astype(BF16), cim_ref[k], preferred_element_type=F32))

    y = y_ref[...] + d_ref[...] * u_ref[...].astype(F32)
    y = _gelu_tanh(y)
    z = jnp.dot(y.astype(BF16), wglu_ref[...], preferred_element_type=F32)
    o_ref[...] = (y * jax.nn.sigmoid(z)).astype(BF16)


def _s5(proj, bm, cre, cim, a_re, a_im, d_skip, w_glu, batch, steps=S5_T):
    n = proj.shape[0]
    seq = n // batch
    steps = min(steps, seq)
    rows = steps * batch
    nstate = SSM_GROUPS * SSM_STATE
    const = lambda shape: pl.BlockSpec(shape, lambda i: (0,) * len(shape))
    return pl.pallas_call(
        functools.partial(_s5_kernel, steps=steps, batch=batch),
        grid=(seq // steps,),
        in_specs=[pl.BlockSpec((rows, SSM_WIDTH), lambda i: (i, 0)),
                  const(bm.shape), const(cre.shape), const(cim.shape),
                  const(a_re.shape), const(a_im.shape), const(d_skip.shape), const(w_glu.shape)],
        out_specs=pl.BlockSpec((rows, SSM_WIDTH), lambda i: (i, 0)),
        out_shape=jax.ShapeDtypeStruct((n, SSM_WIDTH), BF16),
        scratch_shapes=[pltpu.VMEM((rows, nstate), F32), pltpu.VMEM((rows, nstate), F32),
                        pltpu.VMEM((batch, nstate), F32), pltpu.VMEM((batch, nstate), F32),
                        pltpu.VMEM((rows, SSM_WIDTH), F32)],
        compiler_params=_cparams(("arbitrary",)),
        name="s5",
    )(proj, bm, cre, cim, a_re, a_im, d_skip, w_glu)


def _s5_params(lam_re, lam_im, log_dt, b_re, b_im, c_re, c_im, d_skip):
    lam = lax.complex(lam_re.astype(F32), lam_im.astype(F32))
    dt = jnp.exp(log_dt.astype(F32))[:, None]
    lam_bar = jnp.exp(lam * dt)
    b_bar = ((lam_bar - 1.0) / lam)[:, :, None] * lax.complex(b_re.astype(F32), b_im.astype(F32))
    gl = SSM_GROUPS // S5_KB
    eye = jnp.eye(gl, dtype=F32)

    def in_blocks(b):
        b = b.reshape(S5_KB, gl, SSM_STATE, SSM_GROUP)
        m = jnp.einsum('kgpc,gh->kgchp', b, eye)
        return m.reshape(S5_KB, gl * SSM_GROUP, gl * SSM_STATE)

    def out_blocks(c):
        c = c.reshape(S5_KB, gl, SSM_GROUP, SSM_STATE)
        m = jnp.einsum('kgcp,gh->kgphc', c, eye)
        return m.reshape(S5_KB, gl * SSM_STATE, gl * SSM_GROUP)

    bm = jnp.concatenate([in_blocks(jnp.real(b_bar)), in_blocks(jnp.imag(b_bar))], axis=-1).astype(BF16)
    cre = out_blocks(c_re.astype(F32)).astype(BF16)
    cim = out_blocks(-c_im.astype(F32)).astype(BF16)
    a_re = jnp.real(lam_bar).reshape(1, -1)
    a_im = jnp.imag(lam_bar).reshape(1, -1)
    return bm, cre, cim, a_re, a_im, d_skip.astype(F32).reshape(1, -1)


def _attn_kernel(q_ref, kp_ref, kc_ref, vp_ref, vc_ref, o_ref, lse_ref, *, steps):
    jb = pl.program_id(1)
    q = q_ref[...]
    k2 = jnp.concatenate([kp_ref[...], kc_ref[...]], axis=0)
    v2 = jnp.concatenate([vp_ref[...], vc_ref[...]], axis=0)
    qi = lax.broadcasted_iota(jnp.int32, (QBLK, 2 * QBLK), 0)
    kj = lax.broadcasted_iota(jnp.int32, (QBLK, 2 * QBLK), 1)
    dist = qi + QBLK - kj
    allowed = (dist >= 0) & (dist <= steps) & ((kj >= QBLK) | (jb > 0))
    outs, lses = [], []
    for h in range(HEADS):
        hs = slice(h * HEAD_DIM, (h + 1) * HEAD_DIM)
        s = lax.dot_general(q[:, hs], k2[:, hs], (((1,), (1,)), ((), ())),
                            preferred_element_type=F32) * (HEAD_DIM ** -0.5)
        s = jnp.where(allowed, s, NEG)
        m = jnp.max(s, axis=-1, keepdims=True)
        p = jnp.exp(s - m)
        denom = jnp.sum(p, axis=-1, keepdims=True)
        o = jnp.dot(p.astype(BF16), v2[:, hs], preferred_element_type=F32) / denom
        outs.append(o)
        lses.append(m + jnp.log(denom))
    o_ref[...] = jnp.concatenate(outs, axis=-1).astype(BF16)
    lse_ref[...] = jnp.concatenate(lses, axis=-1)


def _attn(q, k, v, steps):
    classes, n, width = q.shape
    nb = n // QBLK
    cur = pl.BlockSpec((None, QBLK, width), lambda c, j: (c, j, 0))
    prev = pl.BlockSpec((None, QBLK, width), lambda c, j: (c, jnp.maximum(j - 1, 0), 0))
    return pl.pallas_call(
        functools.partial(_attn_kernel, steps=steps),
        grid=(classes, nb),
        in_specs=[cur, prev, cur, prev, cur],
        out_specs=[cur, pl.BlockSpec((None, QBLK, HEADS), lambda c, j: (c, j, 0))],
        out_shape=[jax.ShapeDtypeStruct((classes, n, width), BF16),
                   jax.ShapeDtypeStruct((classes, n, HEADS), F32)],
        compiler_params=_cparams(("arbitrary", "arbitrary")),
        name="attn",
    )(q, k, k, v, v)


def _layer_norm(z, g, b):
    mu = jnp.mean(z, axis=-1, keepdims=True)
    zc = z - mu
    var = jnp.mean(zc * zc, axis=-1, keepdims=True)
    return zc * lax.rsqrt(var + LN_EPS) * g + b


def _row_words(d):
    return d // (2 * LANES)


def _store_rows(ref, y):
    rows, d = y.shape
    sub = _row_words(d)
    for s in range(sub):
        lo = y[:, (2 * s) * LANES:(2 * s + 1) * LANES].astype(BF16).astype(F32)
        hi = y[:, (2 * s + 1) * LANES:(2 * s + 2) * LANES].astype(BF16).astype(F32)
        word = (pltpu.bitcast(lo, jnp.uint32) >> 16) | (pltpu.bitcast(hi, jnp.uint32) & HI_HALF)
        ref[pl.ds(s, rows, stride=sub), :] = word


def _load_rows(ref, rows, sub):
    parts = []
    for s in range(sub):
        word = ref[pl.ds(s, rows, stride=sub), :]
        parts.append(pltpu.bitcast(word << 16, F32))
        parts.append(pltpu.bitcast(word & HI_HALF, F32))
    return jnp.concatenate(parts, axis=-1)


def _expand_heads(w, e_ref):
    hi = w.astype(BF16)
    lo = (w - hi.astype(F32)).astype(BF16)
    return (jnp.dot(hi, e_ref[...], preferred_element_type=F32)
            + jnp.dot(lo, e_ref[...], preferred_element_type=F32))


def _merge_kernel(x_ref, yssm_ref, o0_ref, o1_ref, o2_ref, lse_ref, gs_ref, ga_ref,
                  wbs_ref, wba_ref, wout_ref, e_ref, g_ref, b_ref, o_ref, orow_ref, *, alpha):
    lse = lse_ref[...]
    l0, l1, l2 = lse[:, 0:HEADS], lse[:, HEADS:2 * HEADS], lse[:, 2 * HEADS:3 * HEADS]
    m = jnp.maximum(jnp.maximum(l0, l1), l2)
    e0, e1, e2 = jnp.exp(l0 - m), jnp.exp(l1 - m), jnp.exp(l2 - m)
    den = e0 + e1 + e2
    y_attn = (_expand_heads(e0 / den, e_ref) * o0_ref[...].astype(F32)
              + _expand_heads(e1 / den, e_ref) * o1_ref[...].astype(F32)
              + _expand_heads(e2 / den, e_ref) * o2_ref[...].astype(F32))
    ya = jnp.dot(y_attn.astype(BF16), wba_ref[...], preferred_element_type=F32)
    ys = jnp.dot(yssm_ref[...], wbs_ref[...], preferred_element_type=F32)
    merged = (jax.nn.sigmoid(gs_ref[...].astype(F32)) * ys
              + jax.nn.sigmoid(ga_ref[...].astype(F32)) * ya)
    mix = jnp.dot(merged.astype(BF16), wout_ref[...], preferred_element_type=F32)
    y = _layer_norm(alpha * x_ref[...] + mix, g_ref[...], b_ref[...])
    o_ref[...] = y
    _store_rows(orow_ref, y)


def _merge(x, y_ssm, o0, o1, o2, lse, proj, w_br_ssm, w_br_attn, w_out, expand, ln_g, ln_b,
           alpha, tm=MERGE_TM):
    n, d = x.shape
    tm = min(tm, n)
    gate_blk = proj.shape[1] // d
    row = lambda w: pl.BlockSpec((tm, w), lambda i: (i, 0))
    const = lambda a: pl.BlockSpec(a.shape, lambda i: (0,) * a.ndim)
    return pl.pallas_call(
        functools.partial(_merge_kernel, alpha=alpha),
        grid=(n // tm,),
        in_specs=[row(d), row(SSM_WIDTH), row(ATTN_WIDTH), row(ATTN_WIDTH), row(ATTN_WIDTH),
                  row(lse.shape[1]),
                  pl.BlockSpec((tm, d), lambda i: (i, gate_blk - 2)),
                  pl.BlockSpec((tm, d), lambda i: (i, gate_blk - 1)),
                  const(w_br_ssm), const(w_br_attn), const(w_out), const(expand),
                  const(ln_g), const(ln_b)],
        out_specs=[row(d), pl.BlockSpec((tm * _row_words(d), LANES), lambda i: (i, 0))],
        out_shape=[jax.ShapeDtypeStruct((n, d), F32),
                   jax.ShapeDtypeStruct((n * _row_words(d), LANES), jnp.uint32)],
        compiler_params=_cparams(("arbitrary",)),
        name="merge",
    )(x, y_ssm, o0, o1, o2, lse, proj, proj, w_br_ssm, w_br_attn, w_out, expand, ln_g, ln_b)


def _first_argmax(v, iota, size, axis):
    m = jnp.max(v, axis=axis, keepdims=True)
    idx = jnp.min(jnp.where(v == m, iota, size), axis=axis, keepdims=True)
    return m, idx


def _route_kernel(x_ref, rwh_ref, rwl_ref, bias_ref, eg_ref, tri_ref,
                  eidx_ref, gate_ref, rank_ref, cnt_ref, carry_ref):
    @pl.when(pl.program_id(0) == 0)
    def _():
        carry_ref[...] = jnp.zeros_like(carry_ref)

    x = x_ref[...]
    tm = x.shape[0]
    xh = x.astype(BF16)
    xl = (x - xh.astype(F32)).astype(BF16)
    nt = (((1,), (1,)), ((), ()))
    logits = (lax.dot_general(rwh_ref[...], xh, nt, preferred_element_type=F32)
              + lax.dot_general(rwh_ref[...], xl, nt, preferred_element_type=F32)
              + lax.dot_general(rwl_ref[...], xh, nt, preferred_element_type=F32))
    scores = jax.nn.sigmoid(logits)
    sel = scores + bias_ref[...]

    sel3 = sel.reshape(N_EXPERT_GROUPS, GROUP_SIZE, tm)
    iw = lax.broadcasted_iota(jnp.int32, sel3.shape, 1)
    m1, i1 = _first_argmax(sel3, iw, GROUP_SIZE, 1)
    m2 = jnp.max(jnp.where(iw == i1, NEG, sel3), axis=1, keepdims=True)
    gs = (m1 + m2).reshape(N_EXPERT_GROUPS, tm)

    ig = lax.broadcasted_iota(jnp.int32, gs.shape, 0)
    gmask = jnp.zeros(gs.shape, F32)
    for _ in range(TOPK_GROUPS):
        _, gi = _first_argmax(gs, ig, N_EXPERT_GROUPS, 0)
        hit = ig == gi
        gmask = jnp.where(hit, 1.0, gmask)
        gs = jnp.where(hit, NEG, gs)
    emask = jnp.dot(eg_ref[...], gmask.astype(BF16), preferred_element_type=F32) > 0.5

    masked = jnp.where(emask, sel, NEG)
    ie = lax.broadcasted_iota(jnp.int32, masked.shape, 0)
    chosen = jnp.zeros(masked.shape, F32)
    idxs, vals = [], []
    for _ in range(TOP_K):
        _, ei = _first_argmax(masked, ie, N_EXPERTS, 0)
        hit = ie == ei
        idxs.append(ei)
        vals.append(jnp.sum(jnp.where(hit, scores, 0.0), axis=0, keepdims=True))
        chosen = jnp.where(hit, 1.0, chosen)
        masked = jnp.where(hit, NEG, masked)
    total = vals[0]
    for v in vals[1:]:
        total = total + v

    prefix = jnp.dot(chosen.astype(BF16), tri_ref[...], preferred_element_type=F32)
    pos = carry_ref[:, 0:1] + prefix - 1.0
    ranks = [jnp.sum(jnp.where(ie == ei, pos, 0.0), axis=0, keepdims=True) for ei in idxs]
    carry = carry_ref[...] + jnp.sum(chosen, axis=1, keepdims=True)
    carry_ref[...] = carry
    cnt_ref[...] = carry

    eidx_ref[...] = jnp.concatenate(idxs, axis=0)
    gate_ref[...] = jnp.concatenate([v / total * ROUTED_SCALE for v in vals], axis=0)
    rank_ref[...] = jnp.concatenate(ranks, axis=0).astype(jnp.int32)


def _route(x, rw_hi, rw_lo, bias, eg, tri, tm=ROUTE_TM):
    n, d = x.shape
    tm = min(tm, n)
    const = lambda a: pl.BlockSpec(a.shape, lambda i: (0,) * a.ndim)
    col = pl.BlockSpec((TOP_K, tm), lambda i: (0, i))
    return pl.pallas_call(
        _route_kernel,
        grid=(n // tm,),
        in_specs=[pl.BlockSpec((tm, d), lambda i: (i, 0)),
                  const(rw_hi), const(rw_lo), const(bias), const(eg), const(tri)],
        out_specs=[col, col, col, pl.BlockSpec((N_EXPERTS, LANES), lambda i: (0, 0))],
        out_shape=[jax.ShapeDtypeStruct((TOP_K, n), jnp.int32),
                   jax.ShapeDtypeStruct((TOP_K, n), F32),
                   jax.ShapeDtypeStruct((TOP_K, n), jnp.int32),
                   jax.ShapeDtypeStruct((N_EXPERTS, LANES), F32)],
        scratch_shapes=[pltpu.VMEM((N_EXPERTS, LANES), F32)],
        compiler_params=_cparams(("arbitrary",)),
        name="route",
    )(x, rw_hi, rw_lo, bias, eg, tri)


def _sc_gather_rows(src, idx, sub, chunk=SC_CHUNK):
    s = src.shape[0] // sub
    m = idx.shape[0]
    info = plsc.get_sparse_core_info()
    n_workers = info.num_cores * info.num_subcores
    per_worker = m // n_workers
    n_chunks = per_worker // chunk
    assert n_chunks * chunk * n_workers == m
    mesh = plsc.VectorSubcoreMesh(core_axis_name="c", subcore_axis_name="s")

    @functools.partial(
        pl.kernel, mesh=mesh,
        out_type=jax.ShapeDtypeStruct((m, sub, LANES), src.dtype),
        scratch_types=[pltpu.VMEM((n_chunks, chunk), jnp.int32),
                       pltpu.VMEM((chunk, sub, LANES), src.dtype),
                       pltpu.SemaphoreType.DMA])
    def gather(src_hbm, idx_hbm, out_hbm, idx_v, rows_v, sem):
        wid = lax.axis_index("s") * info.num_cores + lax.axis_index("c")
        pltpu.sync_copy(idx_hbm.at[wid], idx_v)

        @pl.loop(0, n_chunks)
        def _(j):
            pltpu.async_copy(src_hbm.at[idx_v.at[j]], rows_v, sem).wait()
            pltpu.sync_copy(rows_v, out_hbm.at[pl.ds(wid * per_worker + j * chunk, chunk)])

    out = gather(src.reshape(s, sub, LANES), idx.reshape(n_workers, n_chunks, chunk))
    return out.reshape(m * sub, LANES)


def _sc_scatter_rows(src, dest, n_out, sub, chunk=SC_CHUNK):
    copies, n = dest.shape
    info = plsc.get_sparse_core_info()
    n_workers = info.num_cores * info.num_subcores
    per_worker = n // n_workers
    n_chunks = per_worker // chunk
    assert n_chunks * chunk * n_workers == n and copies * n == n_out
    mesh = plsc.VectorSubcoreMesh(core_axis_name="c", subcore_axis_name="s")
    idx = dest.reshape(copies, n_workers, n_chunks, chunk).transpose(1, 2, 0, 3)
    idx = idx.reshape(n_workers, n_chunks * copies, chunk)

    @functools.partial(
        pl.kernel, mesh=mesh,
        out_type=jax.ShapeDtypeStruct((n_out, sub, LANES), src.dtype),
        scratch_types=[pltpu.VMEM((n_chunks * copies, chunk), jnp.int32),
                       pltpu.VMEM((chunk, sub, LANES), src.dtype),
                       pltpu.SemaphoreType.DMA])
    def scatter(src_hbm, idx_hbm, out_hbm, idx_v, rows_v, sem):
        wid = lax.axis_index("s") * info.num_cores + lax.axis_index("c")
        pltpu.sync_copy(idx_hbm.at[wid], idx_v)

        @pl.loop(0, n_chunks)
        def _(j):
            pltpu.sync_copy(src_hbm.at[pl.ds(wid * per_worker + j * chunk, chunk)], rows_v)
            for c in range(copies):
                pltpu.async_copy(rows_v, out_hbm.at[idx_v.at[j * copies + c]], sem).wait()

    out = scatter(src.reshape(n, sub, LANES), idx)
    return out.reshape(n_out * sub, LANES)


def _gmm_kernel(be_ref, nu_ref, x_ref, wg_ref, wu_ref, wd_ref, o_ref, wgu_s, wd_s, *, hidden, blk, sub):
    b = pl.program_id(0)
    live = b < nu_ref[0]

    @pl.when(live & ((b == 0) | (be_ref[b] != be_ref[jnp.maximum(b - 1, 0)])))
    def _():
        wgu_s[:, :hidden] = wg_ref[...].astype(BF16)
        wgu_s[:, hidden:] = wu_ref[...].astype(BF16)
        wd_s[...] = wd_ref[...].astype(BF16)

    @pl.when(live)
    def _():
        x = _load_rows(x_ref, blk, sub).astype(BF16)
        gu = jnp.dot(x, wgu_s[...], preferred_element_type=F32)
        g, u = gu[:, :hidden], gu[:, hidden:]
        h = (g * jax.nn.sigmoid(g) * u).astype(BF16)
        _store_rows(o_ref, jnp.dot(h, wd_s[...], preferred_element_type=F32))


def _gmm(xs, w_gate, w_up, w_down, layer, blk_e, n_used, blk=MOE_BLK):
    hidden, d = w_down.shape[2:]
    sub = _row_words(d)
    n_rows = xs.shape[0] // sub
    row = lambda b, be, nu: (jnp.minimum(b, nu[0] - 1), 0)
    expert = lambda b, be, nu: (layer, be[b], 0, 0)
    return pl.pallas_call(
        functools.partial(_gmm_kernel, hidden=hidden, blk=blk, sub=sub),
        grid_spec=pltpu.PrefetchScalarGridSpec(
            num_scalar_prefetch=2,
            grid=(n_rows // blk,),
            in_specs=[pl.BlockSpec((blk * sub, LANES), row),
                      pl.BlockSpec((None, None, d, hidden), expert),
                      pl.BlockSpec((None, None, d, hidden), expert),
                      pl.BlockSpec((None, None, hidden, d), expert)],
            out_specs=pl.BlockSpec((blk * sub, LANES), row),
            scratch_shapes=[pltpu.VMEM((d, 2 * hidden), BF16), pltpu.VMEM((hidden, d), BF16)]),
        out_shape=jax.ShapeDtypeStruct((n_rows * sub, LANES), jnp.uint32),
        compiler_params=_cparams(("arbitrary",)),
        name="expert_gmm",
    )(blk_e, n_used, xs, w_gate, w_up, w_down)


def _combine_kernel(x_ref, yg_ref, gate_ref, wgu_ref, wd_ref, g_ref, b_ref, o_ref, *, alpha, hidden):
    x = x_ref[...]
    gu = jnp.dot(x.astype(BF16), wgu_ref[...], preferred_element_type=F32)
    g, u = gu[:, :hidden], gu[:, hidden:]
    h = (g * jax.nn.sigmoid(g) * u).astype(BF16)
    acc = jnp.dot(h, wd_ref[...], preferred_element_type=F32)
    gates = gate_ref[...]
    tm, d = x.shape
    for k in range(TOP_K):
        acc = acc + gates[:, k:k + 1] * _load_rows(yg_ref.at[k], tm, _row_words(d))
    o_ref[...] = _layer_norm(alpha * x + acc, g_ref[...], b_ref[...])


def _combine(x, yg, gates_t, w_gu, w_down, ln_g, ln_b, alpha, tm=COMBINE_TM):
    n, d = x.shape
    tm = min(tm, n)
    sub = _row_words(d)
    hidden = w_down.shape[0]
    const = lambda a: pl.BlockSpec(a.shape, lambda i: (0,) * a.ndim)
    return pl.pallas_call(
        functools.partial(_combine_kernel, alpha=alpha, hidden=hidden),
        grid=(n // tm,),
        in_specs=[pl.BlockSpec((tm, d), lambda i: (i, 0)),
                  pl.BlockSpec((TOP_K, tm * sub, LANES), lambda i: (0, i, 0)),
                  pl.BlockSpec((tm, TOP_K), lambda i: (i, 0)),
                  const(w_gu), const(w_down), const(ln_g), const(ln_b)],
        out_specs=pl.BlockSpec((tm, d), lambda i: (i, 0)),
        out_shape=jax.ShapeDtypeStruct((n, d), F32),
        compiler_params=_cparams(("arbitrary",)),
        name="combine",
    )(x, yg, gates_t, w_gu, w_down, ln_g, ln_b)


def _token_mixer(x, batch, w_in, b_in, s5p, w_glu, w_br_ssm, w_br_attn, w_out, expand, ln_g, ln_b, alpha):
    n, d = x.shape
    seq = n // batch
    proj = _inproj(x, w_in, b_in)
    y_ssm = _s5(proj, *s5p, w_glu, batch)

    qkv0 = SSM_WIDTH
    n_grp = len(ATTN_PATTERNS)
    qkv_w = n_grp * ATTN_WIDTH
    outs, lses = [], []
    for g, (window, dil) in enumerate(ATTN_PATTERNS):
        assert window // dil == QBLK and (seq // dil) % QBLK == 0

        def classes(col0):
            t = proj[:, col0 + g * ATTN_WIDTH: col0 + (g + 1) * ATTN_WIDTH]
            t = t.reshape(seq // dil, dil * batch, ATTN_WIDTH)
            return jnp.swapaxes(t, 0, 1)

        o_g, lse_g = _attn(classes(qkv0), classes(qkv0 + qkv_w), classes(qkv0 + 2 * qkv_w),
                           window // dil)
        outs.append(jnp.swapaxes(o_g, 0, 1).reshape(n, ATTN_WIDTH))
        lses.append(jnp.swapaxes(lse_g, 0, 1).reshape(n, HEADS))
    lse = jnp.concatenate(lses, axis=-1)
    return _merge(x, y_ssm, outs[0], outs[1], outs[2], lse, proj, w_br_ssm, w_br_attn, w_out,
                  expand, ln_g, ln_b, alpha)


def _moe(x, x_rows, rw_hi, rw_lo, bias, eg, tri, w_gate, w_up, w_down, layer, sh_gu, sh_down, ln_g, ln_b, alpha):
    n, d = x.shape
    sub = _row_words(d)
    eidx, gates, rank, cnt = _route(x, rw_hi, rw_lo, bias, eg, tri)
    counts = cnt[:, 0].astype(jnp.int32)
    nk = n * TOP_K
    n_rows = nk + N_EXPERTS * MOE_BLK
    n_blocks = n_rows // MOE_BLK
    pad_counts = (counts + MOE_BLK - 1) // MOE_BLK * MOE_BLK
    pad_end = jnp.cumsum(pad_counts)
    pad_start = pad_end - pad_counts
    onehot = eidx[:, :, None] == jnp.arange(N_EXPERTS, dtype=jnp.int32)
    dest = jnp.sum(jnp.where(onehot, pad_start, 0), axis=-1) + rank
    seg_start = jnp.concatenate([pad_start + counts, pad_end[-1:]])
    seg_len = jnp.concatenate([pad_counts - counts, n_rows - pad_end[-1:]])
    seg_end = jnp.cumsum(seg_len)
    q = jnp.arange(n, dtype=jnp.int32)[:, None]
    seg = jnp.sum((q >= seg_end[None, :]).astype(jnp.int32), axis=1, keepdims=True)
    seg_hot = seg == jnp.arange(N_EXPERTS + 1, dtype=jnp.int32)[None, :]
    pad_dest = jnp.sum(jnp.where(seg_hot, seg_start - (seg_end - seg_len), 0), axis=1) + q[:, 0]
    dest_all = jnp.concatenate([dest, pad_dest[None, :]], axis=0).astype(jnp.int32)
    blk_first = jnp.arange(n_blocks, dtype=jnp.int32)[:, None] * MOE_BLK
    blk_e = jnp.minimum(jnp.sum((blk_first >= pad_end[None, :]).astype(jnp.int32), axis=1), N_EXPERTS - 1)
    n_used = (pad_end[-1:] // MOE_BLK).astype(jnp.int32)

    xs = _sc_scatter_rows(x_rows, dest_all, n_rows, sub)
    ys = _gmm(xs, w_gate, w_up, w_down, layer, blk_e, n_used)
    yg = _sc_gather_rows(ys, dest.reshape(nk), sub).reshape(TOP_K, n * sub, LANES)
    return _combine(x, yg, gates.T, sh_gu, sh_down, ln_g, ln_b, alpha)


def kernel(x, w_in, b_in, ssm_lam_re, ssm_lam_im, ssm_log_dt, ssm_b_re, ssm_b_im, ssm_c_re, ssm_c_im, ssm_d, w_glu, w_br_ssm, w_br_attn, w_out, ln1_g, ln1_b, router_w, router_bias, exp_w_gate, exp_w_up, exp_w_down, sh_w_gate, sh_w_up, sh_w_down, ln2_g, ln2_b):
    batch, seq, d = x.shape
    depth = w_in.shape[0]
    assert batch == SUBLANES
    alpha = (2 * depth) ** 0.25
    n = batch * seq
    xt = jnp.swapaxes(x, 0, 1).reshape(n, d)

    expand = jnp.repeat(jnp.eye(HEADS, dtype=BF16), HEAD_DIM, axis=1)
    eg = jnp.repeat(jnp.eye(N_EXPERT_GROUPS, dtype=BF16), GROUP_SIZE, axis=0)
    tm = min(ROUTE_TM, n)
    tri = (jnp.arange(tm)[:, None] <= jnp.arange(tm)[None, :]).astype(BF16)
    row = lambda a: a.astype(F32).reshape(1, -1)

    for l in range(depth):
        s5p = _s5_params(ssm_lam_re[l], ssm_lam_im[l], ssm_log_dt[l], ssm_b_re[l], ssm_b_im[l],
                         ssm_c_re[l], ssm_c_im[l], ssm_d[l])
        xt, xt_rows = _token_mixer(xt, batch, w_in[l].astype(BF16), row(b_in[l]), s5p,
                                   w_glu[l].astype(BF16), w_br_ssm[l].astype(BF16),
                                   w_br_attn[l].astype(BF16), w_out[l].astype(BF16),
                                   expand, row(ln1_g[l]), row(ln1_b[l]), alpha)
        rwt = router_w[l].astype(F32).T
        rw_hi = rwt.astype(BF16)
        rw_lo = (rwt - rw_hi.astype(F32)).astype(BF16)
        sh_gu = jnp.concatenate([sh_w_gate[l], sh_w_up[l]], axis=-1).astype(BF16)
        xt = _moe(xt, xt_rows, rw_hi, rw_lo, router_bias[l].astype(F32).reshape(-1, 1), eg, tri,
                  exp_w_gate, exp_w_up, exp_w_down, l, sh_gu, sh_w_down[l].astype(BF16),
                  row(ln2_g[l]), row(ln2_b[l]), alpha)
    return jnp.swapaxes(xt.reshape(seq, batch, d), 0, 1)
```

```python
import functools
import math

import jax
import jax.numpy as jnp
import numpy as np
from jax import lax
from jax.experimental import pallas as pl
from jax.experimental.pallas import tpu as pltpu
from jax.experimental.pallas import tpu_sc as plsc

F32 = jnp.float32
BF16 = jnp.bfloat16

SSM_GROUP = 16
SSM_GROUPS = 32
SSM_WIDTH = SSM_GROUP * SSM_GROUPS
SSM_STATE = 64
HEAD_DIM = 64
HEADS = 8
ATTN_PATTERNS = ((128, 1), (512, 4), (2048, 16))
ATTN_WIDTH = HEADS * HEAD_DIM
QBLK = 128
N_EXPERTS = 64
TOP_K = 8
N_EXPERT_GROUPS = 8
GROUP_SIZE = N_EXPERTS // N_EXPERT_GROUPS
TOPK_GROUPS = 4
ROUTED_SCALE = 2.5
LN_EPS = 1e-5
NEG = -1e30

LANES = 128
SUBLANES = 8
VMEM_LIMIT = 56 * 1024 * 1024

INPROJ_TM = 1024
INPROJ_TN = 1024
S5_T = 128
S5_KB = 4
MERGE_TM = 512
ROUTE_TM = 512
MOE_BLK = 512
MOE_PARTS = 2
SC_CHUNK = 128
HI_HALF = np.uint32(0xFFFF0000)
COMBINE_TM = 256


def _cparams(sem):
    return pltpu.CompilerParams(dimension_semantics=sem, vmem_limit_bytes=VMEM_LIMIT)


def _inproj_kernel(x_ref, w_ref, b_ref, o_ref, xb_ref):
    @pl.when(pl.program_id(1) == 0)
    def _():
        xb_ref[...] = x_ref[...].astype(BF16)

    acc = jnp.dot(xb_ref[...], w_ref[...], preferred_element_type=F32)
    o_ref[...] = (acc + b_ref[...]).astype(BF16)


def _inproj(x, w, b, tm=INPROJ_TM, tn=INPROJ_TN):
    n, d = x.shape
    width = w.shape[1]
    tm = min(tm, n)
    return pl.pallas_call(
        _inproj_kernel,
        grid=(n // tm, width // tn),
        in_specs=[pl.BlockSpec((tm, d), lambda i, j: (i, 0)),
                  pl.BlockSpec((d, tn), lambda i, j: (0, j)),
                  pl.BlockSpec((1, tn), lambda i, j: (0, j))],
        out_specs=pl.BlockSpec((tm, tn), lambda i, j: (i, j)),
        out_shape=jax.ShapeDtypeStruct((n, width), BF16),
        scratch_shapes=[pltpu.VMEM((tm, d), BF16)],
        compiler_params=_cparams(("arbitrary", "arbitrary")),
        name="inproj",
    )(x, w, b)


def _gelu_tanh(x):
    c = math.sqrt(2.0 / math.pi)
    return 0.5 * x * (1.0 + jnp.tanh(c * (x + 0.044715 * (x * x * x))))


def _s5_kernel(u_ref, bm_ref, cre_ref, cim_ref, are_ref, aim_ref, d_ref, wglu_ref,
               o_ref, sre_ref, sim_ref, st_re_ref, st_im_ref, y_ref, *, steps, batch):
    kw = SSM_WIDTH // S5_KB
    sw = SSM_GROUPS * SSM_STATE // S5_KB

    @pl.when(pl.program_id(0) == 0)
    def _():
        st_re_ref[...] = jnp.zeros_like(st_re_ref)
        st_im_ref[...] = jnp.zeros_like(st_im_ref)

    for k in range(S5_KB):
        ls = slice(k * sw, (k + 1) * sw)
        bu = jnp.dot(u_ref[:, k * kw:(k + 1) * kw], bm_ref[k], preferred_element_type=F32)
        sre_ref[:, ls] = bu[:, :sw]
        sim_ref[:, ls] = bu[:, sw:]

        ar = jnp.broadcast_to(are_ref[:, ls], (batch, sw))
        ai = jnp.broadcast_to(aim_ref[:, ls], (batch, sw))

        def step(t, carry):
            sr, si = carry
            rows = pl.ds(pl.multiple_of(t * batch, batch), batch)
            nr = ar * sr - ai * si + sre_ref[rows, ls]
            ni = ar * si + ai * sr + sim_ref[rows, ls]
            sre_ref[rows, ls] = nr
            sim_ref[rows, ls] = ni
            return nr, ni

        sr, si = lax.fori_loop(0, steps, step, (st_re_ref[:, ls], st_im_ref[:, ls]))
        st_re_ref[:, ls] = sr
        st_im_ref[:, ls] = si

        y_ref[:, k * kw:(k + 1) * kw] = (
            jnp.dot(sre_ref[:, ls].astype(BF16), cre_ref[k], preferred_element_type=F32)
            + jnp.dot(sim_ref[:, ls].astype(BF16), cim_ref[k], preferred_element_type=F32))

    y = y_ref[...] + d_ref[...] * u_ref[...].astype(F32)
    y = _gelu_tanh(y)
    z = jnp.dot(y.astype(BF16), wglu_ref[...], preferred_element_type=F32)
    o_ref[...] = (y * jax.nn.sigmoid(z)).astype(BF16)


def _s5(proj, bm, cre, cim, a_re, a_im, d_skip, w_glu, batch, steps=S5_T):
    n = proj.shape[0]
    seq = n // batch
    steps = min(steps, seq)
    rows = steps * batch
    nstate = SSM_GROUPS * SSM_STATE
    const = lambda shape: pl.BlockSpec(shape, lambda i: (0,) * len(shape))
    return pl.pallas_call(
        functools.partial(_s5_kernel, steps=steps, batch=batch),
        grid=(seq // steps,),
        in_specs=[pl.BlockSpec((rows, SSM_WIDTH), lambda i: (i, 0)),
                  const(bm.shape), const(cre.shape), const(cim.shape),
                  const(a_re.shape), const(a_im.shape), const(d_skip.shape), const(w_glu.shape)],
        out_specs=pl.BlockSpec((rows, SSM_WIDTH), lambda i: (i, 0)),
        out_shape=jax.ShapeDtypeStruct((n, SSM_WIDTH), BF16),
        scratch_shapes=[pltpu.VMEM((rows, nstate), F32), pltpu.VMEM((rows, nstate), F32),
                        pltpu.VMEM((batch, nstate), F32), pltpu.VMEM((batch, nstate), F32),
                        pltpu.VMEM((rows, SSM_WIDTH), F32)],
        compiler_params=_cparams(("arbitrary",)),
        name="s5",
    )(proj, bm, cre, cim, a_re, a_im, d_skip, w_glu)


def _s5_params(lam_re, lam_im, log_dt, b_re, b_im, c_re, c_im, d_skip):
    lam = lax.complex(lam_re.astype(F32), lam_im.astype(F32))
    dt = jnp.exp(log_dt.astype(F32))[:, None]
    lam_bar = jnp.exp(lam * dt)
    b_bar = ((lam_bar - 1.0) / lam)[:, :, None] * lax.complex(b_re.astype(F32), b_im.astype(F32))
    gl = SSM_GROUPS // S5_KB
    eye = jnp.eye(gl, dtype=F32)

    def in_blocks(b):
        b = b.reshape(S5_KB, gl, SSM_STATE, SSM_GROUP)
        m = jnp.einsum('kgpc,gh->kgchp', b, eye)
        return m.reshape(S5_KB, gl * SSM_GROUP, gl * SSM_STATE)

    def out_blocks(c):
        c = c.reshape(S5_KB, gl, SSM_GROUP, SSM_STATE)
        m = jnp.einsum('kgcp,gh->kgphc', c, eye)
        return m.reshape(S5_KB, gl * SSM_STATE, gl * SSM_GROUP)

    bm = jnp.concatenate([in_blocks(jnp.real(b_bar)), in_blocks(jnp.imag(b_bar))], axis=-1).astype(BF16)
    cre = out_blocks(c_re.astype(F32)).astype(BF16)
    cim = out_blocks(-c_im.astype(F32)).astype(BF16)
    a_re = jnp.real(lam_bar).reshape(1, -1)
    a_im = jnp.imag(lam_bar).reshape(1, -1)
    return bm, cre, cim, a_re, a_im, d_skip.astype(F32).reshape(1, -1)


def _attn_kernel(q_ref, kp_ref, kc_ref, vp_ref, vc_ref, o_ref, lse_ref, *, steps):
    jb = pl.program_id(1)
    q = q_ref[...] * (HEAD_DIM ** -0.5)
    k2 = jnp.concatenate([kp_ref[...], kc_ref[...]], axis=0)
    v2 = jnp.concatenate([vp_ref[...], vc_ref[...]], axis=0)
    qi = lax.broadcasted_iota(jnp.int32, (2 * QBLK, 2 * QBLK), 0) % QBLK
    kj = lax.broadcasted_iota(jnp.int32, (2 * QBLK, 2 * QBLK), 1)
    dist = qi + QBLK - kj
    allowed = (dist >= 0) & (dist <= steps) & ((kj >= QBLK) | (jb > 0))
    first = lax.broadcasted_iota(jnp.int32, (QBLK, LANES), 1) < HEAD_DIM
    zero = jnp.zeros((), BF16)
    outs, lses = [], []
    for slab in range(ATTN_WIDTH // LANES):
        ls = slice(slab * LANES, (slab + 1) * LANES)
        qs = jnp.concatenate([jnp.where(first, q[:, ls], zero), jnp.where(first, zero, q[:, ls])], axis=0)
        s = lax.dot_general(qs, k2[:, ls], (((1,), (1,)), ((), ())), preferred_element_type=F32)
        s = jnp.where(allowed, s, NEG)
        m = jnp.max(s, axis=-1, keepdims=True)
        p = jnp.exp(s - m)
        denom = jnp.sum(p, axis=-1, keepdims=True)
        r = jnp.dot(p.astype(BF16), v2[:, ls], preferred_element_type=F32) / denom
        outs.append(jnp.where(first, r[:QBLK], r[QBLK:]))
        lse = m + jnp.log(denom)
        lses += [lse[:QBLK], lse[QBLK:]]
    o_ref[...] = jnp.concatenate(outs, axis=-1).astype(BF16)
    lse_ref[...] = jnp.concatenate(lses, axis=-1)


def _attn(q, k, v, steps):
    classes, n, width = q.shape
    nb = n // QBLK
    cur = pl.BlockSpec((None, QBLK, width), lambda c, j: (c, j, 0))
    prev = pl.BlockSpec((None, QBLK, width), lambda c, j: (c, jnp.maximum(j - 1, 0), 0))
    return pl.pallas_call(
        functools.partial(_attn_kernel, steps=steps),
        grid=(classes, nb),
        in_specs=[cur, prev, cur, prev, cur],
        out_specs=[cur, pl.BlockSpec((None, QBLK, HEADS), lambda c, j: (c, j, 0))],
        out_shape=[jax.ShapeDtypeStruct((classes, n, width), BF16),
                   jax.ShapeDtypeStruct((classes, n, HEADS), F32)],
        compiler_params=_cparams(("arbitrary", "arbitrary")),
        name="attn",
    )(q, k, k, v, v)


def _layer_norm(z, g, b):
    mu = jnp.mean(z, axis=-1, keepdims=True)
    zc = z - mu
    var = jnp.mean(zc * zc, axis=-1, keepdims=True)
    return zc * lax.rsqrt(var + LN_EPS) * g + b


def _row_words(d):
    return d // (2 * LANES)


def _store_rows(ref, y):
    rows, d = y.shape
    sub = _row_words(d)
    for s in range(sub):
        lo = y[:, (2 * s) * LANES:(2 * s + 1) * LANES].astype(BF16).astype(F32)
        hi = y[:, (2 * s + 1) * LANES:(2 * s + 2) * LANES].astype(BF16).astype(F32)
        word = (pltpu.bitcast(lo, jnp.uint32) >> 16) | (pltpu.bitcast(hi, jnp.uint32) & HI_HALF)
        ref[pl.ds(s, rows, stride=sub), :] = word


def _load_rows(ref, rows, sub):
    parts = []
    for s in range(sub):
        word = ref[pl.ds(s, rows, stride=sub), :]
        parts.append(pltpu.bitcast(word << 16, F32))
        parts.append(pltpu.bitcast(word & HI_HALF, F32))
    return jnp.concatenate(parts, axis=-1)


def _expand_heads(w, e_ref):
    hi = w.astype(BF16)
    lo = (w - hi.astype(F32)).astype(BF16)
    return (jnp.dot(hi, e_ref[...], preferred_element_type=F32)
            + jnp.dot(lo, e_ref[...], preferred_element_type=F32))


def _merge_kernel(x_ref, yssm_ref, o0_ref, o1_ref, o2_ref, lse_ref, gs_ref, ga_ref,
                  wbs_ref, wba_ref, wout_ref, e_ref, g_ref, b_ref, o_ref, orow_ref, *, alpha):
    lse = lse_ref[...]
    l0, l1, l2 = lse[:, 0:HEADS], lse[:, HEADS:2 * HEADS], lse[:, 2 * HEADS:3 * HEADS]
    m = jnp.maximum(jnp.maximum(l0, l1), l2)
    e0, e1, e2 = jnp.exp(l0 - m), jnp.exp(l1 - m), jnp.exp(l2 - m)
    den = e0 + e1 + e2
    y_attn = (_expand_heads(e0 / den, e_ref) * o0_ref[...].astype(F32)
              + _expand_heads(e1 / den, e_ref) * o1_ref[...].astype(F32)
              + _expand_heads(e2 / den, e_ref) * o2_ref[...].astype(F32))
    ya = jnp.dot(y_attn.astype(BF16), wba_ref[...], preferred_element_type=F32)
    ys = jnp.dot(yssm_ref[...], wbs_ref[...], preferred_element_type=F32)
    merged = (jax.nn.sigmoid(gs_ref[...].astype(F32)) * ys
              + jax.nn.sigmoid(ga_ref[...].astype(F32)) * ya)
    mix = jnp.dot(merged.astype(BF16), wout_ref[...], preferred_element_type=F32)
    y = _layer_norm(alpha * x_ref[...] + mix, g_ref[...], b_ref[...])
    o_ref[...] = y
    _store_rows(orow_ref, y)


def _merge(x, y_ssm, o0, o1, o2, lse, proj, w_br_ssm, w_br_attn, w_out, expand, ln_g, ln_b,
           alpha, tm=MERGE_TM):
    n, d = x.shape
    tm = min(tm, n)
    gate_blk = proj.shape[1] // d
    row = lambda w: pl.BlockSpec((tm, w), lambda i: (i, 0))
    const = lambda a: pl.BlockSpec(a.shape, lambda i: (0,) * a.ndim)
    return pl.pallas_call(
        functools.partial(_merge_kernel, alpha=alpha),
        grid=(n // tm,),
        in_specs=[row(d), row(SSM_WIDTH), row(ATTN_WIDTH), row(ATTN_WIDTH), row(ATTN_WIDTH),
                  row(lse.shape[1]),
                  pl.BlockSpec((tm, d), lambda i: (i, gate_blk - 2)),
                  pl.BlockSpec((tm, d), lambda i: (i, gate_blk - 1)),
                  const(w_br_ssm), const(w_br_attn), const(w_out), const(expand),
                  const(ln_g), const(ln_b)],
        out_specs=[row(d), pl.BlockSpec((tm * _row_words(d), LANES), lambda i: (i, 0))],
        out_shape=[jax.ShapeDtypeStruct((n, d), F32),
                   jax.ShapeDtypeStruct((n * _row_words(d), LANES), jnp.uint32)],
        compiler_params=_cparams(("arbitrary",)),
        name="merge",
    )(x, y_ssm, o0, o1, o2, lse, proj, proj, w_br_ssm, w_br_attn, w_out, expand, ln_g, ln_b)


def _first_argmax(v, iota, size, axis):
    m = jnp.max(v, axis=axis, keepdims=True)
    idx = jnp.min(jnp.where(v == m, iota, size), axis=axis, keepdims=True)
    return m, idx


def _route_kernel(x_ref, rwh_ref, rwl_ref, bias_ref, eg_ref, tri_ref,
                  eidx_ref, gate_ref, rank_ref, cnt_ref, carry_ref):
    @pl.when(pl.program_id(0) == 0)
    def _():
        carry_ref[...] = jnp.zeros_like(carry_ref)

    x = x_ref[...]
    tm = x.shape[0]
    xh = x.astype(BF16)
    xl = (x - xh.astype(F32)).astype(BF16)
    nt = (((1,), (1,)), ((), ()))
    logits = (lax.dot_general(rwh_ref[...], xh, nt, preferred_element_type=F32)
              + lax.dot_general(rwh_ref[...], xl, nt, preferred_element_type=F32)
              + lax.dot_general(rwl_ref[...], xh, nt, preferred_element_type=F32))
    scores = jax.nn.sigmoid(logits)
    sel = scores + bias_ref[...]

    sel3 = sel.reshape(N_EXPERT_GROUPS, GROUP_SIZE, tm)
    iw = lax.broadcasted_iota(jnp.int32, sel3.shape, 1)
    m1, i1 = _first_argmax(sel3, iw, GROUP_SIZE, 1)
    m2 = jnp.max(jnp.where(iw == i1, NEG, sel3), axis=1, keepdims=True)
    gs = (m1 + m2).reshape(N_EXPERT_GROUPS, tm)

    ig = lax.broadcasted_iota(jnp.int32, gs.shape, 0)
    gmask = jnp.zeros(gs.shape, F32)
    for _ in range(TOPK_GROUPS):
        _, gi = _first_argmax(gs, ig, N_EXPERT_GROUPS, 0)
        hit = ig == gi
        gmask = jnp.where(hit, 1.0, gmask)
        gs = jnp.where(hit, NEG, gs)
    emask = jnp.dot(eg_ref[...], gmask.astype(BF16), preferred_element_type=F32) > 0.5

    masked = jnp.where(emask, sel, NEG)
    ie = lax.broadcasted_iota(jnp.int32, masked.shape, 0)
    chosen = jnp.zeros(masked.shape, F32)
    idxs, vals = [], []
    for _ in range(TOP_K):
        _, ei = _first_argmax(masked, ie, N_EXPERTS, 0)
        hit = ie == ei
        idxs.append(ei)
        vals.append(jnp.sum(jnp.where(hit, scores, 0.0), axis=0, keepdims=True))
        chosen = jnp.where(hit, 1.0, chosen)
        masked = jnp.where(hit, NEG, masked)
    total = vals[0]
    for v in vals[1:]:
        total = total + v

    prefix = jnp.dot(chosen.astype(BF16), tri_ref[...], preferred_element_type=F32)
    pos = carry_ref[:, 0:1] + prefix - 1.0
    ranks = [jnp.sum(jnp.where(ie == ei, pos, 0.0), axis=0, keepdims=True) for ei in idxs]
    carry = carry_ref[...] + jnp.sum(chosen, axis=1, keepdims=True)
    carry_ref[...] = carry
    cnt_ref[...] = carry

    eidx_ref[...] = jnp.concatenate(idxs, axis=0)
    gate_ref[...] = jnp.concatenate([v / total * ROUTED_SCALE for v in vals], axis=0)
    rank_ref[...] = jnp.concatenate(ranks, axis=0).astype(jnp.int32)


def _route(x, row0, n, rw_hi, rw_lo, bias, eg, tri, tm=ROUTE_TM):
    d = x.shape[1]
    tm = min(tm, n)
    blk0 = row0 // tm
    const = lambda a: pl.BlockSpec(a.shape, lambda i: (0,) * a.ndim)
    col = pl.BlockSpec((TOP_K, tm), lambda i: (0, i))
    return pl.pallas_call(
        _route_kernel,
        grid=(n // tm,),
        in_specs=[pl.BlockSpec((tm, d), lambda i: (i + blk0, 0)),
                  const(rw_hi), const(rw_lo), const(bias), const(eg), const(tri)],
        out_specs=[col, col, col, pl.BlockSpec((N_EXPERTS, LANES), lambda i: (0, 0))],
        out_shape=[jax.ShapeDtypeStruct((TOP_K, n), jnp.int32),
                   jax.ShapeDtypeStruct((TOP_K, n), F32),
                   jax.ShapeDtypeStruct((TOP_K, n), jnp.int32),
                   jax.ShapeDtypeStruct((N_EXPERTS, LANES), F32)],
        scratch_shapes=[pltpu.VMEM((N_EXPERTS, LANES), F32)],
        compiler_params=_cparams(("arbitrary",)),
        name="route",
    )(x, rw_hi, rw_lo, bias, eg, tri)


def _sc_gather_rows(src, idx, sub, chunk=SC_CHUNK):
    s = src.shape[0] // sub
    m = idx.shape[0]
    info = plsc.get_sparse_core_info()
    n_workers = info.num_cores * info.num_subcores
    per_worker = m // n_workers
    n_chunks = per_worker // chunk
    assert n_chunks * chunk * n_workers == m
    mesh = plsc.VectorSubcoreMesh(core_axis_name="c", subcore_axis_name="s")

    @functools.partial(
        pl.kernel, mesh=mesh,
        out_type=jax.ShapeDtypeStruct((m, sub, LANES), src.dtype),
        scratch_types=[pltpu.VMEM((n_chunks, chunk), jnp.int32),
                       pltpu.VMEM((chunk, sub, LANES), src.dtype),
                       pltpu.SemaphoreType.DMA])
    def gather(src_hbm, idx_hbm, out_hbm, idx_v, rows_v, sem):
        wid = lax.axis_index("s") * info.num_cores + lax.axis_index("c")
        pltpu.sync_copy(idx_hbm.at[wid], idx_v)

        @pl.loop(0, n_chunks)
        def _(j):
            pltpu.async_copy(src_hbm.at[idx_v.at[j]], rows_v, sem).wait()
            pltpu.sync_copy(rows_v, out_hbm.at[pl.ds(wid * per_worker + j * chunk, chunk)])

    out = gather(src.reshape(s, sub, LANES), idx.reshape(n_workers, n_chunks, chunk))
    return out.reshape(m * sub, LANES)


def _sc_scatter_rows(src, row0, dest, n_out, sub, chunk=SC_CHUNK):
    copies, n = dest.shape
    s = src.shape[0] // sub
    info = plsc.get_sparse_core_info()
    n_workers = info.num_cores * info.num_subcores
    per_worker = n // n_workers
    n_chunks = per_worker // chunk
    assert n_chunks * chunk * n_workers == n and copies * n == n_out
    mesh = plsc.VectorSubcoreMesh(core_axis_name="c", subcore_axis_name="s")
    idx = dest.reshape(copies, n_workers, n_chunks, chunk).transpose(1, 2, 0, 3)
    idx = idx.reshape(n_workers, n_chunks * copies, chunk)

    @functools.partial(
        pl.kernel, mesh=mesh,
        out_type=jax.ShapeDtypeStruct((n_out, sub, LANES), src.dtype),
        scratch_types=[pltpu.VMEM((n_chunks * copies, chunk), jnp.int32),
                       pltpu.VMEM((chunk, sub, LANES), src.dtype),
                       pltpu.SemaphoreType.DMA])
    def scatter(src_hbm, idx_hbm, out_hbm, idx_v, rows_v, sem):
        wid = lax.axis_index("s") * info.num_cores + lax.axis_index("c")
        pltpu.sync_copy(idx_hbm.at[wid], idx_v)

        @pl.loop(0, n_chunks)
        def _(j):
            pltpu.sync_copy(src_hbm.at[pl.ds(row0 + wid * per_worker + j * chunk, chunk)], rows_v)
            for c in range(copies):
                pltpu.async_copy(rows_v, out_hbm.at[idx_v.at[j * copies + c]], sem).wait()

    out = scatter(src.reshape(s, sub, LANES), idx)
    return out.reshape(n_out * sub, LANES)


def _gmm_kernel(be_ref, nu_ref, x_ref, wg_ref, wu_ref, wd_ref, o_ref, wgu_s, wd_s, *, hidden, blk, sub):
    b = pl.program_id(0)
    live = b < nu_ref[0]

    @pl.when(live & ((b == 0) | (be_ref[b] != be_ref[jnp.maximum(b - 1, 0)])))
    def _():
        wgu_s[:, :hidden] = wg_ref[...].astype(BF16)
        wgu_s[:, hidden:] = wu_ref[...].astype(BF16)
        wd_s[...] = wd_ref[...].astype(BF16)

    @pl.when(live)
    def _():
        x = _load_rows(x_ref, blk, sub).astype(BF16)
        gu = jnp.dot(x, wgu_s[...], preferred_element_type=F32)
        g, u = gu[:, :hidden], gu[:, hidden:]
        h = (g * jax.nn.sigmoid(g) * u).astype(BF16)
        _store_rows(o_ref, jnp.dot(h, wd_s[...], preferred_element_type=F32))


def _gmm(xs, w_gate, w_up, w_down, layer, blk_e, n_used, blk=MOE_BLK):
    hidden, d = w_down.shape[2:]
    sub = _row_words(d)
    n_rows = xs.shape[0] // sub
    row = lambda b, be, nu: (jnp.minimum(b, nu[0] - 1), 0)
    expert = lambda b, be, nu: (layer, be[b], 0, 0)
    return pl.pallas_call(
        functools.partial(_gmm_kernel, hidden=hidden, blk=blk, sub=sub),
        grid_spec=pltpu.PrefetchScalarGridSpec(
            num_scalar_prefetch=2,
            grid=(n_rows // blk,),
            in_specs=[pl.BlockSpec((blk * sub, LANES), row),
                      pl.BlockSpec((None, None, d, hidden), expert),
                      pl.BlockSpec((None, None, d, hidden), expert),
                      pl.BlockSpec((None, None, hidden, d), expert)],
            out_specs=pl.BlockSpec((blk * sub, LANES), row),
            scratch_shapes=[pltpu.VMEM((d, 2 * hidden), BF16), pltpu.VMEM((hidden, d), BF16)]),
        out_shape=jax.ShapeDtypeStruct((n_rows * sub, LANES), jnp.uint32),
        compiler_params=_cparams(("arbitrary",)),
        name="expert_gmm",
    )(blk_e, n_used, xs, w_gate, w_up, w_down)


def _combine_kernel(x_ref, yg_ref, gate_ref, wgu_ref, wd_ref, g_ref, b_ref, *rest, alpha, hidden):
    o_ref = rest[-1]
    x = x_ref[...]
    gu = jnp.dot(x.astype(BF16), wgu_ref[...], preferred_element_type=F32)
    g, u = gu[:, :hidden], gu[:, hidden:]
    h = (g * jax.nn.sigmoid(g) * u).astype(BF16)
    acc = jnp.dot(h, wd_ref[...], preferred_element_type=F32)
    gates = gate_ref[...]
    tm, d = x.shape
    for k in range(TOP_K):
        acc = acc + gates[:, k:k + 1] * _load_rows(yg_ref.at[k], tm, _row_words(d))
    o_ref[...] = _layer_norm(alpha * x + acc, g_ref[...], b_ref[...])


def _combine(x, row0, yg, gates_t, w_gu, w_down, ln_g, ln_b, alpha, out_prev=None, tm=COMBINE_TM):
    n, d = x.shape
    n_part = gates_t.shape[0]
    tm = min(tm, n_part)
    blk0 = row0 // tm
    sub = _row_words(d)
    hidden = w_down.shape[0]
    const = lambda a: pl.BlockSpec(a.shape, lambda i: (0,) * a.ndim)
    kern = functools.partial(_combine_kernel, alpha=alpha, hidden=hidden)
    in_specs = [pl.BlockSpec((tm, d), lambda i: (i + blk0, 0)),
                pl.BlockSpec((TOP_K, tm * sub, LANES), lambda i: (0, i, 0)),
                pl.BlockSpec((tm, TOP_K), lambda i: (i, 0)),
                const(w_gu), const(w_down), const(ln_g), const(ln_b)]
    args = [x, yg, gates_t, w_gu, w_down, ln_g, ln_b]
    aliases = {}
    if out_prev is not None:
        in_specs.append(pl.BlockSpec(memory_space=pl.ANY))
        args.append(out_prev)
        aliases = {7: 0}
    return pl.pallas_call(
        kern,
        grid=(n_part // tm,),
        in_specs=in_specs,
        out_specs=pl.BlockSpec((tm, d), lambda i: (i + blk0, 0)),
        out_shape=jax.ShapeDtypeStruct((n, d), F32),
        input_output_aliases=aliases,
        compiler_params=_cparams(("arbitrary",)),
        name="combine",
    )(*args)


def _token_mixer(x, batch, w_in, b_in, s5p, w_glu, w_br_ssm, w_br_attn, w_out, expand, ln_g, ln_b, alpha):
    n, d = x.shape
    seq = n // batch
    proj = _inproj(x, w_in, b_in)
    y_ssm = _s5(proj, *s5p, w_glu, batch)

    qkv0 = SSM_WIDTH
    n_grp = len(ATTN_PATTERNS)
    qkv_w = n_grp * ATTN_WIDTH
    outs, lses = [], []
    for g, (window, dil) in enumerate(ATTN_PATTERNS):
        assert window // dil == QBLK and (seq // dil) % QBLK == 0

        def classes(col0):
            t = proj[:, col0 + g * ATTN_WIDTH: col0 + (g + 1) * ATTN_WIDTH]
            t = t.reshape(seq // dil, dil * batch, ATTN_WIDTH)
            return jnp.swapaxes(t, 0, 1)

        o_g, lse_g = _attn(classes(qkv0), classes(qkv0 + qkv_w), classes(qkv0 + 2 * qkv_w),
                           window // dil)
        outs.append(jnp.swapaxes(o_g, 0, 1).reshape(n, ATTN_WIDTH))
        lses.append(jnp.swapaxes(lse_g, 0, 1).reshape(n, HEADS))
    lse = jnp.concatenate(lses, axis=-1)
    return _merge(x, y_ssm, outs[0], outs[1], outs[2], lse, proj, w_br_ssm, w_br_attn, w_out,
                  expand, ln_g, ln_b, alpha)


def _moe_part(x, x_rows, row0, n, out_prev, rw_hi, rw_lo, bias, eg, tri, w_gate, w_up, w_down, layer,
              sh_gu, sh_down, ln_g, ln_b, alpha):
    d = x.shape[1]
    sub = _row_words(d)
    eidx, gates, rank, cnt = _route(x, row0, n, rw_hi, rw_lo, bias, eg, tri)
    counts = cnt[:, 0].astype(jnp.int32)
    nk = n * TOP_K
    n_pad = N_EXPERTS * MOE_BLK
    n_rows = nk + n_pad
    n_blocks = n_rows // MOE_BLK
    assert n_pad % n == 0
    pad_counts = (counts + MOE_BLK - 1) // MOE_BLK * MOE_BLK
    pad_end = jnp.cumsum(pad_counts)
    pad_start = pad_end - pad_counts
    onehot = eidx[:, :, None] == jnp.arange(N_EXPERTS, dtype=jnp.int32)
    dest = jnp.sum(jnp.where(onehot, pad_start, 0), axis=-1) + rank
    seg_start = jnp.concatenate([pad_start + counts, pad_end[-1:]])
    seg_len = jnp.concatenate([pad_counts - counts, n_rows - pad_end[-1:]])
    seg_end = jnp.cumsum(seg_len)
    q = jnp.arange(n_pad, dtype=jnp.int32)[:, None]
    seg = jnp.sum((q >= seg_end[None, :]).astype(jnp.int32), axis=1, keepdims=True)
    seg_hot = seg == jnp.arange(N_EXPERTS + 1, dtype=jnp.int32)[None, :]
    pad_dest = jnp.sum(jnp.where(seg_hot, seg_start - (seg_end - seg_len), 0), axis=1) + q[:, 0]
    dest_all = jnp.concatenate([dest, pad_dest.reshape(n_pad // n, n)], axis=0).astype(jnp.int32)
    blk_first = jnp.arange(n_blocks, dtype=jnp.int32)[:, None] * MOE_BLK
    blk_e = jnp.minimum(jnp.sum((blk_first >= pad_end[None, :]).astype(jnp.int32), axis=1), N_EXPERTS - 1)
    n_used = (pad_end[-1:] // MOE_BLK).astype(jnp.int32)

    xs = _sc_scatter_rows(x_rows, row0, dest_all, n_rows, sub)
    ys = _gmm(xs, w_gate, w_up, w_down, layer, blk_e, n_used)
    yg = _sc_gather_rows(ys, dest.reshape(nk), sub).reshape(TOP_K, n * sub, LANES)
    return _combine(x, row0, yg, gates.T, sh_gu, sh_down, ln_g, ln_b, alpha, out_prev)


def _moe(x, x_rows, *params):
    n = x.shape[0]
    part = n // MOE_PARTS
    out = None
    for i in range(MOE_PARTS):
        out = _moe_part(x, x_rows, i * part, part, out, *params)
    return out


def kernel(x, w_in, b_in, ssm_lam_re, ssm_lam_im, ssm_log_dt, ssm_b_re, ssm_b_im, ssm_c_re, ssm_c_im, ssm_d, w_glu, w_br_ssm, w_br_attn, w_out, ln1_g, ln1_b, router_w, router_bias, exp_w_gate, exp_w_up, exp_w_down, sh_w_gate, sh_w_up, sh_w_down, ln2_g, ln2_b):
    batch, seq, d = x.shape
    depth = w_in.shape[0]
    assert batch == SUBLANES
    alpha = (2 * depth) ** 0.25
    n = batch * seq
    xt = jnp.swapaxes(x, 0, 1).reshape(n, d)

    expand = jnp.repeat(jnp.eye(HEADS, dtype=BF16), HEAD_DIM, axis=1)
    eg = jnp.repeat(jnp.eye(N_EXPERT_GROUPS, dtype=BF16), GROUP_SIZE, axis=0)
    tm = min(ROUTE_TM, n)
    tri = (jnp.arange(tm)[:, None] <= jnp.arange(tm)[None, :]).astype(BF16)
    row = lambda a: a.astype(F32).reshape(1, -1)

    for l in range(depth):
        s5p = _s5_params(ssm_lam_re[l], ssm_lam_im[l], ssm_log_dt[l], ssm_b_re[l], ssm_b_im[l],
                         ssm_c_re[l], ssm_c_im[l], ssm_d[l])
        xt, xt_rows = _token_mixer(xt, batch, w_in[l].astype(BF16), row(b_in[l]), s5p,
                                   w_glu[l].astype(BF16), w_br_ssm[l].astype(BF16),
                                   w_br_attn[l].astype(BF16), w_out[l].astype(BF16),
                                   expand, row(ln1_g[l]), row(ln1_b[l]), alpha)
        rwt = router_w[l].astype(F32).T
        rw_hi = rwt.astype(BF16)
        rw_lo = (rwt - rw_hi.astype(F32)).astype(BF16)
        sh_gu = jnp.concatenate([sh_w_gate[l], sh_w_up[l]], axis=-1).astype(BF16)
        xt = _moe(xt, xt_rows, rw_hi, rw_lo, router_bias[l].astype(F32).reshape(-1, 1), eg, tri,
                  exp_w_gate, exp_w_up, exp_w_down, l, sh_gu, sh_w_down[l].astype(BF16),
                  row(ln2_g[l]), row(ln2_b[l]), alpha)
    return jnp.swapaxes(xt.reshape(seq, batch, d), 0, 1)
```

```python
import functools
import math

import jax
import jax.numpy as jnp
import numpy as np
from jax import lax
from jax.experimental import pallas as pl
from jax.experimental.pallas import tpu as pltpu
from jax.experimental.pallas import tpu_sc as plsc

F32 = jnp.float32
BF16 = jnp.bfloat16

SSM_GROUP = 16
SSM_GROUPS = 32
SSM_WIDTH = SSM_GROUP * SSM_GROUPS
SSM_STATE = 64
HEAD_DIM = 64
HEADS = 8
ATTN_PATTERNS = ((128, 1), (512, 4), (2048, 16))
ATTN_WIDTH = HEADS * HEAD_DIM
QBLK = 128
N_EXPERTS = 64
TOP_K = 8
N_EXPERT_GROUPS = 8
GROUP_SIZE = N_EXPERTS // N_EXPERT_GROUPS
TOPK_GROUPS = 4
ROUTED_SCALE = 2.5
LN_EPS = 1e-5
NEG = -1e30

LANES = 128
SUBLANES = 8
VMEM_LIMIT = 56 * 1024 * 1024

INPROJ_TM = 1024
INPROJ_TN = 1280
QKV_TM = 2048
S5_T = 128
S5_KB = 4
MERGE_TM = 512
ROUTE_TM = 512
MOE_BLK = 512
MOE_PARTS = 2
SC_CHUNK = 128
HI_HALF = np.uint32(0xFFFF0000)
COMBINE_TM = 256


def _cparams(sem):
    return pltpu.CompilerParams(dimension_semantics=sem, vmem_limit_bytes=VMEM_LIMIT)


def _inproj_kernel(x_ref, w_ref, b_ref, o_ref):
    acc = jnp.dot(x_ref[...], w_ref[...], preferred_element_type=F32)
    o_ref[...] = (acc + b_ref[...]).astype(BF16)


def _inproj(xb, w, b, tm=INPROJ_TM, tn=INPROJ_TN):
    n, d = xb.shape
    width = w.shape[1]
    tm = min(tm, n)
    return pl.pallas_call(
        _inproj_kernel,
        grid=(n // tm, width // tn),
        in_specs=[pl.BlockSpec((tm, d), lambda i, j: (i, 0)),
                  pl.BlockSpec((d, tn), lambda i, j: (0, j)),
                  pl.BlockSpec((1, tn), lambda i, j: (0, j))],
        out_specs=pl.BlockSpec((tm, tn), lambda i, j: (i, j)),
        out_shape=jax.ShapeDtypeStruct((n, width), BF16),
        compiler_params=_cparams(("arbitrary", "arbitrary")),
        name="inproj",
    )(xb, w, b)


def _inproj_qkv_kernel(x_ref, w_ref, b_ref, q_ref, k_ref, v_ref, y_ref, *, classes):
    x = x_ref[...]
    rows = x.shape[0] // classes
    for part, o_ref in enumerate((q_ref, k_ref, v_ref)):
        cols = slice(part * ATTN_WIDTH, (part + 1) * ATTN_WIDTH)
        y = jnp.dot(x, w_ref[:, cols], preferred_element_type=F32) + b_ref[:, cols]
        for s in range(ATTN_WIDTH // LANES):
            y_ref[s] = y[:, s * LANES:(s + 1) * LANES]

        def put(c, _):
            tiles = [y_ref[s, pl.ds(c, rows, stride=classes), :] for s in range(ATTN_WIDTH // LANES)]
            o_ref[c] = jnp.concatenate(tiles, axis=-1).astype(BF16)
            return 0

        lax.fori_loop(0, classes, put, 0)


def _inproj_qkv(xb, w, b, classes, tm=QKV_TM):
    n, d = xb.shape
    tm = min(tm, n)
    rows = tm // classes
    out = pl.BlockSpec((classes, rows, ATTN_WIDTH), lambda i: (0, i, 0))
    shape = jax.ShapeDtypeStruct((classes, n // classes, ATTN_WIDTH), BF16)
    return pl.pallas_call(
        functools.partial(_inproj_qkv_kernel, classes=classes),
        grid=(n // tm,),
        in_specs=[pl.BlockSpec((tm, d), lambda i: (i, 0)),
                  pl.BlockSpec(w.shape, lambda i: (0, 0)),
                  pl.BlockSpec(b.shape, lambda i: (0, 0))],
        out_specs=[out, out, out],
        out_shape=[shape, shape, shape],
        scratch_shapes=[pltpu.VMEM((ATTN_WIDTH // LANES, tm, LANES), F32)],
        compiler_params=_cparams(("arbitrary",)),
        name="inproj_qkv",
    )(xb, w, b)


def _gelu_tanh(x):
    c = math.sqrt(2.0 / math.pi)
    return 0.5 * x * (1.0 + jnp.tanh(c * (x + 0.044715 * (x * x * x))))


def _s5_kernel(u_ref, bm_ref, cre_ref, cim_ref, are_ref, aim_ref, d_ref, wglu_ref,
               o_ref, sre_ref, sim_ref, st_re_ref, st_im_ref, y_ref, *, steps, batch):
    kw = SSM_WIDTH // S5_KB
    sw = SSM_GROUPS * SSM_STATE // S5_KB

    @pl.when(pl.program_id(0) == 0)
    def _():
        st_re_ref[...] = jnp.zeros_like(st_re_ref)
        st_im_ref[...] = jnp.zeros_like(st_im_ref)

    for k in range(S5_KB):
        ls = slice(k * sw, (k + 1) * sw)
        bu = jnp.dot(u_ref[:, k * kw:(k + 1) * kw], bm_ref[k], preferred_element_type=F32)
        sre_ref[:, ls] = bu[:, :sw]
        sim_ref[:, ls] = bu[:, sw:]

        ar = jnp.broadcast_to(are_ref[:, ls], (batch, sw))
        ai = jnp.broadcast_to(aim_ref[:, ls], (batch, sw))

        def step(t, carry):
            sr, si = carry
            rows = pl.ds(pl.multiple_of(t * batch, batch), batch)
            nr = ar * sr - ai * si + sre_ref[rows, ls]
            ni = ar * si + ai * sr + sim_ref[rows, ls]
            sre_ref[rows, ls] = nr
            sim_ref[rows, ls] = ni
            return nr, ni

        sr, si = lax.fori_loop(0, steps, step, (st_re_ref[:, ls], st_im_ref[:, ls]))
        st_re_ref[:, ls] = sr
        st_im_ref[:, ls] = si

        y_ref[:, k * kw:(k + 1) * kw] = (
            jnp.dot(sre_ref[:, ls].astype(BF16), cre_ref[k], preferred_element_type=F32)
            + jnp.dot(sim_ref[:, ls].astype(BF16), cim_ref[k], preferred_element_type=F32))

    y = y_ref[...] + d_ref[...] * u_ref[...].astype(F32)
    y = _gelu_tanh(y)
    z = jnp.dot(y.astype(BF16), wglu_ref[...], preferred_element_type=F32)
    o_ref[...] = (y * jax.nn.sigmoid(z)).astype(BF16)


def _s5(proj, bm, cre, cim, a_re, a_im, d_skip, w_glu, batch, steps=S5_T):
    n = proj.shape[0]
    u_blk = proj.shape[1] // SSM_WIDTH - 1
    seq = n // batch
    steps = min(steps, seq)
    rows = steps * batch
    nstate = SSM_GROUPS * SSM_STATE
    const = lambda shape: pl.BlockSpec(shape, lambda i: (0,) * len(shape))
    return pl.pallas_call(
        functools.partial(_s5_kernel, steps=steps, batch=batch),
        grid=(seq // steps,),
        in_specs=[pl.BlockSpec((rows, SSM_WIDTH), lambda i: (i, u_blk)),
                  const(bm.shape), const(cre.shape), const(cim.shape),
                  const(a_re.shape), const(a_im.shape), const(d_skip.shape), const(w_glu.shape)],
        out_specs=pl.BlockSpec((rows, SSM_WIDTH), lambda i: (i, 0)),
        out_shape=jax.ShapeDtypeStruct((n, SSM_WIDTH), BF16),
        scratch_shapes=[pltpu.VMEM((rows, nstate), F32), pltpu.VMEM((rows, nstate), F32),
                        pltpu.VMEM((batch, nstate), F32), pltpu.VMEM((batch, nstate), F32),
                        pltpu.VMEM((rows, SSM_WIDTH), F32)],
        compiler_params=_cparams(("arbitrary",)),
        name="s5",
    )(proj, bm, cre, cim, a_re, a_im, d_skip, w_glu)


def _s5_params(lam_re, lam_im, log_dt, b_re, b_im, c_re, c_im, d_skip):
    lam = lax.complex(lam_re.astype(F32), lam_im.astype(F32))
    dt = jnp.exp(log_dt.astype(F32))[:, None]
    lam_bar = jnp.exp(lam * dt)
    b_bar = ((lam_bar - 1.0) / lam)[:, :, None] * lax.complex(b_re.astype(F32), b_im.astype(F32))
    gl = SSM_GROUPS // S5_KB
    eye = jnp.eye(gl, dtype=F32)

    def in_blocks(b):
        b = b.reshape(S5_KB, gl, SSM_STATE, SSM_GROUP)
        m = jnp.einsum('kgpc,gh->kgchp', b, eye)
        return m.reshape(S5_KB, gl * SSM_GROUP, gl * SSM_STATE)

    def out_blocks(c):
        c = c.reshape(S5_KB, gl, SSM_GROUP, SSM_STATE)
        m = jnp.einsum('kgcp,gh->kgphc', c, eye)
        return m.reshape(S5_KB, gl * SSM_STATE, gl * SSM_GROUP)

    bm = jnp.concatenate([in_blocks(jnp.real(b_bar)), in_blocks(jnp.imag(b_bar))], axis=-1).astype(BF16)
    cre = out_blocks(c_re.astype(F32)).astype(BF16)
    cim = out_blocks(-c_im.astype(F32)).astype(BF16)
    a_re = jnp.real(lam_bar).reshape(1, -1)
    a_im = jnp.imag(lam_bar).reshape(1, -1)
    return bm, cre, cim, a_re, a_im, d_skip.astype(F32).reshape(1, -1)


def _attn_kernel(q_ref, kp_ref, kc_ref, vp_ref, vc_ref, o_ref, lse_ref, *, steps):
    jb = pl.program_id(1)
    q = q_ref[...] * (HEAD_DIM ** -0.5)
    k2 = jnp.concatenate([kp_ref[...], kc_ref[...]], axis=0)
    v2 = jnp.concatenate([vp_ref[...], vc_ref[...]], axis=0)
    qi = lax.broadcasted_iota(jnp.int32, (2 * QBLK, 2 * QBLK), 0) % QBLK
    kj = lax.broadcasted_iota(jnp.int32, (2 * QBLK, 2 * QBLK), 1)
    dist = qi + QBLK - kj
    allowed = (dist >= 0) & (dist <= steps) & ((kj >= QBLK) | (jb > 0))
    first = lax.broadcasted_iota(jnp.int32, (QBLK, LANES), 1) < HEAD_DIM
    zero = jnp.zeros((), BF16)
    outs, lses = [], []
    for slab in range(ATTN_WIDTH // LANES):
        ls = slice(slab * LANES, (slab + 1) * LANES)
        qs = jnp.concatenate([jnp.where(first, q[:, ls], zero), jnp.where(first, zero, q[:, ls])], axis=0)
        s = lax.dot_general(qs, k2[:, ls], (((1,), (1,)), ((), ())), preferred_element_type=F32)
        s = jnp.where(allowed, s, NEG)
        m = jnp.max(s, axis=-1, keepdims=True)
        p = jnp.exp(s - m)
        denom = jnp.sum(p, axis=-1, keepdims=True)
        r = jnp.dot(p.astype(BF16), v2[:, ls], preferred_element_type=F32) / denom
        outs.append(jnp.where(first, r[:QBLK], r[QBLK:]))
        lse = m + jnp.log(denom)
        lses += [lse[:QBLK], lse[QBLK:]]
    o_ref[...] = jnp.concatenate(outs, axis=-1).astype(BF16)
    lse_ref[...] = jnp.concatenate(lses, axis=-1)


def _attn(q, k, v, steps):
    classes, n, width = q.shape
    nb = n // QBLK
    cur = pl.BlockSpec((None, QBLK, width), lambda c, j: (c, j, 0))
    prev = pl.BlockSpec((None, QBLK, width), lambda c, j: (c, jnp.maximum(j - 1, 0), 0))
    return pl.pallas_call(
        functools.partial(_attn_kernel, steps=steps),
        grid=(classes, nb),
        in_specs=[cur, prev, cur, prev, cur],
        out_specs=[cur, pl.BlockSpec((None, QBLK, HEADS), lambda c, j: (c, j, 0))],
        out_shape=[jax.ShapeDtypeStruct((classes, n, width), BF16),
                   jax.ShapeDtypeStruct((classes, n, HEADS), F32)],
        compiler_params=_cparams(("arbitrary", "arbitrary")),
        name="attn",
    )(q, k, k, v, v)


def _layer_norm(z, g, b):
    mu = jnp.mean(z, axis=-1, keepdims=True)
    zc = z - mu
    var = jnp.mean(zc * zc, axis=-1, keepdims=True)
    return zc * lax.rsqrt(var + LN_EPS) * g + b


def _row_words(d):
    return d // (2 * LANES)


def _store_rows(ref, y):
    rows, d = y.shape
    sub = _row_words(d)
    for s in range(sub):
        lo = y[:, (2 * s) * LANES:(2 * s + 1) * LANES].astype(BF16).astype(F32)
        hi = y[:, (2 * s + 1) * LANES:(2 * s + 2) * LANES].astype(BF16).astype(F32)
        word = (pltpu.bitcast(lo, jnp.uint32) >> 16) | (pltpu.bitcast(hi, jnp.uint32) & HI_HALF)
        ref[pl.ds(s, rows, stride=sub), :] = word


def _load_rows(ref, rows, sub):
    parts = []
    for s in range(sub):
        word = ref[pl.ds(s, rows, stride=sub), :]
        parts.append(pltpu.bitcast(word << 16, F32))
        parts.append(pltpu.bitcast(word & HI_HALF, F32))
    return jnp.concatenate(parts, axis=-1)


def _expand_heads(w, e_ref):
    hi = w.astype(BF16)
    lo = (w - hi.astype(F32)).astype(BF16)
    return (jnp.dot(hi, e_ref[...], preferred_element_type=F32)
            + jnp.dot(lo, e_ref[...], preferred_element_type=F32))


def _merge_kernel(x_ref, yssm_ref, o0_ref, o1_ref, o2_ref, lse_ref, gs_ref, ga_ref,
                  wbs_ref, wba_ref, wout_ref, e_ref, g_ref, b_ref, o_ref, orow_ref, *, alpha):
    lse = lse_ref[...]
    l0, l1, l2 = lse[:, 0:HEADS], lse[:, HEADS:2 * HEADS], lse[:, 2 * HEADS:3 * HEADS]
    m = jnp.maximum(jnp.maximum(l0, l1), l2)
    e0, e1, e2 = jnp.exp(l0 - m), jnp.exp(l1 - m), jnp.exp(l2 - m)
    den = e0 + e1 + e2
    y_attn = (_expand_heads(e0 / den, e_ref) * o0_ref[...].astype(F32)
              + _expand_heads(e1 / den, e_ref) * o1_ref[...].astype(F32)
              + _expand_heads(e2 / den, e_ref) * o2_ref[...].astype(F32))
    ya = jnp.dot(y_attn.astype(BF16), wba_ref[...], preferred_element_type=F32)
    ys = jnp.dot(yssm_ref[...], wbs_ref[...], preferred_element_type=F32)
    merged = (jax.nn.sigmoid(gs_ref[...].astype(F32)) * ys
              + jax.nn.sigmoid(ga_ref[...].astype(F32)) * ya)
    mix = jnp.dot(merged.astype(BF16), wout_ref[...], preferred_element_type=F32)
    y = _layer_norm(alpha * x_ref[...] + mix, g_ref[...], b_ref[...])
    o_ref[...] = y
    _store_rows(orow_ref, y)


def _merge(x, y_ssm, o0, o1, o2, lse, proj, w_br_ssm, w_br_attn, w_out, expand, ln_g, ln_b,
           alpha, tm=MERGE_TM):
    n, d = x.shape
    tm = min(tm, n)
    gate_blk = proj.shape[1] // d
    row = lambda w: pl.BlockSpec((tm, w), lambda i: (i, 0))
    const = lambda a: pl.BlockSpec(a.shape, lambda i: (0,) * a.ndim)
    return pl.pallas_call(
        functools.partial(_merge_kernel, alpha=alpha),
        grid=(n // tm,),
        in_specs=[row(d), row(SSM_WIDTH), row(ATTN_WIDTH), row(ATTN_WIDTH), row(ATTN_WIDTH),
                  row(lse.shape[1]),
                  pl.BlockSpec((tm, d), lambda i: (i, gate_blk - 2)),
                  pl.BlockSpec((tm, d), lambda i: (i, gate_blk - 1)),
                  const(w_br_ssm), const(w_br_attn), const(w_out), const(expand),
                  const(ln_g), const(ln_b)],
        out_specs=[row(d), pl.BlockSpec((tm * _row_words(d), LANES), lambda i: (i, 0))],
        out_shape=[jax.ShapeDtypeStruct((n, d), F32),
                   jax.ShapeDtypeStruct((n * _row_words(d), LANES), jnp.uint32)],
        compiler_params=_cparams(("arbitrary",)),
        name="merge",
    )(x, y_ssm, o0, o1, o2, lse, proj, proj, w_br_ssm, w_br_attn, w_out, expand, ln_g, ln_b)


def _first_argmax(v, iota, size, axis):
    m = jnp.max(v, axis=axis, keepdims=True)
    idx = jnp.min(jnp.where(v == m, iota, size), axis=axis, keepdims=True)
    return m, idx


def _route_kernel(x_ref, rwh_ref, rwl_ref, bias_ref, eg_ref, tri_ref,
                  eidx_ref, gate_ref, rank_ref, cnt_ref, carry_ref):
    @pl.when(pl.program_id(0) == 0)
    def _():
        carry_ref[...] = jnp.zeros_like(carry_ref)

    x = x_ref[...]
    tm = x.shape[0]
    xh = x.astype(BF16)
    xl = (x - xh.astype(F32)).astype(BF16)
    nt = (((1,), (1,)), ((), ()))
    logits = (lax.dot_general(rwh_ref[...], xh, nt, preferred_element_type=F32)
              + lax.dot_general(rwh_ref[...], xl, nt, preferred_element_type=F32)
              + lax.dot_general(rwl_ref[...], xh, nt, preferred_element_type=F32))
    scores = jax.nn.sigmoid(logits)
    sel = scores + bias_ref[...]

    sel3 = sel.reshape(N_EXPERT_GROUPS, GROUP_SIZE, tm)
    iw = lax.broadcasted_iota(jnp.int32, sel3.shape, 1)
    m1, i1 = _first_argmax(sel3, iw, GROUP_SIZE, 1)
    m2 = jnp.max(jnp.where(iw == i1, NEG, sel3), axis=1, keepdims=True)
    gs = (m1 + m2).reshape(N_EXPERT_GROUPS, tm)

    ig = lax.broadcasted_iota(jnp.int32, gs.shape, 0)
    gmask = jnp.zeros(gs.shape, F32)
    for _ in range(TOPK_GROUPS):
        _, gi = _first_argmax(gs, ig, N_EXPERT_GROUPS, 0)
        hit = ig == gi
        gmask = jnp.where(hit, 1.0, gmask)
        gs = jnp.where(hit, NEG, gs)
    emask = jnp.dot(eg_ref[...], gmask.astype(BF16), preferred_element_type=F32) > 0.5

    masked = jnp.where(emask, sel, NEG)
    ie = lax.broadcasted_iota(jnp.int32, masked.shape, 0)
    chosen = jnp.zeros(masked.shape, F32)
    idxs, vals = [], []
    for _ in range(TOP_K):
        _, ei = _first_argmax(masked, ie, N_EXPERTS, 0)
        hit = ie == ei
        idxs.append(ei)
        vals.append(jnp.sum(jnp.where(hit, scores, 0.0), axis=0, keepdims=True))
        chosen = jnp.where(hit, 1.0, chosen)
        masked = jnp.where(hit, NEG, masked)
    total = vals[0]
    for v in vals[1:]:
        total = total + v

    prefix = jnp.dot(chosen.astype(BF16), tri_ref[...], preferred_element_type=F32)
    pos = carry_ref[:, 0:1] + prefix - 1.0
    ranks = [jnp.sum(jnp.where(ie == ei, pos, 0.0), axis=0, keepdims=True) for ei in idxs]
    carry = carry_ref[...] + jnp.sum(chosen, axis=1, keepdims=True)
    carry_ref[...] = carry
    cnt_ref[...] = carry

    eidx_ref[...] = jnp.concatenate(idxs, axis=0)
    gate_ref[...] = jnp.concatenate([v / total * ROUTED_SCALE for v in vals], axis=0)
    rank_ref[...] = jnp.concatenate(ranks, axis=0).astype(jnp.int32)


def _route(x, row0, n, rw_hi, rw_lo, bias, eg, tri, tm=ROUTE_TM):
    d = x.shape[1]
    tm = min(tm, n)
    blk0 = row0 // tm
    const = lambda a: pl.BlockSpec(a.shape, lambda i: (0,) * a.ndim)
    col = pl.BlockSpec((TOP_K, tm), lambda i: (0, i))
    return pl.pallas_call(
        _route_kernel,
        grid=(n // tm,),
        in_specs=[pl.BlockSpec((tm, d), lambda i: (i + blk0, 0)),
                  const(rw_hi), const(rw_lo), const(bias), const(eg), const(tri)],
        out_specs=[col, col, col, pl.BlockSpec((N_EXPERTS, LANES), lambda i: (0, 0))],
        out_shape=[jax.ShapeDtypeStruct((TOP_K, n), jnp.int32),
                   jax.ShapeDtypeStruct((TOP_K, n), F32),
                   jax.ShapeDtypeStruct((TOP_K, n), jnp.int32),
                   jax.ShapeDtypeStruct((N_EXPERTS, LANES), F32)],
        scratch_shapes=[pltpu.VMEM((N_EXPERTS, LANES), F32)],
        compiler_params=_cparams(("arbitrary",)),
        name="route",
    )(x, rw_hi, rw_lo, bias, eg, tri)


def _sc_gather_rows(src, idx, sub, chunk=SC_CHUNK):
    s = src.shape[0] // sub
    m = idx.shape[0]
    info = plsc.get_sparse_core_info()
    n_workers = info.num_cores * info.num_subcores
    per_worker = m // n_workers
    n_chunks = per_worker // chunk
    assert n_chunks * chunk * n_workers == m
    mesh = plsc.VectorSubcoreMesh(core_axis_name="c", subcore_axis_name="s")

    @functools.partial(
        pl.kernel, mesh=mesh,
        out_type=jax.ShapeDtypeStruct((m, sub, LANES), src.dtype),
        scratch_types=[pltpu.VMEM((n_chunks, chunk), jnp.int32),
                       pltpu.VMEM((chunk, sub, LANES), src.dtype),
                       pltpu.SemaphoreType.DMA])
    def gather(src_hbm, idx_hbm, out_hbm, idx_v, rows_v, sem):
        wid = lax.axis_index("s") * info.num_cores + lax.axis_index("c")
        pltpu.sync_copy(idx_hbm.at[wid], idx_v)

        @pl.loop(0, n_chunks)
        def _(j):
            pltpu.async_copy(src_hbm.at[idx_v.at[j]], rows_v, sem).wait()
            pltpu.sync_copy(rows_v, out_hbm.at[pl.ds(wid * per_worker + j * chunk, chunk)])

    out = gather(src.reshape(s, sub, LANES), idx.reshape(n_workers, n_chunks, chunk))
    return out.reshape(m * sub, LANES)


def _sc_scatter_rows(src, row0, dest, n_out, sub, chunk=SC_CHUNK):
    copies, n = dest.shape
    s = src.shape[0] // sub
    info = plsc.get_sparse_core_info()
    n_workers = info.num_cores * info.num_subcores
    per_worker = n // n_workers
    n_chunks = per_worker // chunk
    assert n_chunks * chunk * n_workers == n and copies * n == n_out
    mesh = plsc.VectorSubcoreMesh(core_axis_name="c", subcore_axis_name="s")
    idx = dest.reshape(copies, n_workers, n_chunks, chunk).transpose(1, 2, 0, 3)
    idx = idx.reshape(n_workers, n_chunks * copies, chunk)

    @functools.partial(
        pl.kernel, mesh=mesh,
        out_type=jax.ShapeDtypeStruct((n_out, sub, LANES), src.dtype),
        scratch_types=[pltpu.VMEM((n_chunks * copies, chunk), jnp.int32),
                       pltpu.VMEM((chunk, sub, LANES), src.dtype),
                       pltpu.SemaphoreType.DMA])
    def scatter(src_hbm, idx_hbm, out_hbm, idx_v, rows_v, sem):
        wid = lax.axis_index("s") * info.num_cores + lax.axis_index("c")
        pltpu.sync_copy(idx_hbm.at[wid], idx_v)

        @pl.loop(0, n_chunks)
        def _(j):
            pltpu.sync_copy(src_hbm.at[pl.ds(row0 + wid * per_worker + j * chunk, chunk)], rows_v)
            for c in range(copies):
                pltpu.async_copy(rows_v, out_hbm.at[idx_v.at[j * copies + c]], sem).wait()

    out = scatter(src.reshape(s, sub, LANES), idx)
    return out.reshape(n_out * sub, LANES)


def _gmm_kernel(be_ref, nu_ref, x_ref, wg_ref, wu_ref, wd_ref, o_ref, wgu_s, wd_s, *, hidden, blk, sub):
    b = pl.program_id(0)
    live = b < nu_ref[0]

    @pl.when(live & ((b == 0) | (be_ref[b] != be_ref[jnp.maximum(b - 1, 0)])))
    def _():
        wgu_s[:, :hidden] = wg_ref[...].astype(BF16)
        wgu_s[:, hidden:] = wu_ref[...].astype(BF16)
        wd_s[...] = wd_ref[...].astype(BF16)

    @pl.when(live)
    def _():
        x = _load_rows(x_ref, blk, sub).astype(BF16)
        gu = jnp.dot(x, wgu_s[...], preferred_element_type=F32)
        g, u = gu[:, :hidden], gu[:, hidden:]
        h = (g * jax.nn.sigmoid(g) * u).astype(BF16)
        _store_rows(o_ref, jnp.dot(h, wd_s[...], preferred_element_type=F32))


def _gmm(xs, w_gate, w_up, w_down, layer, blk_e, n_used, blk=MOE_BLK):
    hidden, d = w_down.shape[2:]
    sub = _row_words(d)
    n_rows = xs.shape[0] // sub
    row = lambda b, be, nu: (jnp.minimum(b, nu[0] - 1), 0)
    expert = lambda b, be, nu: (layer, be[b], 0, 0)
    return pl.pallas_call(
        functools.partial(_gmm_kernel, hidden=hidden, blk=blk, sub=sub),
        grid_spec=pltpu.PrefetchScalarGridSpec(
            num_scalar_prefetch=2,
            grid=(n_rows // blk,),
            in_specs=[pl.BlockSpec((blk * sub, LANES), row),
                      pl.BlockSpec((None, None, d, hidden), expert),
                      pl.BlockSpec((None, None, d, hidden), expert),
                      pl.BlockSpec((None, None, hidden, d), expert)],
            out_specs=pl.BlockSpec((blk * sub, LANES), row),
            scratch_shapes=[pltpu.VMEM((d, 2 * hidden), BF16), pltpu.VMEM((hidden, d), BF16)]),
        out_shape=jax.ShapeDtypeStruct((n_rows * sub, LANES), jnp.uint32),
        compiler_params=_cparams(("arbitrary",)),
        name="expert_gmm",
    )(blk_e, n_used, xs, w_gate, w_up, w_down)


def _combine_kernel(x_ref, yg_ref, gate_ref, wgu_ref, wd_ref, g_ref, b_ref, *rest, alpha, hidden):
    o_ref, ob_ref = rest[-2:]
    x = x_ref[...]
    gu = jnp.dot(x.astype(BF16), wgu_ref[...], preferred_element_type=F32)
    g, u = gu[:, :hidden], gu[:, hidden:]
    h = (g * jax.nn.sigmoid(g) * u).astype(BF16)
    acc = jnp.dot(h, wd_ref[...], preferred_element_type=F32)
    gates = gate_ref[...]
    tm, d = x.shape
    for k in range(TOP_K):
        acc = acc + gates[:, k:k + 1] * _load_rows(yg_ref.at[k], tm, _row_words(d))
    y = _layer_norm(alpha * x + acc, g_ref[...], b_ref[...])
    o_ref[...] = y
    ob_ref[...] = y.astype(BF16)


def _combine(x, row0, yg, gates_t, w_gu, w_down, ln_g, ln_b, alpha, prev=None, tm=COMBINE_TM):
    n, d = x.shape
    n_part = gates_t.shape[0]
    tm = min(tm, n_part)
    blk0 = row0 // tm
    sub = _row_words(d)
    hidden = w_down.shape[0]
    const = lambda a: pl.BlockSpec(a.shape, lambda i: (0,) * a.ndim)
    rows = pl.BlockSpec((tm, d), lambda i: (i + blk0, 0))
    in_specs = [rows,
                pl.BlockSpec((TOP_K, tm * sub, LANES), lambda i: (0, i, 0)),
                pl.BlockSpec((tm, TOP_K), lambda i: (i, 0)),
                const(w_gu), const(w_down), const(ln_g), const(ln_b)]
    args = [x, yg, gates_t, w_gu, w_down, ln_g, ln_b]
    aliases = {}
    if prev is not None:
        aliases = {len(args): 0, len(args) + 1: 1}
        in_specs += [pl.BlockSpec(memory_space=pl.ANY)] * 2
        args += list(prev)
    return pl.pallas_call(
        functools.partial(_combine_kernel, alpha=alpha, hidden=hidden),
        grid=(n_part // tm,),
        in_specs=in_specs,
        out_specs=[rows, rows],
        out_shape=[jax.ShapeDtypeStruct((n, d), F32), jax.ShapeDtypeStruct((n, d), BF16)],
        input_output_aliases=aliases,
        compiler_params=_cparams(("arbitrary",)),
        name="combine",
    )(*args)


def _split_w_in(w_in, b_in, d):
    qkv0 = SSM_WIDTH
    qkv_w = len(ATTN_PATTERNS) * ATTN_WIDTH
    gates0 = qkv0 + 3 * qkv_w
    cols = lambda a, c0, w: a[:, c0:c0 + w]
    pick = lambda a: jnp.concatenate([cols(a, gates0, 2 * d), cols(a, 0, SSM_WIDTH)], axis=1)
    main = (pick(w_in).astype(BF16), pick(b_in))
    groups = []
    for g in range(len(ATTN_PATTERNS)):
        sel = lambda a: jnp.concatenate([cols(a, qkv0 + s * qkv_w + g * ATTN_WIDTH, ATTN_WIDTH)
                                         for s in range(3)], axis=1)
        groups.append((sel(w_in).astype(BF16), sel(b_in)))
    return main, groups


def _token_mixer(x, xb, batch, w_in, b_in, s5p, w_glu, w_br_ssm, w_br_attn, w_out, expand, ln_g, ln_b, alpha):
    n, d = x.shape
    seq = n // batch
    (w_main, b_main), groups = _split_w_in(w_in, b_in, d)
    proj = _inproj(xb, w_main, b_main)
    y_ssm = _s5(proj, *s5p, w_glu, batch)

    outs, lses = [], []
    for (window, dil), (w_g, b_g) in zip(ATTN_PATTERNS, groups):
        assert window // dil == QBLK and (seq // dil) % QBLK == 0
        q, k, v = _inproj_qkv(xb, w_g, b_g, dil * batch)
        o_g, lse_g = _attn(q, k, v, window // dil)
        outs.append(jnp.swapaxes(o_g, 0, 1).reshape(n, ATTN_WIDTH))
        lses.append(jnp.swapaxes(lse_g, 0, 1).reshape(n, HEADS))
    lse = jnp.concatenate(lses, axis=-1)
    return _merge(x, y_ssm, outs[0], outs[1], outs[2], lse, proj, w_br_ssm, w_br_attn, w_out,
                  expand, ln_g, ln_b, alpha)


def _moe_part(x, x_rows, row0, n, out_prev, rw_hi, rw_lo, bias, eg, tri, w_gate, w_up, w_down, layer,
              sh_gu, sh_down, ln_g, ln_b, alpha):
    d = x.shape[1]
    sub = _row_words(d)
    eidx, gates, rank, cnt = _route(x, row0, n, rw_hi, rw_lo, bias, eg, tri)
    counts = cnt[:, 0].astype(jnp.int32)
    nk = n * TOP_K
    n_pad = N_EXPERTS * MOE_BLK
    n_rows = nk + n_pad
    n_blocks = n_rows // MOE_BLK
    assert n_pad % n == 0
    pad_counts = (counts + MOE_BLK - 1) // MOE_BLK * MOE_BLK
    pad_end = jnp.cumsum(pad_counts)
    pad_start = pad_end - pad_counts
    onehot = eidx[:, :, None] == jnp.arange(N_EXPERTS, dtype=jnp.int32)
    dest = jnp.sum(jnp.where(onehot, pad_start, 0), axis=-1) + rank
    seg_start = jnp.concatenate([pad_start + counts, pad_end[-1:]])
    seg_len = jnp.concatenate([pad_counts - counts, n_rows - pad_end[-1:]])
    seg_end = jnp.cumsum(seg_len)
    q = jnp.arange(n_pad, dtype=jnp.int32)[:, None]
    seg = jnp.sum((q >= seg_end[None, :]).astype(jnp.int32), axis=1, keepdims=True)
    seg_hot = seg == jnp.arange(N_EXPERTS + 1, dtype=jnp.int32)[None, :]
    pad_dest = jnp.sum(jnp.where(seg_hot, seg_start - (seg_end - seg_len), 0), axis=1) + q[:, 0]
    dest_all = jnp.concatenate([dest, pad_dest.reshape(n_pad // n, n)], axis=0).astype(jnp.int32)
    blk_first = jnp.arange(n_blocks, dtype=jnp.int32)[:, None] * MOE_BLK
    blk_e = jnp.minimum(jnp.sum((blk_first >= pad_end[None, :]).astype(jnp.int32), axis=1), N_EXPERTS - 1)
    n_used = (pad_end[-1:] // MOE_BLK).astype(jnp.int32)

    xs = _sc_scatter_rows(x_rows, row0, dest_all, n_rows, sub)
    ys = _gmm(xs, w_gate, w_up, w_down, layer, blk_e, n_used)
    yg = _sc_gather_rows(ys, dest.reshape(nk), sub).reshape(TOP_K, n * sub, LANES)
    return _combine(x, row0, yg, gates.T, sh_gu, sh_down, ln_g, ln_b, alpha, out_prev)


def _moe(x, x_rows, *params):
    n = x.shape[0]
    part = n // MOE_PARTS
    out = None
    for i in range(MOE_PARTS):
        out = _moe_part(x, x_rows, i * part, part, out, *params)
    return out


def kernel(x, w_in, b_in, ssm_lam_re, ssm_lam_im, ssm_log_dt, ssm_b_re, ssm_b_im, ssm_c_re, ssm_c_im, ssm_d, w_glu, w_br_ssm, w_br_attn, w_out, ln1_g, ln1_b, router_w, router_bias, exp_w_gate, exp_w_up, exp_w_down, sh_w_gate, sh_w_up, sh_w_down, ln2_g, ln2_b):
    batch, seq, d = x.shape
    depth = w_in.shape[0]
    assert batch == SUBLANES
    alpha = (2 * depth) ** 0.25
    n = batch * seq
    xt = jnp.swapaxes(x, 0, 1).reshape(n, d)
    xb = xt.astype(BF16)

    expand = jnp.repeat(jnp.eye(HEADS, dtype=BF16), HEAD_DIM, axis=1)
    eg = jnp.repeat(jnp.eye(N_EXPERT_GROUPS, dtype=BF16), GROUP_SIZE, axis=0)
    tm = min(ROUTE_TM, n)
    tri = (jnp.arange(tm)[:, None] <= jnp.arange(tm)[None, :]).astype(BF16)
    row = lambda a: a.astype(F32).reshape(1, -1)

    for l in range(depth):
        s5p = _s5_params(ssm_lam_re[l], ssm_lam_im[l], ssm_log_dt[l], ssm_b_re[l], ssm_b_im[l],
                         ssm_c_re[l], ssm_c_im[l], ssm_d[l])
        xt, xt_rows = _token_mixer(xt, xb, batch, w_in[l], row(b_in[l]), s5p,
                                   w_glu[l].astype(BF16), w_br_ssm[l].astype(BF16),
                                   w_br_attn[l].astype(BF16), w_out[l].astype(BF16),
                                   expand, row(ln1_g[l]), row(ln1_b[l]), alpha)
        rwt = router_w[l].astype(F32).T
        rw_hi = rwt.astype(BF16)
        rw_lo = (rwt - rw_hi.astype(F32)).astype(BF16)
        sh_gu = jnp.concatenate([sh_w_gate[l], sh_w_up[l]], axis=-1).astype(BF16)
        xt, xb = _moe(xt, xt_rows, rw_hi, rw_lo, router_bias[l].astype(F32).reshape(-1, 1), eg, tri,
                  exp_w_gate, exp_w_up, exp_w_down, l, sh_gu, sh_w_down[l].astype(BF16),
                  row(ln2_g[l]), row(ln2_b[l]), alpha)
    return jnp.swapaxes(xt.reshape(seq, batch, d), 0, 1)
```

```python
import functools
import math

import jax
import jax.numpy as jnp
import numpy as np
from jax import lax
from jax.experimental import pallas as pl
from jax.experimental.pallas import tpu as pltpu
from jax.experimental.pallas import tpu_sc as plsc

F32 = jnp.float32
BF16 = jnp.bfloat16

SSM_GROUP = 16
SSM_GROUPS = 32
SSM_WIDTH = SSM_GROUP * SSM_GROUPS
SSM_STATE = 64
HEAD_DIM = 64
HEADS = 8
ATTN_PATTERNS = ((128, 1), (512, 4), (2048, 16))
ATTN_WIDTH = HEADS * HEAD_DIM
QBLK = 128
N_EXPERTS = 64
TOP_K = 8
N_EXPERT_GROUPS = 8
GROUP_SIZE = N_EXPERTS // N_EXPERT_GROUPS
TOPK_GROUPS = 4
ROUTED_SCALE = 2.5
LN_EPS = 1e-5
NEG = -1e30

LANES = 128
SUBLANES = 8
VMEM_LIMIT = 56 * 1024 * 1024

INPROJ_TM = 1024
INPROJ_TN = 1280
QKV_TM = 2048
S5_T = 128
S5_KB = 4
MERGE_TM = 512
ROUTE_TM = 512
MOE_BLK = 512
MOE_PARTS = 2
SC_CHUNK = 128
HI_HALF = np.uint32(0xFFFF0000)
COMBINE_TM = 256


def _cparams(sem):
    return pltpu.CompilerParams(dimension_semantics=sem, vmem_limit_bytes=VMEM_LIMIT)


def _inproj_kernel(x_ref, w_ref, b_ref, o_ref):
    acc = jnp.dot(x_ref[...], w_ref[...], preferred_element_type=F32)
    o_ref[...] = (acc + b_ref[...]).astype(BF16)


def _inproj(xb, w, b, tm=INPROJ_TM, tn=INPROJ_TN):
    n, d = xb.shape
    width = w.shape[1]
    tm = min(tm, n)
    return pl.pallas_call(
        _inproj_kernel,
        grid=(n // tm, width // tn),
        in_specs=[pl.BlockSpec((tm, d), lambda i, j: (i, 0)),
                  pl.BlockSpec((d, tn), lambda i, j: (0, j)),
                  pl.BlockSpec((1, tn), lambda i, j: (0, j))],
        out_specs=pl.BlockSpec((tm, tn), lambda i, j: (i, j)),
        out_shape=jax.ShapeDtypeStruct((n, width), BF16),
        compiler_params=_cparams(("arbitrary", "arbitrary")),
        name="inproj",
    )(xb, w, b)


def _inproj_qkv_kernel(x_ref, w_ref, b_ref, q_ref, k_ref, v_ref, y_ref, *, classes):
    x = x_ref[...]
    rows = x.shape[0] // classes
    for part, o_ref in enumerate((q_ref, k_ref, v_ref)):
        cols = slice(part * ATTN_WIDTH, (part + 1) * ATTN_WIDTH)
        y = jnp.dot(x, w_ref[:, cols], preferred_element_type=F32) + b_ref[:, cols]
        y = pltpu.einshape("jcf->cjf", y.reshape(rows, classes, ATTN_WIDTH))
        o_ref[...] = y.astype(BF16)


def _inproj_qkv(xb, w, b, classes, tm=QKV_TM):
    n, d = xb.shape
    tm = min(tm, n)
    rows = tm // classes
    out = pl.BlockSpec((classes, rows, ATTN_WIDTH), lambda i: (0, i, 0))
    shape = jax.ShapeDtypeStruct((classes, n // classes, ATTN_WIDTH), BF16)
    return pl.pallas_call(
        functools.partial(_inproj_qkv_kernel, classes=classes),
        grid=(n // tm,),
        in_specs=[pl.BlockSpec((tm, d), lambda i: (i, 0)),
                  pl.BlockSpec(w.shape, lambda i: (0, 0)),
                  pl.BlockSpec(b.shape, lambda i: (0, 0))],
        out_specs=[out, out, out],
        out_shape=[shape, shape, shape],
        scratch_shapes=[pltpu.VMEM((ATTN_WIDTH // LANES, tm, LANES), F32)],
        compiler_params=_cparams(("arbitrary",)),
        name="inproj_qkv",
    )(xb, w, b)


def _gelu_tanh(x):
    c = math.sqrt(2.0 / math.pi)
    return 0.5 * x * (1.0 + jnp.tanh(c * (x + 0.044715 * (x * x * x))))


def _s5_kernel(u_ref, bm_ref, cre_ref, cim_ref, are_ref, aim_ref, d_ref, wglu_ref,
               o_ref, sre_ref, sim_ref, st_re_ref, st_im_ref, y_ref, *, steps, batch):
    kw = SSM_WIDTH // S5_KB
    sw = SSM_GROUPS * SSM_STATE // S5_KB

    @pl.when(pl.program_id(0) == 0)
    def _():
        st_re_ref[...] = jnp.zeros_like(st_re_ref)
        st_im_ref[...] = jnp.zeros_like(st_im_ref)

    for k in range(S5_KB):
        ls = slice(k * sw, (k + 1) * sw)
        bu = jnp.dot(u_ref[:, k * kw:(k + 1) * kw], bm_ref[k], preferred_element_type=F32)
        sre_ref[:, ls] = bu[:, :sw]
        sim_ref[:, ls] = bu[:, sw:]

        ar = jnp.broadcast_to(are_ref[:, ls], (batch, sw))
        ai = jnp.broadcast_to(aim_ref[:, ls], (batch, sw))

        def step(t, carry):
            sr, si = carry
            rows = pl.ds(pl.multiple_of(t * batch, batch), batch)
            nr = ar * sr - ai * si + sre_ref[rows, ls]
            ni = ar * si + ai * sr + sim_ref[rows, ls]
            sre_ref[rows, ls] = nr
            sim_ref[rows, ls] = ni
            return nr, ni

        sr, si = lax.fori_loop(0, steps, step, (st_re_ref[:, ls], st_im_ref[:, ls]))
        st_re_ref[:, ls] = sr
        st_im_ref[:, ls] = si

        y_ref[:, k * kw:(k + 1) * kw] = (
            jnp.dot(sre_ref[:, ls].astype(BF16), cre_ref[k], preferred_element_type=F32)
            + jnp.dot(sim_ref[:, ls].astype(BF16), cim_ref[k], preferred_element_type=F32))

    y = y_ref[...] + d_ref[...] * u_ref[...].astype(F32)
    y = _gelu_tanh(y)
    z = jnp.dot(y.astype(BF16), wglu_ref[...], preferred_element_type=F32)
    o_ref[...] = (y * jax.nn.sigmoid(z)).astype(BF16)


def _s5(proj, bm, cre, cim, a_re, a_im, d_skip, w_glu, batch, steps=S5_T):
    n = proj.shape[0]
    u_blk = proj.shape[1] // SSM_WIDTH - 1
    seq = n // batch
    steps = min(steps, seq)
    rows = steps * batch
    nstate = SSM_GROUPS * SSM_STATE
    const = lambda shape: pl.BlockSpec(shape, lambda i: (0,) * len(shape))
    return pl.pallas_call(
        functools.partial(_s5_kernel, steps=steps, batch=batch),
        grid=(seq // steps,),
        in_specs=[pl.BlockSpec((rows, SSM_WIDTH), lambda i: (i, u_blk)),
                  const(bm.shape), const(cre.shape), const(cim.shape),
                  const(a_re.shape), const(a_im.shape), const(d_skip.shape), const(w_glu.shape)],
        out_specs=pl.BlockSpec((rows, SSM_WIDTH), lambda i: (i, 0)),
        out_shape=jax.ShapeDtypeStruct((n, SSM_WIDTH), BF16),
        scratch_shapes=[pltpu.VMEM((rows, nstate), F32), pltpu.VMEM((rows, nstate), F32),
                        pltpu.VMEM((batch, nstate), F32), pltpu.VMEM((batch, nstate), F32),
                        pltpu.VMEM((rows, SSM_WIDTH), F32)],
        compiler_params=_cparams(("arbitrary",)),
        name="s5",
    )(proj, bm, cre, cim, a_re, a_im, d_skip, w_glu)


def _s5_params(lam_re, lam_im, log_dt, b_re, b_im, c_re, c_im, d_skip):
    lam = lax.complex(lam_re.astype(F32), lam_im.astype(F32))
    dt = jnp.exp(log_dt.astype(F32))[:, None]
    lam_bar = jnp.exp(lam * dt)
    b_bar = ((lam_bar - 1.0) / lam)[:, :, None] * lax.complex(b_re.astype(F32), b_im.astype(F32))
    gl = SSM_GROUPS // S5_KB
    eye = jnp.eye(gl, dtype=F32)

    def in_blocks(b):
        b = b.reshape(S5_KB, gl, SSM_STATE, SSM_GROUP)
        m = jnp.einsum('kgpc,gh->kgchp', b, eye)
        return m.reshape(S5_KB, gl * SSM_GROUP, gl * SSM_STATE)

    def out_blocks(c):
        c = c.reshape(S5_KB, gl, SSM_GROUP, SSM_STATE)
        m = jnp.einsum('kgcp,gh->kgphc', c, eye)
        return m.reshape(S5_KB, gl * SSM_STATE, gl * SSM_GROUP)

    bm = jnp.concatenate([in_blocks(jnp.real(b_bar)), in_blocks(jnp.imag(b_bar))], axis=-1).astype(BF16)
    cre = out_blocks(c_re.astype(F32)).astype(BF16)
    cim = out_blocks(-c_im.astype(F32)).astype(BF16)
    a_re = jnp.real(lam_bar).reshape(1, -1)
    a_im = jnp.imag(lam_bar).reshape(1, -1)
    return bm, cre, cim, a_re, a_im, d_skip.astype(F32).reshape(1, -1)


def _attn_kernel(q_ref, kp_ref, kc_ref, vp_ref, vc_ref, o_ref, lse_ref, *, steps):
    jb = pl.program_id(1)
    q = q_ref[...] * (HEAD_DIM ** -0.5)
    k2 = jnp.concatenate([kp_ref[...], kc_ref[...]], axis=0)
    v2 = jnp.concatenate([vp_ref[...], vc_ref[...]], axis=0)
    qi = lax.broadcasted_iota(jnp.int32, (2 * QBLK, 2 * QBLK), 0) % QBLK
    kj = lax.broadcasted_iota(jnp.int32, (2 * QBLK, 2 * QBLK), 1)
    dist = qi + QBLK - kj
    allowed = (dist >= 0) & (dist <= steps) & ((kj >= QBLK) | (jb > 0))
    first = lax.broadcasted_iota(jnp.int32, (QBLK, LANES), 1) < HEAD_DIM
    zero = jnp.zeros((), BF16)
    outs, lses = [], []
    for slab in range(ATTN_WIDTH // LANES):
        ls = slice(slab * LANES, (slab + 1) * LANES)
        qs = jnp.concatenate([jnp.where(first, q[:, ls], zero), jnp.where(first, zero, q[:, ls])], axis=0)
        s = lax.dot_general(qs, k2[:, ls], (((1,), (1,)), ((), ())), preferred_element_type=F32)
        s = jnp.where(allowed, s, NEG)
        m = jnp.max(s, axis=-1, keepdims=True)
        p = jnp.exp(s - m)
        denom = jnp.sum(p, axis=-1, keepdims=True)
        r = jnp.dot(p.astype(BF16), v2[:, ls], preferred_element_type=F32) / denom
        outs.append(jnp.where(first, r[:QBLK], r[QBLK:]))
        lse = m + jnp.log(denom)
        lses += [lse[:QBLK], lse[QBLK:]]
    o_ref[...] = jnp.concatenate(outs, axis=-1).astype(BF16)
    lse_ref[...] = jnp.concatenate(lses, axis=-1)


def _attn(q, k, v, steps):
    classes, n, width = q.shape
    nb = n // QBLK
    cur = pl.BlockSpec((None, QBLK, width), lambda c, j: (c, j, 0))
    prev = pl.BlockSpec((None, QBLK, width), lambda c, j: (c, jnp.maximum(j - 1, 0), 0))
    return pl.pallas_call(
        functools.partial(_attn_kernel, steps=steps),
        grid=(classes, nb),
        in_specs=[cur, prev, cur, prev, cur],
        out_specs=[cur, pl.BlockSpec((None, QBLK, HEADS), lambda c, j: (c, j, 0))],
        out_shape=[jax.ShapeDtypeStruct((classes, n, width), BF16),
                   jax.ShapeDtypeStruct((classes, n, HEADS), F32)],
        compiler_params=_cparams(("arbitrary", "arbitrary")),
        name="attn",
    )(q, k, k, v, v)


def _layer_norm(z, g, b):
    mu = jnp.mean(z, axis=-1, keepdims=True)
    zc = z - mu
    var = jnp.mean(zc * zc, axis=-1, keepdims=True)
    return zc * lax.rsqrt(var + LN_EPS) * g + b


def _row_words(d):
    return d // (2 * LANES)


def _store_rows(ref, y):
    rows, d = y.shape
    sub = _row_words(d)
    for s in range(sub):
        lo = y[:, (2 * s) * LANES:(2 * s + 1) * LANES].astype(BF16).astype(F32)
        hi = y[:, (2 * s + 1) * LANES:(2 * s + 2) * LANES].astype(BF16).astype(F32)
        word = (pltpu.bitcast(lo, jnp.uint32) >> 16) | (pltpu.bitcast(hi, jnp.uint32) & HI_HALF)
        ref[pl.ds(s, rows, stride=sub), :] = word


def _load_rows(ref, rows, sub):
    parts = []
    for s in range(sub):
        word = ref[pl.ds(s, rows, stride=sub), :]
        parts.append(pltpu.bitcast(word << 16, F32))
        parts.append(pltpu.bitcast(word & HI_HALF, F32))
    return jnp.concatenate(parts, axis=-1)


def _expand_heads(w, e_ref):
    hi = w.astype(BF16)
    lo = (w - hi.astype(F32)).astype(BF16)
    return (jnp.dot(hi, e_ref[...], preferred_element_type=F32)
            + jnp.dot(lo, e_ref[...], preferred_element_type=F32))


def _merge_kernel(x_ref, yssm_ref, o0_ref, o1_ref, o2_ref, lse_ref, gs_ref, ga_ref,
                  wbs_ref, wba_ref, wout_ref, e_ref, g_ref, b_ref, o_ref, orow_ref, *, alpha):
    lse = lse_ref[...]
    l0, l1, l2 = lse[:, 0:HEADS], lse[:, HEADS:2 * HEADS], lse[:, 2 * HEADS:3 * HEADS]
    m = jnp.maximum(jnp.maximum(l0, l1), l2)
    e0, e1, e2 = jnp.exp(l0 - m), jnp.exp(l1 - m), jnp.exp(l2 - m)
    den = e0 + e1 + e2
    y_attn = (_expand_heads(e0 / den, e_ref) * o0_ref[...].astype(F32)
              + _expand_heads(e1 / den, e_ref) * o1_ref[...].astype(F32)
              + _expand_heads(e2 / den, e_ref) * o2_ref[...].astype(F32))
    ya = jnp.dot(y_attn.astype(BF16), wba_ref[...], preferred_element_type=F32)
    ys = jnp.dot(yssm_ref[...], wbs_ref[...], preferred_element_type=F32)
    merged = (jax.nn.sigmoid(gs_ref[...].astype(F32)) * ys
              + jax.nn.sigmoid(ga_ref[...].astype(F32)) * ya)
    mix = jnp.dot(merged.astype(BF16), wout_ref[...], preferred_element_type=F32)
    y = _layer_norm(alpha * x_ref[...] + mix, g_ref[...], b_ref[...])
    o_ref[...] = y
    _store_rows(orow_ref, y)


def _merge(x, y_ssm, o0, o1, o2, lse, proj, w_br_ssm, w_br_attn, w_out, expand, ln_g, ln_b,
           alpha, tm=MERGE_TM):
    n, d = x.shape
    tm = min(tm, n)
    gate_blk = proj.shape[1] // d
    row = lambda w: pl.BlockSpec((tm, w), lambda i: (i, 0))
    const = lambda a: pl.BlockSpec(a.shape, lambda i: (0,) * a.ndim)
    return pl.pallas_call(
        functools.partial(_merge_kernel, alpha=alpha),
        grid=(n // tm,),
        in_specs=[row(d), row(SSM_WIDTH), row(ATTN_WIDTH), row(ATTN_WIDTH), row(ATTN_WIDTH),
                  row(lse.shape[1]),
                  pl.BlockSpec((tm, d), lambda i: (i, gate_blk - 2)),
                  pl.BlockSpec((tm, d), lambda i: (i, gate_blk - 1)),
                  const(w_br_ssm), const(w_br_attn), const(w_out), const(expand),
                  const(ln_g), const(ln_b)],
        out_specs=[row(d), pl.BlockSpec((tm * _row_words(d), LANES), lambda i: (i, 0))],
        out_shape=[jax.ShapeDtypeStruct((n, d), F32),
                   jax.ShapeDtypeStruct((n * _row_words(d), LANES), jnp.uint32)],
        compiler_params=_cparams(("arbitrary",)),
        name="merge",
    )(x, y_ssm, o0, o1, o2, lse, proj, proj, w_br_ssm, w_br_attn, w_out, expand, ln_g, ln_b)


def _first_argmax(v, iota, size, axis):
    m = jnp.max(v, axis=axis, keepdims=True)
    idx = jnp.min(jnp.where(v == m, iota, size), axis=axis, keepdims=True)
    return m, idx


def _route_kernel(x_ref, rwh_ref, rwl_ref, bias_ref, eg_ref, tri_ref,
                  eidx_ref, gate_ref, rank_ref, cnt_ref, carry_ref):
    @pl.when(pl.program_id(0) == 0)
    def _():
        carry_ref[...] = jnp.zeros_like(carry_ref)

    x = x_ref[...]
    tm = x.shape[0]
    xh = x.astype(BF16)
    xl = (x - xh.astype(F32)).astype(BF16)
    nt = (((1,), (1,)), ((), ()))
    logits = (lax.dot_general(rwh_ref[...], xh, nt, preferred_element_type=F32)
              + lax.dot_general(rwh_ref[...], xl, nt, preferred_element_type=F32)
              + lax.dot_general(rwl_ref[...], xh, nt, preferred_element_type=F32))
    scores = jax.nn.sigmoid(logits)
    sel = scores + bias_ref[...]

    sel3 = sel.reshape(N_EXPERT_GROUPS, GROUP_SIZE, tm)
    iw = lax.broadcasted_iota(jnp.int32, sel3.shape, 1)
    m1, i1 = _first_argmax(sel3, iw, GROUP_SIZE, 1)
    m2 = jnp.max(jnp.where(iw == i1, NEG, sel3), axis=1, keepdims=True)
    gs = (m1 + m2).reshape(N_EXPERT_GROUPS, tm)

    ig = lax.broadcasted_iota(jnp.int32, gs.shape, 0)
    gmask = jnp.zeros(gs.shape, F32)
    for _ in range(TOPK_GROUPS):
        _, gi = _first_argmax(gs, ig, N_EXPERT_GROUPS, 0)
        hit = ig == gi
        gmask = jnp.where(hit, 1.0, gmask)
        gs = jnp.where(hit, NEG, gs)
    emask = jnp.dot(eg_ref[...], gmask.astype(BF16), preferred_element_type=F32) > 0.5

    masked = jnp.where(emask, sel, NEG)
    ie = lax.broadcasted_iota(jnp.int32, masked.shape, 0)
    chosen = jnp.zeros(masked.shape, F32)
    idxs, vals = [], []
    for _ in range(TOP_K):
        _, ei = _first_argmax(masked, ie, N_EXPERTS, 0)
        hit = ie == ei
        idxs.append(ei)
        vals.append(jnp.sum(jnp.where(hit, scores, 0.0), axis=0, keepdims=True))
        chosen = jnp.where(hit, 1.0, chosen)
        masked = jnp.where(hit, NEG, masked)
    total = vals[0]
    for v in vals[1:]:
        total = total + v

    prefix = jnp.dot(chosen.astype(BF16), tri_ref[...], preferred_element_type=F32)
    pos = carry_ref[:, 0:1] + prefix - 1.0
    ranks = [jnp.sum(jnp.where(ie == ei, pos, 0.0), axis=0, keepdims=True) for ei in idxs]
    carry = carry_ref[...] + jnp.sum(chosen, axis=1, keepdims=True)
    carry_ref[...] = carry
    cnt_ref[...] = carry

    eidx_ref[...] = jnp.concatenate(idxs, axis=0)
    gate_ref[...] = jnp.concatenate([v / total * ROUTED_SCALE for v in vals], axis=0)
    rank_ref[...] = jnp.concatenate(ranks, axis=0).astype(jnp.int32)


def _route(x, row0, n, rw_hi, rw_lo, bias, eg, tri, tm=ROUTE_TM):
    d = x.shape[1]
    tm = min(tm, n)
    blk0 = row0 // tm
    const = lambda a: pl.BlockSpec(a.shape, lambda i: (0,) * a.ndim)
    col = pl.BlockSpec((TOP_K, tm), lambda i: (0, i))
    return pl.pallas_call(
        _route_kernel,
        grid=(n // tm,),
        in_specs=[pl.BlockSpec((tm, d), lambda i: (i + blk0, 0)),
                  const(rw_hi), const(rw_lo), const(bias), const(eg), const(tri)],
        out_specs=[col, col, col, pl.BlockSpec((N_EXPERTS, LANES), lambda i: (0, 0))],
        out_shape=[jax.ShapeDtypeStruct((TOP_K, n), jnp.int32),
                   jax.ShapeDtypeStruct((TOP_K, n), F32),
                   jax.ShapeDtypeStruct((TOP_K, n), jnp.int32),
                   jax.ShapeDtypeStruct((N_EXPERTS, LANES), F32)],
        scratch_shapes=[pltpu.VMEM((N_EXPERTS, LANES), F32)],
        compiler_params=_cparams(("arbitrary",)),
        name="route",
    )(x, rw_hi, rw_lo, bias, eg, tri)


def _sc_gather_rows(src, idx, sub, chunk=SC_CHUNK):
    s = src.shape[0] // sub
    m = idx.shape[0]
    info = plsc.get_sparse_core_info()
    n_workers = info.num_cores * info.num_subcores
    per_worker = m // n_workers
    n_chunks = per_worker // chunk
    assert n_chunks * chunk * n_workers == m
    mesh = plsc.VectorSubcoreMesh(core_axis_name="c", subcore_axis_name="s")

    @functools.partial(
        pl.kernel, mesh=mesh,
        out_type=jax.ShapeDtypeStruct((m, sub, LANES), src.dtype),
        scratch_types=[pltpu.VMEM((n_chunks, chunk), jnp.int32),
                       pltpu.VMEM((chunk, sub, LANES), src.dtype),
                       pltpu.SemaphoreType.DMA])
    def gather(src_hbm, idx_hbm, out_hbm, idx_v, rows_v, sem):
        wid = lax.axis_index("s") * info.num_cores + lax.axis_index("c")
        pltpu.sync_copy(idx_hbm.at[wid], idx_v)

        @pl.loop(0, n_chunks)
        def _(j):
            pltpu.async_copy(src_hbm.at[idx_v.at[j]], rows_v, sem).wait()
            pltpu.sync_copy(rows_v, out_hbm.at[pl.ds(wid * per_worker + j * chunk, chunk)])

    out = gather(src.reshape(s, sub, LANES), idx.reshape(n_workers, n_chunks, chunk))
    return out.reshape(m * sub, LANES)


def _sc_scatter_rows(src, row0, dest, n_out, sub, chunk=SC_CHUNK):
    copies, n = dest.shape
    s = src.shape[0] // sub
    info = plsc.get_sparse_core_info()
    n_workers = info.num_cores * info.num_subcores
    per_worker = n // n_workers
    n_chunks = per_worker // chunk
    assert n_chunks * chunk * n_workers == n and copies * n == n_out
    mesh = plsc.VectorSubcoreMesh(core_axis_name="c", subcore_axis_name="s")
    idx = dest.reshape(copies, n_workers, n_chunks, chunk).transpose(1, 2, 0, 3)
    idx = idx.reshape(n_workers, n_chunks * copies, chunk)

    @functools.partial(
        pl.kernel, mesh=mesh,
        out_type=jax.ShapeDtypeStruct((n_out, sub, LANES), src.dtype),
        scratch_types=[pltpu.VMEM((n_chunks * copies, chunk), jnp.int32),
                       pltpu.VMEM((chunk, sub, LANES), src.dtype),
                       pltpu.SemaphoreType.DMA])
    def scatter(src_hbm, idx_hbm, out_hbm, idx_v, rows_v, sem):
        wid = lax.axis_index("s") * info.num_cores + lax.axis_index("c")
        pltpu.sync_copy(idx_hbm.at[wid], idx_v)

        @pl.loop(0, n_chunks)
        def _(j):
            pltpu.sync_copy(src_hbm.at[pl.ds(row0 + wid * per_worker + j * chunk, chunk)], rows_v)
            for c in range(copies):
                pltpu.async_copy(rows_v, out_hbm.at[idx_v.at[j * copies + c]], sem).wait()

    out = scatter(src.reshape(s, sub, LANES), idx)
    return out.reshape(n_out * sub, LANES)


def _gmm_kernel(be_ref, nu_ref, x_ref, wg_ref, wu_ref, wd_ref, o_ref, wgu_s, wd_s, *, hidden, blk, sub):
    b = pl.program_id(0)
    live = b < nu_ref[0]

    @pl.when(live & ((b == 0) | (be_ref[b] != be_ref[jnp.maximum(b - 1, 0)])))
    def _():
        wgu_s[:, :hidden] = wg_ref[...].astype(BF16)
        wgu_s[:, hidden:] = wu_ref[...].astype(BF16)
        wd_s[...] = wd_ref[...].astype(BF16)

    @pl.when(live)
    def _():
        x = _load_rows(x_ref, blk, sub).astype(BF16)
        gu = jnp.dot(x, wgu_s[...], preferred_element_type=F32)
        g, u = gu[:, :hidden], gu[:, hidden:]
        h = (g * jax.nn.sigmoid(g) * u).astype(BF16)
        _store_rows(o_ref, jnp.dot(h, wd_s[...], preferred_element_type=F32))


def _gmm(xs, w_gate, w_up, w_down, layer, blk_e, n_used, blk=MOE_BLK):
    hidden, d = w_down.shape[2:]
    sub = _row_words(d)
    n_rows = xs.shape[0] // sub
    row = lambda b, be, nu: (jnp.minimum(b, nu[0] - 1), 0)
    expert = lambda b, be, nu: (layer, be[b], 0, 0)
    return pl.pallas_call(
        functools.partial(_gmm_kernel, hidden=hidden, blk=blk, sub=sub),
        grid_spec=pltpu.PrefetchScalarGridSpec(
            num_scalar_prefetch=2,
            grid=(n_rows // blk,),
            in_specs=[pl.BlockSpec((blk * sub, LANES), row),
                      pl.BlockSpec((None, None, d, hidden), expert),
                      pl.BlockSpec((None, None, d, hidden), expert),
                      pl.BlockSpec((None, None, hidden, d), expert)],
            out_specs=pl.BlockSpec((blk * sub, LANES), row),
            scratch_shapes=[pltpu.VMEM((d, 2 * hidden), BF16), pltpu.VMEM((hidden, d), BF16)]),
        out_shape=jax.ShapeDtypeStruct((n_rows * sub, LANES), jnp.uint32),
        compiler_params=_cparams(("arbitrary",)),
        name="expert_gmm",
    )(blk_e, n_used, xs, w_gate, w_up, w_down)


def _combine_kernel(x_ref, yg_ref, gate_ref, wgu_ref, wd_ref, g_ref, b_ref, *rest, alpha, hidden):
    o_ref, ob_ref = rest[-2:]
    x = x_ref[...]
    gu = jnp.dot(x.astype(BF16), wgu_ref[...], preferred_element_type=F32)
    g, u = gu[:, :hidden], gu[:, hidden:]
    h = (g * jax.nn.sigmoid(g) * u).astype(BF16)
    acc = jnp.dot(h, wd_ref[...], preferred_element_type=F32)
    gates = gate_ref[...]
    tm, d = x.shape
    for k in range(TOP_K):
        acc = acc + gates[:, k:k + 1] * _load_rows(yg_ref.at[k], tm, _row_words(d))
    y = _layer_norm(alpha * x + acc, g_ref[...], b_ref[...])
    o_ref[...] = y
    ob_ref[...] = y.astype(BF16)


def _combine(x, row0, yg, gates_t, w_gu, w_down, ln_g, ln_b, alpha, prev=None, tm=COMBINE_TM):
    n, d = x.shape
    n_part = gates_t.shape[0]
    tm = min(tm, n_part)
    blk0 = row0 // tm
    sub = _row_words(d)
    hidden = w_down.shape[0]
    const = lambda a: pl.BlockSpec(a.shape, lambda i: (0,) * a.ndim)
    rows = pl.BlockSpec((tm, d), lambda i: (i + blk0, 0))
    in_specs = [rows,
                pl.BlockSpec((TOP_K, tm * sub, LANES), lambda i: (0, i, 0)),
                pl.BlockSpec((tm, TOP_K), lambda i: (i, 0)),
                const(w_gu), const(w_down), const(ln_g), const(ln_b)]
    args = [x, yg, gates_t, w_gu, w_down, ln_g, ln_b]
    aliases = {}
    if prev is not None:
        aliases = {len(args): 0, len(args) + 1: 1}
        in_specs += [pl.BlockSpec(memory_space=pl.ANY)] * 2
        args += list(prev)
    return pl.pallas_call(
        functools.partial(_combine_kernel, alpha=alpha, hidden=hidden),
        grid=(n_part // tm,),
        in_specs=in_specs,
        out_specs=[rows, rows],
        out_shape=[jax.ShapeDtypeStruct((n, d), F32), jax.ShapeDtypeStruct((n, d), BF16)],
        input_output_aliases=aliases,
        compiler_params=_cparams(("arbitrary",)),
        name="combine",
    )(*args)


def _split_w_in(w_in, b_in, d):
    qkv0 = SSM_WIDTH
    qkv_w = len(ATTN_PATTERNS) * ATTN_WIDTH
    gates0 = qkv0 + 3 * qkv_w
    cols = lambda a, c0, w: a[:, c0:c0 + w]
    pick = lambda a: jnp.concatenate([cols(a, gates0, 2 * d), cols(a, 0, SSM_WIDTH)], axis=1)
    main = (pick(w_in).astype(BF16), pick(b_in))
    groups = []
    for g in range(len(ATTN_PATTERNS)):
        sel = lambda a: jnp.concatenate([cols(a, qkv0 + s * qkv_w + g * ATTN_WIDTH, ATTN_WIDTH)
                                         for s in range(3)], axis=1)
        groups.append((sel(w_in).astype(BF16), sel(b_in)))
    return main, groups


def _token_mixer(x, xb, batch, w_in, b_in, s5p, w_glu, w_br_ssm, w_br_attn, w_out, expand, ln_g, ln_b, alpha):
    n, d = x.shape
    seq = n // batch
    (w_main, b_main), groups = _split_w_in(w_in, b_in, d)
    proj = _inproj(xb, w_main, b_main)
    y_ssm = _s5(proj, *s5p, w_glu, batch)

    outs, lses = [], []
    for (window, dil), (w_g, b_g) in zip(ATTN_PATTERNS, groups):
        assert window // dil == QBLK and (seq // dil) % QBLK == 0
        q, k, v = _inproj_qkv(xb, w_g, b_g, dil * batch)
        o_g, lse_g = _attn(q, k, v, window // dil)
        outs.append(jnp.swapaxes(o_g, 0, 1).reshape(n, ATTN_WIDTH))
        lses.append(jnp.swapaxes(lse_g, 0, 1).reshape(n, HEADS))
    lse = jnp.concatenate(lses, axis=-1)
    return _merge(x, y_ssm, outs[0], outs[1], outs[2], lse, proj, w_br_ssm, w_br_attn, w_out,
                  expand, ln_g, ln_b, alpha)


def _moe_part(x, x_rows, row0, n, out_prev, rw_hi, rw_lo, bias, eg, tri, w_gate, w_up, w_down, layer,
              sh_gu, sh_down, ln_g, ln_b, alpha):
    d = x.shape[1]
    sub = _row_words(d)
    eidx, gates, rank, cnt = _route(x, row0, n, rw_hi, rw_lo, bias, eg, tri)
    counts = cnt[:, 0].astype(jnp.int32)
    nk = n * TOP_K
    n_pad = N_EXPERTS * MOE_BLK
    n_rows = nk + n_pad
    n_blocks = n_rows // MOE_BLK
    assert n_pad % n == 0
    pad_counts = (counts + MOE_BLK - 1) // MOE_BLK * MOE_BLK
    pad_end = jnp.cumsum(pad_counts)
    pad_start = pad_end - pad_counts
    onehot = eidx[:, :, None] == jnp.arange(N_EXPERTS, dtype=jnp.int32)
    dest = jnp.sum(jnp.where(onehot, pad_start, 0), axis=-1) + rank
    seg_start = jnp.concatenate([pad_start + counts, pad_end[-1:]])
    seg_len = jnp.concatenate([pad_counts - counts, n_rows - pad_end[-1:]])
    seg_end = jnp.cumsum(seg_len)
    q = jnp.arange(n_pad, dtype=jnp.int32)[:, None]
    seg = jnp.sum((q >= seg_end[None, :]).astype(jnp.int32), axis=1, keepdims=True)
    seg_hot = seg == jnp.arange(N_EXPERTS + 1, dtype=jnp.int32)[None, :]
    pad_dest = jnp.sum(jnp.where(seg_hot, seg_start - (seg_end - seg_len), 0), axis=1) + q[:, 0]
    dest_all = jnp.concatenate([dest, pad_dest.reshape(n_pad // n, n)], axis=0).astype(jnp.int32)
    blk_first = jnp.arange(n_blocks, dtype=jnp.int32)[:, None] * MOE_BLK
    blk_e = jnp.minimum(jnp.sum((blk_first >= pad_end[None, :]).astype(jnp.int32), axis=1), N_EXPERTS - 1)
    n_used = (pad_end[-1:] // MOE_BLK).astype(jnp.int32)

    xs = _sc_scatter_rows(x_rows, row0, dest_all, n_rows, sub)
    ys = _gmm(xs, w_gate, w_up, w_down, layer, blk_e, n_used)
    yg = _sc_gather_rows(ys, dest.reshape(nk), sub).reshape(TOP_K, n * sub, LANES)
    return _combine(x, row0, yg, gates.T, sh_gu, sh_down, ln_g, ln_b, alpha, out_prev)


def _moe(x, x_rows, *params):
    n = x.shape[0]
    part = n // MOE_PARTS
    out = None
    for i in range(MOE_PARTS):
        out = _moe_part(x, x_rows, i * part, part, out, *params)
    return out


def kernel(x, w_in, b_in, ssm_lam_re, ssm_lam_im, ssm_log_dt, ssm_b_re, ssm_b_im, ssm_c_re, ssm_c_im, ssm_d, w_glu, w_br_ssm, w_br_attn, w_out, ln1_g, ln1_b, router_w, router_bias, exp_w_gate, exp_w_up, exp_w_down, sh_w_gate, sh_w_up, sh_w_down, ln2_g, ln2_b):
    batch, seq, d = x.shape
    depth = w_in.shape[0]
    assert batch == SUBLANES
    alpha = (2 * depth) ** 0.25
    n = batch * seq
    xt = jnp.swapaxes(x, 0, 1).reshape(n, d)
    xb = xt.astype(BF16)

    expand = jnp.repeat(jnp.eye(HEADS, dtype=BF16), HEAD_DIM, axis=1)
    eg = jnp.repeat(jnp.eye(N_EXPERT_GROUPS, dtype=BF16), GROUP_SIZE, axis=0)
    tm = min(ROUTE_TM, n)
    tri = (jnp.arange(tm)[:, None] <= jnp.arange(tm)[None, :]).astype(BF16)
    row = lambda a: a.astype(F32).reshape(1, -1)

    for l in range(depth):
        s5p = _s5_params(ssm_lam_re[l], ssm_lam_im[l], ssm_log_dt[l], ssm_b_re[l], ssm_b_im[l],
                         ssm_c_re[l], ssm_c_im[l], ssm_d[l])
        xt, xt_rows = _token_mixer(xt, xb, batch, w_in[l], row(b_in[l]), s5p,
                                   w_glu[l].astype(BF16), w_br_ssm[l].astype(BF16),
                                   w_br_attn[l].astype(BF16), w_out[l].astype(BF16),
                                   expand, row(ln1_g[l]), row(ln1_b[l]), alpha)
        rwt = router_w[l].astype(F32).T
        rw_hi = rwt.astype(BF16)
        rw_lo = (rwt - rw_hi.astype(F32)).astype(BF16)
        sh_gu = jnp.concatenate([sh_w_gate[l], sh_w_up[l]], axis=-1).astype(BF16)
        xt, xb = _moe(xt, xt_rows, rw_hi, rw_lo, router_bias[l].astype(F32).reshape(-1, 1), eg, tri,
                  exp_w_gate, exp_w_up, exp_w_down, l, sh_gu, sh_w_down[l].astype(BF16),
                  row(ln2_g[l]), row(ln2_b[l]), alpha)
    return jnp.swapaxes(xt.reshape(seq, batch, d), 0, 1)
```

```python
import functools
import math

import jax
import jax.numpy as jnp
import numpy as np
from jax import lax
from jax.experimental import pallas as pl
from jax.experimental.pallas import tpu as pltpu
from jax.experimental.pallas import tpu_sc as plsc

F32 = jnp.float32
BF16 = jnp.bfloat16

SSM_GROUP = 16
SSM_GROUPS = 32
SSM_WIDTH = SSM_GROUP * SSM_GROUPS
SSM_STATE = 64
HEAD_DIM = 64
HEADS = 8
ATTN_PATTERNS = ((128, 1), (512, 4), (2048, 16))
ATTN_WIDTH = HEADS * HEAD_DIM
QBLK = 128
N_EXPERTS = 64
TOP_K = 8
N_EXPERT_GROUPS = 8
GROUP_SIZE = N_EXPERTS // N_EXPERT_GROUPS
TOPK_GROUPS = 4
ROUTED_SCALE = 2.5
LN_EPS = 1e-5
NEG = -1e30

LANES = 128
SUBLANES = 8
VMEM_LIMIT = 56 * 1024 * 1024

INPROJ_TM = 1024
INPROJ_TN = 1280
QKV_TM = 2048
S5_T = 128
S5_KB = 4
ATTN_SUB_BLOCKS = 2
MERGE_TM = 512
ROUTE_TM = 512
MOE_BLK = 512
MOE_PARTS = 2
SC_CHUNK = 128
HI_HALF = np.uint32(0xFFFF0000)
COMBINE_TM = 256


def _cparams(sem):
    return pltpu.CompilerParams(dimension_semantics=sem, vmem_limit_bytes=VMEM_LIMIT)


def _inproj_kernel(x_ref, w_ref, b_ref, o_ref):
    acc = jnp.dot(x_ref[...], w_ref[...], preferred_element_type=F32)
    o_ref[...] = (acc + b_ref[...]).astype(BF16)


def _inproj(xb, w, b, tm=INPROJ_TM, tn=INPROJ_TN):
    n, d = xb.shape
    width = w.shape[1]
    tm = min(tm, n)
    return pl.pallas_call(
        _inproj_kernel,
        grid=(n // tm, width // tn),
        in_specs=[pl.BlockSpec((tm, d), lambda i, j: (i, 0)),
                  pl.BlockSpec((d, tn), lambda i, j: (0, j)),
                  pl.BlockSpec((1, tn), lambda i, j: (0, j))],
        out_specs=pl.BlockSpec((tm, tn), lambda i, j: (i, j)),
        out_shape=jax.ShapeDtypeStruct((n, width), BF16),
        compiler_params=_cparams(("arbitrary", "arbitrary")),
        name="inproj",
    )(xb, w, b)


def _inproj_qkv_kernel(x_ref, w_ref, b_ref, q_ref, k_ref, v_ref, y_ref, *, classes):
    x = x_ref[...]
    rows = x.shape[0] // classes
    for part, o_ref in enumerate((q_ref, k_ref, v_ref)):
        cols = slice(part * ATTN_WIDTH, (part + 1) * ATTN_WIDTH)
        y = jnp.dot(x, w_ref[:, cols], preferred_element_type=F32) + b_ref[:, cols]
        y = pltpu.einshape("jcf->cjf", y.reshape(rows, classes, ATTN_WIDTH))
        o_ref[...] = y.astype(BF16)


def _inproj_qkv(xb, w, b, classes, tm=QKV_TM):
    n, d = xb.shape
    tm = min(tm, n)
    rows = tm // classes
    out = pl.BlockSpec((classes, rows, ATTN_WIDTH), lambda i: (0, i, 0))
    shape = jax.ShapeDtypeStruct((classes, n // classes, ATTN_WIDTH), BF16)
    return pl.pallas_call(
        functools.partial(_inproj_qkv_kernel, classes=classes),
        grid=(n // tm,),
        in_specs=[pl.BlockSpec((tm, d), lambda i: (i, 0)),
                  pl.BlockSpec(w.shape, lambda i: (0, 0)),
                  pl.BlockSpec(b.shape, lambda i: (0, 0))],
        out_specs=[out, out, out],
        out_shape=[shape, shape, shape],
        scratch_shapes=[pltpu.VMEM((ATTN_WIDTH // LANES, tm, LANES), F32)],
        compiler_params=_cparams(("arbitrary",)),
        name="inproj_qkv",
    )(xb, w, b)


def _gelu_tanh(x):
    c = math.sqrt(2.0 / math.pi)
    return 0.5 * x * (1.0 + jnp.tanh(c * (x + 0.044715 * (x * x * x))))


def _s5_kernel(u_ref, bm_ref, cre_ref, cim_ref, are_ref, aim_ref, d_ref, wglu_ref,
               o_ref, sre_ref, sim_ref, st_re_ref, st_im_ref, y_ref, *, steps, batch):
    kw = SSM_WIDTH // S5_KB
    sw = SSM_GROUPS * SSM_STATE // S5_KB

    @pl.when(pl.program_id(0) == 0)
    def _():
        st_re_ref[...] = jnp.zeros_like(st_re_ref)
        st_im_ref[...] = jnp.zeros_like(st_im_ref)

    for k in range(S5_KB):
        ls = slice(k * sw, (k + 1) * sw)
        bu = jnp.dot(u_ref[:, k * kw:(k + 1) * kw], bm_ref[k], preferred_element_type=F32)
        sre_ref[:, ls] = bu[:, :sw]
        sim_ref[:, ls] = bu[:, sw:]

        ar = jnp.broadcast_to(are_ref[:, ls], (batch, sw))
        ai = jnp.broadcast_to(aim_ref[:, ls], (batch, sw))

        def step(t, carry):
            sr, si = carry
            rows = pl.ds(pl.multiple_of(t * batch, batch), batch)
            nr = ar * sr - ai * si + sre_ref[rows, ls]
            ni = ar * si + ai * sr + sim_ref[rows, ls]
            sre_ref[rows, ls] = nr
            sim_ref[rows, ls] = ni
            return nr, ni

        sr, si = lax.fori_loop(0, steps, step, (st_re_ref[:, ls], st_im_ref[:, ls]))
        st_re_ref[:, ls] = sr
        st_im_ref[:, ls] = si

        y_ref[:, k * kw:(k + 1) * kw] = (
            jnp.dot(sre_ref[:, ls].astype(BF16), cre_ref[k], preferred_element_type=F32)
            + jnp.dot(sim_ref[:, ls].astype(BF16), cim_ref[k], preferred_element_type=F32))

    y = y_ref[...] + d_ref[...] * u_ref[...].astype(F32)
    y = _gelu_tanh(y)
    z = jnp.dot(y.astype(BF16), wglu_ref[...], preferred_element_type=F32)
    o_ref[...] = (y * jax.nn.sigmoid(z)).astype(BF16)


def _s5(proj, bm, cre, cim, a_re, a_im, d_skip, w_glu, batch, steps=S5_T):
    n = proj.shape[0]
    u_blk = proj.shape[1] // SSM_WIDTH - 1
    seq = n // batch
    steps = min(steps, seq)
    rows = steps * batch
    nstate = SSM_GROUPS * SSM_STATE
    const = lambda shape: pl.BlockSpec(shape, lambda i: (0,) * len(shape))
    return pl.pallas_call(
        functools.partial(_s5_kernel, steps=steps, batch=batch),
        grid=(seq // steps,),
        in_specs=[pl.BlockSpec((rows, SSM_WIDTH), lambda i: (i, u_blk)),
                  const(bm.shape), const(cre.shape), const(cim.shape),
                  const(a_re.shape), const(a_im.shape), const(d_skip.shape), const(w_glu.shape)],
        out_specs=pl.BlockSpec((rows, SSM_WIDTH), lambda i: (i, 0)),
        out_shape=jax.ShapeDtypeStruct((n, SSM_WIDTH), BF16),
        scratch_shapes=[pltpu.VMEM((rows, nstate), F32), pltpu.VMEM((rows, nstate), F32),
                        pltpu.VMEM((batch, nstate), F32), pltpu.VMEM((batch, nstate), F32),
                        pltpu.VMEM((rows, SSM_WIDTH), F32)],
        compiler_params=_cparams(("arbitrary",)),
        name="s5",
    )(proj, bm, cre, cim, a_re, a_im, d_skip, w_glu)


def _s5_params(lam_re, lam_im, log_dt, b_re, b_im, c_re, c_im, d_skip):
    lam = lax.complex(lam_re.astype(F32), lam_im.astype(F32))
    dt = jnp.exp(log_dt.astype(F32))[:, None]
    lam_bar = jnp.exp(lam * dt)
    b_bar = ((lam_bar - 1.0) / lam)[:, :, None] * lax.complex(b_re.astype(F32), b_im.astype(F32))
    gl = SSM_GROUPS // S5_KB
    eye = jnp.eye(gl, dtype=F32)

    def in_blocks(b):
        b = b.reshape(S5_KB, gl, SSM_STATE, SSM_GROUP)
        m = jnp.einsum('kgpc,gh->kgchp', b, eye)
        return m.reshape(S5_KB, gl * SSM_GROUP, gl * SSM_STATE)

    def out_blocks(c):
        c = c.reshape(S5_KB, gl, SSM_GROUP, SSM_STATE)
        m = jnp.einsum('kgcp,gh->kgphc', c, eye)
        return m.reshape(S5_KB, gl * SSM_STATE, gl * SSM_GROUP)

    bm = jnp.concatenate([in_blocks(jnp.real(b_bar)), in_blocks(jnp.imag(b_bar))], axis=-1).astype(BF16)
    cre = out_blocks(c_re.astype(F32)).astype(BF16)
    cim = out_blocks(-c_im.astype(F32)).astype(BF16)
    a_re = jnp.real(lam_bar).reshape(1, -1)
    a_im = jnp.imag(lam_bar).reshape(1, -1)
    return bm, cre, cim, a_re, a_im, d_skip.astype(F32).reshape(1, -1)


def _attn_block(q, k2, v2, allowed):
    first = lax.broadcasted_iota(jnp.int32, (QBLK, LANES), 1) < HEAD_DIM
    zero = jnp.zeros((), BF16)
    outs, lses = [], []
    for slab in range(ATTN_WIDTH // LANES):
        ls = slice(slab * LANES, (slab + 1) * LANES)
        qs = jnp.concatenate([jnp.where(first, q[:, ls], zero), jnp.where(first, zero, q[:, ls])], axis=0)
        s = lax.dot_general(qs, k2[:, ls], (((1,), (1,)), ((), ())), preferred_element_type=F32)
        s = jnp.where(allowed, s, NEG)
        m = jnp.max(s, axis=-1, keepdims=True)
        p = jnp.exp(s - m)
        denom = jnp.sum(p, axis=-1, keepdims=True)
        r = jnp.dot(p.astype(BF16), v2[:, ls], preferred_element_type=F32) / denom
        outs.append(jnp.where(first, r[:QBLK], r[QBLK:]))
        lse = m + jnp.log(denom)
        lses += [lse[:QBLK], lse[QBLK:]]
    return jnp.concatenate(outs, axis=-1).astype(BF16), jnp.concatenate(lses, axis=-1)


def _attn_kernel(q_ref, kp_ref, kc_ref, vp_ref, vc_ref, o_ref, lse_ref, *, steps, sub_blocks):
    j = pl.program_id(1)
    qi = lax.broadcasted_iota(jnp.int32, (2 * QBLK, 2 * QBLK), 0) % QBLK
    kj = lax.broadcasted_iota(jnp.int32, (2 * QBLK, 2 * QBLK), 1)
    dist = qi + QBLK - kj
    band = (dist >= 0) & (dist <= steps)
    for b in range(sub_blocks):
        rows = slice(b * QBLK, (b + 1) * QBLK)
        q = q_ref[rows, :] * (HEAD_DIM ** -0.5)
        if b == 0:
            k_prev, v_prev = kp_ref[...], vp_ref[...]
            allowed = band & ((kj >= QBLK) | (j > 0))
        else:
            before = slice((b - 1) * QBLK, b * QBLK)
            k_prev, v_prev = kc_ref[before, :], vc_ref[before, :]
            allowed = band
        k2 = jnp.concatenate([k_prev, kc_ref[rows, :]], axis=0)
        v2 = jnp.concatenate([v_prev, vc_ref[rows, :]], axis=0)
        o, lse = _attn_block(q, k2, v2, allowed)
        o_ref[rows, :] = o
        lse_ref[rows, :] = lse


def _attn(q, k, v, steps, sub_blocks=ATTN_SUB_BLOCKS):
    classes, n, width = q.shape
    sub_blocks = min(sub_blocks, n // QBLK)
    rows = sub_blocks * QBLK
    cur = pl.BlockSpec((None, rows, width), lambda c, j: (c, j, 0))
    prev = pl.BlockSpec((None, QBLK, width), lambda c, j: (c, jnp.maximum(j * sub_blocks - 1, 0), 0))
    return pl.pallas_call(
        functools.partial(_attn_kernel, steps=steps, sub_blocks=sub_blocks),
        grid=(classes, n // rows),
        in_specs=[cur, prev, cur, prev, cur],
        out_specs=[cur, pl.BlockSpec((None, rows, HEADS), lambda c, j: (c, j, 0))],
        out_shape=[jax.ShapeDtypeStruct((classes, n, width), BF16),
                   jax.ShapeDtypeStruct((classes, n, HEADS), F32)],
        compiler_params=_cparams(("arbitrary", "arbitrary")),
        name="attn",
    )(q, k, k, v, v)


def _layer_norm(z, g, b):
    mu = jnp.mean(z, axis=-1, keepdims=True)
    zc = z - mu
    var = jnp.mean(zc * zc, axis=-1, keepdims=True)
    return zc * lax.rsqrt(var + LN_EPS) * g + b


def _row_words(d):
    return d // (2 * LANES)


def _store_rows(ref, y):
    rows, d = y.shape
    sub = _row_words(d)
    for s in range(sub):
        lo = y[:, (2 * s) * LANES:(2 * s + 1) * LANES].astype(BF16).astype(F32)
        hi = y[:, (2 * s + 1) * LANES:(2 * s + 2) * LANES].astype(BF16).astype(F32)
        word = (pltpu.bitcast(lo, jnp.uint32) >> 16) | (pltpu.bitcast(hi, jnp.uint32) & HI_HALF)
        ref[pl.ds(s, rows, stride=sub), :] = word


def _load_rows(ref, rows, sub):
    parts = []
    for s in range(sub):
        word = ref[pl.ds(s, rows, stride=sub), :]
        parts.append(pltpu.bitcast(word << 16, F32))
        parts.append(pltpu.bitcast(word & HI_HALF, F32))
    return jnp.concatenate(parts, axis=-1)


def _expand_heads(w, e_ref):
    hi = w.astype(BF16)
    lo = (w - hi.astype(F32)).astype(BF16)
    return (jnp.dot(hi, e_ref[...], preferred_element_type=F32)
            + jnp.dot(lo, e_ref[...], preferred_element_type=F32))


def _merge_kernel(x_ref, yssm_ref, o0_ref, o1_ref, o2_ref, lse_ref, gs_ref, ga_ref,
                  wbs_ref, wba_ref, wout_ref, e_ref, g_ref, b_ref, o_ref, orow_ref, *, alpha):
    lse = lse_ref[...]
    l0, l1, l2 = lse[:, 0:HEADS], lse[:, HEADS:2 * HEADS], lse[:, 2 * HEADS:3 * HEADS]
    m = jnp.maximum(jnp.maximum(l0, l1), l2)
    e0, e1, e2 = jnp.exp(l0 - m), jnp.exp(l1 - m), jnp.exp(l2 - m)
    den = e0 + e1 + e2
    y_attn = (_expand_heads(e0 / den, e_ref) * o0_ref[...].astype(F32)
              + _expand_heads(e1 / den, e_ref) * o1_ref[...].astype(F32)
              + _expand_heads(e2 / den, e_ref) * o2_ref[...].astype(F32))
    ya = jnp.dot(y_attn.astype(BF16), wba_ref[...], preferred_element_type=F32)
    ys = jnp.dot(yssm_ref[...], wbs_ref[...], preferred_element_type=F32)
    merged = (jax.nn.sigmoid(gs_ref[...].astype(F32)) * ys
              + jax.nn.sigmoid(ga_ref[...].astype(F32)) * ya)
    mix = jnp.dot(merged.astype(BF16), wout_ref[...], preferred_element_type=F32)
    y = _layer_norm(alpha * x_ref[...] + mix, g_ref[...], b_ref[...])
    o_ref[...] = y
    _store_rows(orow_ref, y)


def _merge(x, y_ssm, o0, o1, o2, lse, proj, w_br_ssm, w_br_attn, w_out, expand, ln_g, ln_b,
           alpha, tm=MERGE_TM):
    n, d = x.shape
    tm = min(tm, n)
    gate_blk = proj.shape[1] // d
    row = lambda w: pl.BlockSpec((tm, w), lambda i: (i, 0))
    const = lambda a: pl.BlockSpec(a.shape, lambda i: (0,) * a.ndim)
    return pl.pallas_call(
        functools.partial(_merge_kernel, alpha=alpha),
        grid=(n // tm,),
        in_specs=[row(d), row(SSM_WIDTH), row(ATTN_WIDTH), row(ATTN_WIDTH), row(ATTN_WIDTH),
                  row(lse.shape[1]),
                  pl.BlockSpec((tm, d), lambda i: (i, gate_blk - 2)),
                  pl.BlockSpec((tm, d), lambda i: (i, gate_blk - 1)),
                  const(w_br_ssm), const(w_br_attn), const(w_out), const(expand),
                  const(ln_g), const(ln_b)],
        out_specs=[row(d), pl.BlockSpec((tm * _row_words(d), LANES), lambda i: (i, 0))],
        out_shape=[jax.ShapeDtypeStruct((n, d), F32),
                   jax.ShapeDtypeStruct((n * _row_words(d), LANES), jnp.uint32)],
        compiler_params=_cparams(("arbitrary",)),
        name="merge",
    )(x, y_ssm, o0, o1, o2, lse, proj, proj, w_br_ssm, w_br_attn, w_out, expand, ln_g, ln_b)


def _first_argmax(v, iota, size, axis):
    m = jnp.max(v, axis=axis, keepdims=True)
    idx = jnp.min(jnp.where(v == m, iota, size), axis=axis, keepdims=True)
    return m, idx


def _route_kernel(x_ref, rwh_ref, rwl_ref, bias_ref, eg_ref, tri_ref,
                  eidx_ref, gate_ref, rank_ref, cnt_ref, carry_ref):
    @pl.when(pl.program_id(0) == 0)
    def _():
        carry_ref[...] = jnp.zeros_like(carry_ref)

    x = x_ref[...]
    tm = x.shape[0]
    xh = x.astype(BF16)
    xl = (x - xh.astype(F32)).astype(BF16)
    nt = (((1,), (1,)), ((), ()))
    logits = (lax.dot_general(rwh_ref[...], xh, nt, preferred_element_type=F32)
              + lax.dot_general(rwh_ref[...], xl, nt, preferred_element_type=F32)
              + lax.dot_general(rwl_ref[...], xh, nt, preferred_element_type=F32))
    scores = jax.nn.sigmoid(logits)
    sel = scores + bias_ref[...]

    sel3 = sel.reshape(N_EXPERT_GROUPS, GROUP_SIZE, tm)
    iw = lax.broadcasted_iota(jnp.int32, sel3.shape, 1)
    m1, i1 = _first_argmax(sel3, iw, GROUP_SIZE, 1)
    m2 = jnp.max(jnp.where(iw == i1, NEG, sel3), axis=1, keepdims=True)
    gs = (m1 + m2).reshape(N_EXPERT_GROUPS, tm)

    ig = lax.broadcasted_iota(jnp.int32, gs.shape, 0)
    gmask = jnp.zeros(gs.shape, F32)
    for _ in range(TOPK_GROUPS):
        _, gi = _first_argmax(gs, ig, N_EXPERT_GROUPS, 0)
        hit = ig == gi
        gmask = jnp.where(hit, 1.0, gmask)
        gs = jnp.where(hit, NEG, gs)
    emask = jnp.dot(eg_ref[...], gmask.astype(BF16), preferred_element_type=F32) > 0.5

    masked = jnp.where(emask, sel, NEG)
    ie = lax.broadcasted_iota(jnp.int32, masked.shape, 0)
    chosen = jnp.zeros(masked.shape, F32)
    idxs, vals = [], []
    for _ in range(TOP_K):
        _, ei = _first_argmax(masked, ie, N_EXPERTS, 0)
        hit = ie == ei
        idxs.append(ei)
        vals.append(jnp.sum(jnp.where(hit, scores, 0.0), axis=0, keepdims=True))
        chosen = jnp.where(hit, 1.0, chosen)
        masked = jnp.where(hit, NEG, masked)
    total = vals[0]
    for v in vals[1:]:
        total = total + v

    prefix = jnp.dot(chosen.astype(BF16), tri_ref[...], preferred_element_type=F32)
    pos = carry_ref[:, 0:1] + prefix - 1.0
    ranks = [jnp.sum(jnp.where(ie == ei, pos, 0.0), axis=0, keepdims=True) for ei in idxs]
    carry = carry_ref[...] + jnp.sum(chosen, axis=1, keepdims=True)
    carry_ref[...] = carry
    cnt_ref[...] = carry

    eidx_ref[...] = jnp.concatenate(idxs, axis=0)
    gate_ref[...] = jnp.concatenate([v / total * ROUTED_SCALE for v in vals], axis=0)
    rank_ref[...] = jnp.concatenate(ranks, axis=0).astype(jnp.int32)


def _route(x, row0, n, rw_hi, rw_lo, bias, eg, tri, tm=ROUTE_TM):
    d = x.shape[1]
    tm = min(tm, n)
    blk0 = row0 // tm
    const = lambda a: pl.BlockSpec(a.shape, lambda i: (0,) * a.ndim)
    col = pl.BlockSpec((TOP_K, tm), lambda i: (0, i))
    return pl.pallas_call(
        _route_kernel,
        grid=(n // tm,),
        in_specs=[pl.BlockSpec((tm, d), lambda i: (i + blk0, 0)),
                  const(rw_hi), const(rw_lo), const(bias), const(eg), const(tri)],
        out_specs=[col, col, col, pl.BlockSpec((N_EXPERTS, LANES), lambda i: (0, 0))],
        out_shape=[jax.ShapeDtypeStruct((TOP_K, n), jnp.int32),
                   jax.ShapeDtypeStruct((TOP_K, n), F32),
                   jax.ShapeDtypeStruct((TOP_K, n), jnp.int32),
                   jax.ShapeDtypeStruct((N_EXPERTS, LANES), F32)],
        scratch_shapes=[pltpu.VMEM((N_EXPERTS, LANES), F32)],
        compiler_params=_cparams(("arbitrary",)),
        name="route",
    )(x, rw_hi, rw_lo, bias, eg, tri)


def _sc_gather_rows(src, idx, sub, chunk=SC_CHUNK):
    s = src.shape[0] // sub
    m = idx.shape[0]
    info = plsc.get_sparse_core_info()
    n_workers = info.num_cores * info.num_subcores
    per_worker = m // n_workers
    n_chunks = per_worker // chunk
    assert n_chunks * chunk * n_workers == m
    mesh = plsc.VectorSubcoreMesh(core_axis_name="c", subcore_axis_name="s")

    @functools.partial(
        pl.kernel, mesh=mesh,
        out_type=jax.ShapeDtypeStruct((m, sub, LANES), src.dtype),
        scratch_types=[pltpu.VMEM((n_chunks, chunk), jnp.int32),
                       pltpu.VMEM((chunk, sub, LANES), src.dtype),
                       pltpu.SemaphoreType.DMA])
    def gather(src_hbm, idx_hbm, out_hbm, idx_v, rows_v, sem):
        wid = lax.axis_index("s") * info.num_cores + lax.axis_index("c")
        pltpu.sync_copy(idx_hbm.at[wid], idx_v)

        @pl.loop(0, n_chunks)
        def _(j):
            pltpu.async_copy(src_hbm.at[idx_v.at[j]], rows_v, sem).wait()
            pltpu.sync_copy(rows_v, out_hbm.at[pl.ds(wid * per_worker + j * chunk, chunk)])

    out = gather(src.reshape(s, sub, LANES), idx.reshape(n_workers, n_chunks, chunk))
    return out.reshape(m * sub, LANES)


def _sc_scatter_rows(src, row0, dest, n_out, sub, chunk=SC_CHUNK):
    copies, n = dest.shape
    s = src.shape[0] // sub
    info = plsc.get_sparse_core_info()
    n_workers = info.num_cores * info.num_subcores
    per_worker = n // n_workers
    n_chunks = per_worker // chunk
    assert n_chunks * chunk * n_workers == n and copies * n == n_out
    mesh = plsc.VectorSubcoreMesh(core_axis_name="c", subcore_axis_name="s")
    idx = dest.reshape(copies, n_workers, n_chunks, chunk).transpose(1, 2, 0, 3)
    idx = idx.reshape(n_workers, n_chunks * copies, chunk)

    @functools.partial(
        pl.kernel, mesh=mesh,
        out_type=jax.ShapeDtypeStruct((n_out, sub, LANES), src.dtype),
        scratch_types=[pltpu.VMEM((n_chunks * copies, chunk), jnp.int32),
                       pltpu.VMEM((chunk, sub, LANES), src.dtype),
                       pltpu.SemaphoreType.DMA])
    def scatter(src_hbm, idx_hbm, out_hbm, idx_v, rows_v, sem):
        wid = lax.axis_index("s") * info.num_cores + lax.axis_index("c")
        pltpu.sync_copy(idx_hbm.at[wid], idx_v)

        @pl.loop(0, n_chunks)
        def _(j):
            pltpu.sync_copy(src_hbm.at[pl.ds(row0 + wid * per_worker + j * chunk, chunk)], rows_v)
            for c in range(copies):
                pltpu.async_copy(rows_v, out_hbm.at[idx_v.at[j * copies + c]], sem).wait()

    out = scatter(src.reshape(s, sub, LANES), idx)
    return out.reshape(n_out * sub, LANES)


def _gmm_kernel(be_ref, nu_ref, x_ref, wg_ref, wu_ref, wd_ref, o_ref, wgu_s, wd_s, *, hidden, blk, sub):
    b = pl.program_id(0)
    live = b < nu_ref[0]

    @pl.when(live & ((b == 0) | (be_ref[b] != be_ref[jnp.maximum(b - 1, 0)])))
    def _():
        wgu_s[:, :hidden] = wg_ref[...].astype(BF16)
        wgu_s[:, hidden:] = wu_ref[...].astype(BF16)
        wd_s[...] = wd_ref[...].astype(BF16)

    @pl.when(live)
    def _():
        x = _load_rows(x_ref, blk, sub).astype(BF16)
        gu = jnp.dot(x, wgu_s[...], preferred_element_type=F32)
        g, u = gu[:, :hidden], gu[:, hidden:]
        h = (g * jax.nn.sigmoid(g) * u).astype(BF16)
        _store_rows(o_ref, jnp.dot(h, wd_s[...], preferred_element_type=F32))


def _gmm(xs, w_gate, w_up, w_down, layer, blk_e, n_used, blk=MOE_BLK):
    hidden, d = w_down.shape[2:]
    sub = _row_words(d)
    n_rows = xs.shape[0] // sub
    row = lambda b, be, nu: (jnp.minimum(b, nu[0] - 1), 0)
    expert = lambda b, be, nu: (layer, be[b], 0, 0)
    return pl.pallas_call(
        functools.partial(_gmm_kernel, hidden=hidden, blk=blk, sub=sub),
        grid_spec=pltpu.PrefetchScalarGridSpec(
            num_scalar_prefetch=2,
            grid=(n_rows // blk,),
            in_specs=[pl.BlockSpec((blk * sub, LANES), row),
                      pl.BlockSpec((None, None, d, hidden), expert),
                      pl.BlockSpec((None, None, d, hidden), expert),
                      pl.BlockSpec((None, None, hidden, d), expert)],
            out_specs=pl.BlockSpec((blk * sub, LANES), row),
            scratch_shapes=[pltpu.VMEM((d, 2 * hidden), BF16), pltpu.VMEM((hidden, d), BF16)]),
        out_shape=jax.ShapeDtypeStruct((n_rows * sub, LANES), jnp.uint32),
        compiler_params=_cparams(("arbitrary",)),
        name="expert_gmm",
    )(blk_e, n_used, xs, w_gate, w_up, w_down)


def _combine_kernel(x_ref, yg_ref, gate_ref, wgu_ref, wd_ref, g_ref, b_ref, *rest, alpha, hidden):
    o_ref, ob_ref = rest[-2:]
    x = x_ref[...]
    gu = jnp.dot(x.astype(BF16), wgu_ref[...], preferred_element_type=F32)
    g, u = gu[:, :hidden], gu[:, hidden:]
    h = (g * jax.nn.sigmoid(g) * u).astype(BF16)
    acc = jnp.dot(h, wd_ref[...], preferred_element_type=F32)
    gates = gate_ref[...]
    tm, d = x.shape
    for k in range(TOP_K):
        acc = acc + gates[:, k:k + 1] * _load_rows(yg_ref.at[k], tm, _row_words(d))
    y = _layer_norm(alpha * x + acc, g_ref[...], b_ref[...])
    o_ref[...] = y
    ob_ref[...] = y.astype(BF16)


def _combine(x, row0, yg, gates_t, w_gu, w_down, ln_g, ln_b, alpha, prev=None, tm=COMBINE_TM):
    n, d = x.shape
    n_part = gates_t.shape[0]
    tm = min(tm, n_part)
    blk0 = row0 // tm
    sub = _row_words(d)
    hidden = w_down.shape[0]
    const = lambda a: pl.BlockSpec(a.shape, lambda i: (0,) * a.ndim)
    rows = pl.BlockSpec((tm, d), lambda i: (i + blk0, 0))
    in_specs = [rows,
                pl.BlockSpec((TOP_K, tm * sub, LANES), lambda i: (0, i, 0)),
                pl.BlockSpec((tm, TOP_K), lambda i: (i, 0)),
                const(w_gu), const(w_down), const(ln_g), const(ln_b)]
    args = [x, yg, gates_t, w_gu, w_down, ln_g, ln_b]
    aliases = {}
    if prev is not None:
        aliases = {len(args): 0, len(args) + 1: 1}
        in_specs += [pl.BlockSpec(memory_space=pl.ANY)] * 2
        args += list(prev)
    return pl.pallas_call(
        functools.partial(_combine_kernel, alpha=alpha, hidden=hidden),
        grid=(n_part // tm,),
        in_specs=in_specs,
        out_specs=[rows, rows],
        out_shape=[jax.ShapeDtypeStruct((n, d), F32), jax.ShapeDtypeStruct((n, d), BF16)],
        input_output_aliases=aliases,
        compiler_params=_cparams(("arbitrary",)),
        name="combine",
    )(*args)


def _split_w_in(w_in, b_in, d):
    qkv0 = SSM_WIDTH
    qkv_w = len(ATTN_PATTERNS) * ATTN_WIDTH
    gates0 = qkv0 + 3 * qkv_w
    cols = lambda a, c0, w: a[:, c0:c0 + w]
    pick = lambda a: jnp.concatenate([cols(a, gates0, 2 * d), cols(a, 0, SSM_WIDTH)], axis=1)
    main = (pick(w_in).astype(BF16), pick(b_in))
    groups = []
    for g in range(len(ATTN_PATTERNS)):
        sel = lambda a: jnp.concatenate([cols(a, qkv0 + s * qkv_w + g * ATTN_WIDTH, ATTN_WIDTH)
                                         for s in range(3)], axis=1)
        groups.append((sel(w_in).astype(BF16), sel(b_in)))
    return main, groups


def _token_mixer(x, xb, batch, w_in, b_in, s5p, w_glu, w_br_ssm, w_br_attn, w_out, expand, ln_g, ln_b, alpha):
    n, d = x.shape
    seq = n // batch
    (w_main, b_main), groups = _split_w_in(w_in, b_in, d)
    proj = _inproj(xb, w_main, b_main)
    y_ssm = _s5(proj, *s5p, w_glu, batch)

    outs, lses = [], []
    for (window, dil), (w_g, b_g) in zip(ATTN_PATTERNS, groups):
        assert window // dil == QBLK and (seq // dil) % QBLK == 0
        q, k, v = _inproj_qkv(xb, w_g, b_g, dil * batch)
        o_g, lse_g = _attn(q, k, v, window // dil)
        outs.append(jnp.swapaxes(o_g, 0, 1).reshape(n, ATTN_WIDTH))
        lses.append(jnp.swapaxes(lse_g, 0, 1).reshape(n, HEADS))
    lse = jnp.concatenate(lses, axis=-1)
    return _merge(x, y_ssm, outs[0], outs[1], outs[2], lse, proj, w_br_ssm, w_br_attn, w_out,
                  expand, ln_g, ln_b, alpha)


def _moe_part(x, x_rows, row0, n, out_prev, rw_hi, rw_lo, bias, eg, tri, w_gate, w_up, w_down, layer,
              sh_gu, sh_down, ln_g, ln_b, alpha):
    d = x.shape[1]
    sub = _row_words(d)
    eidx, gates, rank, cnt = _route(x, row0, n, rw_hi, rw_lo, bias, eg, tri)
    counts = cnt[:, 0].astype(jnp.int32)
    nk = n * TOP_K
    n_pad = N_EXPERTS * MOE_BLK
    n_rows = nk + n_pad
    n_blocks = n_rows // MOE_BLK
    assert n_pad % n == 0
    pad_counts = (counts + MOE_BLK - 1) // MOE_BLK * MOE_BLK
    pad_end = jnp.cumsum(pad_counts)
    pad_start = pad_end - pad_counts
    onehot = eidx[:, :, None] == jnp.arange(N_EXPERTS, dtype=jnp.int32)
    dest = jnp.sum(jnp.where(onehot, pad_start, 0), axis=-1) + rank
    seg_start = jnp.concatenate([pad_start + counts, pad_end[-1:]])
    seg_len = jnp.concatenate([pad_counts - counts, n_rows - pad_end[-1:]])
    seg_end = jnp.cumsum(seg_len)
    q = jnp.arange(n_pad, dtype=jnp.int32)[:, None]
    seg = jnp.sum((q >= seg_end[None, :]).astype(jnp.int32), axis=1, keepdims=True)
    seg_hot = seg == jnp.arange(N_EXPERTS + 1, dtype=jnp.int32)[None, :]
    pad_dest = jnp.sum(jnp.where(seg_hot, seg_start - (seg_end - seg_len), 0), axis=1) + q[:, 0]
    dest_all = jnp.concatenate([dest, pad_dest.reshape(n_pad // n, n)], axis=0).astype(jnp.int32)
    blk_first = jnp.arange(n_blocks, dtype=jnp.int32)[:, None] * MOE_BLK
    blk_e = jnp.minimum(jnp.sum((blk_first >= pad_end[None, :]).astype(jnp.int32), axis=1), N_EXPERTS - 1)
    n_used = (pad_end[-1:] // MOE_BLK).astype(jnp.int32)

    xs = _sc_scatter_rows(x_rows, row0, dest_all, n_rows, sub)
    ys = _gmm(xs, w_gate, w_up, w_down, layer, blk_e, n_used)
    yg = _sc_gather_rows(ys, dest.reshape(nk), sub).reshape(TOP_K, n * sub, LANES)
    return _combine(x, row0, yg, gates.T, sh_gu, sh_down, ln_g, ln_b, alpha, out_prev)


def _moe(x, x_rows, *params):
    n = x.shape[0]
    part = n // MOE_PARTS
    out = None
    for i in range(MOE_PARTS):
        out = _moe_part(x, x_rows, i * part, part, out, *params)
    return out


def kernel(x, w_in, b_in, ssm_lam_re, ssm_lam_im, ssm_log_dt, ssm_b_re, ssm_b_im, ssm_c_re, ssm_c_im, ssm_d, w_glu, w_br_ssm, w_br_attn, w_out, ln1_g, ln1_b, router_w, router_bias, exp_w_gate, exp_w_up, exp_w_down, sh_w_gate, sh_w_up, sh_w_down, ln2_g, ln2_b):
    batch, seq, d = x.shape
    depth = w_in.shape[0]
    assert batch == SUBLANES
    alpha = (2 * depth) ** 0.25
    n = batch * seq
    xt = jnp.swapaxes(x, 0, 1).reshape(n, d)
    xb = xt.astype(BF16)

    expand = jnp.repeat(jnp.eye(HEADS, dtype=BF16), HEAD_DIM, axis=1)
    eg = jnp.repeat(jnp.eye(N_EXPERT_GROUPS, dtype=BF16), GROUP_SIZE, axis=0)
    tm = min(ROUTE_TM, n)
    tri = (jnp.arange(tm)[:, None] <= jnp.arange(tm)[None, :]).astype(BF16)
    row = lambda a: a.astype(F32).reshape(1, -1)

    for l in range(depth):
        s5p = _s5_params(ssm_lam_re[l], ssm_lam_im[l], ssm_log_dt[l], ssm_b_re[l], ssm_b_im[l],
                         ssm_c_re[l], ssm_c_im[l], ssm_d[l])
        xt, xt_rows = _token_mixer(xt, xb, batch, w_in[l], row(b_in[l]), s5p,
                                   w_glu[l].astype(BF16), w_br_ssm[l].astype(BF16),
                                   w_br_attn[l].astype(BF16), w_out[l].astype(BF16),
                                   expand, row(ln1_g[l]), row(ln1_b[l]), alpha)
        rwt = router_w[l].astype(F32).T
        rw_hi = rwt.astype(BF16)
        rw_lo = (rwt - rw_hi.astype(F32)).astype(BF16)
        sh_gu = jnp.concatenate([sh_w_gate[l], sh_w_up[l]], axis=-1).astype(BF16)
        xt, xb = _moe(xt, xt_rows, rw_hi, rw_lo, router_bias[l].astype(F32).reshape(-1, 1), eg, tri,
                  exp_w_gate, exp_w_up, exp_w_down, l, sh_gu, sh_w_down[l].astype(BF16),
                  row(ln2_g[l]), row(ln2_b[l]), alpha)
    return jnp.swapaxes(xt.reshape(seq, batch, d), 0, 1)
```

```python
import functools
import math

import jax
import jax.numpy as jnp
import numpy as np
from jax import lax
from jax.experimental import pallas as pl
from jax.experimental.pallas import tpu as pltpu
from jax.experimental.pallas import tpu_sc as plsc

F32 = jnp.float32
BF16 = jnp.bfloat16

SSM_GROUP = 16
SSM_GROUPS = 32
SSM_WIDTH = SSM_GROUP * SSM_GROUPS
SSM_STATE = 64
HEAD_DIM = 64
HEADS = 8
ATTN_PATTERNS = ((128, 1), (512, 4), (2048, 16))
ATTN_WIDTH = HEADS * HEAD_DIM
QBLK = 128
N_EXPERTS = 64
TOP_K = 8
N_EXPERT_GROUPS = 8
GROUP_SIZE = N_EXPERTS // N_EXPERT_GROUPS
TOPK_GROUPS = 4
ROUTED_SCALE = 2.5
LN_EPS = 1e-5
NEG = -1e30

LANES = 128
SUBLANES = 8
VMEM_LIMIT = 56 * 1024 * 1024

INPROJ_TM = 2048
INPROJ_TN = 1280
QKV_TM = 2048
S5_T = 128
S5_KB = 4
ATTN_SUB_BLOCKS = 4
MERGE_TM = 512
ROUTE_TM = 1024
MOE_BLK = 512
MOE_PARTS = 2
SC_CHUNK = 128
HI_HALF = np.uint32(0xFFFF0000)
COMBINE_TM = 512


def _cparams(sem):
    return pltpu.CompilerParams(dimension_semantics=sem, vmem_limit_bytes=VMEM_LIMIT)


def _inproj_kernel(x_ref, w_ref, b_ref, o_ref):
    acc = jnp.dot(x_ref[...], w_ref[...], preferred_element_type=F32)
    o_ref[...] = (acc + b_ref[...]).astype(BF16)


def _inproj(xb, w, b, tm=INPROJ_TM, tn=INPROJ_TN):
    n, d = xb.shape
    width = w.shape[1]
    tm = min(tm, n)
    return pl.pallas_call(
        _inproj_kernel,
        grid=(n // tm, width // tn),
        in_specs=[pl.BlockSpec((tm, d), lambda i, j: (i, 0)),
                  pl.BlockSpec((d, tn), lambda i, j: (0, j)),
                  pl.BlockSpec((1, tn), lambda i, j: (0, j))],
        out_specs=pl.BlockSpec((tm, tn), lambda i, j: (i, j)),
        out_shape=jax.ShapeDtypeStruct((n, width), BF16),
        compiler_params=_cparams(("arbitrary", "arbitrary")),
        name="inproj",
    )(xb, w, b)


def _inproj_qkv_kernel(x_ref, w_ref, b_ref, q_ref, k_ref, v_ref, y_ref, *, classes):
    x = x_ref[...]
    rows = x.shape[0] // classes
    for part, o_ref in enumerate((q_ref, k_ref, v_ref)):
        cols = slice(part * ATTN_WIDTH, (part + 1) * ATTN_WIDTH)
        y = jnp.dot(x, w_ref[:, cols], preferred_element_type=F32) + b_ref[:, cols]
        y = pltpu.einshape("jcf->cjf", y.reshape(rows, classes, ATTN_WIDTH))
        o_ref[...] = y.astype(BF16)


def _inproj_qkv(xb, w, b, classes, tm=QKV_TM):
    n, d = xb.shape
    tm = min(tm, n)
    rows = tm // classes
    out = pl.BlockSpec((classes, rows, ATTN_WIDTH), lambda i: (0, i, 0))
    shape = jax.ShapeDtypeStruct((classes, n // classes, ATTN_WIDTH), BF16)
    return pl.pallas_call(
        functools.partial(_inproj_qkv_kernel, classes=classes),
        grid=(n // tm,),
        in_specs=[pl.BlockSpec((tm, d), lambda i: (i, 0)),
                  pl.BlockSpec(w.shape, lambda i: (0, 0)),
                  pl.BlockSpec(b.shape, lambda i: (0, 0))],
        out_specs=[out, out, out],
        out_shape=[shape, shape, shape],
        scratch_shapes=[pltpu.VMEM((ATTN_WIDTH // LANES, tm, LANES), F32)],
        compiler_params=_cparams(("arbitrary",)),
        name="inproj_qkv",
    )(xb, w, b)


def _gelu_tanh(x):
    c = math.sqrt(2.0 / math.pi)
    return 0.5 * x * (1.0 + jnp.tanh(c * (x + 0.044715 * (x * x * x))))


def _s5_kernel(u_ref, bm_ref, cre_ref, cim_ref, are_ref, aim_ref, d_ref, wglu_ref,
               o_ref, sre_ref, sim_ref, st_re_ref, st_im_ref, y_ref, *, steps, batch):
    kw = SSM_WIDTH // S5_KB
    sw = SSM_GROUPS * SSM_STATE // S5_KB

    @pl.when(pl.program_id(0) == 0)
    def _():
        st_re_ref[...] = jnp.zeros_like(st_re_ref)
        st_im_ref[...] = jnp.zeros_like(st_im_ref)

    for k in range(S5_KB):
        ls = slice(k * sw, (k + 1) * sw)
        bu = jnp.dot(u_ref[:, k * kw:(k + 1) * kw], bm_ref[k], preferred_element_type=F32)
        sre_ref[:, ls] = bu[:, :sw]
        sim_ref[:, ls] = bu[:, sw:]

        ar = jnp.broadcast_to(are_ref[:, ls], (batch, sw))
        ai = jnp.broadcast_to(aim_ref[:, ls], (batch, sw))

        def step(t, carry):
            sr, si = carry
            rows = pl.ds(pl.multiple_of(t * batch, batch), batch)
            nr = ar * sr - ai * si + sre_ref[rows, ls]
            ni = ar * si + ai * sr + sim_ref[rows, ls]
            sre_ref[rows, ls] = nr
            sim_ref[rows, ls] = ni
            return nr, ni

        sr, si = lax.fori_loop(0, steps, step, (st_re_ref[:, ls], st_im_ref[:, ls]))
        st_re_ref[:, ls] = sr
        st_im_ref[:, ls] = si

        y_ref[:, k * kw:(k + 1) * kw] = (
            jnp.dot(sre_ref[:, ls].astype(BF16), cre_ref[k], preferred_element_type=F32)
            + jnp.dot(sim_ref[:, ls].astype(BF16), cim_ref[k], preferred_element_type=F32))

    y = y_ref[...] + d_ref[...] * u_ref[...].astype(F32)
    y = _gelu_tanh(y)
    z = jnp.dot(y.astype(BF16), wglu_ref[...], preferred_element_type=F32)
    o_ref[...] = (y * jax.nn.sigmoid(z)).astype(BF16)


def _s5(proj, bm, cre, cim, a_re, a_im, d_skip, w_glu, batch, steps=S5_T):
    n = proj.shape[0]
    u_blk = proj.shape[1] // SSM_WIDTH - 1
    seq = n // batch
    steps = min(steps, seq)
    rows = steps * batch
    nstate = SSM_GROUPS * SSM_STATE
    const = lambda shape: pl.BlockSpec(shape, lambda i: (0,) * len(shape))
    return pl.pallas_call(
        functools.partial(_s5_kernel, steps=steps, batch=batch),
        grid=(seq // steps,),
        in_specs=[pl.BlockSpec((rows, SSM_WIDTH), lambda i: (i, u_blk)),
                  const(bm.shape), const(cre.shape), const(cim.shape),
                  const(a_re.shape), const(a_im.shape), const(d_skip.shape), const(w_glu.shape)],
        out_specs=pl.BlockSpec((rows, SSM_WIDTH), lambda i: (i, 0)),
        out_shape=jax.ShapeDtypeStruct((n, SSM_WIDTH), BF16),
        scratch_shapes=[pltpu.VMEM((rows, nstate), F32), pltpu.VMEM((rows, nstate), F32),
                        pltpu.VMEM((batch, nstate), F32), pltpu.VMEM((batch, nstate), F32),
                        pltpu.VMEM((rows, SSM_WIDTH), F32)],
        compiler_params=_cparams(("arbitrary",)),
        name="s5",
    )(proj, bm, cre, cim, a_re, a_im, d_skip, w_glu)


def _s5_params(lam_re, lam_im, log_dt, b_re, b_im, c_re, c_im, d_skip):
    lam = lax.complex(lam_re.astype(F32), lam_im.astype(F32))
    dt = jnp.exp(log_dt.astype(F32))[:, None]
    lam_bar = jnp.exp(lam * dt)
    b_bar = ((lam_bar - 1.0) / lam)[:, :, None] * lax.complex(b_re.astype(F32), b_im.astype(F32))
    gl = SSM_GROUPS // S5_KB
    eye = jnp.eye(gl, dtype=F32)

    def in_blocks(b):
        b = b.reshape(S5_KB, gl, SSM_STATE, SSM_GROUP)
        m = jnp.einsum('kgpc,gh->kgchp', b, eye)
        return m.reshape(S5_KB, gl * SSM_GROUP, gl * SSM_STATE)

    def out_blocks(c):
        c = c.reshape(S5_KB, gl, SSM_GROUP, SSM_STATE)
        m = jnp.einsum('kgcp,gh->kgphc', c, eye)
        return m.reshape(S5_KB, gl * SSM_STATE, gl * SSM_GROUP)

    bm = jnp.concatenate([in_blocks(jnp.real(b_bar)), in_blocks(jnp.imag(b_bar))], axis=-1).astype(BF16)
    cre = out_blocks(c_re.astype(F32)).astype(BF16)
    cim = out_blocks(-c_im.astype(F32)).astype(BF16)
    a_re = jnp.real(lam_bar).reshape(1, -1)
    a_im = jnp.imag(lam_bar).reshape(1, -1)
    return bm, cre, cim, a_re, a_im, d_skip.astype(F32).reshape(1, -1)


def _attn_block(q, k2, v2, allowed):
    first = lax.broadcasted_iota(jnp.int32, (QBLK, LANES), 1) < HEAD_DIM
    zero = jnp.zeros((), BF16)
    outs, lses = [], []
    for slab in range(ATTN_WIDTH // LANES):
        ls = slice(slab * LANES, (slab + 1) * LANES)
        qs = jnp.concatenate([jnp.where(first, q[:, ls], zero), jnp.where(first, zero, q[:, ls])], axis=0)
        s = lax.dot_general(qs, k2[:, ls], (((1,), (1,)), ((), ())), preferred_element_type=F32)
        s = jnp.where(allowed, s, NEG)
        m = jnp.max(s, axis=-1, keepdims=True)
        p = jnp.exp(s - m)
        denom = jnp.sum(p, axis=-1, keepdims=True)
        r = jnp.dot(p.astype(BF16), v2[:, ls], preferred_element_type=F32) / denom
        outs.append(jnp.where(first, r[:QBLK], r[QBLK:]))
        lse = m + jnp.log(denom)
        lses += [lse[:QBLK], lse[QBLK:]]
    return jnp.concatenate(outs, axis=-1).astype(BF16), jnp.concatenate(lses, axis=-1)


def _attn_kernel(q_ref, kp_ref, kc_ref, vp_ref, vc_ref, o_ref, lse_ref, *, steps, sub_blocks):
    j = pl.program_id(1)
    qi = lax.broadcasted_iota(jnp.int32, (2 * QBLK, 2 * QBLK), 0) % QBLK
    kj = lax.broadcasted_iota(jnp.int32, (2 * QBLK, 2 * QBLK), 1)
    dist = qi + QBLK - kj
    band = (dist >= 0) & (dist <= steps)
    for b in range(sub_blocks):
        rows = slice(b * QBLK, (b + 1) * QBLK)
        q = q_ref[rows, :] * (HEAD_DIM ** -0.5)
        if b == 0:
            k_prev, v_prev = kp_ref[...], vp_ref[...]
            allowed = band & ((kj >= QBLK) | (j > 0))
        else:
            before = slice((b - 1) * QBLK, b * QBLK)
            k_prev, v_prev = kc_ref[before, :], vc_ref[before, :]
            allowed = band
        k2 = jnp.concatenate([k_prev, kc_ref[rows, :]], axis=0)
        v2 = jnp.concatenate([v_prev, vc_ref[rows, :]], axis=0)
        o, lse = _attn_block(q, k2, v2, allowed)
        o_ref[rows, :] = o
        lse_ref[rows, :] = lse


def _attn(q, k, v, steps, sub_blocks=ATTN_SUB_BLOCKS):
    classes, n, width = q.shape
    sub_blocks = min(sub_blocks, n // QBLK)
    rows = sub_blocks * QBLK
    cur = pl.BlockSpec((None, rows, width), lambda c, j: (c, j, 0))
    prev = pl.BlockSpec((None, QBLK, width), lambda c, j: (c, jnp.maximum(j * sub_blocks - 1, 0), 0))
    return pl.pallas_call(
        functools.partial(_attn_kernel, steps=steps, sub_blocks=sub_blocks),
        grid=(classes, n // rows),
        in_specs=[cur, prev, cur, prev, cur],
        out_specs=[cur, pl.BlockSpec((None, rows, HEADS), lambda c, j: (c, j, 0))],
        out_shape=[jax.ShapeDtypeStruct((classes, n, width), BF16),
                   jax.ShapeDtypeStruct((classes, n, HEADS), F32)],
        compiler_params=_cparams(("arbitrary", "arbitrary")),
        name="attn",
    )(q, k, k, v, v)


def _layer_norm(z, g, b):
    mu = jnp.mean(z, axis=-1, keepdims=True)
    zc = z - mu
    var = jnp.mean(zc * zc, axis=-1, keepdims=True)
    return zc * lax.rsqrt(var + LN_EPS) * g + b


def _row_words(d):
    return d // (2 * LANES)


def _store_rows(ref, y):
    rows, d = y.shape
    sub = _row_words(d)
    for s in range(sub):
        lo = y[:, (2 * s) * LANES:(2 * s + 1) * LANES].astype(BF16).astype(F32)
        hi = y[:, (2 * s + 1) * LANES:(2 * s + 2) * LANES].astype(BF16).astype(F32)
        word = (pltpu.bitcast(lo, jnp.uint32) >> 16) | (pltpu.bitcast(hi, jnp.uint32) & HI_HALF)
        ref[pl.ds(s, rows, stride=sub), :] = word


def _load_rows(ref, rows, sub):
    parts = []
    for s in range(sub):
        word = ref[pl.ds(s, rows, stride=sub), :]
        parts.append(pltpu.bitcast(word << 16, F32))
        parts.append(pltpu.bitcast(word & HI_HALF, F32))
    return jnp.concatenate(parts, axis=-1)


def _expand_heads(w, e_ref):
    hi = w.astype(BF16)
    lo = (w - hi.astype(F32)).astype(BF16)
    return (jnp.dot(hi, e_ref[...], preferred_element_type=F32)
            + jnp.dot(lo, e_ref[...], preferred_element_type=F32))


def _merge_kernel(x_ref, yssm_ref, o0_ref, o1_ref, o2_ref, lse_ref, gs_ref, ga_ref,
                  wbs_ref, wba_ref, wout_ref, e_ref, g_ref, b_ref, o_ref, orow_ref, *, alpha):
    lse = lse_ref[...]
    l0, l1, l2 = lse[:, 0:HEADS], lse[:, HEADS:2 * HEADS], lse[:, 2 * HEADS:3 * HEADS]
    m = jnp.maximum(jnp.maximum(l0, l1), l2)
    e0, e1, e2 = jnp.exp(l0 - m), jnp.exp(l1 - m), jnp.exp(l2 - m)
    den = e0 + e1 + e2
    y_attn = (_expand_heads(e0 / den, e_ref) * o0_ref[...].astype(F32)
              + _expand_heads(e1 / den, e_ref) * o1_ref[...].astype(F32)
              + _expand_heads(e2 / den, e_ref) * o2_ref[...].astype(F32))
    ya = jnp.dot(y_attn.astype(BF16), wba_ref[...], preferred_element_type=F32)
    ys = jnp.dot(yssm_ref[...], wbs_ref[...], preferred_element_type=F32)
    merged = (jax.nn.sigmoid(gs_ref[...].astype(F32)) * ys
              + jax.nn.sigmoid(ga_ref[...].astype(F32)) * ya)
    mix = jnp.dot(merged.astype(BF16), wout_ref[...], preferred_element_type=F32)
    y = _layer_norm(alpha * x_ref[...] + mix, g_ref[...], b_ref[...])
    o_ref[...] = y
    _store_rows(orow_ref, y)


def _merge(x, y_ssm, o0, o1, o2, lse, proj, w_br_ssm, w_br_attn, w_out, expand, ln_g, ln_b,
           alpha, tm=MERGE_TM):
    n, d = x.shape
    tm = min(tm, n)
    gate_blk = proj.shape[1] // d
    row = lambda w: pl.BlockSpec((tm, w), lambda i: (i, 0))
    const = lambda a: pl.BlockSpec(a.shape, lambda i: (0,) * a.ndim)
    return pl.pallas_call(
        functools.partial(_merge_kernel, alpha=alpha),
        grid=(n // tm,),
        in_specs=[row(d), row(SSM_WIDTH), row(ATTN_WIDTH), row(ATTN_WIDTH), row(ATTN_WIDTH),
                  row(lse.shape[1]),
                  pl.BlockSpec((tm, d), lambda i: (i, gate_blk - 2)),
                  pl.BlockSpec((tm, d), lambda i: (i, gate_blk - 1)),
                  const(w_br_ssm), const(w_br_attn), const(w_out), const(expand),
                  const(ln_g), const(ln_b)],
        out_specs=[row(d), pl.BlockSpec((tm * _row_words(d), LANES), lambda i: (i, 0))],
        out_shape=[jax.ShapeDtypeStruct((n, d), F32),
                   jax.ShapeDtypeStruct((n * _row_words(d), LANES), jnp.uint32)],
        compiler_params=_cparams(("arbitrary",)),
        name="merge",
    )(x, y_ssm, o0, o1, o2, lse, proj, proj, w_br_ssm, w_br_attn, w_out, expand, ln_g, ln_b)


def _first_argmax(v, iota, size, axis):
    m = jnp.max(v, axis=axis, keepdims=True)
    idx = jnp.min(jnp.where(v == m, iota, size), axis=axis, keepdims=True)
    return m, idx


def _route_kernel(x_ref, rwh_ref, rwl_ref, bias_ref, eg_ref, tri_ref,
                  eidx_ref, gate_ref, rank_ref, cnt_ref, carry_ref):
    @pl.when(pl.program_id(0) == 0)
    def _():
        carry_ref[...] = jnp.zeros_like(carry_ref)

    x = x_ref[...]
    tm = x.shape[0]
    xh = x.astype(BF16)
    xl = (x - xh.astype(F32)).astype(BF16)
    nt = (((1,), (1,)), ((), ()))
    logits = (lax.dot_general(rwh_ref[...], xh, nt, preferred_element_type=F32)
              + lax.dot_general(rwh_ref[...], xl, nt, preferred_element_type=F32)
              + lax.dot_general(rwl_ref[...], xh, nt, preferred_element_type=F32))
    scores = jax.nn.sigmoid(logits)
    sel = scores + bias_ref[...]

    sel3 = sel.reshape(N_EXPERT_GROUPS, GROUP_SIZE, tm)
    iw = lax.broadcasted_iota(jnp.int32, sel3.shape, 1)
    m1, i1 = _first_argmax(sel3, iw, GROUP_SIZE, 1)
    m2 = jnp.max(jnp.where(iw == i1, NEG, sel3), axis=1, keepdims=True)
    gs = (m1 + m2).reshape(N_EXPERT_GROUPS, tm)

    ig = lax.broadcasted_iota(jnp.int32, gs.shape, 0)
    gmask = jnp.zeros(gs.shape, F32)
    for _ in range(TOPK_GROUPS):
        _, gi = _first_argmax(gs, ig, N_EXPERT_GROUPS, 0)
        hit = ig == gi
        gmask = jnp.where(hit, 1.0, gmask)
        gs = jnp.where(hit, NEG, gs)
    emask = jnp.dot(eg_ref[...], gmask.astype(BF16), preferred_element_type=F32) > 0.5

    masked = jnp.where(emask, sel, NEG)
    ie = lax.broadcasted_iota(jnp.int32, masked.shape, 0)
    chosen = jnp.zeros(masked.shape, F32)
    idxs, vals = [], []
    for _ in range(TOP_K):
        _, ei = _first_argmax(masked, ie, N_EXPERTS, 0)
        hit = ie == ei
        idxs.append(ei)
        vals.append(jnp.sum(jnp.where(hit, scores, 0.0), axis=0, keepdims=True))
        chosen = jnp.where(hit, 1.0, chosen)
        masked = jnp.where(hit, NEG, masked)
    total = vals[0]
    for v in vals[1:]:
        total = total + v

    prefix = jnp.dot(chosen.astype(BF16), tri_ref[...], preferred_element_type=F32)
    pos = carry_ref[:, 0:1] + prefix - 1.0
    ranks = [jnp.sum(jnp.where(ie == ei, pos, 0.0), axis=0, keepdims=True) for ei in idxs]
    carry = carry_ref[...] + jnp.sum(chosen, axis=1, keepdims=True)
    carry_ref[...] = carry
    cnt_ref[...] = carry

    eidx_ref[...] = jnp.concatenate(idxs, axis=0)
    gate_ref[...] = jnp.concatenate([v / total * ROUTED_SCALE for v in vals], axis=0)
    rank_ref[...] = jnp.concatenate(ranks, axis=0).astype(jnp.int32)


def _route(x, row0, n, rw_hi, rw_lo, bias, eg, tri, tm=ROUTE_TM):
    d = x.shape[1]
    tm = min(tm, n)
    blk0 = row0 // tm
    const = lambda a: pl.BlockSpec(a.shape, lambda i: (0,) * a.ndim)
    col = pl.BlockSpec((TOP_K, tm), lambda i: (0, i))
    return pl.pallas_call(
        _route_kernel,
        grid=(n // tm,),
        in_specs=[pl.BlockSpec((tm, d), lambda i: (i + blk0, 0)),
                  const(rw_hi), const(rw_lo), const(bias), const(eg), const(tri)],
        out_specs=[col, col, col, pl.BlockSpec((N_EXPERTS, LANES), lambda i: (0, 0))],
        out_shape=[jax.ShapeDtypeStruct((TOP_K, n), jnp.int32),
                   jax.ShapeDtypeStruct((TOP_K, n), F32),
                   jax.ShapeDtypeStruct((TOP_K, n), jnp.int32),
                   jax.ShapeDtypeStruct((N_EXPERTS, LANES), F32)],
        scratch_shapes=[pltpu.VMEM((N_EXPERTS, LANES), F32)],
        compiler_params=_cparams(("arbitrary",)),
        name="route",
    )(x, rw_hi, rw_lo, bias, eg, tri)


def _sc_gather_rows(src, idx, sub, chunk=SC_CHUNK):
    s = src.shape[0] // sub
    m = idx.shape[0]
    info = plsc.get_sparse_core_info()
    n_workers = info.num_cores * info.num_subcores
    per_worker = m // n_workers
    n_chunks = per_worker // chunk
    assert n_chunks * chunk * n_workers == m
    mesh = plsc.VectorSubcoreMesh(core_axis_name="c", subcore_axis_name="s")

    @functools.partial(
        pl.kernel, mesh=mesh,
        out_type=jax.ShapeDtypeStruct((m, sub, LANES), src.dtype),
        scratch_types=[pltpu.VMEM((n_chunks, chunk), jnp.int32),
                       pltpu.VMEM((chunk, sub, LANES), src.dtype),
                       pltpu.SemaphoreType.DMA])
    def gather(src_hbm, idx_hbm, out_hbm, idx_v, rows_v, sem):
        wid = lax.axis_index("s") * info.num_cores + lax.axis_index("c")
        pltpu.sync_copy(idx_hbm.at[wid], idx_v)

        @pl.loop(0, n_chunks)
        def _(j):
            pltpu.async_copy(src_hbm.at[idx_v.at[j]], rows_v, sem).wait()
            pltpu.sync_copy(rows_v, out_hbm.at[pl.ds(wid * per_worker + j * chunk, chunk)])

    out = gather(src.reshape(s, sub, LANES), idx.reshape(n_workers, n_chunks, chunk))
    return out.reshape(m * sub, LANES)


def _sc_scatter_rows(src, row0, dest, n_out, sub, chunk=SC_CHUNK):
    copies, n = dest.shape
    s = src.shape[0] // sub
    info = plsc.get_sparse_core_info()
    n_workers = info.num_cores * info.num_subcores
    per_worker = n // n_workers
    n_chunks = per_worker // chunk
    assert n_chunks * chunk * n_workers == n and copies * n == n_out
    mesh = plsc.VectorSubcoreMesh(core_axis_name="c", subcore_axis_name="s")
    idx = dest.reshape(copies, n_workers, n_chunks, chunk).transpose(1, 2, 0, 3)
    idx = idx.reshape(n_workers, n_chunks * copies, chunk)

    @functools.partial(
        pl.kernel, mesh=mesh,
        out_type=jax.ShapeDtypeStruct((n_out, sub, LANES), src.dtype),
        scratch_types=[pltpu.VMEM((n_chunks * copies, chunk), jnp.int32),
                       pltpu.VMEM((chunk, sub, LANES), src.dtype),
                       pltpu.SemaphoreType.DMA])
    def scatter(src_hbm, idx_hbm, out_hbm, idx_v, rows_v, sem):
        wid = lax.axis_index("s") * info.num_cores + lax.axis_index("c")
        pltpu.sync_copy(idx_hbm.at[wid], idx_v)

        @pl.loop(0, n_chunks)
        def _(j):
            pltpu.sync_copy(src_hbm.at[pl.ds(row0 + wid * per_worker + j * chunk, chunk)], rows_v)
            for c in range(copies):
                pltpu.async_copy(rows_v, out_hbm.at[idx_v.at[j * copies + c]], sem).wait()

    out = scatter(src.reshape(s, sub, LANES), idx)
    return out.reshape(n_out * sub, LANES)


def _gmm_kernel(be_ref, nu_ref, x_ref, wg_ref, wu_ref, wd_ref, o_ref, wgu_s, wd_s, *, hidden, blk, sub):
    b = pl.program_id(0)
    live = b < nu_ref[0]

    @pl.when(live & ((b == 0) | (be_ref[b] != be_ref[jnp.maximum(b - 1, 0)])))
    def _():
        wgu_s[:, :hidden] = wg_ref[...].astype(BF16)
        wgu_s[:, hidden:] = wu_ref[...].astype(BF16)
        wd_s[...] = wd_ref[...].astype(BF16)

    @pl.when(live)
    def _():
        x = _load_rows(x_ref, blk, sub).astype(BF16)
        gu = jnp.dot(x, wgu_s[...], preferred_element_type=F32)
        g, u = gu[:, :hidden], gu[:, hidden:]
        h = (g * jax.nn.sigmoid(g) * u).astype(BF16)
        _store_rows(o_ref, jnp.dot(h, wd_s[...], preferred_element_type=F32))


def _gmm(xs, w_gate, w_up, w_down, layer, blk_e, n_used, blk=MOE_BLK):
    hidden, d = w_down.shape[2:]
    sub = _row_words(d)
    n_rows = xs.shape[0] // sub
    row = lambda b, be, nu: (jnp.minimum(b, nu[0] - 1), 0)
    expert = lambda b, be, nu: (layer, be[b], 0, 0)
    return pl.pallas_call(
        functools.partial(_gmm_kernel, hidden=hidden, blk=blk, sub=sub),
        grid_spec=pltpu.PrefetchScalarGridSpec(
            num_scalar_prefetch=2,
            grid=(n_rows // blk,),
            in_specs=[pl.BlockSpec((blk * sub, LANES), row),
                      pl.BlockSpec((None, None, d, hidden), expert),
                      pl.BlockSpec((None, None, d, hidden), expert),
                      pl.BlockSpec((None, None, hidden, d), expert)],
            out_specs=pl.BlockSpec((blk * sub, LANES), row),
            scratch_shapes=[pltpu.VMEM((d, 2 * hidden), BF16), pltpu.VMEM((hidden, d), BF16)]),
        out_shape=jax.ShapeDtypeStruct((n_rows * sub, LANES), jnp.uint32),
        compiler_params=_cparams(("arbitrary",)),
        name="expert_gmm",
    )(blk_e, n_used, xs, w_gate, w_up, w_down)


def _combine_kernel(x_ref, yg_ref, gate_ref, wgu_ref, wd_ref, g_ref, b_ref, *rest, alpha, hidden):
    o_ref, ob_ref = rest[-2:]
    x = x_ref[...]
    gu = jnp.dot(x.astype(BF16), wgu_ref[...], preferred_element_type=F32)
    g, u = gu[:, :hidden], gu[:, hidden:]
    h = (g * jax.nn.sigmoid(g) * u).astype(BF16)
    acc = jnp.dot(h, wd_ref[...], preferred_element_type=F32)
    gates = gate_ref[...]
    tm, d = x.shape
    for k in range(TOP_K):
        acc = acc + gates[:, k:k + 1] * _load_rows(yg_ref.at[k], tm, _row_words(d))
    y = _layer_norm(alpha * x + acc, g_ref[...], b_ref[...])
    o_ref[...] = y
    ob_ref[...] = y.astype(BF16)


def _combine(x, row0, yg, gates_t, w_gu, w_down, ln_g, ln_b, alpha, prev=None, tm=COMBINE_TM):
    n, d = x.shape
    n_part = gates_t.shape[0]
    tm = min(tm, n_part)
    blk0 = row0 // tm
    sub = _row_words(d)
    hidden = w_down.shape[0]
    const = lambda a: pl.BlockSpec(a.shape, lambda i: (0,) * a.ndim)
    rows = pl.BlockSpec((tm, d), lambda i: (i + blk0, 0))
    in_specs = [rows,
                pl.BlockSpec((TOP_K, tm * sub, LANES), lambda i: (0, i, 0)),
                pl.BlockSpec((tm, TOP_K), lambda i: (i, 0)),
                const(w_gu), const(w_down), const(ln_g), const(ln_b)]
    args = [x, yg, gates_t, w_gu, w_down, ln_g, ln_b]
    aliases = {}
    if prev is not None:
        aliases = {len(args): 0, len(args) + 1: 1}
        in_specs += [pl.BlockSpec(memory_space=pl.ANY)] * 2
        args += list(prev)
    return pl.pallas_call(
        functools.partial(_combine_kernel, alpha=alpha, hidden=hidden),
        grid=(n_part // tm,),
        in_specs=in_specs,
        out_specs=[rows, rows],
        out_shape=[jax.ShapeDtypeStruct((n, d), F32), jax.ShapeDtypeStruct((n, d), BF16)],
        input_output_aliases=aliases,
        compiler_params=_cparams(("arbitrary",)),
        name="combine",
    )(*args)


def _split_w_in(w_in, b_in, d):
    qkv0 = SSM_WIDTH
    qkv_w = len(ATTN_PATTERNS) * ATTN_WIDTH
    gates0 = qkv0 + 3 * qkv_w
    cols = lambda a, c0, w: a[:, c0:c0 + w]
    pick = lambda a: jnp.concatenate([cols(a, gates0, 2 * d), cols(a, 0, SSM_WIDTH)], axis=1)
    main = (pick(w_in).astype(BF16), pick(b_in))
    groups = []
    for g in range(len(ATTN_PATTERNS)):
        sel = lambda a: jnp.concatenate([cols(a, qkv0 + s * qkv_w + g * ATTN_WIDTH, ATTN_WIDTH)
                                         for s in range(3)], axis=1)
        groups.append((sel(w_in).astype(BF16), sel(b_in)))
    return main, groups


def _token_mixer(x, xb, batch, w_in, b_in, s5p, w_glu, w_br_ssm, w_br_attn, w_out, expand, ln_g, ln_b, alpha):
    n, d = x.shape
    seq = n // batch
    (w_main, b_main), groups = _split_w_in(w_in, b_in, d)
    proj = _inproj(xb, w_main, b_main)
    y_ssm = _s5(proj, *s5p, w_glu, batch)

    outs, lses = [], []
    for (window, dil), (w_g, b_g) in zip(ATTN_PATTERNS, groups):
        assert window // dil == QBLK and (seq // dil) % QBLK == 0
        q, k, v = _inproj_qkv(xb, w_g, b_g, dil * batch)
        o_g, lse_g = _attn(q, k, v, window // dil)
        outs.append(jnp.swapaxes(o_g, 0, 1).reshape(n, ATTN_WIDTH))
        lses.append(jnp.swapaxes(lse_g, 0, 1).reshape(n, HEADS))
    lse = jnp.concatenate(lses, axis=-1)
    return _merge(x, y_ssm, outs[0], outs[1], outs[2], lse, proj, w_br_ssm, w_br_attn, w_out,
                  expand, ln_g, ln_b, alpha)


def _moe_part(x, x_rows, row0, n, out_prev, rw_hi, rw_lo, bias, eg, tri, w_gate, w_up, w_down, layer,
              sh_gu, sh_down, ln_g, ln_b, alpha):
    d = x.shape[1]
    sub = _row_words(d)
    eidx, gates, rank, cnt = _route(x, row0, n, rw_hi, rw_lo, bias, eg, tri)
    counts = cnt[:, 0].astype(jnp.int32)
    nk = n * TOP_K
    n_pad = N_EXPERTS * MOE_BLK
    n_rows = nk + n_pad
    n_blocks = n_rows // MOE_BLK
    assert n_pad % n == 0
    pad_counts = (counts + MOE_BLK - 1) // MOE_BLK * MOE_BLK
    pad_end = jnp.cumsum(pad_counts)
    pad_start = pad_end - pad_counts
    onehot = eidx[:, :, None] == jnp.arange(N_EXPERTS, dtype=jnp.int32)
    dest = jnp.sum(jnp.where(onehot, pad_start, 0), axis=-1) + rank
    seg_start = jnp.concatenate([pad_start + counts, pad_end[-1:]])
    seg_len = jnp.concatenate([pad_counts - counts, n_rows - pad_end[-1:]])
    seg_end = jnp.cumsum(seg_len)
    q = jnp.arange(n_pad, dtype=jnp.int32)[:, None]
    seg = jnp.sum((q >= seg_end[None, :]).astype(jnp.int32), axis=1, keepdims=True)
    seg_hot = seg == jnp.arange(N_EXPERTS + 1, dtype=jnp.int32)[None, :]
    pad_dest = jnp.sum(jnp.where(seg_hot, seg_start - (seg_end - seg_len), 0), axis=1) + q[:, 0]
    dest_all = jnp.concatenate([dest, pad_dest.reshape(n_pad // n, n)], axis=0).astype(jnp.int32)
    blk_first = jnp.arange(n_blocks, dtype=jnp.int32)[:, None] * MOE_BLK
    blk_e = jnp.minimum(jnp.sum((blk_first >= pad_end[None, :]).astype(jnp.int32), axis=1), N_EXPERTS - 1)
    n_used = (pad_end[-1:] // MOE_BLK).astype(jnp.int32)

    xs = _sc_scatter_rows(x_rows, row0, dest_all, n_rows, sub)
    ys = _gmm(xs, w_gate, w_up, w_down, layer, blk_e, n_used)
    yg = _sc_gather_rows(ys, dest.reshape(nk), sub).reshape(TOP_K, n * sub, LANES)
    return _combine(x, row0, yg, gates.T, sh_gu, sh_down, ln_g, ln_b, alpha, out_prev)


def _moe(x, x_rows, *params):
    n = x.shape[0]
    part = n // MOE_PARTS
    out = None
    for i in range(MOE_PARTS):
        out = _moe_part(x, x_rows, i * part, part, out, *params)
    return out


def kernel(x, w_in, b_in, ssm_lam_re, ssm_lam_im, ssm_log_dt, ssm_b_re, ssm_b_im, ssm_c_re, ssm_c_im, ssm_d, w_glu, w_br_ssm, w_br_attn, w_out, ln1_g, ln1_b, router_w, router_bias, exp_w_gate, exp_w_up, exp_w_down, sh_w_gate, sh_w_up, sh_w_down, ln2_g, ln2_b):
    batch, seq, d = x.shape
    depth = w_in.shape[0]
    assert batch == SUBLANES
    alpha = (2 * depth) ** 0.25
    n = batch * seq
    xt = jnp.swapaxes(x, 0, 1).reshape(n, d)
    xb = xt.astype(BF16)

    expand = jnp.repeat(jnp.eye(HEADS, dtype=BF16), HEAD_DIM, axis=1)
    eg = jnp.repeat(jnp.eye(N_EXPERT_GROUPS, dtype=BF16), GROUP_SIZE, axis=0)
    tm = min(ROUTE_TM, n)
    tri = (jnp.arange(tm)[:, None] <= jnp.arange(tm)[None, :]).astype(BF16)
    row = lambda a: a.astype(F32).reshape(1, -1)

    for l in range(depth):
        s5p = _s5_params(ssm_lam_re[l], ssm_lam_im[l], ssm_log_dt[l], ssm_b_re[l], ssm_b_im[l],
                         ssm_c_re[l], ssm_c_im[l], ssm_d[l])
        xt, xt_rows = _token_mixer(xt, xb, batch, w_in[l], row(b_in[l]), s5p,
                                   w_glu[l].astype(BF16), w_br_ssm[l].astype(BF16),
                                   w_br_attn[l].astype(BF16), w_out[l].astype(BF16),
                                   expand, row(ln1_g[l]), row(ln1_b[l]), alpha)
        rwt = router_w[l].astype(F32).T
        rw_hi = rwt.astype(BF16)
        rw_lo = (rwt - rw_hi.astype(F32)).astype(BF16)
        sh_gu = jnp.concatenate([sh_w_gate[l], sh_w_up[l]], axis=-1).astype(BF16)
        xt, xb = _moe(xt, xt_rows, rw_hi, rw_lo, router_bias[l].astype(F32).reshape(-1, 1), eg, tri,
                  exp_w_gate, exp_w_up, exp_w_down, l, sh_gu, sh_w_down[l].astype(BF16),
                  row(ln2_g[l]), row(ln2_b[l]), alpha)
    return jnp.swapaxes(xt.reshape(seq, batch, d), 0, 1)
```

```python
import functools
import math

import jax
import jax.numpy as jnp
import numpy as np
from jax import lax
from jax.experimental import pallas as pl
from jax.experimental.pallas import tpu as pltpu
from jax.experimental.pallas import tpu_sc as plsc

F32 = jnp.float32
BF16 = jnp.bfloat16

SSM_GROUP = 16
SSM_GROUPS = 32
SSM_WIDTH = SSM_GROUP * SSM_GROUPS
SSM_STATE = 64
HEAD_DIM = 64
HEADS = 8
ATTN_PATTERNS = ((128, 1), (512, 4), (2048, 16))
ATTN_WIDTH = HEADS * HEAD_DIM
QBLK = 128
N_EXPERTS = 64
TOP_K = 8
N_EXPERT_GROUPS = 8
GROUP_SIZE = N_EXPERTS // N_EXPERT_GROUPS
TOPK_GROUPS = 4
ROUTED_SCALE = 2.5
LN_EPS = 1e-5
NEG = -1e30

LANES = 128
SUBLANES = 8
VMEM_LIMIT = 56 * 1024 * 1024

INPROJ_TM = 2048
INPROJ_TN = 1280
QKV_TM = 2048
S5_T = 128
S5_KB = 4
ATTN_SUB_BLOCKS = 4
MERGE_TM = 512
ROUTE_TM = 1024
MOE_BLK = 1024
MOE_PARTS = 2
SC_CHUNK = 128
HI_HALF = np.uint32(0xFFFF0000)
COMBINE_TM = 512


def _cparams(sem):
    return pltpu.CompilerParams(dimension_semantics=sem, vmem_limit_bytes=VMEM_LIMIT)


def _inproj_kernel(x_ref, w_ref, b_ref, o_ref):
    acc = jnp.dot(x_ref[...], w_ref[...], preferred_element_type=F32)
    o_ref[...] = (acc + b_ref[...]).astype(BF16)


def _inproj(xb, w, b, tm=INPROJ_TM, tn=INPROJ_TN):
    n, d = xb.shape
    width = w.shape[1]
    tm = min(tm, n)
    return pl.pallas_call(
        _inproj_kernel,
        grid=(n // tm, width // tn),
        in_specs=[pl.BlockSpec((tm, d), lambda i, j: (i, 0)),
                  pl.BlockSpec((d, tn), lambda i, j: (0, j)),
                  pl.BlockSpec((1, tn), lambda i, j: (0, j))],
        out_specs=pl.BlockSpec((tm, tn), lambda i, j: (i, j)),
        out_shape=jax.ShapeDtypeStruct((n, width), BF16),
        compiler_params=_cparams(("arbitrary", "arbitrary")),
        name="inproj",
    )(xb, w, b)


def _inproj_qkv_kernel(x_ref, w_ref, b_ref, q_ref, k_ref, v_ref, y_ref, *, classes):
    x = x_ref[...]
    rows = x.shape[0] // classes
    for part, o_ref in enumerate((q_ref, k_ref, v_ref)):
        cols = slice(part * ATTN_WIDTH, (part + 1) * ATTN_WIDTH)
        y = jnp.dot(x, w_ref[:, cols], preferred_element_type=F32) + b_ref[:, cols]
        y = pltpu.einshape("jcf->cjf", y.reshape(rows, classes, ATTN_WIDTH))
        o_ref[...] = y.astype(BF16)


def _inproj_qkv(xb, w, b, classes, tm=QKV_TM):
    n, d = xb.shape
    tm = min(tm, n)
    rows = tm // classes
    out = pl.BlockSpec((classes, rows, ATTN_WIDTH), lambda i: (0, i, 0))
    shape = jax.ShapeDtypeStruct((classes, n // classes, ATTN_WIDTH), BF16)
    return pl.pallas_call(
        functools.partial(_inproj_qkv_kernel, classes=classes),
        grid=(n // tm,),
        in_specs=[pl.BlockSpec((tm, d), lambda i: (i, 0)),
                  pl.BlockSpec(w.shape, lambda i: (0, 0)),
                  pl.BlockSpec(b.shape, lambda i: (0, 0))],
        out_specs=[out, out, out],
        out_shape=[shape, shape, shape],
        scratch_shapes=[pltpu.VMEM((ATTN_WIDTH // LANES, tm, LANES), F32)],
        compiler_params=_cparams(("arbitrary",)),
        name="inproj_qkv",
    )(xb, w, b)


def _gelu_tanh(x):
    c = math.sqrt(2.0 / math.pi)
    return 0.5 * x * (1.0 + jnp.tanh(c * (x + 0.044715 * (x * x * x))))


def _s5_kernel(u_ref, bm_ref, cre_ref, cim_ref, are_ref, aim_ref, d_ref, wglu_ref,
               o_ref, sre_ref, sim_ref, st_re_ref, st_im_ref, y_ref, *, steps, batch):
    kw = SSM_WIDTH // S5_KB
    sw = SSM_GROUPS * SSM_STATE // S5_KB

    @pl.when(pl.program_id(0) == 0)
    def _():
        st_re_ref[...] = jnp.zeros_like(st_re_ref)
        st_im_ref[...] = jnp.zeros_like(st_im_ref)

    for k in range(S5_KB):
        ls = slice(k * sw, (k + 1) * sw)
        bu = jnp.dot(u_ref[:, k * kw:(k + 1) * kw], bm_ref[k], preferred_element_type=F32)
        sre_ref[:, ls] = bu[:, :sw]
        sim_ref[:, ls] = bu[:, sw:]

        ar = jnp.broadcast_to(are_ref[:, ls], (batch, sw))
        ai = jnp.broadcast_to(aim_ref[:, ls], (batch, sw))

        def step(t, carry):
            sr, si = carry
            rows = pl.ds(pl.multiple_of(t * batch, batch), batch)
            nr = ar * sr - ai * si + sre_ref[rows, ls]
            ni = ar * si + ai * sr + sim_ref[rows, ls]
            sre_ref[rows, ls] = nr
            sim_ref[rows, ls] = ni
            return nr, ni

        sr, si = lax.fori_loop(0, steps, step, (st_re_ref[:, ls], st_im_ref[:, ls]))
        st_re_ref[:, ls] = sr
        st_im_ref[:, ls] = si

        y_ref[:, k * kw:(k + 1) * kw] = (
            jnp.dot(sre_ref[:, ls].astype(BF16), cre_ref[k], preferred_element_type=F32)
            + jnp.dot(sim_ref[:, ls].astype(BF16), cim_ref[k], preferred_element_type=F32))

    y = y_ref[...] + d_ref[...] * u_ref[...].astype(F32)
    y = _gelu_tanh(y)
    z = jnp.dot(y.astype(BF16), wglu_ref[...], preferred_element_type=F32)
    o_ref[...] = (y * jax.nn.sigmoid(z)).astype(BF16)


def _s5(proj, bm, cre, cim, a_re, a_im, d_skip, w_glu, batch, steps=S5_T):
    n = proj.shape[0]
    u_blk = proj.shape[1] // SSM_WIDTH - 1
    seq = n // batch
    steps = min(steps, seq)
    rows = steps * batch
    nstate = SSM_GROUPS * SSM_STATE
    const = lambda shape: pl.BlockSpec(shape, lambda i: (0,) * len(shape))
    return pl.pallas_call(
        functools.partial(_s5_kernel, steps=steps, batch=batch),
        grid=(seq // steps,),
        in_specs=[pl.BlockSpec((rows, SSM_WIDTH), lambda i: (i, u_blk)),
                  const(bm.shape), const(cre.shape), const(cim.shape),
                  const(a_re.shape), const(a_im.shape), const(d_skip.shape), const(w_glu.shape)],
        out_specs=pl.BlockSpec((rows, SSM_WIDTH), lambda i: (i, 0)),
        out_shape=jax.ShapeDtypeStruct((n, SSM_WIDTH), BF16),
        scratch_shapes=[pltpu.VMEM((rows, nstate), F32), pltpu.VMEM((rows, nstate), F32),
                        pltpu.VMEM((batch, nstate), F32), pltpu.VMEM((batch, nstate), F32),
                        pltpu.VMEM((rows, SSM_WIDTH), F32)],
        compiler_params=_cparams(("arbitrary",)),
        name="s5",
    )(proj, bm, cre, cim, a_re, a_im, d_skip, w_glu)


def _s5_params(lam_re, lam_im, log_dt, b_re, b_im, c_re, c_im, d_skip):
    lam = lax.complex(lam_re.astype(F32), lam_im.astype(F32))
    dt = jnp.exp(log_dt.astype(F32))[:, None]
    lam_bar = jnp.exp(lam * dt)
    b_bar = ((lam_bar - 1.0) / lam)[:, :, None] * lax.complex(b_re.astype(F32), b_im.astype(F32))
    gl = SSM_GROUPS // S5_KB
    eye = jnp.eye(gl, dtype=F32)

    def in_blocks(b):
        b = b.reshape(S5_KB, gl, SSM_STATE, SSM_GROUP)
        m = jnp.einsum('kgpc,gh->kgchp', b, eye)
        return m.reshape(S5_KB, gl * SSM_GROUP, gl * SSM_STATE)

    def out_blocks(c):
        c = c.reshape(S5_KB, gl, SSM_GROUP, SSM_STATE)
        m = jnp.einsum('kgcp,gh->kgphc', c, eye)
        return m.reshape(S5_KB, gl * SSM_STATE, gl * SSM_GROUP)

    bm = jnp.concatenate([in_blocks(jnp.real(b_bar)), in_blocks(jnp.imag(b_bar))], axis=-1).astype(BF16)
    cre = out_blocks(c_re.astype(F32)).astype(BF16)
    cim = out_blocks(-c_im.astype(F32)).astype(BF16)
    a_re = jnp.real(lam_bar).reshape(1, -1)
    a_im = jnp.imag(lam_bar).reshape(1, -1)
    return bm, cre, cim, a_re, a_im, d_skip.astype(F32).reshape(1, -1)


def _attn_block(q, k2, v2, allowed):
    first = lax.broadcasted_iota(jnp.int32, (QBLK, LANES), 1) < HEAD_DIM
    zero = jnp.zeros((), BF16)
    outs, lses = [], []
    for slab in range(ATTN_WIDTH // LANES):
        ls = slice(slab * LANES, (slab + 1) * LANES)
        qs = jnp.concatenate([jnp.where(first, q[:, ls], zero), jnp.where(first, zero, q[:, ls])], axis=0)
        s = lax.dot_general(qs, k2[:, ls], (((1,), (1,)), ((), ())), preferred_element_type=F32)
        s = jnp.where(allowed, s, NEG)
        m = jnp.max(s, axis=-1, keepdims=True)
        p = jnp.exp(s - m)
        denom = jnp.sum(p, axis=-1, keepdims=True)
        r = jnp.dot(p.astype(BF16), v2[:, ls], preferred_element_type=F32) / denom
        outs.append(jnp.where(first, r[:QBLK], r[QBLK:]))
        lse = m + jnp.log(denom)
        lses += [lse[:QBLK], lse[QBLK:]]
    return jnp.concatenate(outs, axis=-1).astype(BF16), jnp.concatenate(lses, axis=-1)


def _attn_kernel(q_ref, kp_ref, kc_ref, vp_ref, vc_ref, o_ref, lse_ref, *, steps, sub_blocks):
    j = pl.program_id(1)
    qi = lax.broadcasted_iota(jnp.int32, (2 * QBLK, 2 * QBLK), 0) % QBLK
    kj = lax.broadcasted_iota(jnp.int32, (2 * QBLK, 2 * QBLK), 1)
    dist = qi + QBLK - kj
    band = (dist >= 0) & (dist <= steps)
    for b in range(sub_blocks):
        rows = slice(b * QBLK, (b + 1) * QBLK)
        q = q_ref[rows, :] * (HEAD_DIM ** -0.5)
        if b == 0:
            k_prev, v_prev = kp_ref[...], vp_ref[...]
            allowed = band & ((kj >= QBLK) | (j > 0))
        else:
            before = slice((b - 1) * QBLK, b * QBLK)
            k_prev, v_prev = kc_ref[before, :], vc_ref[before, :]
            allowed = band
        k2 = jnp.concatenate([k_prev, kc_ref[rows, :]], axis=0)
        v2 = jnp.concatenate([v_prev, vc_ref[rows, :]], axis=0)
        o, lse = _attn_block(q, k2, v2, allowed)
        o_ref[rows, :] = o
        lse_ref[rows, :] = lse


def _attn(q, k, v, steps, sub_blocks=ATTN_SUB_BLOCKS):
    classes, n, width = q.shape
    sub_blocks = min(sub_blocks, n // QBLK)
    rows = sub_blocks * QBLK
    cur = pl.BlockSpec((None, rows, width), lambda c, j: (c, j, 0))
    prev = pl.BlockSpec((None, QBLK, width), lambda c, j: (c, jnp.maximum(j * sub_blocks - 1, 0), 0))
    return pl.pallas_call(
        functools.partial(_attn_kernel, steps=steps, sub_blocks=sub_blocks),
        grid=(classes, n // rows),
        in_specs=[cur, prev, cur, prev, cur],
        out_specs=[cur, pl.BlockSpec((None, rows, HEADS), lambda c, j: (c, j, 0))],
        out_shape=[jax.ShapeDtypeStruct((classes, n, width), BF16),
                   jax.ShapeDtypeStruct((classes, n, HEADS), F32)],
        compiler_params=_cparams(("arbitrary", "arbitrary")),
        name="attn",
    )(q, k, k, v, v)


def _layer_norm(z, g, b):
    mu = jnp.mean(z, axis=-1, keepdims=True)
    zc = z - mu
    var = jnp.mean(zc * zc, axis=-1, keepdims=True)
    return zc * lax.rsqrt(var + LN_EPS) * g + b


def _row_words(d):
    return d // (2 * LANES)


def _store_rows(ref, y):
    rows, d = y.shape
    sub = _row_words(d)
    for s in range(sub):
        lo = y[:, (2 * s) * LANES:(2 * s + 1) * LANES].astype(BF16).astype(F32)
        hi = y[:, (2 * s + 1) * LANES:(2 * s + 2) * LANES].astype(BF16).astype(F32)
        word = (pltpu.bitcast(lo, jnp.uint32) >> 16) | (pltpu.bitcast(hi, jnp.uint32) & HI_HALF)
        ref[pl.ds(s, rows, stride=sub), :] = word


def _load_rows(ref, rows, sub):
    parts = []
    for s in range(sub):
        word = ref[pl.ds(s, rows, stride=sub), :]
        parts.append(pltpu.bitcast(word << 16, F32))
        parts.append(pltpu.bitcast(word & HI_HALF, F32))
    return jnp.concatenate(parts, axis=-1)


def _expand_heads(w, e_ref):
    hi = w.astype(BF16)
    lo = (w - hi.astype(F32)).astype(BF16)
    return jnp.dot(jnp.concatenate([hi, lo], axis=-1), e_ref[...], preferred_element_type=F32)


def _merge_kernel(x_ref, yssm_ref, o0_ref, o1_ref, o2_ref, lse_ref, gs_ref, ga_ref,
                  wbs_ref, wba_ref, wout_ref, e_ref, g_ref, b_ref, o_ref, orow_ref, *, alpha):
    lse = lse_ref[...]
    l0, l1, l2 = lse[:, 0:HEADS], lse[:, HEADS:2 * HEADS], lse[:, 2 * HEADS:3 * HEADS]
    m = jnp.maximum(jnp.maximum(l0, l1), l2)
    e0, e1, e2 = jnp.exp(l0 - m), jnp.exp(l1 - m), jnp.exp(l2 - m)
    den = e0 + e1 + e2
    y_attn = (_expand_heads(e0 / den, e_ref) * o0_ref[...].astype(F32)
              + _expand_heads(e1 / den, e_ref) * o1_ref[...].astype(F32)
              + _expand_heads(e2 / den, e_ref) * o2_ref[...].astype(F32))
    ya = jnp.dot(y_attn.astype(BF16), wba_ref[...], preferred_element_type=F32)
    ys = jnp.dot(yssm_ref[...], wbs_ref[...], preferred_element_type=F32)
    merged = (jax.nn.sigmoid(gs_ref[...].astype(F32)) * ys
              + jax.nn.sigmoid(ga_ref[...].astype(F32)) * ya)
    mix = jnp.dot(merged.astype(BF16), wout_ref[...], preferred_element_type=F32)
    y = _layer_norm(alpha * x_ref[...] + mix, g_ref[...], b_ref[...])
    o_ref[...] = y
    _store_rows(orow_ref, y)


def _merge(x, y_ssm, o0, o1, o2, lse, proj, w_br_ssm, w_br_attn, w_out, expand, ln_g, ln_b,
           alpha, tm=MERGE_TM):
    n, d = x.shape
    tm = min(tm, n)
    gate_blk = proj.shape[1] // d
    row = lambda w: pl.BlockSpec((tm, w), lambda i: (i, 0))
    const = lambda a: pl.BlockSpec(a.shape, lambda i: (0,) * a.ndim)
    return pl.pallas_call(
        functools.partial(_merge_kernel, alpha=alpha),
        grid=(n // tm,),
        in_specs=[row(d), row(SSM_WIDTH), row(ATTN_WIDTH), row(ATTN_WIDTH), row(ATTN_WIDTH),
                  row(lse.shape[1]),
                  pl.BlockSpec((tm, d), lambda i: (i, gate_blk - 2)),
                  pl.BlockSpec((tm, d), lambda i: (i, gate_blk - 1)),
                  const(w_br_ssm), const(w_br_attn), const(w_out), const(expand),
                  const(ln_g), const(ln_b)],
        out_specs=[row(d), pl.BlockSpec((tm * _row_words(d), LANES), lambda i: (i, 0))],
        out_shape=[jax.ShapeDtypeStruct((n, d), F32),
                   jax.ShapeDtypeStruct((n * _row_words(d), LANES), jnp.uint32)],
        compiler_params=_cparams(("arbitrary",)),
        name="merge",
    )(x, y_ssm, o0, o1, o2, lse, proj, proj, w_br_ssm, w_br_attn, w_out, expand, ln_g, ln_b)


def _first_argmax(v, iota, size, axis):
    m = jnp.max(v, axis=axis, keepdims=True)
    idx = jnp.min(jnp.where(v == m, iota, size), axis=axis, keepdims=True)
    return m, idx


def _route_kernel(x_ref, rwh_ref, rwl_ref, bias_ref, eg_ref, tri_ref,
                  eidx_ref, gate_ref, rank_ref, cnt_ref, carry_ref):
    @pl.when(pl.program_id(0) == 0)
    def _():
        carry_ref[...] = jnp.zeros_like(carry_ref)

    x = x_ref[...]
    tm = x.shape[0]
    xh = x.astype(BF16)
    xl = (x - xh.astype(F32)).astype(BF16)
    nt = (((1,), (1,)), ((), ()))
    logits = (lax.dot_general(rwh_ref[...], xh, nt, preferred_element_type=F32)
              + lax.dot_general(rwh_ref[...], xl, nt, preferred_element_type=F32)
              + lax.dot_general(rwl_ref[...], xh, nt, preferred_element_type=F32))
    scores = jax.nn.sigmoid(logits)
    sel = scores + bias_ref[...]

    sel3 = sel.reshape(N_EXPERT_GROUPS, GROUP_SIZE, tm)
    iw = lax.broadcasted_iota(jnp.int32, sel3.shape, 1)
    m1, i1 = _first_argmax(sel3, iw, GROUP_SIZE, 1)
    m2 = jnp.max(jnp.where(iw == i1, NEG, sel3), axis=1, keepdims=True)
    gs = (m1 + m2).reshape(N_EXPERT_GROUPS, tm)

    ig = lax.broadcasted_iota(jnp.int32, gs.shape, 0)
    gmask = jnp.zeros(gs.shape, F32)
    for _ in range(TOPK_GROUPS):
        _, gi = _first_argmax(gs, ig, N_EXPERT_GROUPS, 0)
        hit = ig == gi
        gmask = jnp.where(hit, 1.0, gmask)
        gs = jnp.where(hit, NEG, gs)
    emask = jnp.dot(eg_ref[...], gmask.astype(BF16), preferred_element_type=F32) > 0.5

    masked = jnp.where(emask, sel, NEG)
    ie = lax.broadcasted_iota(jnp.int32, masked.shape, 0)
    chosen = jnp.zeros(masked.shape, F32)
    idxs, vals = [], []
    for _ in range(TOP_K):
        _, ei = _first_argmax(masked, ie, N_EXPERTS, 0)
        hit = ie == ei
        idxs.append(ei)
        vals.append(jnp.sum(jnp.where(hit, scores, 0.0), axis=0, keepdims=True))
        chosen = jnp.where(hit, 1.0, chosen)
        masked = jnp.where(hit, NEG, masked)
    total = vals[0]
    for v in vals[1:]:
        total = total + v

    prefix = jnp.dot(chosen.astype(BF16), tri_ref[...], preferred_element_type=F32)
    pos = carry_ref[:, 0:1] + prefix - 1.0
    ranks = [jnp.sum(jnp.where(ie == ei, pos, 0.0), axis=0, keepdims=True) for ei in idxs]
    carry = carry_ref[...] + jnp.sum(chosen, axis=1, keepdims=True)
    carry_ref[...] = carry
    cnt_ref[...] = carry

    eidx_ref[...] = jnp.concatenate(idxs, axis=0)
    gate_ref[...] = jnp.concatenate([v / total * ROUTED_SCALE for v in vals], axis=0)
    rank_ref[...] = jnp.concatenate(ranks, axis=0).astype(jnp.int32)


def _route(x, row0, n, rw_hi, rw_lo, bias, eg, tri, tm=ROUTE_TM):
    d = x.shape[1]
    tm = min(tm, n)
    blk0 = row0 // tm
    const = lambda a: pl.BlockSpec(a.shape, lambda i: (0,) * a.ndim)
    col = pl.BlockSpec((TOP_K, tm), lambda i: (0, i))
    return pl.pallas_call(
        _route_kernel,
        grid=(n // tm,),
        in_specs=[pl.BlockSpec((tm, d), lambda i: (i + blk0, 0)),
                  const(rw_hi), const(rw_lo), const(bias), const(eg), const(tri)],
        out_specs=[col, col, col, pl.BlockSpec((N_EXPERTS, LANES), lambda i: (0, 0))],
        out_shape=[jax.ShapeDtypeStruct((TOP_K, n), jnp.int32),
                   jax.ShapeDtypeStruct((TOP_K, n), F32),
                   jax.ShapeDtypeStruct((TOP_K, n), jnp.int32),
                   jax.ShapeDtypeStruct((N_EXPERTS, LANES), F32)],
        scratch_shapes=[pltpu.VMEM((N_EXPERTS, LANES), F32)],
        compiler_params=_cparams(("arbitrary",)),
        name="route",
    )(x, rw_hi, rw_lo, bias, eg, tri)


def _sc_gather_rows(src, idx, sub, chunk=SC_CHUNK):
    s = src.shape[0] // sub
    m = idx.shape[0]
    info = plsc.get_sparse_core_info()
    n_workers = info.num_cores * info.num_subcores
    per_worker = m // n_workers
    n_chunks = per_worker // chunk
    assert n_chunks * chunk * n_workers == m
    mesh = plsc.VectorSubcoreMesh(core_axis_name="c", subcore_axis_name="s")

    @functools.partial(
        pl.kernel, mesh=mesh,
        out_type=jax.ShapeDtypeStruct((m, sub, LANES), src.dtype),
        scratch_types=[pltpu.VMEM((n_chunks, chunk), jnp.int32),
                       pltpu.VMEM((chunk, sub, LANES), src.dtype),
                       pltpu.SemaphoreType.DMA])
    def gather(src_hbm, idx_hbm, out_hbm, idx_v, rows_v, sem):
        wid = lax.axis_index("s") * info.num_cores + lax.axis_index("c")
        pltpu.sync_copy(idx_hbm.at[wid], idx_v)

        @pl.loop(0, n_chunks)
        def _(j):
            pltpu.async_copy(src_hbm.at[idx_v.at[j]], rows_v, sem).wait()
            pltpu.sync_copy(rows_v, out_hbm.at[pl.ds(wid * per_worker + j * chunk, chunk)])

    out = gather(src.reshape(s, sub, LANES), idx.reshape(n_workers, n_chunks, chunk))
    return out.reshape(m * sub, LANES)


def _sc_scatter_rows(src, row0, dest, n_out, sub, chunk=SC_CHUNK):
    copies, n = dest.shape
    s = src.shape[0] // sub
    info = plsc.get_sparse_core_info()
    n_workers = info.num_cores * info.num_subcores
    per_worker = n // n_workers
    n_chunks = per_worker // chunk
    assert n_chunks * chunk * n_workers == n and copies * n == n_out
    mesh = plsc.VectorSubcoreMesh(core_axis_name="c", subcore_axis_name="s")
    idx = dest.reshape(copies, n_workers, n_chunks, chunk).transpose(1, 2, 0, 3)
    idx = idx.reshape(n_workers, n_chunks * copies, chunk)

    @functools.partial(
        pl.kernel, mesh=mesh,
        out_type=jax.ShapeDtypeStruct((n_out, sub, LANES), src.dtype),
        scratch_types=[pltpu.VMEM((n_chunks * copies, chunk), jnp.int32),
                       pltpu.VMEM((chunk, sub, LANES), src.dtype),
                       pltpu.SemaphoreType.DMA])
    def scatter(src_hbm, idx_hbm, out_hbm, idx_v, rows_v, sem):
        wid = lax.axis_index("s") * info.num_cores + lax.axis_index("c")
        pltpu.sync_copy(idx_hbm.at[wid], idx_v)

        @pl.loop(0, n_chunks)
        def _(j):
            pltpu.sync_copy(src_hbm.at[pl.ds(row0 + wid * per_worker + j * chunk, chunk)], rows_v)
            for c in range(copies):
                pltpu.async_copy(rows_v, out_hbm.at[idx_v.at[j * copies + c]], sem).wait()

    out = scatter(src.reshape(s, sub, LANES), idx)
    return out.reshape(n_out * sub, LANES)


def _gmm_kernel(be_ref, nu_ref, x_ref, wg_ref, wu_ref, wd_ref, o_ref, wgu_s, wd_s, *, hidden, blk, sub):
    b = pl.program_id(0)
    live = b < nu_ref[0]

    @pl.when(live & ((b == 0) | (be_ref[b] != be_ref[jnp.maximum(b - 1, 0)])))
    def _():
        wgu_s[:, :hidden] = wg_ref[...].astype(BF16)
        wgu_s[:, hidden:] = wu_ref[...].astype(BF16)
        wd_s[...] = wd_ref[...].astype(BF16)

    @pl.when(live)
    def _():
        x = _load_rows(x_ref, blk, sub).astype(BF16)
        gu = jnp.dot(x, wgu_s[...], preferred_element_type=F32)
        g, u = gu[:, :hidden], gu[:, hidden:]
        h = (g * jax.nn.sigmoid(g) * u).astype(BF16)
        _store_rows(o_ref, jnp.dot(h, wd_s[...], preferred_element_type=F32))


def _gmm(xs, w_gate, w_up, w_down, layer, blk_e, n_used, blk=MOE_BLK):
    hidden, d = w_down.shape[2:]
    sub = _row_words(d)
    n_rows = xs.shape[0] // sub
    row = lambda b, be, nu: (jnp.minimum(b, nu[0] - 1), 0)
    expert = lambda b, be, nu: (layer, be[b], 0, 0)
    return pl.pallas_call(
        functools.partial(_gmm_kernel, hidden=hidden, blk=blk, sub=sub),
        grid_spec=pltpu.PrefetchScalarGridSpec(
            num_scalar_prefetch=2,
            grid=(n_rows // blk,),
            in_specs=[pl.BlockSpec((blk * sub, LANES), row),
                      pl.BlockSpec((None, None, d, hidden), expert),
                      pl.BlockSpec((None, None, d, hidden), expert),
                      pl.BlockSpec((None, None, hidden, d), expert)],
            out_specs=pl.BlockSpec((blk * sub, LANES), row),
            scratch_shapes=[pltpu.VMEM((d, 2 * hidden), BF16), pltpu.VMEM((hidden, d), BF16)]),
        out_shape=jax.ShapeDtypeStruct((n_rows * sub, LANES), jnp.uint32),
        compiler_params=_cparams(("arbitrary",)),
        name="expert_gmm",
    )(blk_e, n_used, xs, w_gate, w_up, w_down)


def _combine_kernel(x_ref, yg_ref, gate_ref, wgu_ref, wd_ref, g_ref, b_ref, *rest, alpha, hidden):
    o_ref, ob_ref = rest[-2:]
    x = x_ref[...]
    gu = jnp.dot(x.astype(BF16), wgu_ref[...], preferred_element_type=F32)
    g, u = gu[:, :hidden], gu[:, hidden:]
    h = (g * jax.nn.sigmoid(g) * u).astype(BF16)
    acc = jnp.dot(h, wd_ref[...], preferred_element_type=F32)
    gates = gate_ref[...]
    tm, d = x.shape
    for k in range(TOP_K):
        acc = acc + gates[:, k:k + 1] * _load_rows(yg_ref.at[k], tm, _row_words(d))
    y = _layer_norm(alpha * x + acc, g_ref[...], b_ref[...])
    o_ref[...] = y
    ob_ref[...] = y.astype(BF16)


def _combine(x, row0, yg, gates_t, w_gu, w_down, ln_g, ln_b, alpha, prev=None, tm=COMBINE_TM):
    n, d = x.shape
    n_part = gates_t.shape[0]
    tm = min(tm, n_part)
    blk0 = row0 // tm
    sub = _row_words(d)
    hidden = w_down.shape[0]
    const = lambda a: pl.BlockSpec(a.shape, lambda i: (0,) * a.ndim)
    rows = pl.BlockSpec((tm, d), lambda i: (i + blk0, 0))
    in_specs = [rows,
                pl.BlockSpec((TOP_K, tm * sub, LANES), lambda i: (0, i, 0)),
                pl.BlockSpec((tm, TOP_K), lambda i: (i, 0)),
                const(w_gu), const(w_down), const(ln_g), const(ln_b)]
    args = [x, yg, gates_t, w_gu, w_down, ln_g, ln_b]
    aliases = {}
    if prev is not None:
        aliases = {len(args): 0, len(args) + 1: 1}
        in_specs += [pl.BlockSpec(memory_space=pl.ANY)] * 2
        args += list(prev)
    return pl.pallas_call(
        functools.partial(_combine_kernel, alpha=alpha, hidden=hidden),
        grid=(n_part // tm,),
        in_specs=in_specs,
        out_specs=[rows, rows],
        out_shape=[jax.ShapeDtypeStruct((n, d), F32), jax.ShapeDtypeStruct((n, d), BF16)],
        input_output_aliases=aliases,
        compiler_params=_cparams(("arbitrary",)),
        name="combine",
    )(*args)


def _split_w_in(w_in, b_in, d):
    qkv0 = SSM_WIDTH
    qkv_w = len(ATTN_PATTERNS) * ATTN_WIDTH
    gates0 = qkv0 + 3 * qkv_w
    cols = lambda a, c0, w: a[:, c0:c0 + w]
    pick = lambda a: jnp.concatenate([cols(a, gates0, 2 * d), cols(a, 0, SSM_WIDTH)], axis=1)
    main = (pick(w_in).astype(BF16), pick(b_in))
    groups = []
    for g in range(len(ATTN_PATTERNS)):
        sel = lambda a: jnp.concatenate([cols(a, qkv0 + s * qkv_w + g * ATTN_WIDTH, ATTN_WIDTH)
                                         for s in range(3)], axis=1)
        groups.append((sel(w_in).astype(BF16), sel(b_in)))
    return main, groups


def _token_mixer(x, xb, batch, w_in, b_in, s5p, w_glu, w_br_ssm, w_br_attn, w_out, expand, ln_g, ln_b, alpha):
    n, d = x.shape
    seq = n // batch
    (w_main, b_main), groups = _split_w_in(w_in, b_in, d)
    proj = _inproj(xb, w_main, b_main)
    y_ssm = _s5(proj, *s5p, w_glu, batch)

    outs, lses = [], []
    for (window, dil), (w_g, b_g) in zip(ATTN_PATTERNS, groups):
        assert window // dil == QBLK and (seq // dil) % QBLK == 0
        q, k, v = _inproj_qkv(xb, w_g, b_g, dil * batch)
        o_g, lse_g = _attn(q, k, v, window // dil)
        outs.append(jnp.swapaxes(o_g, 0, 1).reshape(n, ATTN_WIDTH))
        lses.append(jnp.swapaxes(lse_g, 0, 1).reshape(n, HEADS))
    lse = jnp.concatenate(lses, axis=-1)
    return _merge(x, y_ssm, outs[0], outs[1], outs[2], lse, proj, w_br_ssm, w_br_attn, w_out,
                  expand, ln_g, ln_b, alpha)


def _moe_part(x, x_rows, row0, n, out_prev, rw_hi, rw_lo, bias, eg, tri, w_gate, w_up, w_down, layer,
              sh_gu, sh_down, ln_g, ln_b, alpha):
    d = x.shape[1]
    sub = _row_words(d)
    eidx, gates, rank, cnt = _route(x, row0, n, rw_hi, rw_lo, bias, eg, tri)
    counts = cnt[:, 0].astype(jnp.int32)
    nk = n * TOP_K
    n_pad = N_EXPERTS * MOE_BLK
    n_rows = nk + n_pad
    n_blocks = n_rows // MOE_BLK
    assert n_pad % n == 0
    pad_counts = (counts + MOE_BLK - 1) // MOE_BLK * MOE_BLK
    pad_end = jnp.cumsum(pad_counts)
    pad_start = pad_end - pad_counts
    onehot = eidx[:, :, None] == jnp.arange(N_EXPERTS, dtype=jnp.int32)
    dest = jnp.sum(jnp.where(onehot, pad_start, 0), axis=-1) + rank
    seg_start = jnp.concatenate([pad_start + counts, pad_end[-1:]])
    seg_len = jnp.concatenate([pad_counts - counts, n_rows - pad_end[-1:]])
    seg_end = jnp.cumsum(seg_len)
    q = jnp.arange(n_pad, dtype=jnp.int32)[:, None]
    seg = jnp.sum((q >= seg_end[None, :]).astype(jnp.int32), axis=1, keepdims=True)
    seg_hot = seg == jnp.arange(N_EXPERTS + 1, dtype=jnp.int32)[None, :]
    pad_dest = jnp.sum(jnp.where(seg_hot, seg_start - (seg_end - seg_len), 0), axis=1) + q[:, 0]
    dest_all = jnp.concatenate([dest, pad_dest.reshape(n_pad // n, n)], axis=0).astype(jnp.int32)
    blk_first = jnp.arange(n_blocks, dtype=jnp.int32)[:, None] * MOE_BLK
    blk_e = jnp.minimum(jnp.sum((blk_first >= pad_end[None, :]).astype(jnp.int32), axis=1), N_EXPERTS - 1)
    n_used = (pad_end[-1:] // MOE_BLK).astype(jnp.int32)

    xs = _sc_scatter_rows(x_rows, row0, dest_all, n_rows, sub)
    ys = _gmm(xs, w_gate, w_up, w_down, layer, blk_e, n_used)
    yg = _sc_gather_rows(ys, dest.reshape(nk), sub).reshape(TOP_K, n * sub, LANES)
    return _combine(x, row0, yg, gates.T, sh_gu, sh_down, ln_g, ln_b, alpha, out_prev)


def _moe(x, x_rows, *params):
    n = x.shape[0]
    part = n // MOE_PARTS
    out = None
    for i in range(MOE_PARTS):
        out = _moe_part(x, x_rows, i * part, part, out, *params)
    return out


def kernel(x, w_in, b_in, ssm_lam_re, ssm_lam_im, ssm_log_dt, ssm_b_re, ssm_b_im, ssm_c_re, ssm_c_im, ssm_d, w_glu, w_br_ssm, w_br_attn, w_out, ln1_g, ln1_b, router_w, router_bias, exp_w_gate, exp_w_up, exp_w_down, sh_w_gate, sh_w_up, sh_w_down, ln2_g, ln2_b):
    batch, seq, d = x.shape
    depth = w_in.shape[0]
    assert batch == SUBLANES
    alpha = (2 * depth) ** 0.25
    n = batch * seq
    xt = jnp.swapaxes(x, 0, 1).reshape(n, d)
    xb = xt.astype(BF16)

    expand = jnp.tile(jnp.repeat(jnp.eye(HEADS, dtype=BF16), HEAD_DIM, axis=1), (2, 1))
    eg = jnp.repeat(jnp.eye(N_EXPERT_GROUPS, dtype=BF16), GROUP_SIZE, axis=0)
    tm = min(ROUTE_TM, n)
    tri = (jnp.arange(tm)[:, None] <= jnp.arange(tm)[None, :]).astype(BF16)
    row = lambda a: a.astype(F32).reshape(1, -1)

    for l in range(depth):
        s5p = _s5_params(ssm_lam_re[l], ssm_lam_im[l], ssm_log_dt[l], ssm_b_re[l], ssm_b_im[l],
                         ssm_c_re[l], ssm_c_im[l], ssm_d[l])
        xt, xt_rows = _token_mixer(xt, xb, batch, w_in[l], row(b_in[l]), s5p,
                                   w_glu[l].astype(BF16), w_br_ssm[l].astype(BF16),
                                   w_br_attn[l].astype(BF16), w_out[l].astype(BF16),
                                   expand, row(ln1_g[l]), row(ln1_b[l]), alpha)
        rwt = router_w[l].astype(F32).T
        rw_hi = rwt.astype(BF16)
        rw_lo = (rwt - rw_hi.astype(F32)).astype(BF16)
        sh_gu = jnp.concatenate([sh_w_gate[l], sh_w_up[l]], axis=-1).astype(BF16)
        xt, xb = _moe(xt, xt_rows, rw_hi, rw_lo, router_bias[l].astype(F32).reshape(-1, 1), eg, tri,
                  exp_w_gate, exp_w_up, exp_w_down, l, sh_gu, sh_w_down[l].astype(BF16),
                  row(ln2_g[l]), row(ln2_b[l]), alpha)
    return jnp.swapaxes(xt.reshape(seq, batch, d), 0, 1)
```

```python
import functools
import math

import jax
import jax.numpy as jnp
import numpy as np
from jax import lax
from jax.experimental import pallas as pl
from jax.experimental.pallas import tpu as pltpu
from jax.experimental.pallas import tpu_sc as plsc

F32 = jnp.float32
BF16 = jnp.bfloat16

SSM_GROUP = 16
SSM_GROUPS = 32
SSM_WIDTH = SSM_GROUP * SSM_GROUPS
SSM_STATE = 64
HEAD_DIM = 64
HEADS = 8
ATTN_PATTERNS = ((128, 1), (512, 4), (2048, 16))
ATTN_WIDTH = HEADS * HEAD_DIM
QBLK = 128
N_EXPERTS = 64
TOP_K = 8
N_EXPERT_GROUPS = 8
GROUP_SIZE = N_EXPERTS // N_EXPERT_GROUPS
TOPK_GROUPS = 4
ROUTED_SCALE = 2.5
LN_EPS = 1e-5
NEG = -1e30

LANES = 128
SUBLANES = 8
VMEM_LIMIT = 56 * 1024 * 1024

INPROJ_TM = 2048
INPROJ_TN = 1280
QKV_TM = 2048
S5_T = 128
S5_KB = 4
ATTN_SUB_BLOCKS = 4
MERGE_TM = 512
ROUTE_TM = 1024
MOE_BLK = 1024
MOE_PARTS = 2
SC_CHUNK = 128
HI_HALF = np.uint32(0xFFFF0000)
COMBINE_TM = 512


def _cparams(sem):
    return pltpu.CompilerParams(dimension_semantics=sem, vmem_limit_bytes=VMEM_LIMIT)


def _inproj_kernel(x_ref, w_ref, b_ref, o_ref):
    acc = jnp.dot(x_ref[...].astype(BF16), w_ref[...], preferred_element_type=F32)
    o_ref[...] = (acc + b_ref[...]).astype(BF16)


def _inproj(xb, w, b, tm=INPROJ_TM, tn=INPROJ_TN):
    n, d = xb.shape
    width = w.shape[1]
    tm = min(tm, n)
    return pl.pallas_call(
        _inproj_kernel,
        grid=(n // tm, width // tn),
        in_specs=[pl.BlockSpec((tm, d), lambda i, j: (i, 0)),
                  pl.BlockSpec((d, tn), lambda i, j: (0, j)),
                  pl.BlockSpec((1, tn), lambda i, j: (0, j))],
        out_specs=pl.BlockSpec((tm, tn), lambda i, j: (i, j)),
        out_shape=jax.ShapeDtypeStruct((n, width), BF16),
        compiler_params=_cparams(("arbitrary", "arbitrary")),
        name="inproj",
    )(xb, w, b)


def _inproj_qkv_kernel(x_ref, w_ref, b_ref, q_ref, k_ref, v_ref, *, classes):
    x = x_ref[...].astype(BF16)
    rows = x.shape[0] // classes
    for part, o_ref in enumerate((q_ref, k_ref, v_ref)):
        cols = slice(part * ATTN_WIDTH, (part + 1) * ATTN_WIDTH)
        y = jnp.dot(x, w_ref[:, cols], preferred_element_type=F32) + b_ref[:, cols]
        y = pltpu.einshape("jcf->cjf", y.reshape(rows, classes, ATTN_WIDTH))
        o_ref[...] = y.astype(BF16)


def _inproj_qkv(xb, w, b, classes, tm=QKV_TM):
    n, d = xb.shape
    tm = min(tm, n)
    rows = tm // classes
    out = pl.BlockSpec((classes, rows, ATTN_WIDTH), lambda i: (0, i, 0))
    shape = jax.ShapeDtypeStruct((classes, n // classes, ATTN_WIDTH), BF16)
    return pl.pallas_call(
        functools.partial(_inproj_qkv_kernel, classes=classes),
        grid=(n // tm,),
        in_specs=[pl.BlockSpec((tm, d), lambda i: (i, 0)),
                  pl.BlockSpec(w.shape, lambda i: (0, 0)),
                  pl.BlockSpec(b.shape, lambda i: (0, 0))],
        out_specs=[out, out, out],
        out_shape=[shape, shape, shape],
        compiler_params=_cparams(("arbitrary",)),
        name="inproj_qkv",
    )(xb, w, b)


def _gelu_tanh(x):
    c = math.sqrt(2.0 / math.pi)
    return 0.5 * x * (1.0 + jnp.tanh(c * (x + 0.044715 * (x * x * x))))


def _s5_kernel(u_ref, bm_ref, cre_ref, cim_ref, are_ref, aim_ref, d_ref, wglu_ref,
               o_ref, sre_ref, sim_ref, st_re_ref, st_im_ref, y_ref, *, steps, batch):
    kw = SSM_WIDTH // S5_KB
    sw = SSM_GROUPS * SSM_STATE // S5_KB

    @pl.when(pl.program_id(0) == 0)
    def _():
        st_re_ref[...] = jnp.zeros_like(st_re_ref)
        st_im_ref[...] = jnp.zeros_like(st_im_ref)

    for k in range(S5_KB):
        ls = slice(k * sw, (k + 1) * sw)
        bu = jnp.dot(u_ref[:, k * kw:(k + 1) * kw], bm_ref[k], preferred_element_type=F32)
        sre_ref[:, ls] = bu[:, :sw]
        sim_ref[:, ls] = bu[:, sw:]

        ar = jnp.broadcast_to(are_ref[:, ls], (batch, sw))
        ai = jnp.broadcast_to(aim_ref[:, ls], (batch, sw))

        def step(t, carry):
            sr, si = carry
            rows = pl.ds(pl.multiple_of(t * batch, batch), batch)
            nr = ar * sr - ai * si + sre_ref[rows, ls]
            ni = ar * si + ai * sr + sim_ref[rows, ls]
            sre_ref[rows, ls] = nr
            sim_ref[rows, ls] = ni
            return nr, ni

        sr, si = lax.fori_loop(0, steps, step, (st_re_ref[:, ls], st_im_ref[:, ls]), unroll=True)
        st_re_ref[:, ls] = sr
        st_im_ref[:, ls] = si

        y_ref[:, k * kw:(k + 1) * kw] = (
            jnp.dot(sre_ref[:, ls].astype(BF16), cre_ref[k], preferred_element_type=F32)
            + jnp.dot(sim_ref[:, ls].astype(BF16), cim_ref[k], preferred_element_type=F32))

    y = y_ref[...] + d_ref[...] * u_ref[...].astype(F32)
    y = _gelu_tanh(y)
    z = jnp.dot(y.astype(BF16), wglu_ref[...], preferred_element_type=F32)
    o_ref[...] = (y * jax.nn.sigmoid(z)).astype(BF16)


def _s5(proj, bm, cre, cim, a_re, a_im, d_skip, w_glu, batch, steps=S5_T):
    n = proj.shape[0]
    u_blk = proj.shape[1] // SSM_WIDTH - 1
    seq = n // batch
    steps = min(steps, seq)
    rows = steps * batch
    nstate = SSM_GROUPS * SSM_STATE
    const = lambda shape: pl.BlockSpec(shape, lambda i: (0,) * len(shape))
    return pl.pallas_call(
        functools.partial(_s5_kernel, steps=steps, batch=batch),
        grid=(seq // steps,),
        in_specs=[pl.BlockSpec((rows, SSM_WIDTH), lambda i: (i, u_blk)),
                  const(bm.shape), const(cre.shape), const(cim.shape),
                  const(a_re.shape), const(a_im.shape), const(d_skip.shape), const(w_glu.shape)],
        out_specs=pl.BlockSpec((rows, SSM_WIDTH), lambda i: (i, 0)),
        out_shape=jax.ShapeDtypeStruct((n, SSM_WIDTH), BF16),
        scratch_shapes=[pltpu.VMEM((rows, nstate), F32), pltpu.VMEM((rows, nstate), F32),
                        pltpu.VMEM((batch, nstate), F32), pltpu.VMEM((batch, nstate), F32),
                        pltpu.VMEM((rows, SSM_WIDTH), F32)],
        compiler_params=_cparams(("arbitrary",)),
        name="s5",
    )(proj, bm, cre, cim, a_re, a_im, d_skip, w_glu)


def _s5_params(lam_re, lam_im, log_dt, b_re, b_im, c_re, c_im, d_skip):
    lam = lax.complex(lam_re.astype(F32), lam_im.astype(F32))
    dt = jnp.exp(log_dt.astype(F32))[:, None]
    lam_bar = jnp.exp(lam * dt)
    b_bar = ((lam_bar - 1.0) / lam)[:, :, None] * lax.complex(b_re.astype(F32), b_im.astype(F32))
    gl = SSM_GROUPS // S5_KB
    eye = jnp.eye(gl, dtype=F32)

    def in_blocks(b):
        b = b.reshape(S5_KB, gl, SSM_STATE, SSM_GROUP)
        m = jnp.einsum('kgpc,gh->kgchp', b, eye)
        return m.reshape(S5_KB, gl * SSM_GROUP, gl * SSM_STATE)

    def out_blocks(c):
        c = c.reshape(S5_KB, gl, SSM_GROUP, SSM_STATE)
        m = jnp.einsum('kgcp,gh->kgphc', c, eye)
        return m.reshape(S5_KB, gl * SSM_STATE, gl * SSM_GROUP)

    bm = jnp.concatenate([in_blocks(jnp.real(b_bar)), in_blocks(jnp.imag(b_bar))], axis=-1).astype(BF16)
    cre = out_blocks(c_re.astype(F32)).astype(BF16)
    cim = out_blocks(-c_im.astype(F32)).astype(BF16)
    a_re = jnp.real(lam_bar).reshape(1, -1)
    a_im = jnp.imag(lam_bar).reshape(1, -1)
    return bm, cre, cim, a_re, a_im, d_skip.astype(F32).reshape(1, -1)


def _attn_block(q, k2, v2, allowed):
    first = lax.broadcasted_iota(jnp.int32, (QBLK, LANES), 1) < HEAD_DIM
    zero = jnp.zeros((), BF16)
    outs, lses = [], []
    for slab in range(ATTN_WIDTH // LANES):
        ls = slice(slab * LANES, (slab + 1) * LANES)
        qs = jnp.concatenate([jnp.where(first, q[:, ls], zero), jnp.where(first, zero, q[:, ls])], axis=0)
        s = lax.dot_general(qs, k2[:, ls], (((1,), (1,)), ((), ())), preferred_element_type=F32)
        s = jnp.where(allowed, s, NEG)
        m = jnp.max(s, axis=-1, keepdims=True)
        p = jnp.exp(s - m)
        denom = jnp.sum(p, axis=-1, keepdims=True)
        r = jnp.dot(p.astype(BF16), v2[:, ls], preferred_element_type=F32) / denom
        outs.append(jnp.where(first, r[:QBLK], r[QBLK:]))
        lse = m + jnp.log(denom)
        lses += [lse[:QBLK], lse[QBLK:]]
    return jnp.concatenate(outs, axis=-1).astype(BF16), jnp.concatenate(lses, axis=-1)


def _attn_kernel(q_ref, kp_ref, kc_ref, vp_ref, vc_ref, o_ref, lse_ref, *, steps, sub_blocks):
    j = pl.program_id(1)
    qi = lax.broadcasted_iota(jnp.int32, (2 * QBLK, 2 * QBLK), 0) % QBLK
    kj = lax.broadcasted_iota(jnp.int32, (2 * QBLK, 2 * QBLK), 1)
    dist = qi + QBLK - kj
    band = (dist >= 0) & (dist <= steps)
    for b in range(sub_blocks):
        rows = slice(b * QBLK, (b + 1) * QBLK)
        q = q_ref[rows, :] * (HEAD_DIM ** -0.5)
        if b == 0:
            k_prev, v_prev = kp_ref[...], vp_ref[...]
            allowed = band & ((kj >= QBLK) | (j > 0))
        else:
            before = slice((b - 1) * QBLK, b * QBLK)
            k_prev, v_prev = kc_ref[before, :], vc_ref[before, :]
            allowed = band
        k2 = jnp.concatenate([k_prev, kc_ref[rows, :]], axis=0)
        v2 = jnp.concatenate([v_prev, vc_ref[rows, :]], axis=0)
        o, lse = _attn_block(q, k2, v2, allowed)
        o_ref[rows, :] = o
        lse_ref[rows, :] = lse


def _attn(q, k, v, steps, sub_blocks=ATTN_SUB_BLOCKS):
    classes, n, width = q.shape
    sub_blocks = min(sub_blocks, n // QBLK)
    rows = sub_blocks * QBLK
    cur = pl.BlockSpec((None, rows, width), lambda c, j: (c, j, 0))
    prev = pl.BlockSpec((None, QBLK, width), lambda c, j: (c, jnp.maximum(j * sub_blocks - 1, 0), 0))
    return pl.pallas_call(
        functools.partial(_attn_kernel, steps=steps, sub_blocks=sub_blocks),
        grid=(classes, n // rows),
        in_specs=[cur, prev, cur, prev, cur],
        out_specs=[cur, pl.BlockSpec((None, rows, HEADS), lambda c, j: (c, j, 0))],
        out_shape=[jax.ShapeDtypeStruct((classes, n, width), BF16),
                   jax.ShapeDtypeStruct((classes, n, HEADS), F32)],
        compiler_params=_cparams(("arbitrary", "arbitrary")),
        name="attn",
    )(q, k, k, v, v)


def _layer_norm(z, g, b):
    mu = jnp.mean(z, axis=-1, keepdims=True)
    zc = z - mu
    var = jnp.mean(zc * zc, axis=-1, keepdims=True)
    return zc * lax.rsqrt(var + LN_EPS) * g + b


def _row_words(d):
    return d // (2 * LANES)


def _store_rows(ref, y):
    rows, d = y.shape
    sub = _row_words(d)
    for s in range(sub):
        lo = y[:, (2 * s) * LANES:(2 * s + 1) * LANES].astype(BF16).astype(F32)
        hi = y[:, (2 * s + 1) * LANES:(2 * s + 2) * LANES].astype(BF16).astype(F32)
        word = (pltpu.bitcast(lo, jnp.uint32) >> 16) | (pltpu.bitcast(hi, jnp.uint32) & HI_HALF)
        ref[pl.ds(s, rows, stride=sub), :] = word


def _load_rows(ref, rows, sub):
    parts = []
    for s in range(sub):
        word = ref[pl.ds(s, rows, stride=sub), :]
        parts.append(pltpu.bitcast(word << 16, F32))
        parts.append(pltpu.bitcast(word & HI_HALF, F32))
    return jnp.concatenate(parts, axis=-1)


def _expand_heads(w, e_ref):
    hi = w.astype(BF16)
    lo = (w - hi.astype(F32)).astype(BF16)
    return jnp.dot(jnp.concatenate([hi, lo], axis=-1), e_ref[...], preferred_element_type=F32)


def _merge_kernel(x_ref, yssm_ref, o0_ref, o1_ref, o2_ref, lse_ref, gs_ref, ga_ref,
                  wbs_ref, wba_ref, wout_ref, e_ref, g_ref, b_ref, o_ref, orow_ref, *, alpha):
    lse = lse_ref[...]
    l0, l1, l2 = lse[:, 0:HEADS], lse[:, HEADS:2 * HEADS], lse[:, 2 * HEADS:3 * HEADS]
    m = jnp.maximum(jnp.maximum(l0, l1), l2)
    e0, e1, e2 = jnp.exp(l0 - m), jnp.exp(l1 - m), jnp.exp(l2 - m)
    den = e0 + e1 + e2
    y_attn = (_expand_heads(e0 / den, e_ref) * o0_ref[...].astype(F32)
              + _expand_heads(e1 / den, e_ref) * o1_ref[...].astype(F32)
              + _expand_heads(e2 / den, e_ref) * o2_ref[...].astype(F32))
    ya = jnp.dot(y_attn.astype(BF16), wba_ref[...], preferred_element_type=F32)
    ys = jnp.dot(yssm_ref[...], wbs_ref[...], preferred_element_type=F32)
    merged = (jax.nn.sigmoid(gs_ref[...].astype(F32)) * ys
              + jax.nn.sigmoid(ga_ref[...].astype(F32)) * ya)
    mix = jnp.dot(merged.astype(BF16), wout_ref[...], preferred_element_type=F32)
    y = _layer_norm(alpha * x_ref[...] + mix, g_ref[...], b_ref[...])
    o_ref[...] = y
    _store_rows(orow_ref, y)


def _merge(x, y_ssm, o0, o1, o2, lse, proj, w_br_ssm, w_br_attn, w_out, expand, ln_g, ln_b,
           alpha, tm=MERGE_TM):
    n, d = x.shape
    tm = min(tm, n)
    gate_blk = proj.shape[1] // d
    row = lambda w: pl.BlockSpec((tm, w), lambda i: (i, 0))
    const = lambda a: pl.BlockSpec(a.shape, lambda i: (0,) * a.ndim)
    return pl.pallas_call(
        functools.partial(_merge_kernel, alpha=alpha),
        grid=(n // tm,),
        in_specs=[row(d), row(SSM_WIDTH), row(ATTN_WIDTH), row(ATTN_WIDTH), row(ATTN_WIDTH),
                  row(lse.shape[1]),
                  pl.BlockSpec((tm, d), lambda i: (i, gate_blk - 2)),
                  pl.BlockSpec((tm, d), lambda i: (i, gate_blk - 1)),
                  const(w_br_ssm), const(w_br_attn), const(w_out), const(expand),
                  const(ln_g), const(ln_b)],
        out_specs=[row(d), pl.BlockSpec((tm * _row_words(d), LANES), lambda i: (i, 0))],
        out_shape=[jax.ShapeDtypeStruct((n, d), F32),
                   jax.ShapeDtypeStruct((n * _row_words(d), LANES), jnp.uint32)],
        compiler_params=_cparams(("arbitrary",)),
        name="merge",
    )(x, y_ssm, o0, o1, o2, lse, proj, proj, w_br_ssm, w_br_attn, w_out, expand, ln_g, ln_b)


def _first_argmax(v, iota, size, axis):
    m = jnp.max(v, axis=axis, keepdims=True)
    idx = jnp.min(jnp.where(v == m, iota, size), axis=axis, keepdims=True)
    return m, idx


def _route_kernel(x_ref, rwh_ref, rwl_ref, bias_ref, eg_ref, tri_ref,
                  eidx_ref, gate_ref, rank_ref, cnt_ref, carry_ref):
    @pl.when(pl.program_id(0) == 0)
    def _():
        carry_ref[...] = jnp.zeros_like(carry_ref)

    x = x_ref[...]
    tm = x.shape[0]
    xh = x.astype(BF16)
    xl = (x - xh.astype(F32)).astype(BF16)
    nt = (((1,), (1,)), ((), ()))
    logits = (lax.dot_general(rwh_ref[...], xh, nt, preferred_element_type=F32)
              + lax.dot_general(rwh_ref[...], xl, nt, preferred_element_type=F32)
              + lax.dot_general(rwl_ref[...], xh, nt, preferred_element_type=F32))
    scores = jax.nn.sigmoid(logits)
    sel = scores + bias_ref[...]

    sel3 = sel.reshape(N_EXPERT_GROUPS, GROUP_SIZE, tm)
    iw = lax.broadcasted_iota(jnp.int32, sel3.shape, 1)
    m1, i1 = _first_argmax(sel3, iw, GROUP_SIZE, 1)
    m2 = jnp.max(jnp.where(iw == i1, NEG, sel3), axis=1, keepdims=True)
    gs = (m1 + m2).reshape(N_EXPERT_GROUPS, tm)

    ig = lax.broadcasted_iota(jnp.int32, gs.shape, 0)
    gmask = jnp.zeros(gs.shape, F32)
    for _ in range(TOPK_GROUPS):
        _, gi = _first_argmax(gs, ig, N_EXPERT_GROUPS, 0)
        hit = ig == gi
        gmask = jnp.where(hit, 1.0, gmask)
        gs = jnp.where(hit, NEG, gs)
    emask = jnp.dot(eg_ref[...], gmask.astype(BF16), preferred_element_type=F32) > 0.5

    masked = jnp.where(emask, sel, NEG)
    ie = lax.broadcasted_iota(jnp.int32, masked.shape, 0)
    chosen = jnp.zeros(masked.shape, F32)
    idxs, vals = [], []
    for _ in range(TOP_K):
        _, ei = _first_argmax(masked, ie, N_EXPERTS, 0)
        hit = ie == ei
        idxs.append(ei)
        vals.append(jnp.sum(jnp.where(hit, scores, 0.0), axis=0, keepdims=True))
        chosen = jnp.where(hit, 1.0, chosen)
        masked = jnp.where(hit, NEG, masked)
    total = vals[0]
    for v in vals[1:]:
        total = total + v

    prefix = jnp.dot(chosen.astype(BF16), tri_ref[...], preferred_element_type=F32)
    pos = carry_ref[:, 0:1] + prefix - 1.0
    ranks = [jnp.sum(jnp.where(ie == ei, pos, 0.0), axis=0, keepdims=True) for ei in idxs]
    carry = carry_ref[...] + jnp.sum(chosen, axis=1, keepdims=True)
    carry_ref[...] = carry
    cnt_ref[...] = carry

    eidx_ref[...] = jnp.concatenate(idxs, axis=0)
    gate_ref[...] = jnp.concatenate([v / total * ROUTED_SCALE for v in vals], axis=0)
    rank_ref[...] = jnp.concatenate(ranks, axis=0).astype(jnp.int32)


def _route(x, row0, n, rw_hi, rw_lo, bias, eg, tri, tm=ROUTE_TM):
    d = x.shape[1]
    tm = min(tm, n)
    blk0 = row0 // tm
    const = lambda a: pl.BlockSpec(a.shape, lambda i: (0,) * a.ndim)
    col = pl.BlockSpec((TOP_K, tm), lambda i: (0, i))
    return pl.pallas_call(
        _route_kernel,
        grid=(n // tm,),
        in_specs=[pl.BlockSpec((tm, d), lambda i: (i + blk0, 0)),
                  const(rw_hi), const(rw_lo), const(bias), const(eg), const(tri)],
        out_specs=[col, col, col, pl.BlockSpec((N_EXPERTS, LANES), lambda i: (0, 0))],
        out_shape=[jax.ShapeDtypeStruct((TOP_K, n), jnp.int32),
                   jax.ShapeDtypeStruct((TOP_K, n), F32),
                   jax.ShapeDtypeStruct((TOP_K, n), jnp.int32),
                   jax.ShapeDtypeStruct((N_EXPERTS, LANES), F32)],
        scratch_shapes=[pltpu.VMEM((N_EXPERTS, LANES), F32)],
        compiler_params=_cparams(("arbitrary",)),
        name="route",
    )(x, rw_hi, rw_lo, bias, eg, tri)


def _sc_gather_rows(src, idx, sub, chunk=SC_CHUNK):
    s = src.shape[0] // sub
    m = idx.shape[0]
    info = plsc.get_sparse_core_info()
    n_workers = info.num_cores * info.num_subcores
    per_worker = m // n_workers
    n_chunks = per_worker // chunk
    assert n_chunks * chunk * n_workers == m
    mesh = plsc.VectorSubcoreMesh(core_axis_name="c", subcore_axis_name="s")

    @functools.partial(
        pl.kernel, mesh=mesh,
        out_type=jax.ShapeDtypeStruct((m, sub, LANES), src.dtype),
        scratch_types=[pltpu.VMEM((n_chunks, chunk), jnp.int32),
                       pltpu.VMEM((chunk, sub, LANES), src.dtype),
                       pltpu.SemaphoreType.DMA])
    def gather(src_hbm, idx_hbm, out_hbm, idx_v, rows_v, sem):
        wid = lax.axis_index("s") * info.num_cores + lax.axis_index("c")
        pltpu.sync_copy(idx_hbm.at[wid], idx_v)

        @pl.loop(0, n_chunks)
        def _(j):
            pltpu.async_copy(src_hbm.at[idx_v.at[j]], rows_v, sem).wait()
            pltpu.sync_copy(rows_v, out_hbm.at[pl.ds(wid * per_worker + j * chunk, chunk)])

    out = gather(src.reshape(s, sub, LANES), idx.reshape(n_workers, n_chunks, chunk))
    return out.reshape(m * sub, LANES)


def _sc_scatter_rows(src, row0, dest, n_out, sub, chunk=SC_CHUNK):
    copies, n = dest.shape
    s = src.shape[0] // sub
    info = plsc.get_sparse_core_info()
    n_workers = info.num_cores * info.num_subcores
    per_worker = n // n_workers
    n_chunks = per_worker // chunk
    assert n_chunks * chunk * n_workers == n and copies * n == n_out
    mesh = plsc.VectorSubcoreMesh(core_axis_name="c", subcore_axis_name="s")
    idx = dest.reshape(copies, n_workers, n_chunks, chunk).transpose(1, 2, 0, 3)
    idx = idx.reshape(n_workers, n_chunks * copies, chunk)

    @functools.partial(
        pl.kernel, mesh=mesh,
        out_type=jax.ShapeDtypeStruct((n_out, sub, LANES), src.dtype),
        scratch_types=[pltpu.VMEM((n_chunks * copies, chunk), jnp.int32),
                       pltpu.VMEM((chunk, sub, LANES), src.dtype),
                       pltpu.SemaphoreType.DMA])
    def scatter(src_hbm, idx_hbm, out_hbm, idx_v, rows_v, sem):
        wid = lax.axis_index("s") * info.num_cores + lax.axis_index("c")
        pltpu.sync_copy(idx_hbm.at[wid], idx_v)

        @pl.loop(0, n_chunks)
        def _(j):
            pltpu.sync_copy(src_hbm.at[pl.ds(row0 + wid * per_worker + j * chunk, chunk)], rows_v)
            for c in range(copies):
                pltpu.async_copy(rows_v, out_hbm.at[idx_v.at[j * copies + c]], sem).wait()

    out = scatter(src.reshape(s, sub, LANES), idx)
    return out.reshape(n_out * sub, LANES)


def _gmm_kernel(be_ref, nu_ref, x_ref, wg_ref, wu_ref, wd_ref, o_ref, wgu_s, wd_s, *, hidden, blk, sub):
    b = pl.program_id(0)
    live = b < nu_ref[0]

    @pl.when(live & ((b == 0) | (be_ref[b] != be_ref[jnp.maximum(b - 1, 0)])))
    def _():
        wgu_s[:, :hidden] = wg_ref[...].astype(BF16)
        wgu_s[:, hidden:] = wu_ref[...].astype(BF16)
        wd_s[...] = wd_ref[...].astype(BF16)

    @pl.when(live)
    def _():
        x = _load_rows(x_ref, blk, sub).astype(BF16)
        gu = jnp.dot(x, wgu_s[...], preferred_element_type=F32)
        g, u = gu[:, :hidden], gu[:, hidden:]
        h = (g * jax.nn.sigmoid(g) * u).astype(BF16)
        _store_rows(o_ref, jnp.dot(h, wd_s[...], preferred_element_type=F32))


def _gmm(xs, w_gate, w_up, w_down, layer, blk_e, n_used, blk=MOE_BLK):
    hidden, d = w_down.shape[2:]
    sub = _row_words(d)
    n_rows = xs.shape[0] // sub
    row = lambda b, be, nu: (jnp.minimum(b, nu[0] - 1), 0)
    expert = lambda b, be, nu: (layer, be[b], 0, 0)
    return pl.pallas_call(
        functools.partial(_gmm_kernel, hidden=hidden, blk=blk, sub=sub),
        grid_spec=pltpu.PrefetchScalarGridSpec(
            num_scalar_prefetch=2,
            grid=(n_rows // blk,),
            in_specs=[pl.BlockSpec((blk * sub, LANES), row),
                      pl.BlockSpec((None, None, d, hidden), expert),
                      pl.BlockSpec((None, None, d, hidden), expert),
                      pl.BlockSpec((None, None, hidden, d), expert)],
            out_specs=pl.BlockSpec((blk * sub, LANES), row),
            scratch_shapes=[pltpu.VMEM((d, 2 * hidden), BF16), pltpu.VMEM((hidden, d), BF16)]),
        out_shape=jax.ShapeDtypeStruct((n_rows * sub, LANES), jnp.uint32),
        compiler_params=_cparams(("arbitrary",)),
        name="expert_gmm",
    )(blk_e, n_used, xs, w_gate, w_up, w_down)


def _combine_kernel(x_ref, yg_ref, gate_ref, wgu_ref, wd_ref, g_ref, b_ref, *rest, alpha, hidden, final_batch):
    x = x_ref[...]
    gu = jnp.dot(x.astype(BF16), wgu_ref[...], preferred_element_type=F32)
    g, u = gu[:, :hidden], gu[:, hidden:]
    h = (g * jax.nn.sigmoid(g) * u).astype(BF16)
    acc = jnp.dot(h, wd_ref[...], preferred_element_type=F32)
    gates = gate_ref[...]
    tm, d = x.shape
    for k in range(TOP_K):
        acc = acc + gates[:, k:k + 1] * _load_rows(yg_ref.at[k], tm, _row_words(d))
    y = _layer_norm(alpha * x + acc, g_ref[...], b_ref[...])
    if final_batch:
        rest[-1][...] = pltpu.einshape("tbd->btd", y.reshape(tm // final_batch, final_batch, d))
    else:
        o_ref, ob_ref = rest[-2:]
        o_ref[...] = y
        ob_ref[...] = y.astype(BF16)


def _combine(x, row0, yg, gates_t, w_gu, w_down, ln_g, ln_b, alpha, prev=None, final_batch=0, tm=COMBINE_TM):
    n, d = x.shape
    n_part = gates_t.shape[0]
    tm = min(tm, n_part)
    blk0 = row0 // tm
    sub = _row_words(d)
    hidden = w_down.shape[0]
    const = lambda a: pl.BlockSpec(a.shape, lambda i: (0,) * a.ndim)
    rows = pl.BlockSpec((tm, d), lambda i: (i + blk0, 0))
    in_specs = [rows,
                pl.BlockSpec((TOP_K, tm * sub, LANES), lambda i: (0, i, 0)),
                pl.BlockSpec((tm, TOP_K), lambda i: (i, 0)),
                const(w_gu), const(w_down), const(ln_g), const(ln_b)]
    args = [x, yg, gates_t, w_gu, w_down, ln_g, ln_b]
    if final_batch:
        out_specs = [pl.BlockSpec((final_batch, tm // final_batch, d), lambda i: (0, i + blk0, 0))]
        out_shape = [jax.ShapeDtypeStruct((final_batch, n // final_batch, d), F32)]
    else:
        out_specs = [rows, rows]
        out_shape = [jax.ShapeDtypeStruct((n, d), F32), jax.ShapeDtypeStruct((n, d), BF16)]
    aliases = {}
    if prev is not None:
        aliases = {len(args) + o: o for o in range(len(out_shape))}
        in_specs += [pl.BlockSpec(memory_space=pl.ANY)] * len(out_shape)
        args += list(prev)
    return pl.pallas_call(
        functools.partial(_combine_kernel, alpha=alpha, hidden=hidden, final_batch=final_batch),
        grid=(n_part // tm,),
        in_specs=in_specs,
        out_specs=out_specs,
        out_shape=out_shape,
        input_output_aliases=aliases,
        compiler_params=_cparams(("arbitrary",)),
        name="combine",
    )(*args)


def _split_w_in(w_in, b_in, d):
    qkv0 = SSM_WIDTH
    qkv_w = len(ATTN_PATTERNS) * ATTN_WIDTH
    gates0 = qkv0 + 3 * qkv_w
    cols = lambda a, c0, w: a[:, c0:c0 + w]
    pick = lambda a: jnp.concatenate([cols(a, gates0, 2 * d), cols(a, 0, SSM_WIDTH)], axis=1)
    main = (pick(w_in).astype(BF16), pick(b_in))
    groups = []
    for g in range(len(ATTN_PATTERNS)):
        sel = lambda a: jnp.concatenate([cols(a, qkv0 + s * qkv_w + g * ATTN_WIDTH, ATTN_WIDTH)
                                         for s in range(3)], axis=1)
        groups.append((sel(w_in).astype(BF16), sel(b_in)))
    return main, groups


def _token_mixer(x, xb, batch, w_in, b_in, s5p, w_glu, w_br_ssm, w_br_attn, w_out, expand, ln_g, ln_b, alpha):
    n, d = x.shape
    seq = n // batch
    (w_main, b_main), groups = _split_w_in(w_in, b_in, d)
    proj = _inproj(xb, w_main, b_main)
    y_ssm = _s5(proj, *s5p, w_glu, batch)

    outs, lses = [], []
    for (window, dil), (w_g, b_g) in zip(ATTN_PATTERNS, groups):
        assert window // dil == QBLK and (seq // dil) % QBLK == 0
        q, k, v = _inproj_qkv(xb, w_g, b_g, dil * batch)
        o_g, lse_g = _attn(q, k, v, window // dil)
        outs.append(jnp.swapaxes(o_g, 0, 1).reshape(n, ATTN_WIDTH))
        lses.append(jnp.swapaxes(lse_g, 0, 1).reshape(n, HEADS))
    lse = jnp.concatenate(lses, axis=-1)
    return _merge(x, y_ssm, outs[0], outs[1], outs[2], lse, proj, w_br_ssm, w_br_attn, w_out,
                  expand, ln_g, ln_b, alpha)


def _moe_plan(x, row0, n, rw_hi, rw_lo, bias, eg, tri):
    eidx, gates, rank, cnt = _route(x, row0, n, rw_hi, rw_lo, bias, eg, tri)
    counts = cnt[:, 0].astype(jnp.int32)
    nk = n * TOP_K
    n_pad = N_EXPERTS * MOE_BLK
    n_rows = nk + n_pad
    n_blocks = n_rows // MOE_BLK
    assert n_pad % n == 0
    pad_counts = (counts + MOE_BLK - 1) // MOE_BLK * MOE_BLK
    pad_end = jnp.cumsum(pad_counts)
    pad_start = pad_end - pad_counts
    onehot = eidx[:, :, None] == jnp.arange(N_EXPERTS, dtype=jnp.int32)
    dest = jnp.sum(jnp.where(onehot, pad_start, 0), axis=-1) + rank
    seg_start = jnp.concatenate([pad_start + counts, pad_end[-1:]])
    seg_len = jnp.concatenate([pad_counts - counts, n_rows - pad_end[-1:]])
    seg_end = jnp.cumsum(seg_len)
    q = jnp.arange(n_pad, dtype=jnp.int32)[:, None]
    seg = jnp.sum((q >= seg_end[None, :]).astype(jnp.int32), axis=1, keepdims=True)
    seg_hot = seg == jnp.arange(N_EXPERTS + 1, dtype=jnp.int32)[None, :]
    pad_dest = jnp.sum(jnp.where(seg_hot, seg_start - (seg_end - seg_len), 0), axis=1) + q[:, 0]
    dest_all = jnp.concatenate([dest, pad_dest.reshape(n_pad // n, n)], axis=0).astype(jnp.int32)
    blk_first = jnp.arange(n_blocks, dtype=jnp.int32)[:, None] * MOE_BLK
    blk_e = jnp.minimum(jnp.sum((blk_first >= pad_end[None, :]).astype(jnp.int32), axis=1), N_EXPERTS - 1)
    n_used = (pad_end[-1:] // MOE_BLK).astype(jnp.int32)
    return dest_all, dest.reshape(nk), gates.T, blk_e, n_used


def _moe(x, x_rows, final_batch, rw_hi, rw_lo, bias, eg, tri, w_gate, w_up, w_down, layer,
         sh_gu, sh_down, ln_g, ln_b, alpha):
    n, d = x.shape
    sub = _row_words(d)
    part = n // MOE_PARTS
    plans = [_moe_plan(x, i * part, part, rw_hi, rw_lo, bias, eg, tri) for i in range(MOE_PARTS)]
    out = None
    for i, (dest_all, dest, gates_t, blk_e, n_used) in enumerate(plans):
        xs = _sc_scatter_rows(x_rows, i * part, dest_all, dest_all.size, sub)
        ys = _gmm(xs, w_gate, w_up, w_down, layer, blk_e, n_used)
        yg = _sc_gather_rows(ys, dest, sub).reshape(TOP_K, part * sub, LANES)
        out = _combine(x, i * part, yg, gates_t, sh_gu, sh_down, ln_g, ln_b, alpha, out, final_batch)
    return out


def kernel(x, w_in, b_in, ssm_lam_re, ssm_lam_im, ssm_log_dt, ssm_b_re, ssm_b_im, ssm_c_re, ssm_c_im, ssm_d, w_glu, w_br_ssm, w_br_attn, w_out, ln1_g, ln1_b, router_w, router_bias, exp_w_gate, exp_w_up, exp_w_down, sh_w_gate, sh_w_up, sh_w_down, ln2_g, ln2_b):
    batch, seq, d = x.shape
    depth = w_in.shape[0]
    assert batch == SUBLANES
    alpha = (2 * depth) ** 0.25
    n = batch * seq
    xt = jnp.swapaxes(x, 0, 1).reshape(n, d)
    xb = xt

    expand = jnp.tile(jnp.repeat(jnp.eye(HEADS, dtype=BF16), HEAD_DIM, axis=1), (2, 1))
    eg = jnp.repeat(jnp.eye(N_EXPERT_GROUPS, dtype=BF16), GROUP_SIZE, axis=0)
    tm = min(ROUTE_TM, n)
    tri = (jnp.arange(tm)[:, None] <= jnp.arange(tm)[None, :]).astype(BF16)
    row = lambda a: a.astype(F32).reshape(1, -1)

    for l in range(depth):
        s5p = _s5_params(ssm_lam_re[l], ssm_lam_im[l], ssm_log_dt[l], ssm_b_re[l], ssm_b_im[l],
                         ssm_c_re[l], ssm_c_im[l], ssm_d[l])
        xt, xt_rows = _token_mixer(xt, xb, batch, w_in[l], row(b_in[l]), s5p,
                                   w_glu[l].astype(BF16), w_br_ssm[l].astype(BF16),
                                   w_br_attn[l].astype(BF16), w_out[l].astype(BF16),
                                   expand, row(ln1_g[l]), row(ln1_b[l]), alpha)
        rwt = router_w[l].astype(F32).T
        rw_hi = rwt.astype(BF16)
        rw_lo = (rwt - rw_hi.astype(F32)).astype(BF16)
        sh_gu = jnp.concatenate([sh_w_gate[l], sh_w_up[l]], axis=-1).astype(BF16)
        last = l == depth - 1
        out = _moe(xt, xt_rows, batch if last else 0, rw_hi, rw_lo,
                   router_bias[l].astype(F32).reshape(-1, 1), eg, tri,
                   exp_w_gate, exp_w_up, exp_w_down, l, sh_gu, sh_w_down[l].astype(BF16),
                   row(ln2_g[l]), row(ln2_b[l]), alpha)
        if not last:
            xt, xb = out
    return out[0]
```

```python
import functools
import math

import jax
import jax.numpy as jnp
import numpy as np
from jax import lax
from jax.experimental import pallas as pl
from jax.experimental.pallas import tpu as pltpu
from jax.experimental.pallas import tpu_sc as plsc

F32 = jnp.float32
BF16 = jnp.bfloat16

SSM_GROUP = 16
SSM_GROUPS = 32
SSM_WIDTH = SSM_GROUP * SSM_GROUPS
SSM_STATE = 64
HEAD_DIM = 64
HEADS = 8
ATTN_PATTERNS = ((128, 1), (512, 4), (2048, 16))
ATTN_WIDTH = HEADS * HEAD_DIM
QBLK = 128
N_EXPERTS = 64
TOP_K = 8
N_EXPERT_GROUPS = 8
GROUP_SIZE = N_EXPERTS // N_EXPERT_GROUPS
TOPK_GROUPS = 4
ROUTED_SCALE = 2.5
LN_EPS = 1e-5
NEG = -1e30

LANES = 128
SUBLANES = 8
VMEM_LIMIT = 56 * 1024 * 1024

INPROJ_TM = 2048
INPROJ_TN = 1280
QKV_TM = 2048
S5_T = 128
S5_KB = 4
ATTN_SUB_BLOCKS = 4
MERGE_TM = 512
MOE_BLK = 1024
MOE_PARTS = 2
SC_CHUNK = 128
HI_HALF = np.uint32(0xFFFF0000)
COMBINE_TM = 512


def _cparams(sem):
    return pltpu.CompilerParams(dimension_semantics=sem, vmem_limit_bytes=VMEM_LIMIT)


def _inproj_kernel(x_ref, w_ref, b_ref, o_ref):
    acc = jnp.dot(x_ref[...].astype(BF16), w_ref[...], preferred_element_type=F32)
    o_ref[...] = (acc + b_ref[...]).astype(BF16)


def _inproj(xb, w, b, tm=INPROJ_TM, tn=INPROJ_TN):
    n, d = xb.shape
    width = w.shape[1]
    tm = min(tm, n)
    return pl.pallas_call(
        _inproj_kernel,
        grid=(n // tm, width // tn),
        in_specs=[pl.BlockSpec((tm, d), lambda i, j: (i, 0)),
                  pl.BlockSpec((d, tn), lambda i, j: (0, j)),
                  pl.BlockSpec((1, tn), lambda i, j: (0, j))],
        out_specs=pl.BlockSpec((tm, tn), lambda i, j: (i, j)),
        out_shape=jax.ShapeDtypeStruct((n, width), BF16),
        compiler_params=_cparams(("arbitrary", "arbitrary")),
        name="inproj",
    )(xb, w, b)


def _inproj_qkv_kernel(x_ref, w_ref, b_ref, q_ref, k_ref, v_ref, *, classes):
    x = x_ref[...].astype(BF16)
    rows = x.shape[0] // classes
    for part, o_ref in enumerate((q_ref, k_ref, v_ref)):
        cols = slice(part * ATTN_WIDTH, (part + 1) * ATTN_WIDTH)
        y = jnp.dot(x, w_ref[:, cols], preferred_element_type=F32) + b_ref[:, cols]
        y = y.astype(BF16).reshape(rows, classes, ATTN_WIDTH)
        o_ref[...] = pltpu.einshape("jcf->cjf", y)


def _inproj_qkv(xb, w, b, classes, tm=QKV_TM):
    n, d = xb.shape
    tm = min(tm, n)
    rows = tm // classes
    out = pl.BlockSpec((classes, rows, ATTN_WIDTH), lambda i: (0, i, 0))
    shape = jax.ShapeDtypeStruct((classes, n // classes, ATTN_WIDTH), BF16)
    return pl.pallas_call(
        functools.partial(_inproj_qkv_kernel, classes=classes),
        grid=(n // tm,),
        in_specs=[pl.BlockSpec((tm, d), lambda i: (i, 0)),
                  pl.BlockSpec(w.shape, lambda i: (0, 0)),
                  pl.BlockSpec(b.shape, lambda i: (0, 0))],
        out_specs=[out, out, out],
        out_shape=[shape, shape, shape],
        compiler_params=_cparams(("arbitrary",)),
        name="inproj_qkv",
    )(xb, w, b)


def _gelu_tanh(x):
    c = math.sqrt(2.0 / math.pi)
    return 0.5 * x * (1.0 + jnp.tanh(c * (x + 0.044715 * (x * x * x))))


def _s5_kernel(u_ref, bm_ref, cre_ref, cim_ref, are_ref, aim_ref, d_ref, wglu_ref,
               o_ref, sre_ref, sim_ref, st_re_ref, st_im_ref, y_ref, *, steps, batch):
    kw = SSM_WIDTH // S5_KB
    sw = SSM_GROUPS * SSM_STATE // S5_KB

    @pl.when(pl.program_id(0) == 0)
    def _():
        st_re_ref[...] = jnp.zeros_like(st_re_ref)
        st_im_ref[...] = jnp.zeros_like(st_im_ref)

    for k in range(S5_KB):
        ls = slice(k * sw, (k + 1) * sw)
        bu = jnp.dot(u_ref[:, k * kw:(k + 1) * kw], bm_ref[k], preferred_element_type=F32)
        sre_ref[:, ls] = bu[:, :sw]
        sim_ref[:, ls] = bu[:, sw:]

        ar = jnp.broadcast_to(are_ref[:, ls], (batch, sw))
        ai = jnp.broadcast_to(aim_ref[:, ls], (batch, sw))

        def step(t, carry):
            sr, si = carry
            rows = pl.ds(pl.multiple_of(t * batch, batch), batch)
            nr = ar * sr - ai * si + sre_ref[rows, ls]
            ni = ar * si + ai * sr + sim_ref[rows, ls]
            sre_ref[rows, ls] = nr
            sim_ref[rows, ls] = ni
            return nr, ni

        sr, si = lax.fori_loop(0, steps, step, (st_re_ref[:, ls], st_im_ref[:, ls]), unroll=True)
        st_re_ref[:, ls] = sr
        st_im_ref[:, ls] = si

        y_ref[:, k * kw:(k + 1) * kw] = (
            jnp.dot(sre_ref[:, ls].astype(BF16), cre_ref[k], preferred_element_type=F32)
            + jnp.dot(sim_ref[:, ls].astype(BF16), cim_ref[k], preferred_element_type=F32))

    y = y_ref[...] + d_ref[...] * u_ref[...].astype(F32)
    y = _gelu_tanh(y)
    z = jnp.dot(y.astype(BF16), wglu_ref[...], preferred_element_type=F32)
    o_ref[...] = (y * jax.nn.sigmoid(z)).astype(BF16)


def _s5(proj, bm, cre, cim, a_re, a_im, d_skip, w_glu, batch, steps=S5_T):
    n = proj.shape[0]
    u_blk = proj.shape[1] // SSM_WIDTH - 1
    seq = n // batch
    steps = min(steps, seq)
    rows = steps * batch
    nstate = SSM_GROUPS * SSM_STATE
    const = lambda shape: pl.BlockSpec(shape, lambda i: (0,) * len(shape))
    return pl.pallas_call(
        functools.partial(_s5_kernel, steps=steps, batch=batch),
        grid=(seq // steps,),
        in_specs=[pl.BlockSpec((rows, SSM_WIDTH), lambda i: (i, u_blk)),
                  const(bm.shape), const(cre.shape), const(cim.shape),
                  const(a_re.shape), const(a_im.shape), const(d_skip.shape), const(w_glu.shape)],
        out_specs=pl.BlockSpec((rows, SSM_WIDTH), lambda i: (i, 0)),
        out_shape=jax.ShapeDtypeStruct((n, SSM_WIDTH), BF16),
        scratch_shapes=[pltpu.VMEM((rows, nstate), F32), pltpu.VMEM((rows, nstate), F32),
                        pltpu.VMEM((batch, nstate), F32), pltpu.VMEM((batch, nstate), F32),
                        pltpu.VMEM((rows, SSM_WIDTH), F32)],
        compiler_params=_cparams(("arbitrary",)),
        name="s5",
    )(proj, bm, cre, cim, a_re, a_im, d_skip, w_glu)


def _s5_params(lam_re, lam_im, log_dt, b_re, b_im, c_re, c_im, d_skip):
    lam = lax.complex(lam_re.astype(F32), lam_im.astype(F32))
    dt = jnp.exp(log_dt.astype(F32))[:, None]
    lam_bar = jnp.exp(lam * dt)
    b_bar = ((lam_bar - 1.0) / lam)[:, :, None] * lax.complex(b_re.astype(F32), b_im.astype(F32))
    gl = SSM_GROUPS // S5_KB
    eye = jnp.eye(gl, dtype=F32)

    def in_blocks(b):
        b = b.reshape(S5_KB, gl, SSM_STATE, SSM_GROUP)
        m = jnp.einsum('kgpc,gh->kgchp', b, eye)
        return m.reshape(S5_KB, gl * SSM_GROUP, gl * SSM_STATE)

    def out_blocks(c):
        c = c.reshape(S5_KB, gl, SSM_GROUP, SSM_STATE)
        m = jnp.einsum('kgcp,gh->kgphc', c, eye)
        return m.reshape(S5_KB, gl * SSM_STATE, gl * SSM_GROUP)

    bm = jnp.concatenate([in_blocks(jnp.real(b_bar)), in_blocks(jnp.imag(b_bar))], axis=-1).astype(BF16)
    cre = out_blocks(c_re.astype(F32)).astype(BF16)
    cim = out_blocks(-c_im.astype(F32)).astype(BF16)
    a_re = jnp.real(lam_bar).reshape(1, -1)
    a_im = jnp.imag(lam_bar).reshape(1, -1)
    return bm, cre, cim, a_re, a_im, d_skip.astype(F32).reshape(1, -1)


def _attn_block(q, k2, v2, allowed):
    first = lax.broadcasted_iota(jnp.int32, (QBLK, LANES), 1) < HEAD_DIM
    zero = jnp.zeros((), BF16)
    outs, lses = [], []
    for slab in range(ATTN_WIDTH // LANES):
        ls = slice(slab * LANES, (slab + 1) * LANES)
        qs = jnp.concatenate([jnp.where(first, q[:, ls], zero), jnp.where(first, zero, q[:, ls])], axis=0)
        s = lax.dot_general(qs, k2[:, ls], (((1,), (1,)), ((), ())), preferred_element_type=F32)
        s = jnp.where(allowed, s, NEG)
        m = jnp.max(s, axis=-1, keepdims=True)
        p = jnp.exp(s - m)
        denom = jnp.sum(p, axis=-1, keepdims=True)
        r = jnp.dot(p.astype(BF16), v2[:, ls], preferred_element_type=F32) / denom
        outs.append(jnp.where(first, r[:QBLK], r[QBLK:]))
        lse = m + jnp.log(denom)
        lses += [lse[:QBLK], lse[QBLK:]]
    return jnp.concatenate(outs, axis=-1).astype(BF16), jnp.concatenate(lses, axis=-1)


def _attn_kernel(q_ref, kp_ref, kc_ref, vp_ref, vc_ref, o_ref, lse_ref, *, steps, sub_blocks):
    j = pl.program_id(1)
    qi = lax.broadcasted_iota(jnp.int32, (2 * QBLK, 2 * QBLK), 0) % QBLK
    kj = lax.broadcasted_iota(jnp.int32, (2 * QBLK, 2 * QBLK), 1)
    dist = qi + QBLK - kj
    band = (dist >= 0) & (dist <= steps)
    for b in range(sub_blocks):
        rows = slice(b * QBLK, (b + 1) * QBLK)
        q = q_ref[rows, :] * (HEAD_DIM ** -0.5)
        if b == 0:
            k_prev, v_prev = kp_ref[...], vp_ref[...]
            allowed = band & ((kj >= QBLK) | (j > 0))
        else:
            before = slice((b - 1) * QBLK, b * QBLK)
            k_prev, v_prev = kc_ref[before, :], vc_ref[before, :]
            allowed = band
        k2 = jnp.concatenate([k_prev, kc_ref[rows, :]], axis=0)
        v2 = jnp.concatenate([v_prev, vc_ref[rows, :]], axis=0)
        o, lse = _attn_block(q, k2, v2, allowed)
        o_ref[rows, :] = o
        lse_ref[rows, :] = lse


def _attn(q, k, v, steps, sub_blocks=ATTN_SUB_BLOCKS):
    classes, n, width = q.shape
    sub_blocks = min(sub_blocks, n // QBLK)
    rows = sub_blocks * QBLK
    cur = pl.BlockSpec((None, rows, width), lambda c, j: (c, j, 0))
    prev = pl.BlockSpec((None, QBLK, width), lambda c, j: (c, jnp.maximum(j * sub_blocks - 1, 0), 0))
    return pl.pallas_call(
        functools.partial(_attn_kernel, steps=steps, sub_blocks=sub_blocks),
        grid=(classes, n // rows),
        in_specs=[cur, prev, cur, prev, cur],
        out_specs=[cur, pl.BlockSpec((None, rows, HEADS), lambda c, j: (c, j, 0))],
        out_shape=[jax.ShapeDtypeStruct((classes, n, width), BF16),
                   jax.ShapeDtypeStruct((classes, n, HEADS), F32)],
        compiler_params=_cparams(("arbitrary", "arbitrary")),
        name="attn",
    )(q, k, k, v, v)


def _layer_norm(z, g, b):
    mu = jnp.mean(z, axis=-1, keepdims=True)
    zc = z - mu
    var = jnp.mean(zc * zc, axis=-1, keepdims=True)
    return zc * lax.rsqrt(var + LN_EPS) * g + b


def _row_words(d):
    return d // (2 * LANES)


def _store_rows(ref, y):
    rows, d = y.shape
    sub = _row_words(d)
    for s in range(sub):
        lo = y[:, (2 * s) * LANES:(2 * s + 1) * LANES].astype(BF16).astype(F32)
        hi = y[:, (2 * s + 1) * LANES:(2 * s + 2) * LANES].astype(BF16).astype(F32)
        word = (pltpu.bitcast(lo, jnp.uint32) >> 16) | (pltpu.bitcast(hi, jnp.uint32) & HI_HALF)
        ref[pl.ds(s, rows, stride=sub), :] = word


def _load_rows(ref, rows, sub):
    parts = []
    for s in range(sub):
        word = ref[pl.ds(s, rows, stride=sub), :]
        parts.append(pltpu.bitcast(word << 16, F32))
        parts.append(pltpu.bitcast(word & HI_HALF, F32))
    return jnp.concatenate(parts, axis=-1)


def _expand_heads(w, e_ref):
    hi = w.astype(BF16)
    lo = (w - hi.astype(F32)).astype(BF16)
    return jnp.dot(jnp.concatenate([hi, lo], axis=-1), e_ref[...], preferred_element_type=F32)


def _merge_kernel(x_ref, yssm_ref, o0_ref, o1_ref, o2_ref, lse_ref, gs_ref, ga_ref,
                  wbs_ref, wba_ref, wout_ref, e_ref, g_ref, b_ref,
                  rwh_ref, rwl_ref, rbias_ref, eg_ref, tri_ref,
                  o_ref, orow_ref, eidx_ref, gate_ref, rank_ref, cnt_ref, carry_ref,
                  *, alpha, tiles_per_part):
    lse = lse_ref[...]
    l0, l1, l2 = lse[:, 0:HEADS], lse[:, HEADS:2 * HEADS], lse[:, 2 * HEADS:3 * HEADS]
    m = jnp.maximum(jnp.maximum(l0, l1), l2)
    e0, e1, e2 = jnp.exp(l0 - m), jnp.exp(l1 - m), jnp.exp(l2 - m)
    den = e0 + e1 + e2
    y_attn = (_expand_heads(e0 / den, e_ref) * o0_ref[...].astype(F32)
              + _expand_heads(e1 / den, e_ref) * o1_ref[...].astype(F32)
              + _expand_heads(e2 / den, e_ref) * o2_ref[...].astype(F32))
    ya = jnp.dot(y_attn.astype(BF16), wba_ref[...], preferred_element_type=F32)
    ys = jnp.dot(yssm_ref[...], wbs_ref[...], preferred_element_type=F32)
    merged = (jax.nn.sigmoid(gs_ref[...].astype(F32)) * ys
              + jax.nn.sigmoid(ga_ref[...].astype(F32)) * ya)
    mix = jnp.dot(merged.astype(BF16), wout_ref[...], preferred_element_type=F32)
    y = _layer_norm(alpha * x_ref[...] + mix, g_ref[...], b_ref[...])
    o_ref[...] = y
    _store_rows(orow_ref, y)

    @pl.when(pl.program_id(0) % tiles_per_part == 0)
    def _():
        carry_ref[...] = jnp.zeros_like(carry_ref)

    eidx, gate, rank, carry = _route_tile(y, rwh_ref, rwl_ref, rbias_ref, eg_ref, tri_ref, carry_ref[...])
    carry_ref[...] = carry
    cnt_ref[...] = carry
    eidx_ref[...] = eidx
    gate_ref[...] = gate
    rank_ref[...] = rank


def _merge(x, y_ssm, o0, o1, o2, lse, proj, w_br_ssm, w_br_attn, w_out, expand, ln_g, ln_b,
           rw_hi, rw_lo, r_bias, eg, tri, alpha, tm=MERGE_TM):
    n, d = x.shape
    tm = min(tm, n)
    tiles_per_part = n // MOE_PARTS // tm
    gate_blk = proj.shape[1] // d
    row = lambda w: pl.BlockSpec((tm, w), lambda i: (i, 0))
    col = pl.BlockSpec((TOP_K, tm), lambda i: (0, i))
    const = lambda a: pl.BlockSpec(a.shape, lambda i: (0,) * a.ndim)
    return pl.pallas_call(
        functools.partial(_merge_kernel, alpha=alpha, tiles_per_part=tiles_per_part),
        grid=(n // tm,),
        in_specs=[row(d), row(SSM_WIDTH), row(ATTN_WIDTH), row(ATTN_WIDTH), row(ATTN_WIDTH),
                  row(lse.shape[1]),
                  pl.BlockSpec((tm, d), lambda i: (i, gate_blk - 2)),
                  pl.BlockSpec((tm, d), lambda i: (i, gate_blk - 1)),
                  const(w_br_ssm), const(w_br_attn), const(w_out), const(expand),
                  const(ln_g), const(ln_b),
                  const(rw_hi), const(rw_lo), const(r_bias), const(eg), const(tri)],
        out_specs=[row(d), pl.BlockSpec((tm * _row_words(d), LANES), lambda i: (i, 0)),
                   col, col, col,
                   pl.BlockSpec((None, N_EXPERTS, LANES), lambda i: (i // tiles_per_part, 0, 0))],
        out_shape=[jax.ShapeDtypeStruct((n, d), F32),
                   jax.ShapeDtypeStruct((n * _row_words(d), LANES), jnp.uint32),
                   jax.ShapeDtypeStruct((TOP_K, n), jnp.int32),
                   jax.ShapeDtypeStruct((TOP_K, n), F32),
                   jax.ShapeDtypeStruct((TOP_K, n), jnp.int32),
                   jax.ShapeDtypeStruct((MOE_PARTS, N_EXPERTS, LANES), F32)],
        scratch_shapes=[pltpu.VMEM((N_EXPERTS, LANES), F32)],
        compiler_params=_cparams(("arbitrary",)),
        name="merge",
    )(x, y_ssm, o0, o1, o2, lse, proj, proj, w_br_ssm, w_br_attn, w_out, expand, ln_g, ln_b,
      rw_hi, rw_lo, r_bias, eg, tri)


def _first_argmax(v, iota, size, axis):
    m = jnp.max(v, axis=axis, keepdims=True)
    idx = jnp.min(jnp.where(v == m, iota, size), axis=axis, keepdims=True)
    return m, idx


def _route_tile(x, rwh_ref, rwl_ref, bias_ref, eg_ref, tri_ref, carry):
    tm = x.shape[0]
    xh = x.astype(BF16)
    xl = (x - xh.astype(F32)).astype(BF16)
    nt = (((1,), (1,)), ((), ()))
    logits = (lax.dot_general(rwh_ref[...], xh, nt, preferred_element_type=F32)
              + lax.dot_general(rwh_ref[...], xl, nt, preferred_element_type=F32)
              + lax.dot_general(rwl_ref[...], xh, nt, preferred_element_type=F32))
    scores = jax.nn.sigmoid(logits)
    sel = scores + bias_ref[...]

    sel3 = sel.reshape(N_EXPERT_GROUPS, GROUP_SIZE, tm)
    iw = lax.broadcasted_iota(jnp.int32, sel3.shape, 1)
    m1, i1 = _first_argmax(sel3, iw, GROUP_SIZE, 1)
    m2 = jnp.max(jnp.where(iw == i1, NEG, sel3), axis=1, keepdims=True)
    gs = (m1 + m2).reshape(N_EXPERT_GROUPS, tm)

    ig = lax.broadcasted_iota(jnp.int32, gs.shape, 0)
    gmask = jnp.zeros(gs.shape, F32)
    for _ in range(TOPK_GROUPS):
        _, gi = _first_argmax(gs, ig, N_EXPERT_GROUPS, 0)
        hit = ig == gi
        gmask = jnp.where(hit, 1.0, gmask)
        gs = jnp.where(hit, NEG, gs)
    emask = jnp.dot(eg_ref[...], gmask.astype(BF16), preferred_element_type=F32) > 0.5

    masked = jnp.where(emask, sel, NEG)
    ie = lax.broadcasted_iota(jnp.int32, masked.shape, 0)
    chosen = jnp.zeros(masked.shape, F32)
    idxs, vals = [], []
    for _ in range(TOP_K):
        _, ei = _first_argmax(masked, ie, N_EXPERTS, 0)
        hit = ie == ei
        idxs.append(ei)
        vals.append(jnp.sum(jnp.where(hit, scores, 0.0), axis=0, keepdims=True))
        chosen = jnp.where(hit, 1.0, chosen)
        masked = jnp.where(hit, NEG, masked)
    total = vals[0]
    for v in vals[1:]:
        total = total + v

    prefix = jnp.dot(chosen.astype(BF16), tri_ref[...], preferred_element_type=F32)
    pos = carry[:, 0:1] + prefix - 1.0
    ranks = [jnp.sum(jnp.where(ie == ei, pos, 0.0), axis=0, keepdims=True) for ei in idxs]
    carry = carry + jnp.sum(chosen, axis=1, keepdims=True)
    eidx = jnp.concatenate(idxs, axis=0)
    gate = jnp.concatenate([v / total * ROUTED_SCALE for v in vals], axis=0)
    rank = jnp.concatenate(ranks, axis=0).astype(jnp.int32)
    return eidx, gate, rank, carry


def _sc_gather_rows(src, idx, sub, chunk=SC_CHUNK):
    s = src.shape[0] // sub
    m = idx.shape[0]
    info = plsc.get_sparse_core_info()
    n_workers = info.num_cores * info.num_subcores
    per_worker = m // n_workers
    n_chunks = per_worker // chunk
    assert n_chunks * chunk * n_workers == m
    mesh = plsc.VectorSubcoreMesh(core_axis_name="c", subcore_axis_name="s")

    @functools.partial(
        pl.kernel, mesh=mesh,
        out_type=jax.ShapeDtypeStruct((m, sub, LANES), src.dtype),
        scratch_types=[pltpu.VMEM((n_chunks, chunk), jnp.int32),
                       pltpu.VMEM((chunk, sub, LANES), src.dtype),
                       pltpu.SemaphoreType.DMA])
    def gather(src_hbm, idx_hbm, out_hbm, idx_v, rows_v, sem):
        wid = lax.axis_index("s") * info.num_cores + lax.axis_index("c")
        pltpu.sync_copy(idx_hbm.at[wid], idx_v)

        @pl.loop(0, n_chunks)
        def _(j):
            pltpu.async_copy(src_hbm.at[idx_v.at[j]], rows_v, sem).wait()
            pltpu.sync_copy(rows_v, out_hbm.at[pl.ds(wid * per_worker + j * chunk, chunk)])

    out = gather(src.reshape(s, sub, LANES), idx.reshape(n_workers, n_chunks, chunk))
    return out.reshape(m * sub, LANES)


def _sc_scatter_rows(src, row0, dest, n_out, sub, chunk=SC_CHUNK):
    copies, n = dest.shape
    s = src.shape[0] // sub
    info = plsc.get_sparse_core_info()
    n_workers = info.num_cores * info.num_subcores
    per_worker = n // n_workers
    n_chunks = per_worker // chunk
    assert n_chunks * chunk * n_workers == n and copies * n == n_out
    mesh = plsc.VectorSubcoreMesh(core_axis_name="c", subcore_axis_name="s")
    idx = dest.reshape(copies, n_workers, n_chunks, chunk).transpose(1, 2, 0, 3)
    idx = idx.reshape(n_workers, n_chunks * copies, chunk)

    @functools.partial(
        pl.kernel, mesh=mesh,
        out_type=jax.ShapeDtypeStruct((n_out, sub, LANES), src.dtype),
        scratch_types=[pltpu.VMEM((n_chunks * copies, chunk), jnp.int32),
                       pltpu.VMEM((chunk, sub, LANES), src.dtype),
                       pltpu.SemaphoreType.DMA])
    def scatter(src_hbm, idx_hbm, out_hbm, idx_v, rows_v, sem):
        wid = lax.axis_index("s") * info.num_cores + lax.axis_index("c")
        pltpu.sync_copy(idx_hbm.at[wid], idx_v)

        @pl.loop(0, n_chunks)
        def _(j):
            pltpu.sync_copy(src_hbm.at[pl.ds(row0 + wid * per_worker + j * chunk, chunk)], rows_v)
            for c in range(copies):
                pltpu.async_copy(rows_v, out_hbm.at[idx_v.at[j * copies + c]], sem).wait()

    out = scatter(src.reshape(s, sub, LANES), idx)
    return out.reshape(n_out * sub, LANES)


def _gmm_kernel(be_ref, nu_ref, x_ref, wg_ref, wu_ref, wd_ref, o_ref, wgu_s, wd_s, *, hidden, blk, sub):
    b = pl.program_id(0)
    live = b < nu_ref[0]

    @pl.when(live & ((b == 0) | (be_ref[b] != be_ref[jnp.maximum(b - 1, 0)])))
    def _():
        wgu_s[:, :hidden] = wg_ref[...].astype(BF16)
        wgu_s[:, hidden:] = wu_ref[...].astype(BF16)
        wd_s[...] = wd_ref[...].astype(BF16)

    @pl.when(live)
    def _():
        x = _load_rows(x_ref, blk, sub).astype(BF16)
        gu = jnp.dot(x, wgu_s[...], preferred_element_type=F32)
        g, u = gu[:, :hidden], gu[:, hidden:]
        h = (g * jax.nn.sigmoid(g) * u).astype(BF16)
        _store_rows(o_ref, jnp.dot(h, wd_s[...], preferred_element_type=F32))


def _gmm(xs, w_gate, w_up, w_down, layer, blk_e, n_used, blk=MOE_BLK):
    hidden, d = w_down.shape[2:]
    sub = _row_words(d)
    n_rows = xs.shape[0] // sub
    row = lambda b, be, nu: (jnp.minimum(b, nu[0] - 1), 0)
    expert = lambda b, be, nu: (layer, be[b], 0, 0)
    return pl.pallas_call(
        functools.partial(_gmm_kernel, hidden=hidden, blk=blk, sub=sub),
        grid_spec=pltpu.PrefetchScalarGridSpec(
            num_scalar_prefetch=2,
            grid=(n_rows // blk,),
            in_specs=[pl.BlockSpec((blk * sub, LANES), row),
                      pl.BlockSpec((None, None, d, hidden), expert),
                      pl.BlockSpec((None, None, d, hidden), expert),
                      pl.BlockSpec((None, None, hidden, d), expert)],
            out_specs=pl.BlockSpec((blk * sub, LANES), row),
            scratch_shapes=[pltpu.VMEM((d, 2 * hidden), BF16), pltpu.VMEM((hidden, d), BF16)]),
        out_shape=jax.ShapeDtypeStruct((n_rows * sub, LANES), jnp.uint32),
        compiler_params=_cparams(("arbitrary",)),
        name="expert_gmm",
    )(blk_e, n_used, xs, w_gate, w_up, w_down)


def _combine_kernel(x_ref, yg_ref, gate_ref, wgu_ref, wd_ref, g_ref, b_ref, *rest, alpha, hidden, final_batch):
    x = x_ref[...]
    gu = jnp.dot(x.astype(BF16), wgu_ref[...], preferred_element_type=F32)
    g, u = gu[:, :hidden], gu[:, hidden:]
    h = (g * jax.nn.sigmoid(g) * u).astype(BF16)
    acc = jnp.dot(h, wd_ref[...], preferred_element_type=F32)
    gates = gate_ref[...]
    tm, d = x.shape
    for k in range(TOP_K):
        acc = acc + gates[:, k:k + 1] * _load_rows(yg_ref.at[k], tm, _row_words(d))
    y = _layer_norm(alpha * x + acc, g_ref[...], b_ref[...])
    if final_batch:
        rest[-1][...] = pltpu.einshape("tbd->btd", y.reshape(tm // final_batch, final_batch, d))
    else:
        o_ref, ob_ref = rest[-2:]
        o_ref[...] = y
        ob_ref[...] = y.astype(BF16)


def _combine(x, row0, yg, gates_t, w_gu, w_down, ln_g, ln_b, alpha, prev=None, final_batch=0, tm=COMBINE_TM):
    n, d = x.shape
    n_part = gates_t.shape[0]
    tm = min(tm, n_part)
    blk0 = row0 // tm
    sub = _row_words(d)
    hidden = w_down.shape[0]
    const = lambda a: pl.BlockSpec(a.shape, lambda i: (0,) * a.ndim)
    rows = pl.BlockSpec((tm, d), lambda i: (i + blk0, 0))
    in_specs = [rows,
                pl.BlockSpec((TOP_K, tm * sub, LANES), lambda i: (0, i, 0)),
                pl.BlockSpec((tm, TOP_K), lambda i: (i, 0)),
                const(w_gu), const(w_down), const(ln_g), const(ln_b)]
    args = [x, yg, gates_t, w_gu, w_down, ln_g, ln_b]
    if final_batch:
        out_specs = [pl.BlockSpec((final_batch, tm // final_batch, d), lambda i: (0, i + blk0, 0))]
        out_shape = [jax.ShapeDtypeStruct((final_batch, n // final_batch, d), F32)]
    else:
        out_specs = [rows, rows]
        out_shape = [jax.ShapeDtypeStruct((n, d), F32), jax.ShapeDtypeStruct((n, d), BF16)]
    aliases = {}
    if prev is not None:
        aliases = {len(args) + o: o for o in range(len(out_shape))}
        in_specs += [pl.BlockSpec(memory_space=pl.ANY)] * len(out_shape)
        args += list(prev)
    return pl.pallas_call(
        functools.partial(_combine_kernel, alpha=alpha, hidden=hidden, final_batch=final_batch),
        grid=(n_part // tm,),
        in_specs=in_specs,
        out_specs=out_specs,
        out_shape=out_shape,
        input_output_aliases=aliases,
        compiler_params=_cparams(("arbitrary",)),
        name="combine",
    )(*args)


def _split_w_in(w_in, b_in, d):
    qkv0 = SSM_WIDTH
    qkv_w = len(ATTN_PATTERNS) * ATTN_WIDTH
    gates0 = qkv0 + 3 * qkv_w
    cols = lambda a, c0, w: a[:, c0:c0 + w]
    pick = lambda a: jnp.concatenate([cols(a, gates0, 2 * d), cols(a, 0, SSM_WIDTH)], axis=1)
    main = (pick(w_in).astype(BF16), pick(b_in))
    groups = []
    for g in range(len(ATTN_PATTERNS)):
        sel = lambda a: jnp.concatenate([cols(a, qkv0 + s * qkv_w + g * ATTN_WIDTH, ATTN_WIDTH)
                                         for s in range(3)], axis=1)
        groups.append((sel(w_in).astype(BF16), sel(b_in)))
    return main, groups


def _token_mixer(x, xb, batch, w_in, b_in, s5p, w_glu, w_br_ssm, w_br_attn, w_out, expand, ln_g, ln_b,
                 router, alpha):
    n, d = x.shape
    seq = n // batch
    (w_main, b_main), groups = _split_w_in(w_in, b_in, d)
    proj = _inproj(xb, w_main, b_main)
    y_ssm = _s5(proj, *s5p, w_glu, batch)

    outs, lses = [], []
    for (window, dil), (w_g, b_g) in zip(ATTN_PATTERNS, groups):
        assert window // dil == QBLK and (seq // dil) % QBLK == 0
        q, k, v = _inproj_qkv(xb, w_g, b_g, dil * batch)
        o_g, lse_g = _attn(q, k, v, window // dil)
        outs.append(jnp.swapaxes(o_g, 0, 1).reshape(n, ATTN_WIDTH))
        lses.append(jnp.swapaxes(lse_g, 0, 1).reshape(n, HEADS))
    lse = jnp.concatenate(lses, axis=-1)
    return _merge(x, y_ssm, outs[0], outs[1], outs[2], lse, proj, w_br_ssm, w_br_attn, w_out,
                  expand, ln_g, ln_b, *router, alpha)


def _moe_plan(eidx, gates, rank, cnt):
    n = eidx.shape[1]
    counts = cnt[:, 0].astype(jnp.int32)
    nk = n * TOP_K
    n_pad = N_EXPERTS * MOE_BLK
    n_rows = nk + n_pad
    n_blocks = n_rows // MOE_BLK
    assert n_pad % n == 0
    pad_counts = (counts + MOE_BLK - 1) // MOE_BLK * MOE_BLK
    pad_end = jnp.cumsum(pad_counts)
    pad_start = pad_end - pad_counts
    onehot = eidx[:, :, None] == jnp.arange(N_EXPERTS, dtype=jnp.int32)
    dest = jnp.sum(jnp.where(onehot, pad_start, 0), axis=-1) + rank
    seg_start = jnp.concatenate([pad_start + counts, pad_end[-1:]])
    seg_len = jnp.concatenate([pad_counts - counts, n_rows - pad_end[-1:]])
    seg_end = jnp.cumsum(seg_len)
    q = jnp.arange(n_pad, dtype=jnp.int32)[:, None]
    seg = jnp.sum((q >= seg_end[None, :]).astype(jnp.int32), axis=1, keepdims=True)
    seg_hot = seg == jnp.arange(N_EXPERTS + 1, dtype=jnp.int32)[None, :]
    pad_dest = jnp.sum(jnp.where(seg_hot, seg_start - (seg_end - seg_len), 0), axis=1) + q[:, 0]
    dest_all = jnp.concatenate([dest, pad_dest.reshape(n_pad // n, n)], axis=0).astype(jnp.int32)
    blk_first = jnp.arange(n_blocks, dtype=jnp.int32)[:, None] * MOE_BLK
    blk_e = jnp.minimum(jnp.sum((blk_first >= pad_end[None, :]).astype(jnp.int32), axis=1), N_EXPERTS - 1)
    n_used = (pad_end[-1:] // MOE_BLK).astype(jnp.int32)
    return dest_all, dest.reshape(nk), gates.T, blk_e, n_used


def _moe(x, x_rows, routing, final_batch, w_gate, w_up, w_down, layer, sh_gu, sh_down, ln_g, ln_b, alpha):
    n, d = x.shape
    sub = _row_words(d)
    part = n // MOE_PARTS
    eidx, gates, rank, cnt = routing
    cut = lambda a, i: a[:, i * part:(i + 1) * part]
    plans = [_moe_plan(cut(eidx, i), cut(gates, i), cut(rank, i), cnt[i]) for i in range(MOE_PARTS)]
    out = None
    for i, (dest_all, dest, gates_t, blk_e, n_used) in enumerate(plans):
        xs = _sc_scatter_rows(x_rows, i * part, dest_all, dest_all.size, sub)
        ys = _gmm(xs, w_gate, w_up, w_down, layer, blk_e, n_used)
        yg = _sc_gather_rows(ys, dest, sub).reshape(TOP_K, part * sub, LANES)
        out = _combine(x, i * part, yg, gates_t, sh_gu, sh_down, ln_g, ln_b, alpha, out, final_batch)
    return out


def kernel(x, w_in, b_in, ssm_lam_re, ssm_lam_im, ssm_log_dt, ssm_b_re, ssm_b_im, ssm_c_re, ssm_c_im, ssm_d, w_glu, w_br_ssm, w_br_attn, w_out, ln1_g, ln1_b, router_w, router_bias, exp_w_gate, exp_w_up, exp_w_down, sh_w_gate, sh_w_up, sh_w_down, ln2_g, ln2_b):
    batch, seq, d = x.shape
    depth = w_in.shape[0]
    assert batch == SUBLANES
    alpha = (2 * depth) ** 0.25
    n = batch * seq
    xt = jnp.swapaxes(x, 0, 1).reshape(n, d)
    xb = xt

    expand = jnp.tile(jnp.repeat(jnp.eye(HEADS, dtype=BF16), HEAD_DIM, axis=1), (2, 1))
    eg = jnp.repeat(jnp.eye(N_EXPERT_GROUPS, dtype=BF16), GROUP_SIZE, axis=0)
    tm = min(MERGE_TM, n)
    tri = (jnp.arange(tm)[:, None] <= jnp.arange(tm)[None, :]).astype(BF16)
    row = lambda a: a.astype(F32).reshape(1, -1)

    for l in range(depth):
        s5p = _s5_params(ssm_lam_re[l], ssm_lam_im[l], ssm_log_dt[l], ssm_b_re[l], ssm_b_im[l],
                         ssm_c_re[l], ssm_c_im[l], ssm_d[l])
        rwt = router_w[l].astype(F32).T
        rw_hi = rwt.astype(BF16)
        rw_lo = (rwt - rw_hi.astype(F32)).astype(BF16)
        router = (rw_hi, rw_lo, router_bias[l].astype(F32).reshape(-1, 1), eg, tri)
        xt, xt_rows, *routing = _token_mixer(xt, xb, batch, w_in[l], row(b_in[l]), s5p,
                                             w_glu[l].astype(BF16), w_br_ssm[l].astype(BF16),
                                             w_br_attn[l].astype(BF16), w_out[l].astype(BF16),
                                             expand, row(ln1_g[l]), row(ln1_b[l]), router, alpha)
        sh_gu = jnp.concatenate([sh_w_gate[l], sh_w_up[l]], axis=-1).astype(BF16)
        last = l == depth - 1
        out = _moe(xt, xt_rows, routing, batch if last else 0,
                   exp_w_gate, exp_w_up, exp_w_down, l, sh_gu, sh_w_down[l].astype(BF16),
                   row(ln2_g[l]), row(ln2_b[l]), alpha)
        if not last:
            xt, xb = out
    return out[0]
```

```python
import functools
import math

import jax
import jax.numpy as jnp
import numpy as np
from jax import lax
from jax.experimental import pallas as pl
from jax.experimental.pallas import tpu as pltpu
from jax.experimental.pallas import tpu_sc as plsc

F32 = jnp.float32
BF16 = jnp.bfloat16

SSM_GROUP = 16
SSM_GROUPS = 32
SSM_WIDTH = SSM_GROUP * SSM_GROUPS
SSM_STATE = 64
HEAD_DIM = 64
HEADS = 8
ATTN_PATTERNS = ((128, 1), (512, 4), (2048, 16))
ATTN_WIDTH = HEADS * HEAD_DIM
QBLK = 128
N_EXPERTS = 64
TOP_K = 8
N_EXPERT_GROUPS = 8
GROUP_SIZE = N_EXPERTS // N_EXPERT_GROUPS
TOPK_GROUPS = 4
ROUTED_SCALE = 2.5
LN_EPS = 1e-5
NEG = -1e30

LANES = 128
SUBLANES = 8
VMEM_LIMIT = 56 * 1024 * 1024

INPROJ_TM = 2048
INPROJ_TN = 1280
QKV_TM = 2048
S5_T = 128
S5_KB = 4
ATTN_SUB_BLOCKS = 4
MERGE_TM = 512
ROUTE_TM = 1024
MOE_BLK = 1024
MOE_PARTS = 2
SC_CHUNK = 128
HI_HALF = np.uint32(0xFFFF0000)
COMBINE_TM = 512


def _cparams(sem):
    return pltpu.CompilerParams(dimension_semantics=sem, vmem_limit_bytes=VMEM_LIMIT)


def _inproj_kernel(x_ref, w_ref, b_ref, o_ref):
    acc = jnp.dot(x_ref[...].astype(BF16), w_ref[...], preferred_element_type=F32)
    o_ref[...] = (acc + b_ref[...]).astype(BF16)


def _inproj(xb, w, b, tm=INPROJ_TM, tn=INPROJ_TN):
    n, d = xb.shape
    width = w.shape[1]
    tm = min(tm, n)
    return pl.pallas_call(
        _inproj_kernel,
        grid=(n // tm, width // tn),
        in_specs=[pl.BlockSpec((tm, d), lambda i, j: (i, 0)),
                  pl.BlockSpec((d, tn), lambda i, j: (0, j)),
                  pl.BlockSpec((1, tn), lambda i, j: (0, j))],
        out_specs=pl.BlockSpec((tm, tn), lambda i, j: (i, j)),
        out_shape=jax.ShapeDtypeStruct((n, width), BF16),
        compiler_params=_cparams(("arbitrary", "arbitrary")),
        name="inproj",
    )(xb, w, b)


def _inproj_qkv_kernel(x_ref, w_ref, b_ref, q_ref, k_ref, v_ref, *, classes):
    x = x_ref[...].astype(BF16)
    rows = x.shape[0] // classes
    for part, o_ref in enumerate((q_ref, k_ref, v_ref)):
        cols = slice(part * ATTN_WIDTH, (part + 1) * ATTN_WIDTH)
        y = jnp.dot(x, w_ref[:, cols], preferred_element_type=F32) + b_ref[:, cols]
        y = y.astype(BF16).reshape(rows, classes, ATTN_WIDTH)
        o_ref[...] = pltpu.einshape("jcf->cjf", y)


def _inproj_qkv(xb, w, b, classes, tm=QKV_TM):
    n, d = xb.shape
    tm = min(tm, n)
    rows = tm // classes
    out = pl.BlockSpec((classes, rows, ATTN_WIDTH), lambda i: (0, i, 0))
    shape = jax.ShapeDtypeStruct((classes, n // classes, ATTN_WIDTH), BF16)
    return pl.pallas_call(
        functools.partial(_inproj_qkv_kernel, classes=classes),
        grid=(n // tm,),
        in_specs=[pl.BlockSpec((tm, d), lambda i: (i, 0)),
                  pl.BlockSpec(w.shape, lambda i: (0, 0)),
                  pl.BlockSpec(b.shape, lambda i: (0, 0))],
        out_specs=[out, out, out],
        out_shape=[shape, shape, shape],
        compiler_params=_cparams(("arbitrary",)),
        name="inproj_qkv",
    )(xb, w, b)


def _gelu_tanh(x):
    c = math.sqrt(2.0 / math.pi)
    return 0.5 * x * (1.0 + jnp.tanh(c * (x + 0.044715 * (x * x * x))))


def _s5_kernel(u_ref, bm_ref, cre_ref, cim_ref, are_ref, aim_ref, d_ref, wglu_ref,
               o_ref, sre_ref, sim_ref, st_re_ref, st_im_ref, y_ref, *, steps, batch):
    kw = SSM_WIDTH // S5_KB
    sw = SSM_GROUPS * SSM_STATE // S5_KB

    @pl.when(pl.program_id(0) == 0)
    def _():
        st_re_ref[...] = jnp.zeros_like(st_re_ref)
        st_im_ref[...] = jnp.zeros_like(st_im_ref)

    for k in range(S5_KB):
        ls = slice(k * sw, (k + 1) * sw)
        bu = jnp.dot(u_ref[:, k * kw:(k + 1) * kw], bm_ref[k], preferred_element_type=F32)
        sre_ref[:, ls] = bu[:, :sw]
        sim_ref[:, ls] = bu[:, sw:]

        ar = jnp.broadcast_to(are_ref[:, ls], (batch, sw))
        ai = jnp.broadcast_to(aim_ref[:, ls], (batch, sw))

        def step(t, carry):
            sr, si = carry
            rows = pl.ds(pl.multiple_of(t * batch, batch), batch)
            nr = ar * sr - ai * si + sre_ref[rows, ls]
            ni = ar * si + ai * sr + sim_ref[rows, ls]
            sre_ref[rows, ls] = nr
            sim_ref[rows, ls] = ni
            return nr, ni

        sr, si = lax.fori_loop(0, steps, step, (st_re_ref[:, ls], st_im_ref[:, ls]), unroll=True)
        st_re_ref[:, ls] = sr
        st_im_ref[:, ls] = si

        y_ref[:, k * kw:(k + 1) * kw] = (
            jnp.dot(sre_ref[:, ls].astype(BF16), cre_ref[k], preferred_element_type=F32)
            + jnp.dot(sim_ref[:, ls].astype(BF16), cim_ref[k], preferred_element_type=F32))

    y = y_ref[...] + d_ref[...] * u_ref[...].astype(F32)
    y = _gelu_tanh(y)
    z = jnp.dot(y.astype(BF16), wglu_ref[...], preferred_element_type=F32)
    o_ref[...] = (y * jax.nn.sigmoid(z)).astype(BF16)


def _s5(proj, bm, cre, cim, a_re, a_im, d_skip, w_glu, batch, steps=S5_T):
    n = proj.shape[0]
    u_blk = proj.shape[1] // SSM_WIDTH - 1
    seq = n // batch
    steps = min(steps, seq)
    rows = steps * batch
    nstate = SSM_GROUPS * SSM_STATE
    const = lambda shape: pl.BlockSpec(shape, lambda i: (0,) * len(shape))
    return pl.pallas_call(
        functools.partial(_s5_kernel, steps=steps, batch=batch),
        grid=(seq // steps,),
        in_specs=[pl.BlockSpec((rows, SSM_WIDTH), lambda i: (i, u_blk)),
                  const(bm.shape), const(cre.shape), const(cim.shape),
                  const(a_re.shape), const(a_im.shape), const(d_skip.shape), const(w_glu.shape)],
        out_specs=pl.BlockSpec((rows, SSM_WIDTH), lambda i: (i, 0)),
        out_shape=jax.ShapeDtypeStruct((n, SSM_WIDTH), BF16),
        scratch_shapes=[pltpu.VMEM((rows, nstate), F32), pltpu.VMEM((rows, nstate), F32),
                        pltpu.VMEM((batch, nstate), F32), pltpu.VMEM((batch, nstate), F32),
                        pltpu.VMEM((rows, SSM_WIDTH), F32)],
        compiler_params=_cparams(("arbitrary",)),
        name="s5",
    )(proj, bm, cre, cim, a_re, a_im, d_skip, w_glu)


def _s5_params(lam_re, lam_im, log_dt, b_re, b_im, c_re, c_im, d_skip):
    lam = lax.complex(lam_re.astype(F32), lam_im.astype(F32))
    dt = jnp.exp(log_dt.astype(F32))[:, None]
    lam_bar = jnp.exp(lam * dt)
    b_bar = ((lam_bar - 1.0) / lam)[:, :, None] * lax.complex(b_re.astype(F32), b_im.astype(F32))
    gl = SSM_GROUPS // S5_KB
    eye = jnp.eye(gl, dtype=F32)

    def in_blocks(b):
        b = b.reshape(S5_KB, gl, SSM_STATE, SSM_GROUP)
        m = jnp.einsum('kgpc,gh->kgchp', b, eye)
        return m.reshape(S5_KB, gl * SSM_GROUP, gl * SSM_STATE)

    def out_blocks(c):
        c = c.reshape(S5_KB, gl, SSM_GROUP, SSM_STATE)
        m = jnp.einsum('kgcp,gh->kgphc', c, eye)
        return m.reshape(S5_KB, gl * SSM_STATE, gl * SSM_GROUP)

    bm = jnp.concatenate([in_blocks(jnp.real(b_bar)), in_blocks(jnp.imag(b_bar))], axis=-1).astype(BF16)
    cre = out_blocks(c_re.astype(F32)).astype(BF16)
    cim = out_blocks(-c_im.astype(F32)).astype(BF16)
    a_re = jnp.real(lam_bar).reshape(1, -1)
    a_im = jnp.imag(lam_bar).reshape(1, -1)
    return bm, cre, cim, a_re, a_im, d_skip.astype(F32).reshape(1, -1)


def _attn_block(q, k2, v2, allowed):
    first = lax.broadcasted_iota(jnp.int32, (QBLK, LANES), 1) < HEAD_DIM
    zero = jnp.zeros((), BF16)
    outs, lses = [], []
    for slab in range(ATTN_WIDTH // LANES):
        ls = slice(slab * LANES, (slab + 1) * LANES)
        qs = jnp.concatenate([jnp.where(first, q[:, ls], zero), jnp.where(first, zero, q[:, ls])], axis=0)
        s = lax.dot_general(qs, k2[:, ls], (((1,), (1,)), ((), ())), preferred_element_type=F32)
        s = jnp.where(allowed, s, NEG)
        m = jnp.max(s, axis=-1, keepdims=True)
        p = jnp.exp(s - m)
        denom = jnp.sum(p, axis=-1, keepdims=True)
        r = jnp.dot(p.astype(BF16), v2[:, ls], preferred_element_type=F32) / denom
        outs.append(jnp.where(first, r[:QBLK], r[QBLK:]))
        lse = m + jnp.log(denom)
        lses += [lse[:QBLK], lse[QBLK:]]
    return jnp.concatenate(outs, axis=-1).astype(BF16), jnp.concatenate(lses, axis=-1)


def _attn_kernel(q_ref, kp_ref, kc_ref, vp_ref, vc_ref, o_ref, lse_ref, *, steps, sub_blocks):
    j = pl.program_id(1)
    qi = lax.broadcasted_iota(jnp.int32, (2 * QBLK, 2 * QBLK), 0) % QBLK
    kj = lax.broadcasted_iota(jnp.int32, (2 * QBLK, 2 * QBLK), 1)
    dist = qi + QBLK - kj
    band = (dist >= 0) & (dist <= steps)
    for b in range(sub_blocks):
        rows = slice(b * QBLK, (b + 1) * QBLK)
        q = q_ref[rows, :] * (HEAD_DIM ** -0.5)
        if b == 0:
            k_prev, v_prev = kp_ref[...], vp_ref[...]
            allowed = band & ((kj >= QBLK) | (j > 0))
        else:
            before = slice((b - 1) * QBLK, b * QBLK)
            k_prev, v_prev = kc_ref[before, :], vc_ref[before, :]
            allowed = band
        k2 = jnp.concatenate([k_prev, kc_ref[rows, :]], axis=0)
        v2 = jnp.concatenate([v_prev, vc_ref[rows, :]], axis=0)
        o, lse = _attn_block(q, k2, v2, allowed)
        o_ref[rows, :] = o
        lse_ref[rows, :] = lse


def _attn(q, k, v, steps, sub_blocks=ATTN_SUB_BLOCKS):
    classes, n, width = q.shape
    sub_blocks = min(sub_blocks, n // QBLK)
    rows = sub_blocks * QBLK
    cur = pl.BlockSpec((None, rows, width), lambda c, j: (c, j, 0))
    prev = pl.BlockSpec((None, QBLK, width), lambda c, j: (c, jnp.maximum(j * sub_blocks - 1, 0), 0))
    return pl.pallas_call(
        functools.partial(_attn_kernel, steps=steps, sub_blocks=sub_blocks),
        grid=(classes, n // rows),
        in_specs=[cur, prev, cur, prev, cur],
        out_specs=[cur, pl.BlockSpec((None, rows, HEADS), lambda c, j: (c, j, 0))],
        out_shape=[jax.ShapeDtypeStruct((classes, n, width), BF16),
                   jax.ShapeDtypeStruct((classes, n, HEADS), F32)],
        compiler_params=_cparams(("arbitrary", "arbitrary")),
        name="attn",
    )(q, k, k, v, v)


def _layer_norm(z, g, b):
    mu = jnp.mean(z, axis=-1, keepdims=True)
    zc = z - mu
    var = jnp.mean(zc * zc, axis=-1, keepdims=True)
    return zc * lax.rsqrt(var + LN_EPS) * g + b


def _row_words(d):
    return d // (2 * LANES)


def _store_rows(ref, y):
    rows, d = y.shape
    sub = _row_words(d)
    for s in range(sub):
        lo = y[:, (2 * s) * LANES:(2 * s + 1) * LANES].astype(BF16).astype(F32)
        hi = y[:, (2 * s + 1) * LANES:(2 * s + 2) * LANES].astype(BF16).astype(F32)
        word = (pltpu.bitcast(lo, jnp.uint32) >> 16) | (pltpu.bitcast(hi, jnp.uint32) & HI_HALF)
        ref[pl.ds(s, rows, stride=sub), :] = word


def _load_rows(ref, rows, sub):
    parts = []
    for s in range(sub):
        word = ref[pl.ds(s, rows, stride=sub), :]
        parts.append(pltpu.bitcast(word << 16, F32))
        parts.append(pltpu.bitcast(word & HI_HALF, F32))
    return jnp.concatenate(parts, axis=-1)


def _expand_heads(w, e_ref):
    hi = w.astype(BF16)
    lo = (w - hi.astype(F32)).astype(BF16)
    return jnp.dot(jnp.concatenate([hi, lo], axis=-1), e_ref[...], preferred_element_type=F32)


def _merge_kernel(x_ref, yssm_ref, o0_ref, o1_ref, o2_ref, lse_ref, gs_ref, ga_ref,
                  wbs_ref, wba_ref, wout_ref, e_ref, g_ref, b_ref, o_ref, orow_ref, *, alpha):
    lse = lse_ref[...]
    l0, l1, l2 = lse[:, 0:HEADS], lse[:, HEADS:2 * HEADS], lse[:, 2 * HEADS:3 * HEADS]
    m = jnp.maximum(jnp.maximum(l0, l1), l2)
    e0, e1, e2 = jnp.exp(l0 - m), jnp.exp(l1 - m), jnp.exp(l2 - m)
    den = e0 + e1 + e2
    y_attn = (_expand_heads(e0 / den, e_ref) * o0_ref[...].astype(F32)
              + _expand_heads(e1 / den, e_ref) * o1_ref[...].astype(F32)
              + _expand_heads(e2 / den, e_ref) * o2_ref[...].astype(F32))
    ya = jnp.dot(y_attn.astype(BF16), wba_ref[...], preferred_element_type=F32)
    ys = jnp.dot(yssm_ref[...], wbs_ref[...], preferred_element_type=F32)
    merged = (jax.nn.sigmoid(gs_ref[...].astype(F32)) * ys
              + jax.nn.sigmoid(ga_ref[...].astype(F32)) * ya)
    mix = jnp.dot(merged.astype(BF16), wout_ref[...], preferred_element_type=F32)
    y = _layer_norm(alpha * x_ref[...] + mix, g_ref[...], b_ref[...])
    o_ref[...] = y
    _store_rows(orow_ref, y)


def _merge(x, y_ssm, o0, o1, o2, lse, proj, w_br_ssm, w_br_attn, w_out, expand, ln_g, ln_b,
           alpha, tm=MERGE_TM):
    n, d = x.shape
    tm = min(tm, n)
    gate_blk = proj.shape[1] // d
    row = lambda w: pl.BlockSpec((tm, w), lambda i: (i, 0))
    const = lambda a: pl.BlockSpec(a.shape, lambda i: (0,) * a.ndim)
    return pl.pallas_call(
        functools.partial(_merge_kernel, alpha=alpha),
        grid=(n // tm,),
        in_specs=[row(d), row(SSM_WIDTH), row(ATTN_WIDTH), row(ATTN_WIDTH), row(ATTN_WIDTH),
                  row(lse.shape[1]),
                  pl.BlockSpec((tm, d), lambda i: (i, gate_blk - 2)),
                  pl.BlockSpec((tm, d), lambda i: (i, gate_blk - 1)),
                  const(w_br_ssm), const(w_br_attn), const(w_out), const(expand),
                  const(ln_g), const(ln_b)],
        out_specs=[row(d), pl.BlockSpec((tm * _row_words(d), LANES), lambda i: (i, 0))],
        out_shape=[jax.ShapeDtypeStruct((n, d), F32),
                   jax.ShapeDtypeStruct((n * _row_words(d), LANES), jnp.uint32)],
        compiler_params=_cparams(("arbitrary",)),
        name="merge",
    )(x, y_ssm, o0, o1, o2, lse, proj, proj, w_br_ssm, w_br_attn, w_out, expand, ln_g, ln_b)


def _first_argmax(v, iota, size, axis):
    m = jnp.max(v, axis=axis, keepdims=True)
    idx = jnp.min(jnp.where(v == m, iota, size), axis=axis, keepdims=True)
    return m, idx


def _route_kernel(x_ref, rwh_ref, rwl_ref, bias_ref, eg_ref, tri_ref,
                  eidx_ref, gate_ref, rank_ref, cnt_ref, carry_ref):
    @pl.when(pl.program_id(0) == 0)
    def _():
        carry_ref[...] = jnp.zeros_like(carry_ref)

    x = x_ref[...]
    tm = x.shape[0]
    xh = x.astype(BF16)
    xl = (x - xh.astype(F32)).astype(BF16)
    nt = (((1,), (1,)), ((), ()))
    logits = (lax.dot_general(rwh_ref[...], xh, nt, preferred_element_type=F32)
              + lax.dot_general(rwh_ref[...], xl, nt, preferred_element_type=F32)
              + lax.dot_general(rwl_ref[...], xh, nt, preferred_element_type=F32))
    scores = jax.nn.sigmoid(logits)
    sel = scores + bias_ref[...]

    sel3 = sel.reshape(N_EXPERT_GROUPS, GROUP_SIZE, tm)
    iw = lax.broadcasted_iota(jnp.int32, sel3.shape, 1)
    m1, i1 = _first_argmax(sel3, iw, GROUP_SIZE, 1)
    m2 = jnp.max(jnp.where(iw == i1, NEG, sel3), axis=1, keepdims=True)
    gs = (m1 + m2).reshape(N_EXPERT_GROUPS, tm)

    ig = lax.broadcasted_iota(jnp.int32, gs.shape, 0)
    gmask = jnp.zeros(gs.shape, F32)
    for _ in range(TOPK_GROUPS):
        _, gi = _first_argmax(gs, ig, N_EXPERT_GROUPS, 0)
        hit = ig == gi
        gmask = jnp.where(hit, 1.0, gmask)
        gs = jnp.where(hit, NEG, gs)
    emask = jnp.dot(eg_ref[...], gmask.astype(BF16), preferred_element_type=F32) > 0.5

    masked = jnp.where(emask, sel, NEG)
    ie = lax.broadcasted_iota(jnp.int32, masked.shape, 0)
    chosen = jnp.zeros(masked.shape, F32)
    idxs, vals = [], []
    for _ in range(TOP_K):
        _, ei = _first_argmax(masked, ie, N_EXPERTS, 0)
        hit = ie == ei
        idxs.append(ei)
        vals.append(jnp.sum(jnp.where(hit, scores, 0.0), axis=0, keepdims=True))
        chosen = jnp.where(hit, 1.0, chosen)
        masked = jnp.where(hit, NEG, masked)
    total = vals[0]
    for v in vals[1:]:
        total = total + v

    prefix = jnp.dot(chosen.astype(BF16), tri_ref[...], preferred_element_type=F32)
    pos = carry_ref[:, 0:1] + prefix - 1.0
    ranks = [jnp.sum(jnp.where(ie == ei, pos, 0.0), axis=0, keepdims=True) for ei in idxs]
    carry = carry_ref[...] + jnp.sum(chosen, axis=1, keepdims=True)
    carry_ref[...] = carry
    cnt_ref[...] = carry

    eidx_ref[...] = jnp.concatenate(idxs, axis=0)
    gate_ref[...] = jnp.concatenate([v / total * ROUTED_SCALE for v in vals], axis=0)
    rank_ref[...] = jnp.concatenate(ranks, axis=0).astype(jnp.int32)


def _route(x, row0, n, rw_hi, rw_lo, bias, eg, tri, tm=ROUTE_TM):
    d = x.shape[1]
    tm = min(tm, n)
    blk0 = row0 // tm
    const = lambda a: pl.BlockSpec(a.shape, lambda i: (0,) * a.ndim)
    col = pl.BlockSpec((TOP_K, tm), lambda i: (0, i))
    return pl.pallas_call(
        _route_kernel,
        grid=(n // tm,),
        in_specs=[pl.BlockSpec((tm, d), lambda i: (i + blk0, 0)),
                  const(rw_hi), const(rw_lo), const(bias), const(eg), const(tri)],
        out_specs=[col, col, col, pl.BlockSpec((N_EXPERTS, LANES), lambda i: (0, 0))],
        out_shape=[jax.ShapeDtypeStruct((TOP_K, n), jnp.int32),
                   jax.ShapeDtypeStruct((TOP_K, n), F32),
                   jax.ShapeDtypeStruct((TOP_K, n), jnp.int32),
                   jax.ShapeDtypeStruct((N_EXPERTS, LANES), F32)],
        scratch_shapes=[pltpu.VMEM((N_EXPERTS, LANES), F32)],
        compiler_params=_cparams(("arbitrary",)),
        name="route",
    )(x, rw_hi, rw_lo, bias, eg, tri)


def _sc_gather_rows(src, idx, sub, chunk=SC_CHUNK):
    s = src.shape[0] // sub
    m = idx.shape[0]
    info = plsc.get_sparse_core_info()
    n_workers = info.num_cores * info.num_subcores
    per_worker = m // n_workers
    n_chunks = per_worker // chunk
    assert n_chunks * chunk * n_workers == m
    mesh = plsc.VectorSubcoreMesh(core_axis_name="c", subcore_axis_name="s")

    @functools.partial(
        pl.kernel, mesh=mesh,
        out_type=jax.ShapeDtypeStruct((m, sub, LANES), src.dtype),
        scratch_types=[pltpu.VMEM((n_chunks, chunk), jnp.int32),
                       pltpu.VMEM((chunk, sub, LANES), src.dtype),
                       pltpu.SemaphoreType.DMA])
    def gather(src_hbm, idx_hbm, out_hbm, idx_v, rows_v, sem):
        wid = lax.axis_index("s") * info.num_cores + lax.axis_index("c")
        pltpu.sync_copy(idx_hbm.at[wid], idx_v)

        @pl.loop(0, n_chunks)
        def _(j):
            pltpu.async_copy(src_hbm.at[idx_v.at[j]], rows_v, sem).wait()
            pltpu.sync_copy(rows_v, out_hbm.at[pl.ds(wid * per_worker + j * chunk, chunk)])

    out = gather(src.reshape(s, sub, LANES), idx.reshape(n_workers, n_chunks, chunk))
    return out.reshape(m * sub, LANES)


def _sc_scatter_rows(src, row0, dest, n_out, sub, chunk=SC_CHUNK):
    copies, n = dest.shape
    s = src.shape[0] // sub
    info = plsc.get_sparse_core_info()
    n_workers = info.num_cores * info.num_subcores
    per_worker = n // n_workers
    n_chunks = per_worker // chunk
    assert n_chunks * chunk * n_workers == n and copies * n == n_out
    mesh = plsc.VectorSubcoreMesh(core_axis_name="c", subcore_axis_name="s")
    idx = dest.reshape(copies, n_workers, n_chunks, chunk).transpose(1, 2, 0, 3)
    idx = idx.reshape(n_workers, n_chunks * copies, chunk)

    @functools.partial(
        pl.kernel, mesh=mesh,
        out_type=jax.ShapeDtypeStruct((n_out, sub, LANES), src.dtype),
        scratch_types=[pltpu.VMEM((n_chunks * copies, chunk), jnp.int32),
                       pltpu.VMEM((chunk, sub, LANES), src.dtype),
                       pltpu.SemaphoreType.DMA])
    def scatter(src_hbm, idx_hbm, out_hbm, idx_v, rows_v, sem):
        wid = lax.axis_index("s") * info.num_cores + lax.axis_index("c")
        pltpu.sync_copy(idx_hbm.at[wid], idx_v)

        @pl.loop(0, n_chunks)
        def _(j):
            pltpu.sync_copy(src_hbm.at[pl.ds(row0 + wid * per_worker + j * chunk, chunk)], rows_v)
            for c in range(copies):
                pltpu.async_copy(rows_v, out_hbm.at[idx_v.at[j * copies + c]], sem).wait()

    out = scatter(src.reshape(s, sub, LANES), idx)
    return out.reshape(n_out * sub, LANES)


def _gmm_kernel(be_ref, nu_ref, x_ref, wg_ref, wu_ref, wd_ref, o_ref, wgu_s, wd_s, *, hidden, blk, sub):
    b = pl.program_id(0)
    live = b < nu_ref[0]

    @pl.when(live & ((b == 0) | (be_ref[b] != be_ref[jnp.maximum(b - 1, 0)])))
    def _():
        wgu_s[:, :hidden] = wg_ref[...].astype(BF16)
        wgu_s[:, hidden:] = wu_ref[...].astype(BF16)
        wd_s[...] = wd_ref[...].astype(BF16)

    @pl.when(live)
    def _():
        x = _load_rows(x_ref, blk, sub).astype(BF16)
        gu = jnp.dot(x, wgu_s[...], preferred_element_type=F32)
        g, u = gu[:, :hidden], gu[:, hidden:]
        h = (g * jax.nn.sigmoid(g) * u).astype(BF16)
        _store_rows(o_ref, jnp.dot(h, wd_s[...], preferred_element_type=F32))


def _gmm(xs, w_gate, w_up, w_down, layer, blk_e, n_used, blk=MOE_BLK):
    hidden, d = w_down.shape[2:]
    sub = _row_words(d)
    n_rows = xs.shape[0] // sub
    row = lambda b, be, nu: (jnp.minimum(b, nu[0] - 1), 0)
    expert = lambda b, be, nu: (layer, be[b], 0, 0)
    return pl.pallas_call(
        functools.partial(_gmm_kernel, hidden=hidden, blk=blk, sub=sub),
        grid_spec=pltpu.PrefetchScalarGridSpec(
            num_scalar_prefetch=2,
            grid=(n_rows // blk,),
            in_specs=[pl.BlockSpec((blk * sub, LANES), row),
                      pl.BlockSpec((None, None, d, hidden), expert),
                      pl.BlockSpec((None, None, d, hidden), expert),
                      pl.BlockSpec((None, None, hidden, d), expert)],
            out_specs=pl.BlockSpec((blk * sub, LANES), row),
            scratch_shapes=[pltpu.VMEM((d, 2 * hidden), BF16), pltpu.VMEM((hidden, d), BF16)]),
        out_shape=jax.ShapeDtypeStruct((n_rows * sub, LANES), jnp.uint32),
        compiler_params=_cparams(("arbitrary",)),
        name="expert_gmm",
    )(blk_e, n_used, xs, w_gate, w_up, w_down)


def _combine_kernel(x_ref, yg_ref, gate_ref, wgu_ref, wd_ref, g_ref, b_ref, *rest, alpha, hidden, final_batch):
    x = x_ref[...]
    gu = jnp.dot(x.astype(BF16), wgu_ref[...], preferred_element_type=F32)
    g, u = gu[:, :hidden], gu[:, hidden:]
    h = (g * jax.nn.sigmoid(g) * u).astype(BF16)
    acc = jnp.dot(h, wd_ref[...], preferred_element_type=F32)
    gates = gate_ref[...]
    tm, d = x.shape
    for k in range(TOP_K):
        acc = acc + gates[:, k:k + 1] * _load_rows(yg_ref.at[k], tm, _row_words(d))
    y = _layer_norm(alpha * x + acc, g_ref[...], b_ref[...])
    if final_batch:
        rest[-1][...] = pltpu.einshape("tbd->btd", y.reshape(tm // final_batch, final_batch, d))
    else:
        o_ref, ob_ref = rest[-2:]
        o_ref[...] = y
        ob_ref[...] = y.astype(BF16)


def _combine(x, row0, yg, gates_t, w_gu, w_down, ln_g, ln_b, alpha, prev=None, final_batch=0, tm=COMBINE_TM):
    n, d = x.shape
    n_part = gates_t.shape[0]
    tm = min(tm, n_part)
    blk0 = row0 // tm
    sub = _row_words(d)
    hidden = w_down.shape[0]
    const = lambda a: pl.BlockSpec(a.shape, lambda i: (0,) * a.ndim)
    rows = pl.BlockSpec((tm, d), lambda i: (i + blk0, 0))
    in_specs = [rows,
                pl.BlockSpec((TOP_K, tm * sub, LANES), lambda i: (0, i, 0)),
                pl.BlockSpec((tm, TOP_K), lambda i: (i, 0)),
                const(w_gu), const(w_down), const(ln_g), const(ln_b)]
    args = [x, yg, gates_t, w_gu, w_down, ln_g, ln_b]
    if final_batch:
        out_specs = [pl.BlockSpec((final_batch, tm // final_batch, d), lambda i: (0, i + blk0, 0))]
        out_shape = [jax.ShapeDtypeStruct((final_batch, n // final_batch, d), F32)]
    else:
        out_specs = [rows, rows]
        out_shape = [jax.ShapeDtypeStruct((n, d), F32), jax.ShapeDtypeStruct((n, d), BF16)]
    aliases = {}
    if prev is not None:
        aliases = {len(args) + o: o for o in range(len(out_shape))}
        in_specs += [pl.BlockSpec(memory_space=pl.ANY)] * len(out_shape)
        args += list(prev)
    return pl.pallas_call(
        functools.partial(_combine_kernel, alpha=alpha, hidden=hidden, final_batch=final_batch),
        grid=(n_part // tm,),
        in_specs=in_specs,
        out_specs=out_specs,
        out_shape=out_shape,
        input_output_aliases=aliases,
        compiler_params=_cparams(("arbitrary",)),
        name="combine",
    )(*args)


def _split_w_in(w_in, b_in, d):
    qkv0 = SSM_WIDTH
    qkv_w = len(ATTN_PATTERNS) * ATTN_WIDTH
    gates0 = qkv0 + 3 * qkv_w
    cols = lambda a, c0, w: a[:, c0:c0 + w]
    pick = lambda a: jnp.concatenate([cols(a, gates0, 2 * d), cols(a, 0, SSM_WIDTH)], axis=1)
    main = (pick(w_in).astype(BF16), pick(b_in))
    groups = []
    for g in range(len(ATTN_PATTERNS)):
        sel = lambda a: jnp.concatenate([cols(a, qkv0 + s * qkv_w + g * ATTN_WIDTH, ATTN_WIDTH)
                                         for s in range(3)], axis=1)
        groups.append((sel(w_in).astype(BF16), sel(b_in)))
    return main, groups


def _token_mixer(x, xb, batch, w_in, b_in, s5p, w_glu, w_br_ssm, w_br_attn, w_out, expand, ln_g, ln_b, alpha):
    n, d = x.shape
    seq = n // batch
    (w_main, b_main), groups = _split_w_in(w_in, b_in, d)
    proj = _inproj(xb, w_main, b_main)
    y_ssm = _s5(proj, *s5p, w_glu, batch)

    outs, lses = [], []
    for (window, dil), (w_g, b_g) in zip(ATTN_PATTERNS, groups):
        assert window // dil == QBLK and (seq // dil) % QBLK == 0
        q, k, v = _inproj_qkv(xb, w_g, b_g, dil * batch)
        o_g, lse_g = _attn(q, k, v, window // dil)
        outs.append(jnp.swapaxes(o_g, 0, 1).reshape(n, ATTN_WIDTH))
        lses.append(jnp.swapaxes(lse_g, 0, 1).reshape(n, HEADS))
    lse = jnp.concatenate(lses, axis=-1)
    return _merge(x, y_ssm, outs[0], outs[1], outs[2], lse, proj, w_br_ssm, w_br_attn, w_out,
                  expand, ln_g, ln_b, alpha)


def _moe_plan(x, row0, n, rw_hi, rw_lo, bias, eg, tri):
    eidx, gates, rank, cnt = _route(x, row0, n, rw_hi, rw_lo, bias, eg, tri)
    counts = cnt[:, 0].astype(jnp.int32)
    nk = n * TOP_K
    n_pad = N_EXPERTS * MOE_BLK
    n_rows = nk + n_pad
    n_blocks = n_rows // MOE_BLK
    assert n_pad % n == 0
    pad_counts = (counts + MOE_BLK - 1) // MOE_BLK * MOE_BLK
    pad_end = jnp.cumsum(pad_counts)
    pad_start = pad_end - pad_counts
    onehot = eidx[:, :, None] == jnp.arange(N_EXPERTS, dtype=jnp.int32)
    dest = jnp.sum(jnp.where(onehot, pad_start, 0), axis=-1) + rank
    seg_start = jnp.concatenate([pad_start + counts, pad_end[-1:]])
    seg_len = jnp.concatenate([pad_counts - counts, n_rows - pad_end[-1:]])
    seg_end = jnp.cumsum(seg_len)
    q = jnp.arange(n_pad, dtype=jnp.int32)[:, None]
    seg = jnp.sum((q >= seg_end[None, :]).astype(jnp.int32), axis=1, keepdims=True)
    seg_hot = seg == jnp.arange(N_EXPERTS + 1, dtype=jnp.int32)[None, :]
    pad_dest = jnp.sum(jnp.where(seg_hot, seg_start - (seg_end - seg_len), 0), axis=1) + q[:, 0]
    dest_all = jnp.concatenate([dest, pad_dest.reshape(n_pad // n, n)], axis=0).astype(jnp.int32)
    blk_first = jnp.arange(n_blocks, dtype=jnp.int32)[:, None] * MOE_BLK
    blk_e = jnp.minimum(jnp.sum((blk_first >= pad_end[None, :]).astype(jnp.int32), axis=1), N_EXPERTS - 1)
    n_used = (pad_end[-1:] // MOE_BLK).astype(jnp.int32)
    return dest_all, dest.reshape(nk), gates.T, blk_e, n_used


def _moe(x, x_rows, final_batch, rw_hi, rw_lo, bias, eg, tri, w_gate, w_up, w_down, layer,
         sh_gu, sh_down, ln_g, ln_b, alpha):
    n, d = x.shape
    sub = _row_words(d)
    part = n // MOE_PARTS
    plans = [_moe_plan(x, i * part, part, rw_hi, rw_lo, bias, eg, tri) for i in range(MOE_PARTS)]
    out = None
    for i, (dest_all, dest, gates_t, blk_e, n_used) in enumerate(plans):
        xs = _sc_scatter_rows(x_rows, i * part, dest_all, dest_all.size, sub)
        ys = _gmm(xs, w_gate, w_up, w_down, layer, blk_e, n_used)
        yg = _sc_gather_rows(ys, dest, sub).reshape(TOP_K, part * sub, LANES)
        out = _combine(x, i * part, yg, gates_t, sh_gu, sh_down, ln_g, ln_b, alpha, out, final_batch)
    return out


def kernel(x, w_in, b_in, ssm_lam_re, ssm_lam_im, ssm_log_dt, ssm_b_re, ssm_b_im, ssm_c_re, ssm_c_im, ssm_d, w_glu, w_br_ssm, w_br_attn, w_out, ln1_g, ln1_b, router_w, router_bias, exp_w_gate, exp_w_up, exp_w_down, sh_w_gate, sh_w_up, sh_w_down, ln2_g, ln2_b):
    batch, seq, d = x.shape
    depth = w_in.shape[0]
    assert batch == SUBLANES
    alpha = (2 * depth) ** 0.25
    n = batch * seq
    xt = jnp.swapaxes(x, 0, 1).reshape(n, d)
    xb = xt

    expand = jnp.tile(jnp.repeat(jnp.eye(HEADS, dtype=BF16), HEAD_DIM, axis=1), (2, 1))
    eg = jnp.repeat(jnp.eye(N_EXPERT_GROUPS, dtype=BF16), GROUP_SIZE, axis=0)
    tm = min(ROUTE_TM, n)
    tri = (jnp.arange(tm)[:, None] <= jnp.arange(tm)[None, :]).astype(BF16)
    row = lambda a: a.astype(F32).reshape(1, -1)

    for l in range(depth):
        s5p = _s5_params(ssm_lam_re[l], ssm_lam_im[l], ssm_log_dt[l], ssm_b_re[l], ssm_b_im[l],
                         ssm_c_re[l], ssm_c_im[l], ssm_d[l])
        xt, xt_rows = _token_mixer(xt, xb, batch, w_in[l], row(b_in[l]), s5p,
                                   w_glu[l].astype(BF16), w_br_ssm[l].astype(BF16),
                                   w_br_attn[l].astype(BF16), w_out[l].astype(BF16),
                                   expand, row(ln1_g[l]), row(ln1_b[l]), alpha)
        rwt = router_w[l].astype(F32).T
        rw_hi = rwt.astype(BF16)
        rw_lo = (rwt - rw_hi.astype(F32)).astype(BF16)
        sh_gu = jnp.concatenate([sh_w_gate[l], sh_w_up[l]], axis=-1).astype(BF16)
        last = l == depth - 1
        out = _moe(xt, xt_rows, batch if last else 0, rw_hi, rw_lo,
                   router_bias[l].astype(F32).reshape(-1, 1), eg, tri,
                   exp_w_gate, exp_w_up, exp_w_down, l, sh_gu, sh_w_down[l].astype(BF16),
                   row(ln2_g[l]), row(ln2_b[l]), alpha)
        if not last:
            xt, xb = out
    return out[0]
```

```python
import functools
import math

import jax
import jax.numpy as jnp
import numpy as np
from jax import lax
from jax.experimental import pallas as pl
from jax.experimental.pallas import tpu as pltpu
from jax.experimental.pallas import tpu_sc as plsc

F32 = jnp.float32
BF16 = jnp.bfloat16

SSM_GROUP = 16
SSM_GROUPS = 32
SSM_WIDTH = SSM_GROUP * SSM_GROUPS
SSM_STATE = 64
HEAD_DIM = 64
HEADS = 8
ATTN_PATTERNS = ((128, 1), (512, 4), (2048, 16))
ATTN_WIDTH = HEADS * HEAD_DIM
QBLK = 128
N_EXPERTS = 64
TOP_K = 8
N_EXPERT_GROUPS = 8
GROUP_SIZE = N_EXPERTS // N_EXPERT_GROUPS
TOPK_GROUPS = 4
ROUTED_SCALE = 2.5
LN_EPS = 1e-5
NEG = -1e30

LANES = 128
SUBLANES = 8
VMEM_LIMIT = 56 * 1024 * 1024

INPROJ_TM = 2048
INPROJ_TN = 1280
QKV_TM = 2048
S5_T = 128
S5_KB = 4
ATTN_SUB_BLOCKS = 4
MERGE_TM = 512
ROUTE_TM = 1024
MOE_BLK = 1024
MOE_PARTS = 2
SC_CHUNK = 128
HI_HALF = np.uint32(0xFFFF0000)
COMBINE_TM = 512


def _cparams(sem):
    return pltpu.CompilerParams(dimension_semantics=sem, vmem_limit_bytes=VMEM_LIMIT)


def _inproj_kernel(x_ref, w_ref, b_ref, o_ref):
    acc = jnp.dot(x_ref[...].astype(BF16), w_ref[...], preferred_element_type=F32)
    o_ref[...] = (acc + b_ref[...]).astype(BF16)


def _inproj(xb, w, b, tm=INPROJ_TM, tn=INPROJ_TN):
    n, d = xb.shape
    width = w.shape[1]
    tm = min(tm, n)
    return pl.pallas_call(
        _inproj_kernel,
        grid=(n // tm, width // tn),
        in_specs=[pl.BlockSpec((tm, d), lambda i, j: (i, 0)),
                  pl.BlockSpec((d, tn), lambda i, j: (0, j)),
                  pl.BlockSpec((1, tn), lambda i, j: (0, j))],
        out_specs=pl.BlockSpec((tm, tn), lambda i, j: (i, j)),
        out_shape=jax.ShapeDtypeStruct((n, width), BF16),
        compiler_params=_cparams(("arbitrary", "arbitrary")),
        name="inproj",
    )(xb, w, b)


def _inproj_qkv_kernel(x_ref, w_ref, b_ref, q_ref, k_ref, v_ref, *, classes):
    x = x_ref[...].astype(BF16)
    rows = x.shape[0] // classes
    for part, o_ref in enumerate((q_ref, k_ref, v_ref)):
        cols = slice(part * ATTN_WIDTH, (part + 1) * ATTN_WIDTH)
        y = jnp.dot(x, w_ref[:, cols], preferred_element_type=F32) + b_ref[:, cols]
        y = y.astype(BF16).reshape(rows, classes, ATTN_WIDTH)
        o_ref[...] = pltpu.einshape("jcf->cjf", y)


def _inproj_qkv(xb, w, b, classes, tm=QKV_TM):
    n, d = xb.shape
    tm = min(tm, n)
    rows = tm // classes
    out = pl.BlockSpec((classes, rows, ATTN_WIDTH), lambda i: (0, i, 0))
    shape = jax.ShapeDtypeStruct((classes, n // classes, ATTN_WIDTH), BF16)
    return pl.pallas_call(
        functools.partial(_inproj_qkv_kernel, classes=classes),
        grid=(n // tm,),
        in_specs=[pl.BlockSpec((tm, d), lambda i: (i, 0)),
                  pl.BlockSpec(w.shape, lambda i: (0, 0)),
                  pl.BlockSpec(b.shape, lambda i: (0, 0))],
        out_specs=[out, out, out],
        out_shape=[shape, shape, shape],
        compiler_params=_cparams(("arbitrary",)),
        name="inproj_qkv",
    )(xb, w, b)


def _gelu_tanh(x):
    c = math.sqrt(2.0 / math.pi)
    return 0.5 * x * (1.0 + jnp.tanh(c * (x + 0.044715 * (x * x * x))))


def _s5_kernel(u_ref, bm_ref, cre_ref, cim_ref, are_ref, aim_ref, d_ref, wglu_ref,
               o_ref, sre_ref, sim_ref, st_re_ref, st_im_ref, y_ref, *, steps, batch):
    kw = SSM_WIDTH // S5_KB
    sw = SSM_GROUPS * SSM_STATE // S5_KB

    @pl.when(pl.program_id(0) == 0)
    def _():
        st_re_ref[...] = jnp.zeros_like(st_re_ref)
        st_im_ref[...] = jnp.zeros_like(st_im_ref)

    for k in range(S5_KB):
        ls = slice(k * sw, (k + 1) * sw)
        bu = jnp.dot(u_ref[:, k * kw:(k + 1) * kw], bm_ref[k], preferred_element_type=F32)
        sre_ref[:, ls] = bu[:, :sw]
        sim_ref[:, ls] = bu[:, sw:]

        ar = jnp.broadcast_to(are_ref[:, ls], (batch, sw))
        ai = jnp.broadcast_to(aim_ref[:, ls], (batch, sw))

        def step(t, carry):
            sr, si = carry
            rows = pl.ds(pl.multiple_of(t * batch, batch), batch)
            nr = ar * sr - ai * si + sre_ref[rows, ls]
            ni = ar * si + ai * sr + sim_ref[rows, ls]
            sre_ref[rows, ls] = nr
            sim_ref[rows, ls] = ni
            return nr, ni

        sr, si = lax.fori_loop(0, steps, step, (st_re_ref[:, ls], st_im_ref[:, ls]), unroll=True)
        st_re_ref[:, ls] = sr
        st_im_ref[:, ls] = si

        y_ref[:, k * kw:(k + 1) * kw] = (
            jnp.dot(sre_ref[:, ls].astype(BF16), cre_ref[k], preferred_element_type=F32)
            + jnp.dot(sim_ref[:, ls].astype(BF16), cim_ref[k], preferred_element_type=F32))

    y = y_ref[...] + d_ref[...] * u_ref[...].astype(F32)
    y = _gelu_tanh(y)
    z = jnp.dot(y.astype(BF16), wglu_ref[...], preferred_element_type=F32)
    o_ref[...] = (y * jax.nn.sigmoid(z)).astype(BF16)


def _s5(proj, bm, cre, cim, a_re, a_im, d_skip, w_glu, batch, steps=S5_T):
    n = proj.shape[0]
    u_blk = proj.shape[1] // SSM_WIDTH - 1
    seq = n // batch
    steps = min(steps, seq)
    rows = steps * batch
    nstate = SSM_GROUPS * SSM_STATE
    const = lambda shape: pl.BlockSpec(shape, lambda i: (0,) * len(shape))
    return pl.pallas_call(
        functools.partial(_s5_kernel, steps=steps, batch=batch),
        grid=(seq // steps,),
        in_specs=[pl.BlockSpec((rows, SSM_WIDTH), lambda i: (i, u_blk)),
                  const(bm.shape), const(cre.shape), const(cim.shape),
                  const(a_re.shape), const(a_im.shape), const(d_skip.shape), const(w_glu.shape)],
        out_specs=pl.BlockSpec((rows, SSM_WIDTH), lambda i: (i, 0)),
        out_shape=jax.ShapeDtypeStruct((n, SSM_WIDTH), BF16),
        scratch_shapes=[pltpu.VMEM((rows, nstate), F32), pltpu.VMEM((rows, nstate), F32),
                        pltpu.VMEM((batch, nstate), F32), pltpu.VMEM((batch, nstate), F32),
                        pltpu.VMEM((rows, SSM_WIDTH), F32)],
        compiler_params=_cparams(("arbitrary",)),
        name="s5",
    )(proj, bm, cre, cim, a_re, a_im, d_skip, w_glu)


def _s5_params(lam_re, lam_im, log_dt, b_re, b_im, c_re, c_im, d_skip):
    lam = lax.complex(lam_re.astype(F32), lam_im.astype(F32))
    dt = jnp.exp(log_dt.astype(F32))[:, None]
    lam_bar = jnp.exp(lam * dt)
    b_bar = ((lam_bar - 1.0) / lam)[:, :, None] * lax.complex(b_re.astype(F32), b_im.astype(F32))
    gl = SSM_GROUPS // S5_KB
    eye = jnp.eye(gl, dtype=F32)

    def in_blocks(b):
        b = b.reshape(S5_KB, gl, SSM_STATE, SSM_GROUP)
        m = jnp.einsum('kgpc,gh->kgchp', b, eye)
        return m.reshape(S5_KB, gl * SSM_GROUP, gl * SSM_STATE)

    def out_blocks(c):
        c = c.reshape(S5_KB, gl, SSM_GROUP, SSM_STATE)
        m = jnp.einsum('kgcp,gh->kgphc', c, eye)
        return m.reshape(S5_KB, gl * SSM_STATE, gl * SSM_GROUP)

    bm = jnp.concatenate([in_blocks(jnp.real(b_bar)), in_blocks(jnp.imag(b_bar))], axis=-1).astype(BF16)
    cre = out_blocks(c_re.astype(F32)).astype(BF16)
    cim = out_blocks(-c_im.astype(F32)).astype(BF16)
    a_re = jnp.real(lam_bar).reshape(1, -1)
    a_im = jnp.imag(lam_bar).reshape(1, -1)
    return bm, cre, cim, a_re, a_im, d_skip.astype(F32).reshape(1, -1)


def _attn_block(q, k2, v2, allowed):
    first = lax.broadcasted_iota(jnp.int32, (QBLK, LANES), 1) < HEAD_DIM
    zero = jnp.zeros((), BF16)
    outs, lses = [], []
    for slab in range(ATTN_WIDTH // LANES):
        ls = slice(slab * LANES, (slab + 1) * LANES)
        qs = jnp.concatenate([jnp.where(first, q[:, ls], zero), jnp.where(first, zero, q[:, ls])], axis=0)
        s = lax.dot_general(qs, k2[:, ls], (((1,), (1,)), ((), ())), preferred_element_type=F32)
        s = jnp.where(allowed, s, NEG)
        m = jnp.max(s, axis=-1, keepdims=True)
        p = jnp.exp(s - m)
        denom = jnp.sum(p, axis=-1, keepdims=True)
        r = jnp.dot(p.astype(BF16), v2[:, ls], preferred_element_type=F32) / denom
        outs.append(jnp.where(first, r[:QBLK], r[QBLK:]))
        lse = m + jnp.log(denom)
        lses += [lse[:QBLK], lse[QBLK:]]
    return jnp.concatenate(outs, axis=-1).astype(BF16), jnp.concatenate(lses, axis=-1)


def _attn_kernel(q_ref, kp_ref, kc_ref, vp_ref, vc_ref, o_ref, lse_ref, *, steps, sub_blocks, single_step):
    j = pl.program_id(1)
    qi = lax.broadcasted_iota(jnp.int32, (2 * QBLK, 2 * QBLK), 0) % QBLK
    kj = lax.broadcasted_iota(jnp.int32, (2 * QBLK, 2 * QBLK), 1)
    dist = qi + QBLK - kj
    band = (dist >= 0) & (dist <= steps)
    for b in range(sub_blocks):
        rows = slice(b * QBLK, (b + 1) * QBLK)
        q = q_ref[rows, :] * (HEAD_DIM ** -0.5)
        if b == 0 and single_step:
            k2, v2, allowed = kc_ref[rows, :], vc_ref[rows, :], band[:, QBLK:]
        else:
            if b == 0:
                k_prev, v_prev = kp_ref[...], vp_ref[...]
                allowed = band & ((kj >= QBLK) | (j > 0))
            else:
                before = slice((b - 1) * QBLK, b * QBLK)
                k_prev, v_prev = kc_ref[before, :], vc_ref[before, :]
                allowed = band
            k2 = jnp.concatenate([k_prev, kc_ref[rows, :]], axis=0)
            v2 = jnp.concatenate([v_prev, vc_ref[rows, :]], axis=0)
        o, lse = _attn_block(q, k2, v2, allowed)
        o_ref[rows, :] = o
        lse_ref[rows, :] = lse


def _attn(q, k, v, steps, sub_blocks=ATTN_SUB_BLOCKS):
    classes, n, width = q.shape
    sub_blocks = min(sub_blocks, n // QBLK)
    rows = sub_blocks * QBLK
    cur = pl.BlockSpec((None, rows, width), lambda c, j: (c, j, 0))
    prev = pl.BlockSpec((None, QBLK, width), lambda c, j: (c, jnp.maximum(j * sub_blocks - 1, 0), 0))
    return pl.pallas_call(
        functools.partial(_attn_kernel, steps=steps, sub_blocks=sub_blocks, single_step=n == rows),
        grid=(classes, n // rows),
        in_specs=[cur, prev, cur, prev, cur],
        out_specs=[cur, pl.BlockSpec((None, rows, HEADS), lambda c, j: (c, j, 0))],
        out_shape=[jax.ShapeDtypeStruct((classes, n, width), BF16),
                   jax.ShapeDtypeStruct((classes, n, HEADS), F32)],
        compiler_params=_cparams(("arbitrary", "arbitrary")),
        name="attn",
    )(q, k, k, v, v)


def _layer_norm(z, g, b):
    mu = jnp.mean(z, axis=-1, keepdims=True)
    zc = z - mu
    var = jnp.mean(zc * zc, axis=-1, keepdims=True)
    return zc * lax.rsqrt(var + LN_EPS) * g + b


def _row_words(d):
    return d // (2 * LANES)


def _store_rows(ref, y):
    rows, d = y.shape
    sub = _row_words(d)
    for s in range(sub):
        lo = y[:, (2 * s) * LANES:(2 * s + 1) * LANES].astype(BF16).astype(F32)
        hi = y[:, (2 * s + 1) * LANES:(2 * s + 2) * LANES].astype(BF16).astype(F32)
        word = (pltpu.bitcast(lo, jnp.uint32) >> 16) | (pltpu.bitcast(hi, jnp.uint32) & HI_HALF)
        ref[pl.ds(s, rows, stride=sub), :] = word


def _load_rows(ref, rows, sub):
    parts = []
    for s in range(sub):
        word = ref[pl.ds(s, rows, stride=sub), :]
        parts.append(pltpu.bitcast(word << 16, F32))
        parts.append(pltpu.bitcast(word & HI_HALF, F32))
    return jnp.concatenate(parts, axis=-1)


def _expand_heads(w, e_ref):
    hi = w.astype(BF16)
    lo = (w - hi.astype(F32)).astype(BF16)
    return jnp.dot(jnp.concatenate([hi, lo], axis=-1), e_ref[...], preferred_element_type=F32)


def _merge_kernel(x_ref, yssm_ref, o0_ref, o1_ref, o2_ref, lse_ref, gs_ref, ga_ref,
                  wbs_ref, wba_ref, wout_ref, e_ref, g_ref, b_ref, o_ref, orow_ref, *, alpha):
    lse = lse_ref[...]
    l0, l1, l2 = lse[:, 0:HEADS], lse[:, HEADS:2 * HEADS], lse[:, 2 * HEADS:3 * HEADS]
    m = jnp.maximum(jnp.maximum(l0, l1), l2)
    e0, e1, e2 = jnp.exp(l0 - m), jnp.exp(l1 - m), jnp.exp(l2 - m)
    den = e0 + e1 + e2
    y_attn = (_expand_heads(e0 / den, e_ref) * o0_ref[...].astype(F32)
              + _expand_heads(e1 / den, e_ref) * o1_ref[...].astype(F32)
              + _expand_heads(e2 / den, e_ref) * o2_ref[...].astype(F32))
    ya = jnp.dot(y_attn.astype(BF16), wba_ref[...], preferred_element_type=F32)
    ys = jnp.dot(yssm_ref[...], wbs_ref[...], preferred_element_type=F32)
    merged = (jax.nn.sigmoid(gs_ref[...].astype(F32)) * ys
              + jax.nn.sigmoid(ga_ref[...].astype(F32)) * ya)
    mix = jnp.dot(merged.astype(BF16), wout_ref[...], preferred_element_type=F32)
    y = _layer_norm(alpha * x_ref[...] + mix, g_ref[...], b_ref[...])
    o_ref[...] = y
    _store_rows(orow_ref, y)


def _merge(x, y_ssm, o0, o1, o2, lse, proj, w_br_ssm, w_br_attn, w_out, expand, ln_g, ln_b,
           alpha, tm=MERGE_TM):
    n, d = x.shape
    tm = min(tm, n)
    gate_blk = proj.shape[1] // d
    row = lambda w: pl.BlockSpec((tm, w), lambda i: (i, 0))
    const = lambda a: pl.BlockSpec(a.shape, lambda i: (0,) * a.ndim)
    return pl.pallas_call(
        functools.partial(_merge_kernel, alpha=alpha),
        grid=(n // tm,),
        in_specs=[row(d), row(SSM_WIDTH), row(ATTN_WIDTH), row(ATTN_WIDTH), row(ATTN_WIDTH),
                  row(lse.shape[1]),
                  pl.BlockSpec((tm, d), lambda i: (i, gate_blk - 2)),
                  pl.BlockSpec((tm, d), lambda i: (i, gate_blk - 1)),
                  const(w_br_ssm), const(w_br_attn), const(w_out), const(expand),
                  const(ln_g), const(ln_b)],
        out_specs=[row(d), pl.BlockSpec((tm * _row_words(d), LANES), lambda i: (i, 0))],
        out_shape=[jax.ShapeDtypeStruct((n, d), F32),
                   jax.ShapeDtypeStruct((n * _row_words(d), LANES), jnp.uint32)],
        compiler_params=_cparams(("arbitrary",)),
        name="merge",
    )(x, y_ssm, o0, o1, o2, lse, proj, proj, w_br_ssm, w_br_attn, w_out, expand, ln_g, ln_b)


def _first_argmax(v, iota, size, axis):
    m = jnp.max(v, axis=axis, keepdims=True)
    idx = jnp.min(jnp.where(v == m, iota, size), axis=axis, keepdims=True)
    return m, idx


def _route_kernel(x_ref, rwh_ref, rwl_ref, bias_ref, eg_ref, tri_ref,
                  eidx_ref, gate_ref, rank_ref, cnt_ref, carry_ref):
    @pl.when(pl.program_id(0) == 0)
    def _():
        carry_ref[...] = jnp.zeros_like(carry_ref)

    x = x_ref[...]
    tm = x.shape[0]
    xh = x.astype(BF16)
    xl = (x - xh.astype(F32)).astype(BF16)
    nt = (((1,), (1,)), ((), ()))
    logits = (lax.dot_general(rwh_ref[...], xh, nt, preferred_element_type=F32)
              + lax.dot_general(rwh_ref[...], xl, nt, preferred_element_type=F32)
              + lax.dot_general(rwl_ref[...], xh, nt, preferred_element_type=F32))
    scores = jax.nn.sigmoid(logits)
    sel = scores + bias_ref[...]

    sel3 = sel.reshape(N_EXPERT_GROUPS, GROUP_SIZE, tm)
    iw = lax.broadcasted_iota(jnp.int32, sel3.shape, 1)
    m1, i1 = _first_argmax(sel3, iw, GROUP_SIZE, 1)
    m2 = jnp.max(jnp.where(iw == i1, NEG, sel3), axis=1, keepdims=True)
    gs = (m1 + m2).reshape(N_EXPERT_GROUPS, tm)

    ig = lax.broadcasted_iota(jnp.int32, gs.shape, 0)
    gmask = jnp.zeros(gs.shape, F32)
    for _ in range(TOPK_GROUPS):
        _, gi = _first_argmax(gs, ig, N_EXPERT_GROUPS, 0)
        hit = ig == gi
        gmask = jnp.where(hit, 1.0, gmask)
        gs = jnp.where(hit, NEG, gs)
    emask = jnp.dot(eg_ref[...], gmask.astype(BF16), preferred_element_type=F32) > 0.5

    masked = jnp.where(emask, sel, NEG)
    ie = lax.broadcasted_iota(jnp.int32, masked.shape, 0)
    chosen = jnp.zeros(masked.shape, F32)
    idxs, vals = [], []
    for _ in range(TOP_K):
        _, ei = _first_argmax(masked, ie, N_EXPERTS, 0)
        hit = ie == ei
        idxs.append(ei)
        vals.append(jnp.sum(jnp.where(hit, scores, 0.0), axis=0, keepdims=True))
        chosen = jnp.where(hit, 1.0, chosen)
        masked = jnp.where(hit, NEG, masked)
    total = vals[0]
    for v in vals[1:]:
        total = total + v

    prefix = jnp.dot(chosen.astype(BF16), tri_ref[...], preferred_element_type=F32)
    pos = carry_ref[:, 0:1] + prefix - 1.0
    ranks = [jnp.sum(jnp.where(ie == ei, pos, 0.0), axis=0, keepdims=True) for ei in idxs]
    carry = carry_ref[...] + jnp.sum(chosen, axis=1, keepdims=True)
    carry_ref[...] = carry
    cnt_ref[...] = carry

    eidx_ref[...] = jnp.concatenate(idxs, axis=0)
    gate_ref[...] = jnp.concatenate([v / total * ROUTED_SCALE for v in vals], axis=0)
    rank_ref[...] = jnp.concatenate(ranks, axis=0).astype(jnp.int32)


def _route(x, row0, n, rw_hi, rw_lo, bias, eg, tri, tm=ROUTE_TM):
    d = x.shape[1]
    tm = min(tm, n)
    blk0 = row0 // tm
    const = lambda a: pl.BlockSpec(a.shape, lambda i: (0,) * a.ndim)
    col = pl.BlockSpec((TOP_K, tm), lambda i: (0, i))
    return pl.pallas_call(
        _route_kernel,
        grid=(n // tm,),
        in_specs=[pl.BlockSpec((tm, d), lambda i: (i + blk0, 0)),
                  const(rw_hi), const(rw_lo), const(bias), const(eg), const(tri)],
        out_specs=[col, col, col, pl.BlockSpec((N_EXPERTS, LANES), lambda i: (0, 0))],
        out_shape=[jax.ShapeDtypeStruct((TOP_K, n), jnp.int32),
                   jax.ShapeDtypeStruct((TOP_K, n), F32),
                   jax.ShapeDtypeStruct((TOP_K, n), jnp.int32),
                   jax.ShapeDtypeStruct((N_EXPERTS, LANES), F32)],
        scratch_shapes=[pltpu.VMEM((N_EXPERTS, LANES), F32)],
        compiler_params=_cparams(("arbitrary",)),
        name="route",
    )(x, rw_hi, rw_lo, bias, eg, tri)


def _sc_gather_rows(src, idx, sub, chunk=SC_CHUNK):
    s = src.shape[0] // sub
    m = idx.shape[0]
    info = plsc.get_sparse_core_info()
    n_workers = info.num_cores * info.num_subcores
    per_worker = m // n_workers
    n_chunks = per_worker // chunk
    assert n_chunks * chunk * n_workers == m
    mesh = plsc.VectorSubcoreMesh(core_axis_name="c", subcore_axis_name="s")

    @functools.partial(
        pl.kernel, mesh=mesh,
        out_type=jax.ShapeDtypeStruct((m, sub, LANES), src.dtype),
        scratch_types=[pltpu.VMEM((n_chunks, chunk), jnp.int32),
                       pltpu.VMEM((chunk, sub, LANES), src.dtype),
                       pltpu.SemaphoreType.DMA])
    def gather(src_hbm, idx_hbm, out_hbm, idx_v, rows_v, sem):
        wid = lax.axis_index("s") * info.num_cores + lax.axis_index("c")
        pltpu.sync_copy(idx_hbm.at[wid], idx_v)

        @pl.loop(0, n_chunks)
        def _(j):
            pltpu.async_copy(src_hbm.at[idx_v.at[j]], rows_v, sem).wait()
            pltpu.sync_copy(rows_v, out_hbm.at[pl.ds(wid * per_worker + j * chunk, chunk)])

    out = gather(src.reshape(s, sub, LANES), idx.reshape(n_workers, n_chunks, chunk))
    return out.reshape(m * sub, LANES)


def _sc_scatter_rows(src, row0, dest, n_out, sub, chunk=SC_CHUNK):
    copies, n = dest.shape
    s = src.shape[0] // sub
    info = plsc.get_sparse_core_info()
    n_workers = info.num_cores * info.num_subcores
    per_worker = n // n_workers
    n_chunks = per_worker // chunk
    assert n_chunks * chunk * n_workers == n and copies * n == n_out
    mesh = plsc.VectorSubcoreMesh(core_axis_name="c", subcore_axis_name="s")
    idx = dest.reshape(copies, n_workers, n_chunks, chunk).transpose(1, 2, 0, 3)
    idx = idx.reshape(n_workers, n_chunks * copies, chunk)

    @functools.partial(
        pl.kernel, mesh=mesh,
        out_type=jax.ShapeDtypeStruct((n_out, sub, LANES), src.dtype),
        scratch_types=[pltpu.VMEM((n_chunks * copies, chunk), jnp.int32),
                       pltpu.VMEM((chunk, sub, LANES), src.dtype),
                       pltpu.SemaphoreType.DMA])
    def scatter(src_hbm, idx_hbm, out_hbm, idx_v, rows_v, sem):
        wid = lax.axis_index("s") * info.num_cores + lax.axis_index("c")
        pltpu.sync_copy(idx_hbm.at[wid], idx_v)

        @pl.loop(0, n_chunks)
        def _(j):
            pltpu.sync_copy(src_hbm.at[pl.ds(row0 + wid * per_worker + j * chunk, chunk)], rows_v)
            for c in range(copies):
                pltpu.async_copy(rows_v, out_hbm.at[idx_v.at[j * copies + c]], sem).wait()

    out = scatter(src.reshape(s, sub, LANES), idx)
    return out.reshape(n_out * sub, LANES)


def _gmm_kernel(be_ref, nu_ref, x_ref, wg_ref, wu_ref, wd_ref, o_ref, wgu_s, wd_s, *, hidden, blk, sub):
    b = pl.program_id(0)
    live = b < nu_ref[0]

    @pl.when(live & ((b == 0) | (be_ref[b] != be_ref[jnp.maximum(b - 1, 0)])))
    def _():
        wgu_s[:, :hidden] = wg_ref[...].astype(BF16)
        wgu_s[:, hidden:] = wu_ref[...].astype(BF16)
        wd_s[...] = wd_ref[...].astype(BF16)

    @pl.when(live)
    def _():
        x = _load_rows(x_ref, blk, sub).astype(BF16)
        gu = jnp.dot(x, wgu_s[...], preferred_element_type=F32)
        g, u = gu[:, :hidden], gu[:, hidden:]
        h = (g * jax.nn.sigmoid(g) * u).astype(BF16)
        _store_rows(o_ref, jnp.dot(h, wd_s[...], preferred_element_type=F32))


def _gmm(xs, w_gate, w_up, w_down, layer, blk_e, n_used, blk=MOE_BLK):
    hidden, d = w_down.shape[2:]
    sub = _row_words(d)
    n_rows = xs.shape[0] // sub
    row = lambda b, be, nu: (jnp.minimum(b, nu[0] - 1), 0)
    expert = lambda b, be, nu: (layer, be[b], 0, 0)
    return pl.pallas_call(
        functools.partial(_gmm_kernel, hidden=hidden, blk=blk, sub=sub),
        grid_spec=pltpu.PrefetchScalarGridSpec(
            num_scalar_prefetch=2,
            grid=(n_rows // blk,),
            in_specs=[pl.BlockSpec((blk * sub, LANES), row),
                      pl.BlockSpec((None, None, d, hidden), expert),
                      pl.BlockSpec((None, None, d, hidden), expert),
                      pl.BlockSpec((None, None, hidden, d), expert)],
            out_specs=pl.BlockSpec((blk * sub, LANES), row),
            scratch_shapes=[pltpu.VMEM((d, 2 * hidden), BF16), pltpu.VMEM((hidden, d), BF16)]),
        out_shape=jax.ShapeDtypeStruct((n_rows * sub, LANES), jnp.uint32),
        compiler_params=_cparams(("arbitrary",)),
        name="expert_gmm",
    )(blk_e, n_used, xs, w_gate, w_up, w_down)


def _combine_kernel(x_ref, yg_ref, gate_ref, wgu_ref, wd_ref, g_ref, b_ref, *rest, alpha, hidden, final_batch):
    x = x_ref[...]
    gu = jnp.dot(x.astype(BF16), wgu_ref[...], preferred_element_type=F32)
    g, u = gu[:, :hidden], gu[:, hidden:]
    h = (g * jax.nn.sigmoid(g) * u).astype(BF16)
    acc = jnp.dot(h, wd_ref[...], preferred_element_type=F32)
    gates = gate_ref[...]
    tm, d = x.shape
    for k in range(TOP_K):
        acc = acc + gates[:, k:k + 1] * _load_rows(yg_ref.at[k], tm, _row_words(d))
    y = _layer_norm(alpha * x + acc, g_ref[...], b_ref[...])
    if final_batch:
        rest[-1][...] = pltpu.einshape("tbd->btd", y.reshape(tm // final_batch, final_batch, d))
    else:
        o_ref, ob_ref = rest[-2:]
        o_ref[...] = y
        ob_ref[...] = y.astype(BF16)


def _combine(x, row0, yg, gates_t, w_gu, w_down, ln_g, ln_b, alpha, prev=None, final_batch=0, tm=COMBINE_TM):
    n, d = x.shape
    n_part = gates_t.shape[0]
    tm = min(tm, n_part)
    blk0 = row0 // tm
    sub = _row_words(d)
    hidden = w_down.shape[0]
    const = lambda a: pl.BlockSpec(a.shape, lambda i: (0,) * a.ndim)
    rows = pl.BlockSpec((tm, d), lambda i: (i + blk0, 0))
    in_specs = [rows,
                pl.BlockSpec((TOP_K, tm * sub, LANES), lambda i: (0, i, 0)),
                pl.BlockSpec((tm, TOP_K), lambda i: (i, 0)),
                const(w_gu), const(w_down), const(ln_g), const(ln_b)]
    args = [x, yg, gates_t, w_gu, w_down, ln_g, ln_b]
    if final_batch:
        out_specs = [pl.BlockSpec((final_batch, tm // final_batch, d), lambda i: (0, i + blk0, 0))]
        out_shape = [jax.ShapeDtypeStruct((final_batch, n // final_batch, d), F32)]
    else:
        out_specs = [rows, rows]
        out_shape = [jax.ShapeDtypeStruct((n, d), F32), jax.ShapeDtypeStruct((n, d), BF16)]
    aliases = {}
    if prev is not None:
        aliases = {len(args) + o: o for o in range(len(out_shape))}
        in_specs += [pl.BlockSpec(memory_space=pl.ANY)] * len(out_shape)
        args += list(prev)
    return pl.pallas_call(
        functools.partial(_combine_kernel, alpha=alpha, hidden=hidden, final_batch=final_batch),
        grid=(n_part // tm,),
        in_specs=in_specs,
        out_specs=out_specs,
        out_shape=out_shape,
        input_output_aliases=aliases,
        compiler_params=_cparams(("arbitrary",)),
        name="combine",
    )(*args)


def _split_w_in(w_in, b_in, d):
    qkv0 = SSM_WIDTH
    qkv_w = len(ATTN_PATTERNS) * ATTN_WIDTH
    gates0 = qkv0 + 3 * qkv_w
    cols = lambda a, c0, w: a[:, c0:c0 + w]
    pick = lambda a: jnp.concatenate([cols(a, gates0, 2 * d), cols(a, 0, SSM_WIDTH)], axis=1)
    main = (pick(w_in).astype(BF16), pick(b_in))
    groups = []
    for g in range(len(ATTN_PATTERNS)):
        sel = lambda a: jnp.concatenate([cols(a, qkv0 + s * qkv_w + g * ATTN_WIDTH, ATTN_WIDTH)
                                         for s in range(3)], axis=1)
        groups.append((sel(w_in).astype(BF16), sel(b_in)))
    return main, groups


def _token_mixer(x, xb, batch, w_in, b_in, s5p, w_glu, w_br_ssm, w_br_attn, w_out, expand, ln_g, ln_b, alpha):
    n, d = x.shape
    seq = n // batch
    (w_main, b_main), groups = _split_w_in(w_in, b_in, d)
    proj = _inproj(xb, w_main, b_main)
    y_ssm = _s5(proj, *s5p, w_glu, batch)

    outs, lses = [], []
    for (window, dil), (w_g, b_g) in zip(ATTN_PATTERNS, groups):
        assert window // dil == QBLK and (seq // dil) % QBLK == 0
        q, k, v = _inproj_qkv(xb, w_g, b_g, dil * batch)
        o_g, lse_g = _attn(q, k, v, window // dil)
        outs.append(jnp.swapaxes(o_g, 0, 1).reshape(n, ATTN_WIDTH))
        lses.append(jnp.swapaxes(lse_g, 0, 1).reshape(n, HEADS))
    lse = jnp.concatenate(lses, axis=-1)
    return _merge(x, y_ssm, outs[0], outs[1], outs[2], lse, proj, w_br_ssm, w_br_attn, w_out,
                  expand, ln_g, ln_b, alpha)


def _moe_plan(x, row0, n, rw_hi, rw_lo, bias, eg, tri):
    eidx, gates, rank, cnt = _route(x, row0, n, rw_hi, rw_lo, bias, eg, tri)
    counts = cnt[:, 0].astype(jnp.int32)
    nk = n * TOP_K
    n_pad = N_EXPERTS * MOE_BLK
    n_rows = nk + n_pad
    n_blocks = n_rows // MOE_BLK
    assert n_pad % n == 0
    pad_counts = (counts + MOE_BLK - 1) // MOE_BLK * MOE_BLK
    pad_end = jnp.cumsum(pad_counts)
    pad_start = pad_end - pad_counts
    onehot = eidx[:, :, None] == jnp.arange(N_EXPERTS, dtype=jnp.int32)
    dest = jnp.sum(jnp.where(onehot, pad_start, 0), axis=-1) + rank
    seg_start = jnp.concatenate([pad_start + counts, pad_end[-1:]])
    seg_len = jnp.concatenate([pad_counts - counts, n_rows - pad_end[-1:]])
    seg_end = jnp.cumsum(seg_len)
    q = jnp.arange(n_pad, dtype=jnp.int32)[:, None]
    seg = jnp.sum((q >= seg_end[None, :]).astype(jnp.int32), axis=1, keepdims=True)
    seg_hot = seg == jnp.arange(N_EXPERTS + 1, dtype=jnp.int32)[None, :]
    pad_dest = jnp.sum(jnp.where(seg_hot, seg_start - (seg_end - seg_len), 0), axis=1) + q[:, 0]
    dest_all = jnp.concatenate([dest, pad_dest.reshape(n_pad // n, n)], axis=0).astype(jnp.int32)
    blk_first = jnp.arange(n_blocks, dtype=jnp.int32)[:, None] * MOE_BLK
    blk_e = jnp.minimum(jnp.sum((blk_first >= pad_end[None, :]).astype(jnp.int32), axis=1), N_EXPERTS - 1)
    n_used = (pad_end[-1:] // MOE_BLK).astype(jnp.int32)
    return dest_all, dest.reshape(nk), gates.T, blk_e, n_used


def _moe(x, x_rows, final_batch, rw_hi, rw_lo, bias, eg, tri, w_gate, w_up, w_down, layer,
         sh_gu, sh_down, ln_g, ln_b, alpha):
    n, d = x.shape
    sub = _row_words(d)
    part = n // MOE_PARTS
    plans = [_moe_plan(x, i * part, part, rw_hi, rw_lo, bias, eg, tri) for i in range(MOE_PARTS)]
    out = None
    for i, (dest_all, dest, gates_t, blk_e, n_used) in enumerate(plans):
        xs = _sc_scatter_rows(x_rows, i * part, dest_all, dest_all.size, sub)
        ys = _gmm(xs, w_gate, w_up, w_down, layer, blk_e, n_used)
        yg = _sc_gather_rows(ys, dest, sub).reshape(TOP_K, part * sub, LANES)
        out = _combine(x, i * part, yg, gates_t, sh_gu, sh_down, ln_g, ln_b, alpha, out, final_batch)
    return out


def kernel(x, w_in, b_in, ssm_lam_re, ssm_lam_im, ssm_log_dt, ssm_b_re, ssm_b_im, ssm_c_re, ssm_c_im, ssm_d, w_glu, w_br_ssm, w_br_attn, w_out, ln1_g, ln1_b, router_w, router_bias, exp_w_gate, exp_w_up, exp_w_down, sh_w_gate, sh_w_up, sh_w_down, ln2_g, ln2_b):
    batch, seq, d = x.shape
    depth = w_in.shape[0]
    assert batch == SUBLANES
    alpha = (2 * depth) ** 0.25
    n = batch * seq
    xt = jnp.swapaxes(x, 0, 1).reshape(n, d)
    xb = xt

    expand = jnp.tile(jnp.repeat(jnp.eye(HEADS, dtype=BF16), HEAD_DIM, axis=1), (2, 1))
    eg = jnp.repeat(jnp.eye(N_EXPERT_GROUPS, dtype=BF16), GROUP_SIZE, axis=0)
    tm = min(ROUTE_TM, n)
    tri = (jnp.arange(tm)[:, None] <= jnp.arange(tm)[None, :]).astype(BF16)
    row = lambda a: a.astype(F32).reshape(1, -1)

    for l in range(depth):
        s5p = _s5_params(ssm_lam_re[l], ssm_lam_im[l], ssm_log_dt[l], ssm_b_re[l], ssm_b_im[l],
                         ssm_c_re[l], ssm_c_im[l], ssm_d[l])
        xt, xt_rows = _token_mixer(xt, xb, batch, w_in[l], row(b_in[l]), s5p,
                                   w_glu[l].astype(BF16), w_br_ssm[l].astype(BF16),
                                   w_br_attn[l].astype(BF16), w_out[l].astype(BF16),
                                   expand, row(ln1_g[l]), row(ln1_b[l]), alpha)
        rwt = router_w[l].astype(F32).T
        rw_hi = rwt.astype(BF16)
        rw_lo = (rwt - rw_hi.astype(F32)).astype(BF16)
        sh_gu = jnp.concatenate([sh_w_gate[l], sh_w_up[l]], axis=-1).astype(BF16)
        last = l == depth - 1
        out = _moe(xt, xt_rows, batch if last else 0, rw_hi, rw_lo,
                   router_bias[l].astype(F32).reshape(-1, 1), eg, tri,
                   exp_w_gate, exp_w_up, exp_w_down, l, sh_gu, sh_w_down[l].astype(BF16),
                   row(ln2_g[l]), row(ln2_b[l]), alpha)
        if not last:
            xt, xb = out
    return out[0]
```

```python
import functools
import math

import jax
import jax.numpy as jnp
import numpy as np
from jax import lax
from jax.experimental import pallas as pl
from jax.experimental.pallas import tpu as pltpu
from jax.experimental.pallas import tpu_sc as plsc

F32 = jnp.float32
BF16 = jnp.bfloat16

SSM_GROUP = 16
SSM_GROUPS = 32
SSM_WIDTH = SSM_GROUP * SSM_GROUPS
SSM_STATE = 64
HEAD_DIM = 64
HEADS = 8
ATTN_PATTERNS = ((128, 1), (512, 4), (2048, 16))
ATTN_WIDTH = HEADS * HEAD_DIM
QBLK = 128
N_EXPERTS = 64
TOP_K = 8
N_EXPERT_GROUPS = 8
GROUP_SIZE = N_EXPERTS // N_EXPERT_GROUPS
TOPK_GROUPS = 4
ROUTED_SCALE = 2.5
LN_EPS = 1e-5
NEG = -1e30

LANES = 128
SUBLANES = 8
VMEM_LIMIT = 56 * 1024 * 1024

INPROJ_TM = 2048
INPROJ_TN = 1280
QKV_TM = 2048
S5_T = 128
S5_KB = 4
ATTN_SUB_BLOCKS = 8
MERGE_TM = 1024
ROUTE_TM = 1024
MOE_BLK = 1024
MOE_PARTS = 2
SC_CHUNK = 128
HI_HALF = np.uint32(0xFFFF0000)
COMBINE_TM = 512


def _cparams(sem):
    return pltpu.CompilerParams(dimension_semantics=sem, vmem_limit_bytes=VMEM_LIMIT)


def _inproj_kernel(x_ref, w_ref, b_ref, o_ref):
    acc = jnp.dot(x_ref[...].astype(BF16), w_ref[...], preferred_element_type=F32)
    o_ref[...] = (acc + b_ref[...]).astype(BF16)


def _inproj(xb, w, b, tm=INPROJ_TM, tn=INPROJ_TN):
    n, d = xb.shape
    width = w.shape[1]
    tm = min(tm, n)
    return pl.pallas_call(
        _inproj_kernel,
        grid=(n // tm, width // tn),
        in_specs=[pl.BlockSpec((tm, d), lambda i, j: (i, 0)),
                  pl.BlockSpec((d, tn), lambda i, j: (0, j)),
                  pl.BlockSpec((1, tn), lambda i, j: (0, j))],
        out_specs=pl.BlockSpec((tm, tn), lambda i, j: (i, j)),
        out_shape=jax.ShapeDtypeStruct((n, width), BF16),
        compiler_params=_cparams(("arbitrary", "arbitrary")),
        name="inproj",
    )(xb, w, b)


def _inproj_qkv_kernel(x_ref, w_ref, b_ref, q_ref, k_ref, v_ref, *, classes):
    x = x_ref[...].astype(BF16)
    rows = x.shape[0] // classes
    for part, o_ref in enumerate((q_ref, k_ref, v_ref)):
        cols = slice(part * ATTN_WIDTH, (part + 1) * ATTN_WIDTH)
        y = jnp.dot(x, w_ref[:, cols], preferred_element_type=F32) + b_ref[:, cols]
        y = y.astype(BF16).reshape(rows, classes, ATTN_WIDTH)
        o_ref[...] = pltpu.einshape("jcf->cjf", y)


def _inproj_qkv(xb, w, b, classes, tm=QKV_TM):
    n, d = xb.shape
    tm = min(tm, n)
    rows = tm // classes
    out = pl.BlockSpec((classes, rows, ATTN_WIDTH), lambda i: (0, i, 0))
    shape = jax.ShapeDtypeStruct((classes, n // classes, ATTN_WIDTH), BF16)
    return pl.pallas_call(
        functools.partial(_inproj_qkv_kernel, classes=classes),
        grid=(n // tm,),
        in_specs=[pl.BlockSpec((tm, d), lambda i: (i, 0)),
                  pl.BlockSpec(w.shape, lambda i: (0, 0)),
                  pl.BlockSpec(b.shape, lambda i: (0, 0))],
        out_specs=[out, out, out],
        out_shape=[shape, shape, shape],
        compiler_params=_cparams(("arbitrary",)),
        name="inproj_qkv",
    )(xb, w, b)


def _gelu_tanh(x):
    c = math.sqrt(2.0 / math.pi)
    return 0.5 * x * (1.0 + jnp.tanh(c * (x + 0.044715 * (x * x * x))))


def _s5_kernel(u_ref, bm_ref, cre_ref, cim_ref, are_ref, aim_ref, d_ref, wglu_ref,
               o_ref, sre_ref, sim_ref, st_re_ref, st_im_ref, y_ref, *, steps, batch):
    kw = SSM_WIDTH // S5_KB
    sw = SSM_GROUPS * SSM_STATE // S5_KB

    @pl.when(pl.program_id(0) == 0)
    def _():
        st_re_ref[...] = jnp.zeros_like(st_re_ref)
        st_im_ref[...] = jnp.zeros_like(st_im_ref)

    for k in range(S5_KB):
        ls = slice(k * sw, (k + 1) * sw)
        bu = jnp.dot(u_ref[:, k * kw:(k + 1) * kw], bm_ref[k], preferred_element_type=F32)
        sre_ref[:, ls] = bu[:, :sw]
        sim_ref[:, ls] = bu[:, sw:]

        ar = jnp.broadcast_to(are_ref[:, ls], (batch, sw))
        ai = jnp.broadcast_to(aim_ref[:, ls], (batch, sw))

        def step(t, carry):
            sr, si = carry
            rows = pl.ds(pl.multiple_of(t * batch, batch), batch)
            nr = ar * sr - ai * si + sre_ref[rows, ls]
            ni = ar * si + ai * sr + sim_ref[rows, ls]
            sre_ref[rows, ls] = nr
            sim_ref[rows, ls] = ni
            return nr, ni

        sr, si = lax.fori_loop(0, steps, step, (st_re_ref[:, ls], st_im_ref[:, ls]), unroll=True)
        st_re_ref[:, ls] = sr
        st_im_ref[:, ls] = si

        y_ref[:, k * kw:(k + 1) * kw] = (
            jnp.dot(sre_ref[:, ls].astype(BF16), cre_ref[k], preferred_element_type=F32)
            + jnp.dot(sim_ref[:, ls].astype(BF16), cim_ref[k], preferred_element_type=F32))

    y = y_ref[...] + d_ref[...] * u_ref[...].astype(F32)
    y = _gelu_tanh(y)
    z = jnp.dot(y.astype(BF16), wglu_ref[...], preferred_element_type=F32)
    o_ref[...] = (y * jax.nn.sigmoid(z)).astype(BF16)


def _s5(proj, bm, cre, cim, a_re, a_im, d_skip, w_glu, batch, steps=S5_T):
    n = proj.shape[0]
    u_blk = proj.shape[1] // SSM_WIDTH - 1
    seq = n // batch
    steps = min(steps, seq)
    rows = steps * batch
    nstate = SSM_GROUPS * SSM_STATE
    const = lambda shape: pl.BlockSpec(shape, lambda i: (0,) * len(shape))
    return pl.pallas_call(
        functools.partial(_s5_kernel, steps=steps, batch=batch),
        grid=(seq // steps,),
        in_specs=[pl.BlockSpec((rows, SSM_WIDTH), lambda i: (i, u_blk)),
                  const(bm.shape), const(cre.shape), const(cim.shape),
                  const(a_re.shape), const(a_im.shape), const(d_skip.shape), const(w_glu.shape)],
        out_specs=pl.BlockSpec((rows, SSM_WIDTH), lambda i: (i, 0)),
        out_shape=jax.ShapeDtypeStruct((n, SSM_WIDTH), BF16),
        scratch_shapes=[pltpu.VMEM((rows, nstate), F32), pltpu.VMEM((rows, nstate), F32),
                        pltpu.VMEM((batch, nstate), F32), pltpu.VMEM((batch, nstate), F32),
                        pltpu.VMEM((rows, SSM_WIDTH), F32)],
        compiler_params=_cparams(("arbitrary",)),
        name="s5",
    )(proj, bm, cre, cim, a_re, a_im, d_skip, w_glu)


def _s5_params(lam_re, lam_im, log_dt, b_re, b_im, c_re, c_im, d_skip):
    lam = lax.complex(lam_re.astype(F32), lam_im.astype(F32))
    dt = jnp.exp(log_dt.astype(F32))[:, None]
    lam_bar = jnp.exp(lam * dt)
    b_bar = ((lam_bar - 1.0) / lam)[:, :, None] * lax.complex(b_re.astype(F32), b_im.astype(F32))
    gl = SSM_GROUPS // S5_KB
    eye = jnp.eye(gl, dtype=F32)

    def in_blocks(b):
        b = b.reshape(S5_KB, gl, SSM_STATE, SSM_GROUP)
        m = jnp.einsum('kgpc,gh->kgchp', b, eye)
        return m.reshape(S5_KB, gl * SSM_GROUP, gl * SSM_STATE)

    def out_blocks(c):
        c = c.reshape(S5_KB, gl, SSM_GROUP, SSM_STATE)
        m = jnp.einsum('kgcp,gh->kgphc', c, eye)
        return m.reshape(S5_KB, gl * SSM_STATE, gl * SSM_GROUP)

    bm = jnp.concatenate([in_blocks(jnp.real(b_bar)), in_blocks(jnp.imag(b_bar))], axis=-1).astype(BF16)
    cre = out_blocks(c_re.astype(F32)).astype(BF16)
    cim = out_blocks(-c_im.astype(F32)).astype(BF16)
    a_re = jnp.real(lam_bar).reshape(1, -1)
    a_im = jnp.imag(lam_bar).reshape(1, -1)
    return bm, cre, cim, a_re, a_im, d_skip.astype(F32).reshape(1, -1)


def _attn_block(q, k2, v2, allowed):
    first = lax.broadcasted_iota(jnp.int32, (QBLK, LANES), 1) < HEAD_DIM
    zero = jnp.zeros((), BF16)
    outs, lses = [], []
    for slab in range(ATTN_WIDTH // LANES):
        ls = slice(slab * LANES, (slab + 1) * LANES)
        qs = jnp.concatenate([jnp.where(first, q[:, ls], zero), jnp.where(first, zero, q[:, ls])], axis=0)
        s = lax.dot_general(qs, k2[:, ls], (((1,), (1,)), ((), ())), preferred_element_type=F32)
        s = jnp.where(allowed, s, NEG)
        m = jnp.max(s, axis=-1, keepdims=True)
        p = jnp.exp(s - m)
        denom = jnp.sum(p, axis=-1, keepdims=True)
        r = jnp.dot(p.astype(BF16), v2[:, ls], preferred_element_type=F32) / denom
        outs.append(jnp.where(first, r[:QBLK], r[QBLK:]))
        lse = m + jnp.log(denom)
        lses += [lse[:QBLK], lse[QBLK:]]
    return jnp.concatenate(outs, axis=-1).astype(BF16), jnp.concatenate(lses, axis=-1)


def _attn_kernel(q_ref, kp_ref, kc_ref, vp_ref, vc_ref, o_ref, lse_ref, *, steps, sub_blocks, single_step):
    j = pl.program_id(1)
    qi = lax.broadcasted_iota(jnp.int32, (2 * QBLK, 2 * QBLK), 0) % QBLK
    kj = lax.broadcasted_iota(jnp.int32, (2 * QBLK, 2 * QBLK), 1)
    dist = qi + QBLK - kj
    band = (dist >= 0) & (dist <= steps)
    for b in range(sub_blocks):
        rows = slice(b * QBLK, (b + 1) * QBLK)
        q = q_ref[rows, :] * (HEAD_DIM ** -0.5)
        if b == 0 and single_step:
            k2, v2, allowed = kc_ref[rows, :], vc_ref[rows, :], band[:, QBLK:]
        else:
            if b == 0:
                k_prev, v_prev = kp_ref[...], vp_ref[...]
                allowed = band & ((kj >= QBLK) | (j > 0))
            else:
                before = slice((b - 1) * QBLK, b * QBLK)
                k_prev, v_prev = kc_ref[before, :], vc_ref[before, :]
                allowed = band
            k2 = jnp.concatenate([k_prev, kc_ref[rows, :]], axis=0)
            v2 = jnp.concatenate([v_prev, vc_ref[rows, :]], axis=0)
        o, lse = _attn_block(q, k2, v2, allowed)
        o_ref[rows, :] = o
        lse_ref[rows, :] = lse


def _attn(q, k, v, steps, sub_blocks=ATTN_SUB_BLOCKS):
    classes, n, width = q.shape
    sub_blocks = min(sub_blocks, n // QBLK)
    rows = sub_blocks * QBLK
    cur = pl.BlockSpec((None, rows, width), lambda c, j: (c, j, 0))
    prev = pl.BlockSpec((None, QBLK, width), lambda c, j: (c, jnp.maximum(j * sub_blocks - 1, 0), 0))
    return pl.pallas_call(
        functools.partial(_attn_kernel, steps=steps, sub_blocks=sub_blocks, single_step=n == rows),
        grid=(classes, n // rows),
        in_specs=[cur, prev, cur, prev, cur],
        out_specs=[cur, pl.BlockSpec((None, rows, HEADS), lambda c, j: (c, j, 0))],
        out_shape=[jax.ShapeDtypeStruct((classes, n, width), BF16),
                   jax.ShapeDtypeStruct((classes, n, HEADS), F32)],
        compiler_params=_cparams(("arbitrary", "arbitrary")),
        name="attn",
    )(q, k, k, v, v)


def _layer_norm(z, g, b):
    mu = jnp.mean(z, axis=-1, keepdims=True)
    zc = z - mu
    var = jnp.mean(zc * zc, axis=-1, keepdims=True)
    return zc * lax.rsqrt(var + LN_EPS) * g + b


def _row_words(d):
    return d // (2 * LANES)


def _store_rows(ref, y):
    rows, d = y.shape
    sub = _row_words(d)
    for s in range(sub):
        lo = y[:, (2 * s) * LANES:(2 * s + 1) * LANES].astype(BF16).astype(F32)
        hi = y[:, (2 * s + 1) * LANES:(2 * s + 2) * LANES].astype(BF16).astype(F32)
        word = (pltpu.bitcast(lo, jnp.uint32) >> 16) | (pltpu.bitcast(hi, jnp.uint32) & HI_HALF)
        ref[pl.ds(s, rows, stride=sub), :] = word


def _load_rows(ref, rows, sub):
    parts = []
    for s in range(sub):
        word = ref[pl.ds(s, rows, stride=sub), :]
        parts.append(pltpu.bitcast(word << 16, F32))
        parts.append(pltpu.bitcast(word & HI_HALF, F32))
    return jnp.concatenate(parts, axis=-1)


def _expand_heads(w, e_ref):
    hi = w.astype(BF16)
    lo = (w - hi.astype(F32)).astype(BF16)
    return jnp.dot(jnp.concatenate([hi, lo], axis=-1), e_ref[...], preferred_element_type=F32)


def _merge_kernel(x_ref, yssm_ref, o0_ref, o1_ref, o2_ref, lse_ref, gs_ref, ga_ref,
                  wbs_ref, wba_ref, wout_ref, e_ref, g_ref, b_ref, o_ref, orow_ref, *, alpha):
    lse = lse_ref[...]
    l0, l1, l2 = lse[:, 0:HEADS], lse[:, HEADS:2 * HEADS], lse[:, 2 * HEADS:3 * HEADS]
    m = jnp.maximum(jnp.maximum(l0, l1), l2)
    e0, e1, e2 = jnp.exp(l0 - m), jnp.exp(l1 - m), jnp.exp(l2 - m)
    den = e0 + e1 + e2
    y_attn = (_expand_heads(e0 / den, e_ref) * o0_ref[...].astype(F32)
              + _expand_heads(e1 / den, e_ref) * o1_ref[...].astype(F32)
              + _expand_heads(e2 / den, e_ref) * o2_ref[...].astype(F32))
    ya = jnp.dot(y_attn.astype(BF16), wba_ref[...], preferred_element_type=F32)
    ys = jnp.dot(yssm_ref[...], wbs_ref[...], preferred_element_type=F32)
    merged = (jax.nn.sigmoid(gs_ref[...].astype(F32)) * ys
              + jax.nn.sigmoid(ga_ref[...].astype(F32)) * ya)
    mix = jnp.dot(merged.astype(BF16), wout_ref[...], preferred_element_type=F32)
    y = _layer_norm(alpha * x_ref[...] + mix, g_ref[...], b_ref[...])
    o_ref[...] = y
    _store_rows(orow_ref, y)


def _merge(x, y_ssm, o0, o1, o2, lse, proj, w_br_ssm, w_br_attn, w_out, expand, ln_g, ln_b,
           alpha, tm=MERGE_TM):
    n, d = x.shape
    tm = min(tm, n)
    gate_blk = proj.shape[1] // d
    row = lambda w: pl.BlockSpec((tm, w), lambda i: (i, 0))
    const = lambda a: pl.BlockSpec(a.shape, lambda i: (0,) * a.ndim, pipeline_mode=pl.Buffered(1))
    return pl.pallas_call(
        functools.partial(_merge_kernel, alpha=alpha),
        grid=(n // tm,),
        in_specs=[row(d), row(SSM_WIDTH), row(ATTN_WIDTH), row(ATTN_WIDTH), row(ATTN_WIDTH),
                  row(lse.shape[1]),
                  pl.BlockSpec((tm, d), lambda i: (i, gate_blk - 2)),
                  pl.BlockSpec((tm, d), lambda i: (i, gate_blk - 1)),
                  const(w_br_ssm), const(w_br_attn), const(w_out), const(expand),
                  const(ln_g), const(ln_b)],
        out_specs=[row(d), pl.BlockSpec((tm * _row_words(d), LANES), lambda i: (i, 0))],
        out_shape=[jax.ShapeDtypeStruct((n, d), F32),
                   jax.ShapeDtypeStruct((n * _row_words(d), LANES), jnp.uint32)],
        compiler_params=_cparams(("arbitrary",)),
        name="merge",
    )(x, y_ssm, o0, o1, o2, lse, proj, proj, w_br_ssm, w_br_attn, w_out, expand, ln_g, ln_b)


def _first_argmax(v, iota, size, axis):
    m = jnp.max(v, axis=axis, keepdims=True)
    idx = jnp.min(jnp.where(v == m, iota, size), axis=axis, keepdims=True)
    return m, idx


def _route_kernel(x_ref, rwh_ref, rwl_ref, bias_ref, eg_ref, tri_ref,
                  eidx_ref, gate_ref, rank_ref, cnt_ref, carry_ref):
    @pl.when(pl.program_id(0) == 0)
    def _():
        carry_ref[...] = jnp.zeros_like(carry_ref)

    x = x_ref[...]
    tm = x.shape[0]
    xh = x.astype(BF16)
    xl = (x - xh.astype(F32)).astype(BF16)
    nt = (((1,), (1,)), ((), ()))
    logits = (lax.dot_general(rwh_ref[...], xh, nt, preferred_element_type=F32)
              + lax.dot_general(rwh_ref[...], xl, nt, preferred_element_type=F32)
              + lax.dot_general(rwl_ref[...], xh, nt, preferred_element_type=F32))
    scores = jax.nn.sigmoid(logits)
    sel = scores + bias_ref[...]

    sel3 = sel.reshape(N_EXPERT_GROUPS, GROUP_SIZE, tm)
    iw = lax.broadcasted_iota(jnp.int32, sel3.shape, 1)
    m1, i1 = _first_argmax(sel3, iw, GROUP_SIZE, 1)
    m2 = jnp.max(jnp.where(iw == i1, NEG, sel3), axis=1, keepdims=True)
    gs = (m1 + m2).reshape(N_EXPERT_GROUPS, tm)

    ig = lax.broadcasted_iota(jnp.int32, gs.shape, 0)
    gmask = jnp.zeros(gs.shape, F32)
    for _ in range(TOPK_GROUPS):
        _, gi = _first_argmax(gs, ig, N_EXPERT_GROUPS, 0)
        hit = ig == gi
        gmask = jnp.where(hit, 1.0, gmask)
        gs = jnp.where(hit, NEG, gs)
    emask = jnp.dot(eg_ref[...], gmask.astype(BF16), preferred_element_type=F32) > 0.5

    masked = jnp.where(emask, sel, NEG)
    ie = lax.broadcasted_iota(jnp.int32, masked.shape, 0)
    chosen = jnp.zeros(masked.shape, F32)
    idxs, vals = [], []
    for _ in range(TOP_K):
        _, ei = _first_argmax(masked, ie, N_EXPERTS, 0)
        hit = ie == ei
        idxs.append(ei)
        vals.append(jnp.sum(jnp.where(hit, scores, 0.0), axis=0, keepdims=True))
        chosen = jnp.where(hit, 1.0, chosen)
        masked = jnp.where(hit, NEG, masked)
    total = vals[0]
    for v in vals[1:]:
        total = total + v

    prefix = jnp.dot(chosen.astype(BF16), tri_ref[...], preferred_element_type=F32)
    pos = carry_ref[:, 0:1] + prefix - 1.0
    ranks = [jnp.sum(jnp.where(ie == ei, pos, 0.0), axis=0, keepdims=True) for ei in idxs]
    carry = carry_ref[...] + jnp.sum(chosen, axis=1, keepdims=True)
    carry_ref[...] = carry
    cnt_ref[...] = carry

    eidx_ref[...] = jnp.concatenate(idxs, axis=0)
    gate_ref[...] = jnp.concatenate([v / total * ROUTED_SCALE for v in vals], axis=0)
    rank_ref[...] = jnp.concatenate(ranks, axis=0).astype(jnp.int32)


def _route(x, row0, n, rw_hi, rw_lo, bias, eg, tri, tm=ROUTE_TM):
    d = x.shape[1]
    tm = min(tm, n)
    blk0 = row0 // tm
    const = lambda a: pl.BlockSpec(a.shape, lambda i: (0,) * a.ndim)
    col = pl.BlockSpec((TOP_K, tm), lambda i: (0, i))
    return pl.pallas_call(
        _route_kernel,
        grid=(n // tm,),
        in_specs=[pl.BlockSpec((tm, d), lambda i: (i + blk0, 0)),
                  const(rw_hi), const(rw_lo), const(bias), const(eg), const(tri)],
        out_specs=[col, col, col, pl.BlockSpec((N_EXPERTS, LANES), lambda i: (0, 0))],
        out_shape=[jax.ShapeDtypeStruct((TOP_K, n), jnp.int32),
                   jax.ShapeDtypeStruct((TOP_K, n), F32),
                   jax.ShapeDtypeStruct((TOP_K, n), jnp.int32),
                   jax.ShapeDtypeStruct((N_EXPERTS, LANES), F32)],
        scratch_shapes=[pltpu.VMEM((N_EXPERTS, LANES), F32)],
        compiler_params=_cparams(("arbitrary",)),
        name="route",
    )(x, rw_hi, rw_lo, bias, eg, tri)


def _sc_gather_rows(src, idx, sub, chunk=SC_CHUNK):
    s = src.shape[0] // sub
    m = idx.shape[0]
    info = plsc.get_sparse_core_info()
    n_workers = info.num_cores * info.num_subcores
    per_worker = m // n_workers
    n_chunks = per_worker // chunk
    assert n_chunks * chunk * n_workers == m
    mesh = plsc.VectorSubcoreMesh(core_axis_name="c", subcore_axis_name="s")

    @functools.partial(
        pl.kernel, mesh=mesh,
        out_type=jax.ShapeDtypeStruct((m, sub, LANES), src.dtype),
        scratch_types=[pltpu.VMEM((n_chunks, chunk), jnp.int32),
                       pltpu.VMEM((chunk, sub, LANES), src.dtype),
                       pltpu.SemaphoreType.DMA])
    def gather(src_hbm, idx_hbm, out_hbm, idx_v, rows_v, sem):
        wid = lax.axis_index("s") * info.num_cores + lax.axis_index("c")
        pltpu.sync_copy(idx_hbm.at[wid], idx_v)

        @pl.loop(0, n_chunks)
        def _(j):
            pltpu.async_copy(src_hbm.at[idx_v.at[j]], rows_v, sem).wait()
            pltpu.sync_copy(rows_v, out_hbm.at[pl.ds(wid * per_worker + j * chunk, chunk)])

    out = gather(src.reshape(s, sub, LANES), idx.reshape(n_workers, n_chunks, chunk))
    return out.reshape(m * sub, LANES)


def _sc_scatter_rows(src, row0, dest, n_out, sub, chunk=SC_CHUNK):
    copies, n = dest.shape
    s = src.shape[0] // sub
    info = plsc.get_sparse_core_info()
    n_workers = info.num_cores * info.num_subcores
    per_worker = n // n_workers
    n_chunks = per_worker // chunk
    assert n_chunks * chunk * n_workers == n and copies * n == n_out
    mesh = plsc.VectorSubcoreMesh(core_axis_name="c", subcore_axis_name="s")
    idx = dest.reshape(copies, n_workers, n_chunks, chunk).transpose(1, 2, 0, 3)
    idx = idx.reshape(n_workers, n_chunks * copies, chunk)

    @functools.partial(
        pl.kernel, mesh=mesh,
        out_type=jax.ShapeDtypeStruct((n_out, sub, LANES), src.dtype),
        scratch_types=[pltpu.VMEM((n_chunks * copies, chunk), jnp.int32),
                       pltpu.VMEM((chunk, sub, LANES), src.dtype),
                       pltpu.SemaphoreType.DMA])
    def scatter(src_hbm, idx_hbm, out_hbm, idx_v, rows_v, sem):
        wid = lax.axis_index("s") * info.num_cores + lax.axis_index("c")
        pltpu.sync_copy(idx_hbm.at[wid], idx_v)

        @pl.loop(0, n_chunks)
        def _(j):
            pltpu.sync_copy(src_hbm.at[pl.ds(row0 + wid * per_worker + j * chunk, chunk)], rows_v)
            for c in range(copies):
                pltpu.async_copy(rows_v, out_hbm.at[idx_v.at[j * copies + c]], sem).wait()

    out = scatter(src.reshape(s, sub, LANES), idx)
    return out.reshape(n_out * sub, LANES)


def _gmm_kernel(be_ref, nu_ref, x_ref, wg_ref, wu_ref, wd_ref, o_ref, wgu_s, wd_s, *, hidden, blk, sub):
    b = pl.program_id(0)
    live = b < nu_ref[0]

    @pl.when(live & ((b == 0) | (be_ref[b] != be_ref[jnp.maximum(b - 1, 0)])))
    def _():
        wgu_s[:, :hidden] = wg_ref[...].astype(BF16)
        wgu_s[:, hidden:] = wu_ref[...].astype(BF16)
        wd_s[...] = wd_ref[...].astype(BF16)

    @pl.when(live)
    def _():
        x = _load_rows(x_ref, blk, sub).astype(BF16)
        gu = jnp.dot(x, wgu_s[...], preferred_element_type=F32)
        g, u = gu[:, :hidden], gu[:, hidden:]
        h = (g * jax.nn.sigmoid(g) * u).astype(BF16)
        _store_rows(o_ref, jnp.dot(h, wd_s[...], preferred_element_type=F32))


def _gmm(xs, w_gate, w_up, w_down, layer, blk_e, n_used, blk=MOE_BLK):
    hidden, d = w_down.shape[2:]
    sub = _row_words(d)
    n_rows = xs.shape[0] // sub
    row = lambda b, be, nu: (jnp.minimum(b, nu[0] - 1), 0)
    expert = lambda b, be, nu: (layer, be[b], 0, 0)
    return pl.pallas_call(
        functools.partial(_gmm_kernel, hidden=hidden, blk=blk, sub=sub),
        grid_spec=pltpu.PrefetchScalarGridSpec(
            num_scalar_prefetch=2,
            grid=(n_rows // blk,),
            in_specs=[pl.BlockSpec((blk * sub, LANES), row),
                      pl.BlockSpec((None, None, d, hidden), expert),
                      pl.BlockSpec((None, None, d, hidden), expert),
                      pl.BlockSpec((None, None, hidden, d), expert)],
            out_specs=pl.BlockSpec((blk * sub, LANES), row),
            scratch_shapes=[pltpu.VMEM((d, 2 * hidden), BF16), pltpu.VMEM((hidden, d), BF16)]),
        out_shape=jax.ShapeDtypeStruct((n_rows * sub, LANES), jnp.uint32),
        compiler_params=_cparams(("arbitrary",)),
        name="expert_gmm",
    )(blk_e, n_used, xs, w_gate, w_up, w_down)


def _combine_kernel(x_ref, yg_ref, gate_ref, wgu_ref, wd_ref, g_ref, b_ref, *rest, alpha, hidden, final_batch):
    x = x_ref[...]
    gu = jnp.dot(x.astype(BF16), wgu_ref[...], preferred_element_type=F32)
    g, u = gu[:, :hidden], gu[:, hidden:]
    h = (g * jax.nn.sigmoid(g) * u).astype(BF16)
    acc = jnp.dot(h, wd_ref[...], preferred_element_type=F32)
    gates = gate_ref[...]
    tm, d = x.shape
    for k in range(TOP_K):
        acc = acc + gates[:, k:k + 1] * _load_rows(yg_ref.at[k], tm, _row_words(d))
    y = _layer_norm(alpha * x + acc, g_ref[...], b_ref[...])
    if final_batch:
        rest[-1][...] = pltpu.einshape("tbd->btd", y.reshape(tm // final_batch, final_batch, d))
    else:
        o_ref, ob_ref = rest[-2:]
        o_ref[...] = y
        ob_ref[...] = y.astype(BF16)


def _combine(x, row0, yg, gates_t, w_gu, w_down, ln_g, ln_b, alpha, prev=None, final_batch=0, tm=COMBINE_TM):
    n, d = x.shape
    n_part = gates_t.shape[0]
    tm = min(tm, n_part)
    blk0 = row0 // tm
    sub = _row_words(d)
    hidden = w_down.shape[0]
    const = lambda a: pl.BlockSpec(a.shape, lambda i: (0,) * a.ndim)
    rows = pl.BlockSpec((tm, d), lambda i: (i + blk0, 0))
    in_specs = [rows,
                pl.BlockSpec((TOP_K, tm * sub, LANES), lambda i: (0, i, 0)),
                pl.BlockSpec((tm, TOP_K), lambda i: (i, 0)),
                const(w_gu), const(w_down), const(ln_g), const(ln_b)]
    args = [x, yg, gates_t, w_gu, w_down, ln_g, ln_b]
    if final_batch:
        out_specs = [pl.BlockSpec((final_batch, tm // final_batch, d), lambda i: (0, i + blk0, 0))]
        out_shape = [jax.ShapeDtypeStruct((final_batch, n // final_batch, d), F32)]
    else:
        out_specs = [rows, rows]
        out_shape = [jax.ShapeDtypeStruct((n, d), F32), jax.ShapeDtypeStruct((n, d), BF16)]
    aliases = {}
    if prev is not None:
        aliases = {len(args) + o: o for o in range(len(out_shape))}
        in_specs += [pl.BlockSpec(memory_space=pl.ANY)] * len(out_shape)
        args += list(prev)
    return pl.pallas_call(
        functools.partial(_combine_kernel, alpha=alpha, hidden=hidden, final_batch=final_batch),
        grid=(n_part // tm,),
        in_specs=in_specs,
        out_specs=out_specs,
        out_shape=out_shape,
        input_output_aliases=aliases,
        compiler_params=_cparams(("arbitrary",)),
        name="combine",
    )(*args)


def _split_w_in(w_in, b_in, d):
    qkv0 = SSM_WIDTH
    qkv_w = len(ATTN_PATTERNS) * ATTN_WIDTH
    gates0 = qkv0 + 3 * qkv_w
    cols = lambda a, c0, w: a[:, c0:c0 + w]
    pick = lambda a: jnp.concatenate([cols(a, gates0, 2 * d), cols(a, 0, SSM_WIDTH)], axis=1)
    main = (pick(w_in).astype(BF16), pick(b_in))
    groups = []
    for g in range(len(ATTN_PATTERNS)):
        sel = lambda a: jnp.concatenate([cols(a, qkv0 + s * qkv_w + g * ATTN_WIDTH, ATTN_WIDTH)
                                         for s in range(3)], axis=1)
        groups.append((sel(w_in).astype(BF16), sel(b_in)))
    return main, groups


def _token_mixer(x, xb, batch, w_in, b_in, s5p, w_glu, w_br_ssm, w_br_attn, w_out, expand, ln_g, ln_b, alpha):
    n, d = x.shape
    seq = n // batch
    (w_main, b_main), groups = _split_w_in(w_in, b_in, d)
    proj = _inproj(xb, w_main, b_main)
    y_ssm = _s5(proj, *s5p, w_glu, batch)

    outs, lses = [], []
    for (window, dil), (w_g, b_g) in zip(ATTN_PATTERNS, groups):
        assert window // dil == QBLK and (seq // dil) % QBLK == 0
        q, k, v = _inproj_qkv(xb, w_g, b_g, dil * batch)
        o_g, lse_g = _attn(q, k, v, window // dil)
        outs.append(jnp.swapaxes(o_g, 0, 1).reshape(n, ATTN_WIDTH))
        lses.append(jnp.swapaxes(lse_g, 0, 1).reshape(n, HEADS))
    lse = jnp.concatenate(lses, axis=-1)
    return _merge(x, y_ssm, outs[0], outs[1], outs[2], lse, proj, w_br_ssm, w_br_attn, w_out,
                  expand, ln_g, ln_b, alpha)


def _moe_plan(x, row0, n, rw_hi, rw_lo, bias, eg, tri):
    eidx, gates, rank, cnt = _route(x, row0, n, rw_hi, rw_lo, bias, eg, tri)
    counts = cnt[:, 0].astype(jnp.int32)
    nk = n * TOP_K
    n_pad = N_EXPERTS * MOE_BLK
    n_rows = nk + n_pad
    n_blocks = n_rows // MOE_BLK
    assert n_pad % n == 0
    pad_counts = (counts + MOE_BLK - 1) // MOE_BLK * MOE_BLK
    pad_end = jnp.cumsum(pad_counts)
    pad_start = pad_end - pad_counts
    onehot = eidx[:, :, None] == jnp.arange(N_EXPERTS, dtype=jnp.int32)
    dest = jnp.sum(jnp.where(onehot, pad_start, 0), axis=-1) + rank
    seg_start = jnp.concatenate([pad_start + counts, pad_end[-1:]])
    seg_len = jnp.concatenate([pad_counts - counts, n_rows - pad_end[-1:]])
    seg_end = jnp.cumsum(seg_len)
    q = jnp.arange(n_pad, dtype=jnp.int32)[:, None]
    seg = jnp.sum((q >= seg_end[None, :]).astype(jnp.int32), axis=1, keepdims=True)
    seg_hot = seg == jnp.arange(N_EXPERTS + 1, dtype=jnp.int32)[None, :]
    pad_dest = jnp.sum(jnp.where(seg_hot, seg_start - (seg_end - seg_len), 0), axis=1) + q[:, 0]
    dest_all = jnp.concatenate([dest, pad_dest.reshape(n_pad // n, n)], axis=0).astype(jnp.int32)
    blk_first = jnp.arange(n_blocks, dtype=jnp.int32)[:, None] * MOE_BLK
    blk_e = jnp.minimum(jnp.sum((blk_first >= pad_end[None, :]).astype(jnp.int32), axis=1), N_EXPERTS - 1)
    n_used = (pad_end[-1:] // MOE_BLK).astype(jnp.int32)
    return dest_all, dest.reshape(nk), gates.T, blk_e, n_used


def _moe(x, x_rows, final_batch, rw_hi, rw_lo, bias, eg, tri, w_gate, w_up, w_down, layer,
         sh_gu, sh_down, ln_g, ln_b, alpha):
    n, d = x.shape
    sub = _row_words(d)
    part = n // MOE_PARTS
    plans = [_moe_plan(x, i * part, part, rw_hi, rw_lo, bias, eg, tri) for i in range(MOE_PARTS)]
    out = None
    for i, (dest_all, dest, gates_t, blk_e, n_used) in enumerate(plans):
        xs = _sc_scatter_rows(x_rows, i * part, dest_all, dest_all.size, sub)
        ys = _gmm(xs, w_gate, w_up, w_down, layer, blk_e, n_used)
        yg = _sc_gather_rows(ys, dest, sub).reshape(TOP_K, part * sub, LANES)
        out = _combine(x, i * part, yg, gates_t, sh_gu, sh_down, ln_g, ln_b, alpha, out, final_batch)
    return out


def kernel(x, w_in, b_in, ssm_lam_re, ssm_lam_im, ssm_log_dt, ssm_b_re, ssm_b_im, ssm_c_re, ssm_c_im, ssm_d, w_glu, w_br_ssm, w_br_attn, w_out, ln1_g, ln1_b, router_w, router_bias, exp_w_gate, exp_w_up, exp_w_down, sh_w_gate, sh_w_up, sh_w_down, ln2_g, ln2_b):
    batch, seq, d = x.shape
    depth = w_in.shape[0]
    assert batch == SUBLANES
    alpha = (2 * depth) ** 0.25
    n = batch * seq
    xt = jnp.swapaxes(x, 0, 1).reshape(n, d)
    xb = xt

    expand = jnp.tile(jnp.repeat(jnp.eye(HEADS, dtype=BF16), HEAD_DIM, axis=1), (2, 1))
    eg = jnp.repeat(jnp.eye(N_EXPERT_GROUPS, dtype=BF16), GROUP_SIZE, axis=0)
    tm = min(ROUTE_TM, n)
    tri = (jnp.arange(tm)[:, None] <= jnp.arange(tm)[None, :]).astype(BF16)
    row = lambda a: a.astype(F32).reshape(1, -1)

    for l in range(depth):
        s5p = _s5_params(ssm_lam_re[l], ssm_lam_im[l], ssm_log_dt[l], ssm_b_re[l], ssm_b_im[l],
                         ssm_c_re[l], ssm_c_im[l], ssm_d[l])
        xt, xt_rows = _token_mixer(xt, xb, batch, w_in[l], row(b_in[l]), s5p,
                                   w_glu[l].astype(BF16), w_br_ssm[l].astype(BF16),
                                   w_br_attn[l].astype(BF16), w_out[l].astype(BF16),
                                   expand, row(ln1_g[l]), row(ln1_b[l]), alpha)
        rwt = router_w[l].astype(F32).T
        rw_hi = rwt.astype(BF16)
        rw_lo = (rwt - rw_hi.astype(F32)).astype(BF16)
        sh_gu = jnp.concatenate([sh_w_gate[l], sh_w_up[l]], axis=-1).astype(BF16)
        last = l == depth - 1
        out = _moe(xt, xt_rows, batch if last else 0, rw_hi, rw_lo,
                   router_bias[l].astype(F32).reshape(-1, 1), eg, tri,
                   exp_w_gate, exp_w_up, exp_w_down, l, sh_gu, sh_w_down[l].astype(BF16),
                   row(ln2_g[l]), row(ln2_b[l]), alpha)
        if not last:
            xt, xb = out
    return out[0]
```

```python
import functools
import math

import jax
import jax.numpy as jnp
import numpy as np
from jax import lax
from jax.experimental import pallas as pl
from jax.experimental.pallas import tpu as pltpu
from jax.experimental.pallas import tpu_sc as plsc

F32 = jnp.float32
BF16 = jnp.bfloat16

SSM_GROUP = 16
SSM_GROUPS = 32
SSM_WIDTH = SSM_GROUP * SSM_GROUPS
SSM_STATE = 64
HEAD_DIM = 64
HEADS = 8
ATTN_PATTERNS = ((128, 1), (512, 4), (2048, 16))
ATTN_WIDTH = HEADS * HEAD_DIM
QBLK = 128
N_EXPERTS = 64
TOP_K = 8
N_EXPERT_GROUPS = 8
GROUP_SIZE = N_EXPERTS // N_EXPERT_GROUPS
TOPK_GROUPS = 4
ROUTED_SCALE = 2.5
LN_EPS = 1e-5
NEG = -1e30

LANES = 128
SUBLANES = 8
VMEM_LIMIT = 56 * 1024 * 1024

INPROJ_TM = 2048
INPROJ_TN = 1280
QKV_TM = 2048
S5_T = 128
S5_KB = 4
ATTN_SUB_BLOCKS = 8
MERGE_TM = 1024
ROUTE_TM = 1024
MOE_BLK = 1024
MOE_PARTS = 2
GMM_BUFS = 3
SC_CHUNK = 128
HI_HALF = np.uint32(0xFFFF0000)
COMBINE_TM = 512


def _cparams(sem):
    return pltpu.CompilerParams(dimension_semantics=sem, vmem_limit_bytes=VMEM_LIMIT)


def _inproj_kernel(x_ref, w_ref, b_ref, o_ref):
    acc = jnp.dot(x_ref[...].astype(BF16), w_ref[...], preferred_element_type=F32)
    o_ref[...] = (acc + b_ref[...]).astype(BF16)


def _inproj(xb, w, b, tm=INPROJ_TM, tn=INPROJ_TN):
    n, d = xb.shape
    width = w.shape[1]
    tm = min(tm, n)
    return pl.pallas_call(
        _inproj_kernel,
        grid=(n // tm, width // tn),
        in_specs=[pl.BlockSpec((tm, d), lambda i, j: (i, 0)),
                  pl.BlockSpec((d, tn), lambda i, j: (0, j)),
                  pl.BlockSpec((1, tn), lambda i, j: (0, j))],
        out_specs=pl.BlockSpec((tm, tn), lambda i, j: (i, j)),
        out_shape=jax.ShapeDtypeStruct((n, width), BF16),
        compiler_params=_cparams(("arbitrary", "arbitrary")),
        name="inproj",
    )(xb, w, b)


def _inproj_qkv_kernel(x_ref, w_ref, b_ref, q_ref, k_ref, v_ref, *, classes):
    x = x_ref[...].astype(BF16)
    rows = x.shape[0] // classes
    for part, o_ref in enumerate((q_ref, k_ref, v_ref)):
        cols = slice(part * ATTN_WIDTH, (part + 1) * ATTN_WIDTH)
        y = jnp.dot(x, w_ref[:, cols], preferred_element_type=F32) + b_ref[:, cols]
        y = y.astype(BF16).reshape(rows, classes, ATTN_WIDTH)
        o_ref[...] = pltpu.einshape("jcf->cjf", y)


def _inproj_qkv(xb, w, b, classes, tm=QKV_TM):
    n, d = xb.shape
    tm = min(tm, n)
    rows = tm // classes
    out = pl.BlockSpec((classes, rows, ATTN_WIDTH), lambda i: (0, i, 0))
    shape = jax.ShapeDtypeStruct((classes, n // classes, ATTN_WIDTH), BF16)
    return pl.pallas_call(
        functools.partial(_inproj_qkv_kernel, classes=classes),
        grid=(n // tm,),
        in_specs=[pl.BlockSpec((tm, d), lambda i: (i, 0)),
                  pl.BlockSpec(w.shape, lambda i: (0, 0)),
                  pl.BlockSpec(b.shape, lambda i: (0, 0))],
        out_specs=[out, out, out],
        out_shape=[shape, shape, shape],
        compiler_params=_cparams(("arbitrary",)),
        name="inproj_qkv",
    )(xb, w, b)


def _gelu_tanh(x):
    c = math.sqrt(2.0 / math.pi)
    return 0.5 * x * (1.0 + jnp.tanh(c * (x + 0.044715 * (x * x * x))))


def _s5_kernel(u_ref, bm_ref, cre_ref, cim_ref, are_ref, aim_ref, d_ref, wglu_ref,
               o_ref, sre_ref, sim_ref, st_re_ref, st_im_ref, y_ref, *, steps, batch):
    kw = SSM_WIDTH // S5_KB
    sw = SSM_GROUPS * SSM_STATE // S5_KB

    @pl.when(pl.program_id(0) == 0)
    def _():
        st_re_ref[...] = jnp.zeros_like(st_re_ref)
        st_im_ref[...] = jnp.zeros_like(st_im_ref)

    for k in range(S5_KB):
        ls = slice(k * sw, (k + 1) * sw)
        bu = jnp.dot(u_ref[:, k * kw:(k + 1) * kw], bm_ref[k], preferred_element_type=F32)
        sre_ref[:, ls] = bu[:, :sw]
        sim_ref[:, ls] = bu[:, sw:]

        ar = jnp.broadcast_to(are_ref[:, ls], (batch, sw))
        ai = jnp.broadcast_to(aim_ref[:, ls], (batch, sw))

        def step(t, carry):
            sr, si = carry
            rows = pl.ds(pl.multiple_of(t * batch, batch), batch)
            nr = ar * sr - ai * si + sre_ref[rows, ls]
            ni = ar * si + ai * sr + sim_ref[rows, ls]
            sre_ref[rows, ls] = nr
            sim_ref[rows, ls] = ni
            return nr, ni

        sr, si = lax.fori_loop(0, steps, step, (st_re_ref[:, ls], st_im_ref[:, ls]), unroll=True)
        st_re_ref[:, ls] = sr
        st_im_ref[:, ls] = si

        y_ref[:, k * kw:(k + 1) * kw] = (
            jnp.dot(sre_ref[:, ls].astype(BF16), cre_ref[k], preferred_element_type=F32)
            + jnp.dot(sim_ref[:, ls].astype(BF16), cim_ref[k], preferred_element_type=F32))

    y = y_ref[...] + d_ref[...] * u_ref[...].astype(F32)
    y = _gelu_tanh(y)
    z = jnp.dot(y.astype(BF16), wglu_ref[...], preferred_element_type=F32)
    o_ref[...] = (y * jax.nn.sigmoid(z)).astype(BF16)


def _s5(proj, bm, cre, cim, a_re, a_im, d_skip, w_glu, batch, steps=S5_T):
    n = proj.shape[0]
    u_blk = proj.shape[1] // SSM_WIDTH - 1
    seq = n // batch
    steps = min(steps, seq)
    rows = steps * batch
    nstate = SSM_GROUPS * SSM_STATE
    const = lambda shape: pl.BlockSpec(shape, lambda i: (0,) * len(shape))
    return pl.pallas_call(
        functools.partial(_s5_kernel, steps=steps, batch=batch),
        grid=(seq // steps,),
        in_specs=[pl.BlockSpec((rows, SSM_WIDTH), lambda i: (i, u_blk)),
                  const(bm.shape), const(cre.shape), const(cim.shape),
                  const(a_re.shape), const(a_im.shape), const(d_skip.shape), const(w_glu.shape)],
        out_specs=pl.BlockSpec((rows, SSM_WIDTH), lambda i: (i, 0)),
        out_shape=jax.ShapeDtypeStruct((n, SSM_WIDTH), BF16),
        scratch_shapes=[pltpu.VMEM((rows, nstate), F32), pltpu.VMEM((rows, nstate), F32),
                        pltpu.VMEM((batch, nstate), F32), pltpu.VMEM((batch, nstate), F32),
                        pltpu.VMEM((rows, SSM_WIDTH), F32)],
        compiler_params=_cparams(("arbitrary",)),
        name="s5",
    )(proj, bm, cre, cim, a_re, a_im, d_skip, w_glu)


def _s5_params(lam_re, lam_im, log_dt, b_re, b_im, c_re, c_im, d_skip):
    lam = lax.complex(lam_re.astype(F32), lam_im.astype(F32))
    dt = jnp.exp(log_dt.astype(F32))[:, None]
    lam_bar = jnp.exp(lam * dt)
    b_bar = ((lam_bar - 1.0) / lam)[:, :, None] * lax.complex(b_re.astype(F32), b_im.astype(F32))
    gl = SSM_GROUPS // S5_KB
    eye = jnp.eye(gl, dtype=F32)

    def in_blocks(b):
        b = b.reshape(S5_KB, gl, SSM_STATE, SSM_GROUP)
        m = jnp.einsum('kgpc,gh->kgchp', b, eye)
        return m.reshape(S5_KB, gl * SSM_GROUP, gl * SSM_STATE)

    def out_blocks(c):
        c = c.reshape(S5_KB, gl, SSM_GROUP, SSM_STATE)
        m = jnp.einsum('kgcp,gh->kgphc', c, eye)
        return m.reshape(S5_KB, gl * SSM_STATE, gl * SSM_GROUP)

    bm = jnp.concatenate([in_blocks(jnp.real(b_bar)), in_blocks(jnp.imag(b_bar))], axis=-1).astype(BF16)
    cre = out_blocks(c_re.astype(F32)).astype(BF16)
    cim = out_blocks(-c_im.astype(F32)).astype(BF16)
    a_re = jnp.real(lam_bar).reshape(1, -1)
    a_im = jnp.imag(lam_bar).reshape(1, -1)
    return bm, cre, cim, a_re, a_im, d_skip.astype(F32).reshape(1, -1)


def _attn_block(q, k2, v2, allowed):
    first = lax.broadcasted_iota(jnp.int32, (QBLK, LANES), 1) < HEAD_DIM
    zero = jnp.zeros((), BF16)
    outs, lses = [], []
    for slab in range(ATTN_WIDTH // LANES):
        ls = slice(slab * LANES, (slab + 1) * LANES)
        qs = jnp.concatenate([jnp.where(first, q[:, ls], zero), jnp.where(first, zero, q[:, ls])], axis=0)
        s = lax.dot_general(qs, k2[:, ls], (((1,), (1,)), ((), ())), preferred_element_type=F32)
        s = jnp.where(allowed, s, NEG)
        m = jnp.max(s, axis=-1, keepdims=True)
        p = jnp.exp(s - m)
        denom = jnp.sum(p, axis=-1, keepdims=True)
        r = jnp.dot(p.astype(BF16), v2[:, ls], preferred_element_type=F32) / denom
        outs.append(jnp.where(first, r[:QBLK], r[QBLK:]))
        lse = m + jnp.log(denom)
        lses += [lse[:QBLK], lse[QBLK:]]
    return jnp.concatenate(outs, axis=-1).astype(BF16), jnp.concatenate(lses, axis=-1)


def _attn_kernel(q_ref, kp_ref, kc_ref, vp_ref, vc_ref, o_ref, lse_ref, *, steps, sub_blocks, single_step):
    j = pl.program_id(1)
    qi = lax.broadcasted_iota(jnp.int32, (2 * QBLK, 2 * QBLK), 0) % QBLK
    kj = lax.broadcasted_iota(jnp.int32, (2 * QBLK, 2 * QBLK), 1)
    dist = qi + QBLK - kj
    band = (dist >= 0) & (dist <= steps)
    for b in range(sub_blocks):
        rows = slice(b * QBLK, (b + 1) * QBLK)
        q = q_ref[rows, :] * (HEAD_DIM ** -0.5)
        if b == 0 and single_step:
            k2, v2, allowed = kc_ref[rows, :], vc_ref[rows, :], band[:, QBLK:]
        else:
            if b == 0:
                k_prev, v_prev = kp_ref[...], vp_ref[...]
                allowed = band & ((kj >= QBLK) | (j > 0))
            else:
                before = slice((b - 1) * QBLK, b * QBLK)
                k_prev, v_prev = kc_ref[before, :], vc_ref[before, :]
                allowed = band
            k2 = jnp.concatenate([k_prev, kc_ref[rows, :]], axis=0)
            v2 = jnp.concatenate([v_prev, vc_ref[rows, :]], axis=0)
        o, lse = _attn_block(q, k2, v2, allowed)
        o_ref[rows, :] = o
        lse_ref[rows, :] = lse


def _attn(q, k, v, steps, sub_blocks=ATTN_SUB_BLOCKS):
    classes, n, width = q.shape
    sub_blocks = min(sub_blocks, n // QBLK)
    rows = sub_blocks * QBLK
    cur = pl.BlockSpec((None, rows, width), lambda c, j: (c, j, 0))
    prev = pl.BlockSpec((None, QBLK, width), lambda c, j: (c, jnp.maximum(j * sub_blocks - 1, 0), 0))
    return pl.pallas_call(
        functools.partial(_attn_kernel, steps=steps, sub_blocks=sub_blocks, single_step=n == rows),
        grid=(classes, n // rows),
        in_specs=[cur, prev, cur, prev, cur],
        out_specs=[cur, pl.BlockSpec((None, rows, HEADS), lambda c, j: (c, j, 0))],
        out_shape=[jax.ShapeDtypeStruct((classes, n, width), BF16),
                   jax.ShapeDtypeStruct((classes, n, HEADS), F32)],
        compiler_params=_cparams(("arbitrary", "arbitrary")),
        name="attn",
    )(q, k, k, v, v)


def _layer_norm(z, g, b):
    mu = jnp.mean(z, axis=-1, keepdims=True)
    zc = z - mu
    var = jnp.mean(zc * zc, axis=-1, keepdims=True)
    return zc * lax.rsqrt(var + LN_EPS) * g + b


def _row_words(d):
    return d // (2 * LANES)


def _store_rows(ref, y):
    rows, d = y.shape
    sub = _row_words(d)
    for s in range(sub):
        lo = y[:, (2 * s) * LANES:(2 * s + 1) * LANES].astype(BF16).astype(F32)
        hi = y[:, (2 * s + 1) * LANES:(2 * s + 2) * LANES].astype(BF16).astype(F32)
        word = (pltpu.bitcast(lo, jnp.uint32) >> 16) | (pltpu.bitcast(hi, jnp.uint32) & HI_HALF)
        ref[pl.ds(s, rows, stride=sub), :] = word


def _load_rows(ref, rows, sub):
    parts = []
    for s in range(sub):
        word = ref[pl.ds(s, rows, stride=sub), :]
        parts.append(pltpu.bitcast(word << 16, F32))
        parts.append(pltpu.bitcast(word & HI_HALF, F32))
    return jnp.concatenate(parts, axis=-1)


def _expand_heads(w, e_ref):
    hi = w.astype(BF16)
    lo = (w - hi.astype(F32)).astype(BF16)
    return jnp.dot(jnp.concatenate([hi, lo], axis=-1), e_ref[...], preferred_element_type=F32)


def _merge_kernel(x_ref, yssm_ref, o0_ref, o1_ref, o2_ref, lse_ref, gs_ref, ga_ref,
                  wbs_ref, wba_ref, wout_ref, e_ref, g_ref, b_ref, o_ref, orow_ref, *, alpha):
    lse = lse_ref[...]
    l0, l1, l2 = lse[:, 0:HEADS], lse[:, HEADS:2 * HEADS], lse[:, 2 * HEADS:3 * HEADS]
    m = jnp.maximum(jnp.maximum(l0, l1), l2)
    e0, e1, e2 = jnp.exp(l0 - m), jnp.exp(l1 - m), jnp.exp(l2 - m)
    den = e0 + e1 + e2
    y_attn = (_expand_heads(e0 / den, e_ref) * o0_ref[...].astype(F32)
              + _expand_heads(e1 / den, e_ref) * o1_ref[...].astype(F32)
              + _expand_heads(e2 / den, e_ref) * o2_ref[...].astype(F32))
    ya = jnp.dot(y_attn.astype(BF16), wba_ref[...], preferred_element_type=F32)
    ys = jnp.dot(yssm_ref[...], wbs_ref[...], preferred_element_type=F32)
    merged = (jax.nn.sigmoid(gs_ref[...].astype(F32)) * ys
              + jax.nn.sigmoid(ga_ref[...].astype(F32)) * ya)
    mix = jnp.dot(merged.astype(BF16), wout_ref[...], preferred_element_type=F32)
    y = _layer_norm(alpha * x_ref[...] + mix, g_ref[...], b_ref[...])
    o_ref[...] = y
    _store_rows(orow_ref, y)


def _merge(x, y_ssm, o0, o1, o2, lse, proj, w_br_ssm, w_br_attn, w_out, expand, ln_g, ln_b,
           alpha, tm=MERGE_TM):
    n, d = x.shape
    tm = min(tm, n)
    gate_blk = proj.shape[1] // d
    row = lambda w: pl.BlockSpec((tm, w), lambda i: (i, 0))
    const = lambda a: pl.BlockSpec(a.shape, lambda i: (0,) * a.ndim, pipeline_mode=pl.Buffered(1))
    return pl.pallas_call(
        functools.partial(_merge_kernel, alpha=alpha),
        grid=(n // tm,),
        in_specs=[row(d), row(SSM_WIDTH), row(ATTN_WIDTH), row(ATTN_WIDTH), row(ATTN_WIDTH),
                  row(lse.shape[1]),
                  pl.BlockSpec((tm, d), lambda i: (i, gate_blk - 2)),
                  pl.BlockSpec((tm, d), lambda i: (i, gate_blk - 1)),
                  const(w_br_ssm), const(w_br_attn), const(w_out), const(expand),
                  const(ln_g), const(ln_b)],
        out_specs=[row(d), pl.BlockSpec((tm * _row_words(d), LANES), lambda i: (i, 0))],
        out_shape=[jax.ShapeDtypeStruct((n, d), F32),
                   jax.ShapeDtypeStruct((n * _row_words(d), LANES), jnp.uint32)],
        compiler_params=_cparams(("arbitrary",)),
        name="merge",
    )(x, y_ssm, o0, o1, o2, lse, proj, proj, w_br_ssm, w_br_attn, w_out, expand, ln_g, ln_b)


def _first_argmax(v, iota, size, axis):
    m = jnp.max(v, axis=axis, keepdims=True)
    idx = jnp.min(jnp.where(v == m, iota, size), axis=axis, keepdims=True)
    return m, idx


def _route_kernel(x_ref, rwh_ref, rwl_ref, bias_ref, eg_ref, tri_ref,
                  eidx_ref, gate_ref, rank_ref, cnt_ref, carry_ref):
    @pl.when(pl.program_id(0) == 0)
    def _():
        carry_ref[...] = jnp.zeros_like(carry_ref)

    x = x_ref[...]
    tm = x.shape[0]
    xh = x.astype(BF16)
    xl = (x - xh.astype(F32)).astype(BF16)
    nt = (((1,), (1,)), ((), ()))
    logits = (lax.dot_general(rwh_ref[...], xh, nt, preferred_element_type=F32)
              + lax.dot_general(rwh_ref[...], xl, nt, preferred_element_type=F32)
              + lax.dot_general(rwl_ref[...], xh, nt, preferred_element_type=F32))
    scores = jax.nn.sigmoid(logits)
    sel = scores + bias_ref[...]

    sel3 = sel.reshape(N_EXPERT_GROUPS, GROUP_SIZE, tm)
    iw = lax.broadcasted_iota(jnp.int32, sel3.shape, 1)
    m1, i1 = _first_argmax(sel3, iw, GROUP_SIZE, 1)
    m2 = jnp.max(jnp.where(iw == i1, NEG, sel3), axis=1, keepdims=True)
    gs = (m1 + m2).reshape(N_EXPERT_GROUPS, tm)

    ig = lax.broadcasted_iota(jnp.int32, gs.shape, 0)
    gmask = jnp.zeros(gs.shape, F32)
    for _ in range(TOPK_GROUPS):
        _, gi = _first_argmax(gs, ig, N_EXPERT_GROUPS, 0)
        hit = ig == gi
        gmask = jnp.where(hit, 1.0, gmask)
        gs = jnp.where(hit, NEG, gs)
    emask = jnp.dot(eg_ref[...], gmask.astype(BF16), preferred_element_type=F32) > 0.5

    masked = jnp.where(emask, sel, NEG)
    ie = lax.broadcasted_iota(jnp.int32, masked.shape, 0)
    chosen = jnp.zeros(masked.shape, F32)
    idxs, vals = [], []
    for _ in range(TOP_K):
        _, ei = _first_argmax(masked, ie, N_EXPERTS, 0)
        hit = ie == ei
        idxs.append(ei)
        vals.append(jnp.sum(jnp.where(hit, scores, 0.0), axis=0, keepdims=True))
        chosen = jnp.where(hit, 1.0, chosen)
        masked = jnp.where(hit, NEG, masked)
    total = vals[0]
    for v in vals[1:]:
        total = total + v

    prefix = jnp.dot(chosen.astype(BF16), tri_ref[...], preferred_element_type=F32)
    pos = carry_ref[:, 0:1] + prefix - 1.0
    ranks = [jnp.sum(jnp.where(ie == ei, pos, 0.0), axis=0, keepdims=True) for ei in idxs]
    carry = carry_ref[...] + jnp.sum(chosen, axis=1, keepdims=True)
    carry_ref[...] = carry
    cnt_ref[...] = carry

    eidx_ref[...] = jnp.concatenate(idxs, axis=0)
    gate_ref[...] = jnp.concatenate([v / total * ROUTED_SCALE for v in vals], axis=0)
    rank_ref[...] = jnp.concatenate(ranks, axis=0).astype(jnp.int32)


def _route(x, row0, n, rw_hi, rw_lo, bias, eg, tri, tm=ROUTE_TM):
    d = x.shape[1]
    tm = min(tm, n)
    blk0 = row0 // tm
    const = lambda a: pl.BlockSpec(a.shape, lambda i: (0,) * a.ndim)
    col = pl.BlockSpec((TOP_K, tm), lambda i: (0, i))
    return pl.pallas_call(
        _route_kernel,
        grid=(n // tm,),
        in_specs=[pl.BlockSpec((tm, d), lambda i: (i + blk0, 0)),
                  const(rw_hi), const(rw_lo), const(bias), const(eg), const(tri)],
        out_specs=[col, col, col, pl.BlockSpec((N_EXPERTS, LANES), lambda i: (0, 0))],
        out_shape=[jax.ShapeDtypeStruct((TOP_K, n), jnp.int32),
                   jax.ShapeDtypeStruct((TOP_K, n), F32),
                   jax.ShapeDtypeStruct((TOP_K, n), jnp.int32),
                   jax.ShapeDtypeStruct((N_EXPERTS, LANES), F32)],
        scratch_shapes=[pltpu.VMEM((N_EXPERTS, LANES), F32)],
        compiler_params=_cparams(("arbitrary",)),
        name="route",
    )(x, rw_hi, rw_lo, bias, eg, tri)


def _sc_gather_rows(src, idx, sub, chunk=SC_CHUNK):
    s = src.shape[0] // sub
    m = idx.shape[0]
    info = plsc.get_sparse_core_info()
    n_workers = info.num_cores * info.num_subcores
    per_worker = m // n_workers
    n_chunks = per_worker // chunk
    assert n_chunks * chunk * n_workers == m
    mesh = plsc.VectorSubcoreMesh(core_axis_name="c", subcore_axis_name="s")

    @functools.partial(
        pl.kernel, mesh=mesh,
        out_type=jax.ShapeDtypeStruct((m, sub, LANES), src.dtype),
        scratch_types=[pltpu.VMEM((n_chunks, chunk), jnp.int32),
                       pltpu.VMEM((chunk, sub, LANES), src.dtype),
                       pltpu.SemaphoreType.DMA])
    def gather(src_hbm, idx_hbm, out_hbm, idx_v, rows_v, sem):
        wid = lax.axis_index("s") * info.num_cores + lax.axis_index("c")
        pltpu.sync_copy(idx_hbm.at[wid], idx_v)

        @pl.loop(0, n_chunks)
        def _(j):
            pltpu.async_copy(src_hbm.at[idx_v.at[j]], rows_v, sem).wait()
            pltpu.sync_copy(rows_v, out_hbm.at[pl.ds(wid * per_worker + j * chunk, chunk)])

    out = gather(src.reshape(s, sub, LANES), idx.reshape(n_workers, n_chunks, chunk))
    return out.reshape(m * sub, LANES)


def _sc_scatter_rows(src, row0, dest, n_out, sub, chunk=SC_CHUNK):
    copies, n = dest.shape
    s = src.shape[0] // sub
    info = plsc.get_sparse_core_info()
    n_workers = info.num_cores * info.num_subcores
    per_worker = n // n_workers
    n_chunks = per_worker // chunk
    assert n_chunks * chunk * n_workers == n and copies * n == n_out
    mesh = plsc.VectorSubcoreMesh(core_axis_name="c", subcore_axis_name="s")
    idx = dest.reshape(copies, n_workers, n_chunks, chunk).transpose(1, 2, 0, 3)
    idx = idx.reshape(n_workers, n_chunks * copies, chunk)

    @functools.partial(
        pl.kernel, mesh=mesh,
        out_type=jax.ShapeDtypeStruct((n_out, sub, LANES), src.dtype),
        scratch_types=[pltpu.VMEM((n_chunks * copies, chunk), jnp.int32),
                       pltpu.VMEM((chunk, sub, LANES), src.dtype),
                       pltpu.SemaphoreType.DMA])
    def scatter(src_hbm, idx_hbm, out_hbm, idx_v, rows_v, sem):
        wid = lax.axis_index("s") * info.num_cores + lax.axis_index("c")
        pltpu.sync_copy(idx_hbm.at[wid], idx_v)

        @pl.loop(0, n_chunks)
        def _(j):
            pltpu.sync_copy(src_hbm.at[pl.ds(row0 + wid * per_worker + j * chunk, chunk)], rows_v)
            for c in range(copies):
                pltpu.async_copy(rows_v, out_hbm.at[idx_v.at[j * copies + c]], sem).wait()

    out = scatter(src.reshape(s, sub, LANES), idx)
    return out.reshape(n_out * sub, LANES)


def _gmm_kernel(be_ref, nu_ref, x_hbm, wg_ref, wu_ref, wd_ref, o_ref, wgu_s, wd_s, xbuf, sem, *, hidden, blk, sub):
    b = pl.program_id(0)
    n_live = nu_ref[0]
    live = b < n_live
    rows = blk * sub

    def fetch(block, slot):
        src = x_hbm.at[pl.ds(pl.multiple_of(block * rows, rows), rows)]
        return pltpu.make_async_copy(src, xbuf.at[slot], sem.at[slot])

    @pl.when(b == 0)
    def _():
        for first in range(GMM_BUFS - 1):
            @pl.when(first < n_live)
            def _():
                fetch(first, first).start()

    ahead = b + GMM_BUFS - 1

    @pl.when(ahead < n_live)
    def _():
        fetch(ahead, ahead % GMM_BUFS).start()

    @pl.when(live & ((b == 0) | (be_ref[b] != be_ref[jnp.maximum(b - 1, 0)])))
    def _():
        wgu_s[:, :hidden] = wg_ref[...].astype(BF16)
        wgu_s[:, hidden:] = wu_ref[...].astype(BF16)
        wd_s[...] = wd_ref[...].astype(BF16)

    @pl.when(live)
    def _():
        slot = b % GMM_BUFS
        fetch(b, slot).wait()
        x = _load_rows(xbuf.at[slot], blk, sub).astype(BF16)
        gu = jnp.dot(x, wgu_s[...], preferred_element_type=F32)
        g, u = gu[:, :hidden], gu[:, hidden:]
        h = (g * jax.nn.sigmoid(g) * u).astype(BF16)
        _store_rows(o_ref, jnp.dot(h, wd_s[...], preferred_element_type=F32))


def _gmm(xs, w_gate, w_up, w_down, layer, blk_e, n_used, blk=MOE_BLK):
    hidden, d = w_down.shape[2:]
    sub = _row_words(d)
    n_rows = xs.shape[0] // sub
    row = lambda b, be, nu: (jnp.minimum(b, nu[0] - 1), 0)
    expert = lambda b, be, nu: (layer, be[b], 0, 0)
    return pl.pallas_call(
        functools.partial(_gmm_kernel, hidden=hidden, blk=blk, sub=sub),
        grid_spec=pltpu.PrefetchScalarGridSpec(
            num_scalar_prefetch=2,
            grid=(n_rows // blk,),
            in_specs=[pl.BlockSpec(memory_space=pl.ANY),
                      pl.BlockSpec((None, None, d, hidden), expert),
                      pl.BlockSpec((None, None, d, hidden), expert),
                      pl.BlockSpec((None, None, hidden, d), expert)],
            out_specs=pl.BlockSpec((blk * sub, LANES), row),
            scratch_shapes=[pltpu.VMEM((d, 2 * hidden), BF16), pltpu.VMEM((hidden, d), BF16),
                            pltpu.VMEM((GMM_BUFS, blk * sub, LANES), xs.dtype),
                            pltpu.SemaphoreType.DMA((GMM_BUFS,))]),
        out_shape=jax.ShapeDtypeStruct((n_rows * sub, LANES), jnp.uint32),
        compiler_params=_cparams(("arbitrary",)),
        name="expert_gmm",
    )(blk_e, n_used, xs, w_gate, w_up, w_down)


def _combine_kernel(x_ref, yg_ref, gate_ref, wgu_ref, wd_ref, g_ref, b_ref, *rest, alpha, hidden, final_batch):
    x = x_ref[...]
    gu = jnp.dot(x.astype(BF16), wgu_ref[...], preferred_element_type=F32)
    g, u = gu[:, :hidden], gu[:, hidden:]
    h = (g * jax.nn.sigmoid(g) * u).astype(BF16)
    acc = jnp.dot(h, wd_ref[...], preferred_element_type=F32)
    gates = gate_ref[...]
    tm, d = x.shape
    for k in range(TOP_K):
        acc = acc + gates[:, k:k + 1] * _load_rows(yg_ref.at[k], tm, _row_words(d))
    y = _layer_norm(alpha * x + acc, g_ref[...], b_ref[...])
    if final_batch:
        rest[-1][...] = pltpu.einshape("tbd->btd", y.reshape(tm // final_batch, final_batch, d))
    else:
        o_ref, ob_ref = rest[-2:]
        o_ref[...] = y
        ob_ref[...] = y.astype(BF16)


def _combine(x, row0, yg, gates_t, w_gu, w_down, ln_g, ln_b, alpha, prev=None, final_batch=0, tm=COMBINE_TM):
    n, d = x.shape
    n_part = gates_t.shape[0]
    tm = min(tm, n_part)
    blk0 = row0 // tm
    sub = _row_words(d)
    hidden = w_down.shape[0]
    const = lambda a: pl.BlockSpec(a.shape, lambda i: (0,) * a.ndim)
    rows = pl.BlockSpec((tm, d), lambda i: (i + blk0, 0))
    in_specs = [rows,
                pl.BlockSpec((TOP_K, tm * sub, LANES), lambda i: (0, i, 0)),
                pl.BlockSpec((tm, TOP_K), lambda i: (i, 0)),
                const(w_gu), const(w_down), const(ln_g), const(ln_b)]
    args = [x, yg, gates_t, w_gu, w_down, ln_g, ln_b]
    if final_batch:
        out_specs = [pl.BlockSpec((final_batch, tm // final_batch, d), lambda i: (0, i + blk0, 0))]
        out_shape = [jax.ShapeDtypeStruct((final_batch, n // final_batch, d), F32)]
    else:
        out_specs = [rows, rows]
        out_shape = [jax.ShapeDtypeStruct((n, d), F32), jax.ShapeDtypeStruct((n, d), BF16)]
    aliases = {}
    if prev is not None:
        aliases = {len(args) + o: o for o in range(len(out_shape))}
        in_specs += [pl.BlockSpec(memory_space=pl.ANY)] * len(out_shape)
        args += list(prev)
    return pl.pallas_call(
        functools.partial(_combine_kernel, alpha=alpha, hidden=hidden, final_batch=final_batch),
        grid=(n_part // tm,),
        in_specs=in_specs,
        out_specs=out_specs,
        out_shape=out_shape,
        input_output_aliases=aliases,
        compiler_params=_cparams(("arbitrary",)),
        name="combine",
    )(*args)


def _split_w_in(w_in, b_in, d):
    qkv0 = SSM_WIDTH
    qkv_w = len(ATTN_PATTERNS) * ATTN_WIDTH
    gates0 = qkv0 + 3 * qkv_w
    cols = lambda a, c0, w: a[:, c0:c0 + w]
    pick = lambda a: jnp.concatenate([cols(a, gates0, 2 * d), cols(a, 0, SSM_WIDTH)], axis=1)
    main = (pick(w_in).astype(BF16), pick(b_in))
    groups = []
    for g in range(len(ATTN_PATTERNS)):
        sel = lambda a: jnp.concatenate([cols(a, qkv0 + s * qkv_w + g * ATTN_WIDTH, ATTN_WIDTH)
                                         for s in range(3)], axis=1)
        groups.append((sel(w_in).astype(BF16), sel(b_in)))
    return main, groups


def _token_mixer(x, xb, batch, w_in, b_in, s5p, w_glu, w_br_ssm, w_br_attn, w_out, expand, ln_g, ln_b, alpha):
    n, d = x.shape
    seq = n // batch
    (w_main, b_main), groups = _split_w_in(w_in, b_in, d)
    proj = _inproj(xb, w_main, b_main)
    y_ssm = _s5(proj, *s5p, w_glu, batch)

    outs, lses = [], []
    for (window, dil), (w_g, b_g) in zip(ATTN_PATTERNS, groups):
        assert window // dil == QBLK and (seq // dil) % QBLK == 0
        q, k, v = _inproj_qkv(xb, w_g, b_g, dil * batch)
        o_g, lse_g = _attn(q, k, v, window // dil)
        outs.append(jnp.swapaxes(o_g, 0, 1).reshape(n, ATTN_WIDTH))
        lses.append(jnp.swapaxes(lse_g, 0, 1).reshape(n, HEADS))
    lse = jnp.concatenate(lses, axis=-1)
    return _merge(x, y_ssm, outs[0], outs[1], outs[2], lse, proj, w_br_ssm, w_br_attn, w_out,
                  expand, ln_g, ln_b, alpha)


def _moe_plan(x, row0, n, rw_hi, rw_lo, bias, eg, tri):
    eidx, gates, rank, cnt = _route(x, row0, n, rw_hi, rw_lo, bias, eg, tri)
    counts = cnt[:, 0].astype(jnp.int32)
    nk = n * TOP_K
    n_pad = N_EXPERTS * MOE_BLK
    n_rows = nk + n_pad
    n_blocks = n_rows // MOE_BLK
    assert n_pad % n == 0
    pad_counts = (counts + MOE_BLK - 1) // MOE_BLK * MOE_BLK
    pad_end = jnp.cumsum(pad_counts)
    pad_start = pad_end - pad_counts
    onehot = eidx[:, :, None] == jnp.arange(N_EXPERTS, dtype=jnp.int32)
    dest = jnp.sum(jnp.where(onehot, pad_start, 0), axis=-1) + rank
    seg_start = jnp.concatenate([pad_start + counts, pad_end[-1:]])
    seg_len = jnp.concatenate([pad_counts - counts, n_rows - pad_end[-1:]])
    seg_end = jnp.cumsum(seg_len)
    q = jnp.arange(n_pad, dtype=jnp.int32)[:, None]
    seg = jnp.sum((q >= seg_end[None, :]).astype(jnp.int32), axis=1, keepdims=True)
    seg_hot = seg == jnp.arange(N_EXPERTS + 1, dtype=jnp.int32)[None, :]
    pad_dest = jnp.sum(jnp.where(seg_hot, seg_start - (seg_end - seg_len), 0), axis=1) + q[:, 0]
    dest_all = jnp.concatenate([dest, pad_dest.reshape(n_pad // n, n)], axis=0).astype(jnp.int32)
    blk_first = jnp.arange(n_blocks, dtype=jnp.int32)[:, None] * MOE_BLK
    blk_e = jnp.minimum(jnp.sum((blk_first >= pad_end[None, :]).astype(jnp.int32), axis=1), N_EXPERTS - 1)
    n_used = (pad_end[-1:] // MOE_BLK).astype(jnp.int32)
    return dest_all, dest.reshape(nk), gates.T, blk_e, n_used


def _moe(x, x_rows, final_batch, rw_hi, rw_lo, bias, eg, tri, w_gate, w_up, w_down, layer,
         sh_gu, sh_down, ln_g, ln_b, alpha):
    n, d = x.shape
    sub = _row_words(d)
    part = n // MOE_PARTS
    plans = [_moe_plan(x, i * part, part, rw_hi, rw_lo, bias, eg, tri) for i in range(MOE_PARTS)]
    out = None
    for i, (dest_all, dest, gates_t, blk_e, n_used) in enumerate(plans):
        xs = _sc_scatter_rows(x_rows, i * part, dest_all, dest_all.size, sub)
        ys = _gmm(xs, w_gate, w_up, w_down, layer, blk_e, n_used)
        yg = _sc_gather_rows(ys, dest, sub).reshape(TOP_K, part * sub, LANES)
        out = _combine(x, i * part, yg, gates_t, sh_gu, sh_down, ln_g, ln_b, alpha, out, final_batch)
    return out


def kernel(x, w_in, b_in, ssm_lam_re, ssm_lam_im, ssm_log_dt, ssm_b_re, ssm_b_im, ssm_c_re, ssm_c_im, ssm_d, w_glu, w_br_ssm, w_br_attn, w_out, ln1_g, ln1_b, router_w, router_bias, exp_w_gate, exp_w_up, exp_w_down, sh_w_gate, sh_w_up, sh_w_down, ln2_g, ln2_b):
    batch, seq, d = x.shape
    depth = w_in.shape[0]
    assert batch == SUBLANES
    alpha = (2 * depth) ** 0.25
    n = batch * seq
    xt = jnp.swapaxes(x, 0, 1).reshape(n, d)
    xb = xt

    expand = jnp.tile(jnp.repeat(jnp.eye(HEADS, dtype=BF16), HEAD_DIM, axis=1), (2, 1))
    eg = jnp.repeat(jnp.eye(N_EXPERT_GROUPS, dtype=BF16), GROUP_SIZE, axis=0)
    tm = min(ROUTE_TM, n)
    tri = (jnp.arange(tm)[:, None] <= jnp.arange(tm)[None, :]).astype(BF16)
    row = lambda a: a.astype(F32).reshape(1, -1)

    for l in range(depth):
        s5p = _s5_params(ssm_lam_re[l], ssm_lam_im[l], ssm_log_dt[l], ssm_b_re[l], ssm_b_im[l],
                         ssm_c_re[l], ssm_c_im[l], ssm_d[l])
        xt, xt_rows = _token_mixer(xt, xb, batch, w_in[l], row(b_in[l]), s5p,
                                   w_glu[l].astype(BF16), w_br_ssm[l].astype(BF16),
                                   w_br_attn[l].astype(BF16), w_out[l].astype(BF16),
                                   expand, row(ln1_g[l]), row(ln1_b[l]), alpha)
        rwt = router_w[l].astype(F32).T
        rw_hi = rwt.astype(BF16)
        rw_lo = (rwt - rw_hi.astype(F32)).astype(BF16)
        sh_gu = jnp.concatenate([sh_w_gate[l], sh_w_up[l]], axis=-1).astype(BF16)
        last = l == depth - 1
        out = _moe(xt, xt_rows, batch if last else 0, rw_hi, rw_lo,
                   router_bias[l].astype(F32).reshape(-1, 1), eg, tri,
                   exp_w_gate, exp_w_up, exp_w_down, l, sh_gu, sh_w_down[l].astype(BF16),
                   row(ln2_g[l]), row(ln2_b[l]), alpha)
        if not last:
            xt, xb = out
    return out[0]
```
